```python
import jax, jax.numpy as jnp
from jax import lax
import numpy as np

D_MODEL = 2048
BATCH = 8
SEQ = 8192
DEPTH = 1

A_HEAD_DIM = 128
A_WIDTH = D_MODEL // 2
A_HEADS = A_WIDTH // A_HEAD_DIM
A_CHUNK = 64
B_HEAD_DIM = 128
B_WIDTH = D_MODEL // 2
B_HEADS = B_WIDTH // B_HEAD_DIM
Q_BLOCK = 128
D_FF = -(-(8 * D_MODEL) // (3 * 256)) * 256
N_IN = 4 * A_WIDTH + 3 * B_WIDTH + B_HEADS + 2 * D_MODEL
RMS_EPS = 1e-6

kernel_name = "hgrn2_fox_gated_parallel_sandwich_block"


def _in_split_points():
    sizes = [A_WIDTH] * 4 + [B_WIDTH] * 3 + [B_HEADS, D_MODEL, D_MODEL]
    return [int(v) for v in np.cumsum(sizes)[:-1]]


def rms_norm(x, w):
    xf = x.astype(jnp.float32)
    y = xf * lax.rsqrt(jnp.mean(xf * xf, axis=-1, keepdims=True) + RMS_EPS)
    return (y * w.astype(jnp.float32)).astype(x.dtype)


def hgrn2_mixer(q, f_logit, i, g, lb, norm_w):
    B, S, _ = q.shape
    H, D, C = A_HEADS, A_HEAD_DIM, A_CHUNK
    z = f_logit.astype(jnp.float32)
    lb = lb.astype(jnp.float32)
    log_f = jnp.log(lb + (1.0 - lb) * jax.nn.sigmoid(z))
    k = (1.0 - lb) * jax.nn.sigmoid(-z)

    def to_chunks(t):
        return t.astype(jnp.float32).reshape(B, S // C, C, H, D).transpose(1, 0, 3, 2, 4)

    causal = jnp.tril(jnp.ones((C, C), dtype=bool))[None, None, :, :, None]

    def step(state, inp):
        qc, kc, vc, gc = inp
        b = jnp.cumsum(gc, axis=2)
        o_inter = jnp.einsum('bhtk,bhkv->bhtv', qc * jnp.exp(b), state)
        diff = jnp.where(causal, b[:, :, :, None, :] - b[:, :, None, :, :], -jnp.inf)
        scores = jnp.einsum('bhtsk,bhsk->bhts', qc[:, :, :, None, :] * jnp.exp(diff), kc)
        o_intra = jnp.einsum('bhts,bhsv->bhtv', scores, vc)
        b_last = b[:, :, -1:, :]
        k_dec = kc * jnp.exp(b_last - b)
        new_state = jnp.exp(b_last[:, :, 0, :])[..., None] * state + jnp.einsum('bhsk,bhsv->bhkv', k_dec, vc)
        return new_state, o_inter + o_intra

    state0 = jnp.zeros((B, H, D, D), jnp.float32)
    _, o = lax.scan(step, state0, (to_chunks(q), to_chunks(k), to_chunks(i), to_chunks(log_f)))
    o = o.transpose(1, 0, 3, 2, 4).reshape(B, S, H, D)
    o = rms_norm(o, norm_w) * jax.nn.silu(g.astype(jnp.float32).reshape(B, S, H, D))
    return o.reshape(B, S, A_WIDTH).astype(q.dtype)


def fox_mixer(q, k, v, f_logit):
    B, S, _ = q.shape
    H, D, Qb = B_HEADS, B_HEAD_DIM, Q_BLOCK
    n_blk = S // Qb
    qh = q.reshape(B, S, H, D).transpose(0, 2, 1, 3) * (D ** -0.5)
    kh = k.reshape(B, S, H, D).transpose(0, 2, 1, 3)
    vh = v.reshape(B, S, H, D).transpose(0, 2, 1, 3)
    cum = jnp.cumsum(jax.nn.log_sigmoid(f_logit.astype(jnp.float32)), axis=1).transpose(0, 2, 1)
    q_blocks = qh.reshape(B, H, n_blk, Qb, D).transpose(2, 0, 1, 3, 4)
    c_blocks = cum.reshape(B, H, n_blk, Qb).transpose(2, 0, 1, 3)
    pos_k = jnp.arange(S)

    def block(args):
        idx, q_blk, c_blk = args
        pos_q = idx * Qb + jnp.arange(Qb)
        s = jnp.einsum('bhqd,bhkd->bhqk', q_blk, kh).astype(jnp.float32)
        s = s + c_blk[..., None] - cum[:, :, None, :]
        s = jnp.where((pos_q[:, None] >= pos_k[None, :])[None, None], s, -jnp.inf)
        p = jax.nn.softmax(s, axis=-1)
        return jnp.einsum('bhqk,bhkd->bhqd', p.astype(vh.dtype), vh)

    out = lax.map(block, (jnp.arange(n_blk), q_blocks, c_blocks))
    return out.transpose(1, 0, 3, 2, 4).reshape(B, S, B_WIDTH)


def _fwd_setup_inputs(seed: int = 0) -> dict:
    key = jax.random.key(seed)
    ks = jax.random.split(key, 14)
    f32 = jnp.float32

    def dense(k, fan_in, fan_out):
        return jax.random.normal(k, (DEPTH, fan_in, fan_out), f32) * fan_in ** -0.5

    def gain(k, n):
        return 1.0 + 0.02 * jax.random.normal(k, (DEPTH, n), f32)

    return {
        "x": jax.random.normal(ks[0], (BATCH, SEQ, D_MODEL), f32),
        "w_in": dense(ks[1], D_MODEL, N_IN),
        "b_fox_f": 0.1 * jax.random.normal(ks[2], (DEPTH, B_HEADS), f32),
        "hgrn_lb_logits": jax.random.normal(ks[3], (DEPTH + 1, A_WIDTH), f32),
        "hgrn_norm_w": gain(ks[4], A_HEAD_DIM),
        "w_up_a": dense(ks[5], A_WIDTH, D_MODEL),
        "w_up_b": dense(ks[6], B_WIDTH, D_MODEL),
        "w_o": dense(ks[7], D_MODEL, D_MODEL),
        "norm_mix_pre": gain(ks[8], D_MODEL),
        "norm_mix_post": gain(ks[9], D_MODEL),
        "norm_ffn_pre": gain(ks[10], D_MODEL),
        "norm_ffn_post": gain(ks[11], D_MODEL),
        "w_ffn_in": dense(ks[12], D_MODEL, 2 * D_FF),
        "w_ffn_down": dense(ks[13], D_FF, D_MODEL),
    }


def _fwd_reference(x, w_in, b_fox_f, hgrn_lb_logits, hgrn_norm_w, w_up_a, w_up_b, w_o,
              norm_mix_pre, norm_mix_post, norm_ffn_pre, norm_ffn_post, w_ffn_in, w_ffn_down):
    split_points = _in_split_points()
    lb_table = jnp.cumsum(jax.nn.softmax(hgrn_lb_logits.astype(jnp.float32), axis=0), axis=0)
    for l in range(DEPTH):
        h = rms_norm(x, norm_mix_pre[l])
        proj = h @ w_in[l]
        a_q, a_f, a_i, a_g, b_q, b_k, b_v, b_f, g_a, g_b = jnp.split(proj, split_points, axis=-1)
        y_a = hgrn2_mixer(a_q, a_f, a_i, a_g, lb_table[l], hgrn_norm_w[l]) @ w_up_a[l]
        y_b = fox_mixer(b_q, b_k, b_v, b_f + b_fox_f[l]) @ w_up_b[l]
        merged = jax.nn.sigmoid(g_a) * y_a + jax.nn.sigmoid(g_b) * y_b
        x = x + rms_norm(merged @ w_o[l], norm_mix_post[l])
        h = rms_norm(x, norm_ffn_pre[l])
        gate, up = jnp.split(h @ w_ffn_in[l], 2, axis=-1)
        x = x + rms_norm((jax.nn.silu(gate) * up) @ w_ffn_down[l], norm_ffn_post[l])
    return x


import jax as _jax
import jax.numpy as _jnp

TWIN_FORMAT = 'train_step'
FWD_PARAMS = ['x', 'w_in', 'b_fox_f', 'hgrn_lb_logits', 'hgrn_norm_w', 'w_up_a', 'w_up_b', 'w_o', 'norm_mix_pre', 'norm_mix_post', 'norm_ffn_pre', 'norm_ffn_post', 'w_ffn_in', 'w_ffn_down']
TWIN_WEIGHTS = ['w_in', 'b_fox_f', 'hgrn_lb_logits', 'hgrn_norm_w', 'w_up_a', 'w_up_b', 'w_o', 'norm_mix_pre', 'norm_mix_post', 'norm_ffn_pre', 'norm_ffn_post', 'w_ffn_in', 'w_ffn_down']
TWIN_DIFF_INPUT = 'x'
TWIN_INPUTS = ['x', 'w_in', 'b_fox_f', 'hgrn_lb_logits', 'hgrn_norm_w', 'w_up_a', 'w_up_b', 'w_o', 'norm_mix_pre', 'norm_mix_post', 'norm_ffn_pre', 'norm_ffn_post', 'w_ffn_in', 'w_ffn_down', 'loss_target', 'm_w_in', 'm_b_fox_f', 'm_hgrn_lb_logits', 'm_hgrn_norm_w', 'm_w_up_a', 'm_w_up_b', 'm_w_o', 'm_norm_mix_pre', 'm_norm_mix_post', 'm_norm_ffn_pre', 'm_norm_ffn_post', 'm_w_ffn_in', 'm_w_ffn_down', 'v_w_in', 'v_b_fox_f', 'v_hgrn_lb_logits', 'v_hgrn_norm_w', 'v_w_up_a', 'v_w_up_b', 'v_w_o', 'v_norm_mix_pre', 'v_norm_mix_post', 'v_norm_ffn_pre', 'v_norm_ffn_post', 'v_w_ffn_in', 'v_w_ffn_down']
TWIN_OUTPUTS = ['loss', 'grad_x', 'grad_w_in', 'grad_b_fox_f', 'grad_hgrn_lb_logits', 'grad_hgrn_norm_w', 'grad_w_up_a', 'grad_w_up_b', 'grad_w_o', 'grad_norm_mix_pre', 'grad_norm_mix_post', 'grad_norm_ffn_pre', 'grad_norm_ffn_post', 'grad_w_ffn_in', 'grad_w_ffn_down', 'delta_w_in', 'delta_b_fox_f', 'delta_hgrn_lb_logits', 'delta_hgrn_norm_w', 'delta_w_up_a', 'delta_w_up_b', 'delta_w_o', 'delta_norm_mix_pre', 'delta_norm_mix_post', 'delta_norm_ffn_pre', 'delta_norm_ffn_post', 'delta_w_ffn_in', 'delta_w_ffn_down', 'new_m_w_in', 'new_m_b_fox_f', 'new_m_hgrn_lb_logits', 'new_m_hgrn_norm_w', 'new_m_w_up_a', 'new_m_w_up_b', 'new_m_w_o', 'new_m_norm_mix_pre', 'new_m_norm_mix_post', 'new_m_norm_ffn_pre', 'new_m_norm_ffn_post', 'new_m_w_ffn_in', 'new_m_w_ffn_down', 'new_v_w_in', 'new_v_b_fox_f', 'new_v_hgrn_lb_logits', 'new_v_hgrn_norm_w', 'new_v_w_up_a', 'new_v_w_up_b', 'new_v_w_o', 'new_v_norm_mix_pre', 'new_v_norm_mix_post', 'new_v_norm_ffn_pre', 'new_v_norm_ffn_post', 'new_v_w_ffn_in', 'new_v_w_ffn_down']
TWIN_LEAF_KINDS = {'loss': 'loss', 'grad_x': 'grad_x', 'grad_w_in': 'grad_w', 'grad_b_fox_f': 'grad_w', 'grad_hgrn_lb_logits': 'grad_w', 'grad_hgrn_norm_w': 'grad_w', 'grad_w_up_a': 'grad_w', 'grad_w_up_b': 'grad_w', 'grad_w_o': 'grad_w', 'grad_norm_mix_pre': 'grad_w', 'grad_norm_mix_post': 'grad_w', 'grad_norm_ffn_pre': 'grad_w', 'grad_norm_ffn_post': 'grad_w', 'grad_w_ffn_in': 'grad_w', 'grad_w_ffn_down': 'grad_w', 'delta_w_in': 'delta_w', 'delta_b_fox_f': 'delta_w', 'delta_hgrn_lb_logits': 'delta_w', 'delta_hgrn_norm_w': 'delta_w', 'delta_w_up_a': 'delta_w', 'delta_w_up_b': 'delta_w', 'delta_w_o': 'delta_w', 'delta_norm_mix_pre': 'delta_w', 'delta_norm_mix_post': 'delta_w', 'delta_norm_ffn_pre': 'delta_w', 'delta_norm_ffn_post': 'delta_w', 'delta_w_ffn_in': 'delta_w', 'delta_w_ffn_down': 'delta_w', 'new_m_w_in': 'new_m', 'new_m_b_fox_f': 'new_m', 'new_m_hgrn_lb_logits': 'new_m', 'new_m_hgrn_norm_w': 'new_m', 'new_m_w_up_a': 'new_m', 'new_m_w_up_b': 'new_m', 'new_m_w_o': 'new_m', 'new_m_norm_mix_pre': 'new_m', 'new_m_norm_mix_post': 'new_m', 'new_m_norm_ffn_pre': 'new_m', 'new_m_norm_ffn_post': 'new_m', 'new_m_w_ffn_in': 'new_m', 'new_m_w_ffn_down': 'new_m', 'new_v_w_in': 'new_v', 'new_v_b_fox_f': 'new_v', 'new_v_hgrn_lb_logits': 'new_v', 'new_v_hgrn_norm_w': 'new_v', 'new_v_w_up_a': 'new_v', 'new_v_w_up_b': 'new_v', 'new_v_w_o': 'new_v', 'new_v_norm_mix_pre': 'new_v', 'new_v_norm_mix_post': 'new_v', 'new_v_norm_ffn_pre': 'new_v', 'new_v_norm_ffn_post': 'new_v', 'new_v_w_ffn_in': 'new_v', 'new_v_w_ffn_down': 'new_v'}


def _forward(args):
    return _fwd_reference(*[args[k] for k in FWD_PARAMS])


def _output_shape():
    def fwd():
        inp = _fwd_setup_inputs(0)
        return _fwd_reference(*[inp[k] for k in FWD_PARAMS])
    out = _jax.eval_shape(fwd)
    return out.shape, out.dtype

N_MICROBATCH = 1
ADAM_LR = 0.001
ADAM_B1 = 0.9
ADAM_B2 = 0.999
ADAM_EPS = 1e-08
ADAM_WD = 0.01
ADAM_STEP = 10
PER_EXAMPLE_BATCH_AXIS = {'x': 0, 'loss_target': 0}
SHARED_INPUTS = []
_WEIGHT_DTYPES = {'w_in': _jnp.float32, 'b_fox_f': _jnp.float32, 'hgrn_lb_logits': _jnp.float32, 'hgrn_norm_w': _jnp.float32, 'w_up_a': _jnp.float32, 'w_up_b': _jnp.float32, 'w_o': _jnp.float32, 'norm_mix_pre': _jnp.float32, 'norm_mix_post': _jnp.float32, 'norm_ffn_pre': _jnp.float32, 'norm_ffn_post': _jnp.float32, 'w_ffn_in': _jnp.float32, 'w_ffn_down': _jnp.float32}
MOMENT_SCALE = {'w_in': 2.472452e-01, 'b_fox_f': 2.730590e+00, 'hgrn_lb_logits': 1.467717e-01, 'hgrn_norm_w': 9.400265e-01, 'w_up_a': 2.193103e-01, 'w_up_b': 3.237042e-01, 'w_o': 4.127608e-01, 'norm_mix_pre': 5.708778e-01, 'norm_mix_post': 3.196981e+01, 'norm_ffn_pre': 3.695998e-01, 'norm_ffn_post': 3.197584e+01, 'w_ffn_in': 1.671196e-01, 'w_ffn_down': 3.304834e-01}


def _to_microbatches(a, axis):
    t = _jnp.moveaxis(a, axis, 0)
    t = t.reshape((N_MICROBATCH, t.shape[0] // N_MICROBATCH) + t.shape[1:])
    return _jnp.moveaxis(t, 1, axis + 1)


def setup_inputs(seed: int = 0) -> dict:
    inp = _fwd_setup_inputs(seed)
    key = _jax.random.fold_in(_jax.random.key(seed), 7919)
    shape, _ = _output_shape()
    out = dict(inp)
    out["loss_target"] = _jax.random.normal(_jax.random.fold_in(key, 0), shape, _jnp.float32)
    for i, name in enumerate(TWIN_WEIGHTS):
        w = inp[name].astype(_jnp.float32)
        if MOMENT_SCALE is None:
            s = _jnp.sqrt(_jnp.mean(_jnp.square(w)) + 1e-30)
        else:
            s = MOMENT_SCALE[name]
        km, kv = _jax.random.split(_jax.random.fold_in(key, i + 1))
        out[name] = w
        out["m_" + name] = s * _jax.random.normal(km, w.shape, _jnp.float32)
        out["v_" + name] = (s * s) * _jax.random.uniform(kv, w.shape, _jnp.float32, 0.5, 1.5)
    if N_MICROBATCH > 1:
        for name, axis in PER_EXAMPLE_BATCH_AXIS.items():
            out[name] = _to_microbatches(out[name], axis)
    return {'x': out['x'], 'w_in': out['w_in'], 'b_fox_f': out['b_fox_f'], 'hgrn_lb_logits': out['hgrn_lb_logits'], 'hgrn_norm_w': out['hgrn_norm_w'], 'w_up_a': out['w_up_a'], 'w_up_b': out['w_up_b'], 'w_o': out['w_o'], 'norm_mix_pre': out['norm_mix_pre'], 'norm_mix_post': out['norm_mix_post'], 'norm_ffn_pre': out['norm_ffn_pre'], 'norm_ffn_post': out['norm_ffn_post'], 'w_ffn_in': out['w_ffn_in'], 'w_ffn_down': out['w_ffn_down'], 'loss_target': out['loss_target'], 'm_w_in': out['m_w_in'], 'm_b_fox_f': out['m_b_fox_f'], 'm_hgrn_lb_logits': out['m_hgrn_lb_logits'], 'm_hgrn_norm_w': out['m_hgrn_norm_w'], 'm_w_up_a': out['m_w_up_a'], 'm_w_up_b': out['m_w_up_b'], 'm_w_o': out['m_w_o'], 'm_norm_mix_pre': out['m_norm_mix_pre'], 'm_norm_mix_post': out['m_norm_mix_post'], 'm_norm_ffn_pre': out['m_norm_ffn_pre'], 'm_norm_ffn_post': out['m_norm_ffn_post'], 'm_w_ffn_in': out['m_w_ffn_in'], 'm_w_ffn_down': out['m_w_ffn_down'], 'v_w_in': out['v_w_in'], 'v_b_fox_f': out['v_b_fox_f'], 'v_hgrn_lb_logits': out['v_hgrn_lb_logits'], 'v_hgrn_norm_w': out['v_hgrn_norm_w'], 'v_w_up_a': out['v_w_up_a'], 'v_w_up_b': out['v_w_up_b'], 'v_w_o': out['v_w_o'], 'v_norm_mix_pre': out['v_norm_mix_pre'], 'v_norm_mix_post': out['v_norm_mix_post'], 'v_norm_ffn_pre': out['v_norm_ffn_pre'], 'v_norm_ffn_post': out['v_norm_ffn_post'], 'v_w_ffn_in': out['v_w_ffn_in'], 'v_w_ffn_down': out['v_w_ffn_down']}


def _loss(weights, diff, rest, loss_target):
    with _jax.named_scope("forward"):
        args = {**rest, TWIN_DIFF_INPUT: diff, **{k: w.astype(_WEIGHT_DTYPES[k]) for k, w in weights.items()}}
        y = _forward(args)
    with _jax.named_scope("loss_head"):
        err = _jnp.square(y.astype(_jnp.float32) - loss_target)
        return 0.5 * _jnp.sum(_jnp.mean(err, axis=-1)) if err.ndim else 0.5 * err


def _adamw(w, g, m, v):
    m = ADAM_B1 * m + (1.0 - ADAM_B1) * g
    v = ADAM_B2 * v + (1.0 - ADAM_B2) * _jnp.square(g)
    m_hat = m / (1.0 - ADAM_B1 ** ADAM_STEP)
    v_hat = v / (1.0 - ADAM_B2 ** ADAM_STEP)
    delta = -ADAM_LR * (m_hat / (_jnp.sqrt(v_hat) + ADAM_EPS) + ADAM_WD * w)
    return delta, m, v


def reference(x, w_in, b_fox_f, hgrn_lb_logits, hgrn_norm_w, w_up_a, w_up_b, w_o, norm_mix_pre, norm_mix_post, norm_ffn_pre, norm_ffn_post, w_ffn_in, w_ffn_down, loss_target, m_w_in, m_b_fox_f, m_hgrn_lb_logits, m_hgrn_norm_w, m_w_up_a, m_w_up_b, m_w_o, m_norm_mix_pre, m_norm_mix_post, m_norm_ffn_pre, m_norm_ffn_post, m_w_ffn_in, m_w_ffn_down, v_w_in, v_b_fox_f, v_hgrn_lb_logits, v_hgrn_norm_w, v_w_up_a, v_w_up_b, v_w_o, v_norm_mix_pre, v_norm_mix_post, v_norm_ffn_pre, v_norm_ffn_post, v_w_ffn_in, v_w_ffn_down):
    given = dict(x=x, w_in=w_in, b_fox_f=b_fox_f, hgrn_lb_logits=hgrn_lb_logits, hgrn_norm_w=hgrn_norm_w, w_up_a=w_up_a, w_up_b=w_up_b, w_o=w_o, norm_mix_pre=norm_mix_pre, norm_mix_post=norm_mix_post, norm_ffn_pre=norm_ffn_pre, norm_ffn_post=norm_ffn_post, w_ffn_in=w_ffn_in, w_ffn_down=w_ffn_down, loss_target=loss_target, m_w_in=m_w_in, m_b_fox_f=m_b_fox_f, m_hgrn_lb_logits=m_hgrn_lb_logits, m_hgrn_norm_w=m_hgrn_norm_w, m_w_up_a=m_w_up_a, m_w_up_b=m_w_up_b, m_w_o=m_w_o, m_norm_mix_pre=m_norm_mix_pre, m_norm_mix_post=m_norm_mix_post, m_norm_ffn_pre=m_norm_ffn_pre, m_norm_ffn_post=m_norm_ffn_post, m_w_ffn_in=m_w_ffn_in, m_w_ffn_down=m_w_ffn_down, v_w_in=v_w_in, v_b_fox_f=v_b_fox_f, v_hgrn_lb_logits=v_hgrn_lb_logits, v_hgrn_norm_w=v_hgrn_norm_w, v_w_up_a=v_w_up_a, v_w_up_b=v_w_up_b, v_w_o=v_w_o, v_norm_mix_pre=v_norm_mix_pre, v_norm_mix_post=v_norm_mix_post, v_norm_ffn_pre=v_norm_ffn_pre, v_norm_ffn_post=v_norm_ffn_post, v_w_ffn_in=v_w_ffn_in, v_w_ffn_down=v_w_ffn_down)
    weights = {n: given[n] for n in TWIN_WEIGHTS}
    shared = {n: given[n] for n in SHARED_INPUTS}
    per_example = {n: given[n] for n in ['x']}
    grad_fn = _jax.value_and_grad(_loss, argnums=(0, 1))

    def one_microbatch(ex, loss_target):
        ex = dict(ex)
        diff = ex.pop(TWIN_DIFF_INPUT)
        return grad_fn(weights, diff, {**shared, **ex}, loss_target)

    if N_MICROBATCH == 1:
        loss, (grad_w, grad_x) = one_microbatch(per_example, given["loss_target"])
    else:
        def body(carry, xs):
            loss_sum, grad_sum = carry
            l_k, (gw_k, gx_k) = one_microbatch(xs[0], xs[1])
            with _jax.named_scope("update"):
                return (loss_sum + l_k, _jax.tree.map(_jnp.add, grad_sum, gw_k)), gx_k

        init = (_jnp.zeros((), _jnp.float32), _jax.tree.map(_jnp.zeros_like, weights))
        (loss, grad_w), grad_x = _jax.lax.scan(body, init, (per_example, given["loss_target"]))
    with _jax.named_scope("update"):
        delta_w, new_m, new_v = {}, {}, {}
        for n in TWIN_WEIGHTS:
            delta_w[n], new_m[n], new_v[n] = _adamw(weights[n], grad_w[n], given["m_" + n], given["v_" + n])
    return (loss, grad_x, *[grad_w[n] for n in TWIN_WEIGHTS], *[delta_w[n] for n in TWIN_WEIGHTS],
            *[new_m[n] for n in TWIN_WEIGHTS], *[new_v[n] for n in TWIN_WEIGHTS])
```

```python
from typing import NamedTuple

import numpy as np
import jax
import jax.numpy as jnp
from jax import lax
from jax.experimental import pallas as pl
from jax.experimental.pallas import tpu as pltpu

F32 = jnp.float32
BF16 = jnp.bfloat16
MESH = pl.DeviceIdType.MESH

RMS_EPS = 1e-6
HEAD = 128
CHUNK = 64
GROUP = 4
LEVELS = (32, 16, 8, 4, 2, 1)
NEG = -1e30
FLAT_W = 1024
FLAT_ROWS = 256

ADAM_LR = 0.001
ADAM_B1 = 0.9
ADAM_B2 = 0.999
ADAM_EPS = 1e-08
ADAM_WD = 0.01
ADAM_STEP = 10

VMEM_LIMIT = 56 * 1024 * 1024


class Dims(NamedTuple):
    T: int
    D: int
    FF: int

    @property
    def AW(self):
        return self.D // 2

    @property
    def H(self):
        return self.AW // HEAD

    @property
    def NMAIN(self):
        return 7 * self.AW

    @property
    def NREST(self):
        return 2 * self.D + HEAD

    @property
    def NIN(self):
        return 7 * self.AW + self.H + 2 * self.D


def _cparams(sem, vmem=VMEM_LIMIT, **kw):
    return pltpu.CompilerParams(dimension_semantics=sem, vmem_limit_bytes=vmem, **kw)


def _tile(n, target):
    if n <= target:
        return n
    t = (target // 128) * 128
    while t >= 128:
        if n % t == 0:
            return t
        t -= 128
    raise ValueError(f"no tile for {n}")


def _dot(a, b, dims):
    return lax.dot_general(a, b, (dims, ((), ())), preferred_element_type=F32)


def _nn(a, b):
    return _dot(a, b, ((1,), (0,)))


def _nt(a, b):
    return _dot(a, b, ((1,), (1,)))


def _tn(a, b):
    return _dot(a, b, ((0,), (0,)))


def _sigmoid(x):
    return jax.nn.sigmoid(x)


def _split2(x):
    hi = x.astype(BF16)
    lo = (x - hi.astype(F32)).astype(BF16)
    return hi, lo


def _split3(x):
    hi = x.astype(BF16)
    r = x - hi.astype(F32)
    mid = r.astype(BF16)
    lo = (r - mid.astype(F32)).astype(BF16)
    return hi, mid, lo


def _mm(a, b, mode, out_dtype, name, add=None, tm=512, tn=1024, tk=512):
    if mode == "nn":
        (M, K), (K2, N) = a.shape, b.shape
    elif mode == "nt":
        (M, K), (N, K2) = a.shape, b.shape
    else:
        (K, M), (K2, N) = a.shape, b.shape
    assert K == K2, (a.shape, b.shape, mode)
    tm, tn, tk = _tile(M, tm), _tile(N, tn), _tile(K, tk)
    nk = K // tk
    if mode == "nn":
        a_spec = pl.BlockSpec((tm, tk), lambda i, j, k: (i, k))
        b_spec = pl.BlockSpec((tk, tn), lambda i, j, k: (k, j))
        op = _nn
    elif mode == "nt":
        a_spec = pl.BlockSpec((tm, tk), lambda i, j, k: (i, k))
        b_spec = pl.BlockSpec((tn, tk), lambda i, j, k: (j, k))
        op = _nt
    else:
        a_spec = pl.BlockSpec((tk, tm), lambda i, j, k: (k, i))
        b_spec = pl.BlockSpec((tk, tn), lambda i, j, k: (k, j))
        op = _tn
    o_spec = pl.BlockSpec((tm, tn), lambda i, j, k: (i, j))
    has_add = add is not None

    def body(*refs):
        if has_add:
            a_ref, b_ref, add_ref, o_ref, acc = refs
        else:
            a_ref, b_ref, o_ref, acc = refs
        k = pl.program_id(2)

        @pl.when(k == 0)
        def _():
            acc[...] = jnp.zeros_like(acc)

        acc[...] += op(a_ref[...].astype(BF16), b_ref[...].astype(BF16))

        @pl.when(k == nk - 1)
        def _():
            r = acc[...]
            if has_add:
                r = r + add_ref[...].astype(F32)
            o_ref[...] = r.astype(out_dtype)

    in_specs = [a_spec, b_spec] + ([o_spec] if has_add else [])
    args = (a, b) + ((add,) if has_add else ())
    return pl.pallas_call(
        body, name=name, grid=(M // tm, N // tn, nk),
        in_specs=in_specs, out_specs=o_spec,
        out_shape=jax.ShapeDtypeStruct((M, N), out_dtype),
        scratch_shapes=[pltpu.VMEM((tm, tn), F32)],
        compiler_params=_cparams(("parallel", "parallel", "arbitrary")),
    )(*args)


def _rows(tr, w, col=0):
    return pl.BlockSpec((tr, w), lambda i, *_: (i, col))


def _vec(w):
    return pl.BlockSpec((1, w), lambda i, *_: (0, 0))


def _rstd(v):
    return lax.rsqrt(jnp.mean(v * v, axis=-1, keepdims=True) + RMS_EPS)


def _rms_bwd(dn, n, r):
    return r * (dn - n * jnp.mean(dn * n, axis=-1, keepdims=True))


def _colsum(v):
    return jnp.sum(v, axis=0, keepdims=True)


def _rms_fwd(x, g, tr=256):
    T, D = x.shape

    def body(x_ref, g_ref, h_ref, r_ref):
        xv = x_ref[...]
        r = _rstd(xv)
        h_ref[...] = (xv * r * g_ref[...]).astype(BF16)
        r_ref[...] = r

    return pl.pallas_call(
        body, name="rms_fwd", grid=(T // tr,),
        in_specs=[_rows(tr, D), _vec(D)],
        out_specs=[_rows(tr, D), _rows(tr, 1)],
        out_shape=[jax.ShapeDtypeStruct((T, D), BF16), jax.ShapeDtypeStruct((T, 1), F32)],
        compiler_params=_cparams(("parallel",)),
    )(x, g)


def _merge_fwd(rest, y_a, y_b, tr=256):
    T, D = y_a.shape

    def body(ga_ref, gb_ref, ya_ref, yb_ref, o_ref):
        o_ref[...] = (_sigmoid(ga_ref[...]) * ya_ref[...] + _sigmoid(gb_ref[...]) * yb_ref[...]).astype(BF16)

    return pl.pallas_call(
        body, name="merge_fwd", grid=(T // tr,),
        in_specs=[_rows(tr, D, 0), _rows(tr, D, 1), _rows(tr, D), _rows(tr, D)],
        out_specs=_rows(tr, D),
        out_shape=jax.ShapeDtypeStruct((T, D), BF16),
        compiler_params=_cparams(("parallel",)),
    )(rest, rest, y_a, y_b)


def _post_pre(x, u, g2, g3, tr=256):
    T, D = x.shape

    def body(x_ref, u_ref, g2_ref, g3_ref, x1_ref, r2_ref, h3_ref, r3_ref):
        uv = u_ref[...]
        r2 = _rstd(uv)
        x1 = x_ref[...] + uv * r2 * g2_ref[...]
        r3 = _rstd(x1)
        x1_ref[...] = x1
        r2_ref[...] = r2
        h3_ref[...] = (x1 * r3 * g3_ref[...]).astype(BF16)
        r3_ref[...] = r3

    return pl.pallas_call(
        body, name="post_pre", grid=(T // tr,),
        in_specs=[_rows(tr, D), _rows(tr, D), _vec(D), _vec(D)],
        out_specs=[_rows(tr, D), _rows(tr, 1), _rows(tr, D), _rows(tr, 1)],
        out_shape=[jax.ShapeDtypeStruct((T, D), F32), jax.ShapeDtypeStruct((T, 1), F32),
                   jax.ShapeDtypeStruct((T, D), BF16), jax.ShapeDtypeStruct((T, 1), F32)],
        compiler_params=_cparams(("parallel",)),
    )(x, u, g2, g3)


def _swiglu_fwd(gu, tr=256):
    T, FF2 = gu.shape
    FF = FF2 // 2
    tc = _tile(FF, 1024)
    nc = FF // tc

    def body(g_ref, u_ref, o_ref):
        gv = g_ref[...]
        o_ref[...] = (gv * _sigmoid(gv) * u_ref[...]).astype(BF16)

    return pl.pallas_call(
        body, name="swiglu_fwd", grid=(T // tr, nc),
        in_specs=[pl.BlockSpec((tr, tc), lambda i, j: (i, j)),
                  pl.BlockSpec((tr, tc), lambda i, j: (i, j + nc))],
        out_specs=pl.BlockSpec((tr, tc), lambda i, j: (i, j)),
        out_shape=jax.ShapeDtypeStruct((T, FF), BF16),
        compiler_params=_cparams(("parallel", "parallel")),
    )(gu, gu)


def _loss_bwd(x1, w, g4, tgt, tr=256):
    T, D = x1.shape

    def body(x1_ref, w_ref, g4_ref, t_ref, loss_ref, dy_ref, dw_ref, dg_ref):
        i = pl.program_id(0)

        @pl.when(i == 0)
        def _():
            loss_ref[...] = jnp.zeros_like(loss_ref)
            dg_ref[...] = jnp.zeros_like(dg_ref)

        wv = w_ref[...]
        g4v = g4_ref[...]
        r4 = _rstd(wv)
        n4 = wv * r4
        e = x1_ref[...] + n4 * g4v - t_ref[...]
        loss_ref[...] += 0.5 * jnp.sum(jnp.mean(e * e, axis=-1, keepdims=True), axis=0, keepdims=True)
        dy = e * (1.0 / D)
        dy_ref[...] = dy
        dg_ref[...] += _colsum(dy * n4)
        dw_ref[...] = _rms_bwd(dy * g4v, n4, r4).astype(BF16)

    return pl.pallas_call(
        body, name="loss_bwd", grid=(T // tr,),
        in_specs=[_rows(tr, D), _rows(tr, D), _vec(D), _rows(tr, D)],
        out_specs=[_vec(HEAD), _rows(tr, D), _rows(tr, D), _vec(D)],
        out_shape=[jax.ShapeDtypeStruct((1, HEAD), F32), jax.ShapeDtypeStruct((T, D), F32),
                   jax.ShapeDtypeStruct((T, D), BF16), jax.ShapeDtypeStruct((1, D), F32)],
        compiler_params=_cparams(("arbitrary",)),
    )(x1, w, g4, tgt)


def _swiglu_bwd(gu, dact, tr=128):
    T, FF2 = gu.shape
    FF = FF2 // 2

    def body(g_ref, u_ref, d_ref, o_ref):
        h = pl.program_id(1)
        gv = g_ref[...]
        s = _sigmoid(gv)
        dv = d_ref[...].astype(F32)

        @pl.when(h == 0)
        def _():
            o_ref[...] = (dv * u_ref[...] * (s * (1.0 + gv * (1.0 - s)))).astype(BF16)

        @pl.when(h == 1)
        def _():
            o_ref[...] = (dv * (gv * s)).astype(BF16)

    return pl.pallas_call(
        body, name="swiglu_bwd", grid=(T // tr, 2),
        in_specs=[pl.BlockSpec((tr, FF), lambda i, h: (i, 0)),
                  pl.BlockSpec((tr, FF), lambda i, h: (i, 1)),
                  pl.BlockSpec((tr, FF), lambda i, h: (i, 0))],
        out_specs=pl.BlockSpec((tr, FF), lambda i, h: (i, h)),
        out_shape=jax.ShapeDtypeStruct((T, FF2), BF16),
        compiler_params=_cparams(("parallel", "arbitrary")),
    )(gu, gu, dact)


def _rms_bwd2(dy, dh3, x1, r3, g3, u, r2, g2, tr=256):
    T, D = dy.shape

    def body(dy_ref, dh_ref, x1_ref, r3_ref, g3_ref, u_ref, r2_ref, g2_ref, dx1_ref, du_ref, dg3_ref, dg2_ref):
        i = pl.program_id(0)

        @pl.when(i == 0)
        def _():
            dg3_ref[...] = jnp.zeros_like(dg3_ref)
            dg2_ref[...] = jnp.zeros_like(dg2_ref)

        r3v, r2v = r3_ref[...], r2_ref[...]
        dh = dh_ref[...]
        n3 = x1_ref[...] * r3v
        dg3_ref[...] += _colsum(dh * n3)
        dx1 = dy_ref[...] + _rms_bwd(dh * g3_ref[...], n3, r3v)
        dx1_ref[...] = dx1
        n2 = u_ref[...] * r2v
        dg2_ref[...] += _colsum(dx1 * n2)
        du_ref[...] = _rms_bwd(dx1 * g2_ref[...], n2, r2v).astype(BF16)

    return pl.pallas_call(
        body, name="rms_bwd2", grid=(T // tr,),
        in_specs=[_rows(tr, D), _rows(tr, D), _rows(tr, D), _rows(tr, 1), _vec(D),
                  _rows(tr, D), _rows(tr, 1), _vec(D)],
        out_specs=[_rows(tr, D), _rows(tr, D), _vec(D), _vec(D)],
        out_shape=[jax.ShapeDtypeStruct((T, D), F32), jax.ShapeDtypeStruct((T, D), BF16),
                   jax.ShapeDtypeStruct((1, D), F32), jax.ShapeDtypeStruct((1, D), F32)],
        compiler_params=_cparams(("arbitrary",)),
    )(dy, dh3, x1, r3, g3, u, r2, g2)


def _merge_bwd(dmerged, rest, y_a, y_b, tr=256):
    T, D = dmerged.shape

    def body(dm_ref, ga_ref, gb_ref, ya_ref, yb_ref, dya_ref, dyb_ref, dg_ref):
        h = pl.program_id(1)
        dm = dm_ref[...]

        @pl.when(h == 0)
        def _():
            s = _sigmoid(ga_ref[...])
            dya_ref[...] = (dm * s).astype(BF16)
            dg_ref[...] = (dm * ya_ref[...] * s * (1.0 - s)).astype(BF16)

        @pl.when(h == 1)
        def _():
            s = _sigmoid(gb_ref[...])
            dyb_ref[...] = (dm * s).astype(BF16)
            dg_ref[...] = (dm * yb_ref[...] * s * (1.0 - s)).astype(BF16)

    blk = lambda col: pl.BlockSpec((tr, D), lambda i, h: (i, col))
    return pl.pallas_call(
        body, name="merge_bwd", grid=(T // tr, 2),
        in_specs=[blk(0), blk(0), blk(1), blk(0), blk(0)],
        out_specs=[blk(0), blk(0), pl.BlockSpec((tr, D), lambda i, h: (i, h))],
        out_shape=[jax.ShapeDtypeStruct((T, D), BF16), jax.ShapeDtypeStruct((T, D), BF16),
                   jax.ShapeDtypeStruct((T, 2 * D), BF16)],
        compiler_params=_cparams(("parallel", "arbitrary")),
    )(dmerged, rest, rest, y_a, y_b)


def _rms_bwd1(dx1, dh1, x, r1, g1, tr=256):
    T, D = x.shape

    def body(dx1_ref, dh_ref, x_ref, r_ref, g_ref, dx_ref, dg_ref):
        i = pl.program_id(0)

        @pl.when(i == 0)
        def _():
            dg_ref[...] = jnp.zeros_like(dg_ref)

        rv = r_ref[...]
        dh = dh_ref[...]
        n = x_ref[...] * rv
        dg_ref[...] += _colsum(dh * n)
        dx_ref[...] = dx1_ref[...] + _rms_bwd(dh * g_ref[...], n, rv)

    return pl.pallas_call(
        body, name="rms_bwd1", grid=(T // tr,),
        in_specs=[_rows(tr, D), _rows(tr, D), _rows(tr, D), _rows(tr, 1), _vec(D)],
        out_specs=[_rows(tr, D), _vec(D)],
        out_shape=[jax.ShapeDtypeStruct((T, D), F32), jax.ShapeDtypeStruct((1, D), F32)],
        compiler_params=_cparams(("arbitrary",)),
    )(dx1, dh1, x, r1, g1)


def _hgrn_consts():
    C = CHUNK
    nl = len(LEVELS) + 1
    w = np.zeros((nl, C, C), np.float32)
    w[0] = np.tril(np.ones((C, C), np.float32))
    for li, m in enumerate(LEVELS, start=1):
        for r in range(C):
            mid = (r // (2 * m)) * 2 * m + m
            if r >= mid:
                w[li, r, mid:r + 1] = 1.0
            else:
                w[li, r, r + 1:mid] = 1.0
    w_all = w.reshape(nl * C, C)
    w2 = np.concatenate([w_all, w_all], axis=1)
    w2t = np.concatenate([w_all.T, w_all.T], axis=1)
    R = GROUP * C
    t = np.arange(R)[:, None]
    s = np.arange(R)[None, :]
    masks = np.zeros((nl, R, R), np.float32)
    masks[0] = (t == s)
    for li, m in enumerate(LEVELS, start=1):
        masks[li] = ((t ^ s) < 2 * m)
    return jnp.asarray(w2, BF16), jnp.asarray(w2t, BF16), jnp.asarray(masks, F32)


def _hgrn_gates(z, lg_ref):
    l0 = lg_ref[0:1, :]
    l1 = lg_ref[1:2, :]
    mx = jnp.maximum(l0, l1)
    e0 = jnp.exp(l0 - mx)
    e1 = jnp.exp(l1 - mx)
    lb = e0 / (e0 + e1)
    om = 1.0 - lb
    sg = _sigmoid(z)
    sgm = _sigmoid(-z)
    f = lb + om * sg
    return lb, om, sg, sgm, f, jnp.log(f), om * sgm


def _hgrn_levels(q, kk, lf, w2_ref):
    C = CHUNK
    nl = len(LEVELS) + 1
    lf_hi, lf_lo = _split2(lf)
    per_chunk = []
    for c in range(GROUP):
        rhs = jnp.concatenate([lf_hi[c * C:(c + 1) * C], lf_lo[c * C:(c + 1) * C]], axis=0)
        per_chunk.append(_nn(w2_ref[...], rhs))
    args = [jnp.concatenate([per_chunk[c][l * C:(l + 1) * C] for c in range(GROUP)], axis=0) for l in range(nl)]
    exps = [jnp.exp(a) for a in args]
    row = lax.broadcasted_iota(jnp.int32, q.shape, 0)
    qf, kf, mts = [q], [kk], [None]
    for li, m in enumerate(LEVELS, start=1):
        mt = jnp.where((row & m) != 0, 1.0, 0.0).astype(F32)
        mts.append(mt)
        qf.append(q * exps[li] * mt)
        kf.append(kk * exps[li] * (1.0 - mt))
    return args, exps, mts, qf, kf


def _hgrn_scores(qf, kf, masks_ref):
    p = None
    for l in range(len(qf)):
        pl_ = _nt(qf[l].astype(BF16), kf[l].astype(BF16)) * masks_ref[l]
        p = pl_ if p is None else p + pl_
    return p


def _hgrn_fwd(main, lb_logits, norm_w, dm):
    T, H, C = dm.T, dm.H, CHUNK
    R = GROUP * C
    nj = T // R
    w2, _, masks = _hgrn_consts()

    def body(q_ref, z_ref, v_ref, g_ref, lg_ref, nw_ref, w2_ref, masks_ref, ya_ref, o_ref, sp_ref, st_ref):
        j = pl.program_id(1)

        @pl.when(j == 0)
        def _():
            st_ref[...] = jnp.zeros_like(st_ref)

        q = q_ref[...]
        v = v_ref[...]
        vb = v.astype(BF16)
        _, _, _, _, _, lf, kk = _hgrn_gates(z_ref[...], lg_ref)
        args, exps, _, qf, kf = _hgrn_levels(q, kk, lf, w2_ref)
        p = _hgrn_scores(qf, kf, masks_ref)
        o_intra = _nn(p.astype(BF16), vb)
        b, eb = args[0], exps[0]
        o_inter = []
        for c in range(GROUP):
            sl = slice(c * C, (c + 1) * C)
            st = st_ref[...]
            sp_ref[0, c] = st
            blast = b[c * C + C - 1:c * C + C, :]
            o_inter.append(_nt((q[sl] * eb[sl]).astype(BF16), st.astype(BF16)))
            kd = (kk[sl] * jnp.exp(blast - b[sl])).astype(BF16)
            st_ref[...] = st * jnp.exp(blast) + _tn(vb[sl], kd)
        o = o_intra + jnp.concatenate(o_inter, axis=0)
        o_ref[...] = o
        gv = g_ref[...]
        ya_ref[...] = (o * _rstd(o) * nw_ref[...] * (gv * _sigmoid(gv))).astype(BF16)

    def col(sec):
        return pl.BlockSpec((R, HEAD), lambda h, j: (j, sec * H + h))

    return pl.pallas_call(
        body, name="hgrn_fwd", grid=(H, nj),
        in_specs=[col(0), col(1), col(2), col(3),
                  pl.BlockSpec((2, HEAD), lambda h, j: (0, h)),
                  pl.BlockSpec((1, HEAD), lambda h, j: (0, 0)),
                  pl.BlockSpec(w2.shape, lambda h, j: (0, 0)),
                  pl.BlockSpec(masks.shape, lambda h, j: (0, 0, 0))],
        out_specs=[pl.BlockSpec((R, HEAD), lambda h, j: (j, h)),
                   pl.BlockSpec((R, HEAD), lambda h, j: (j, h)),
                   pl.BlockSpec((1, GROUP, HEAD, HEAD), lambda h, j: (h, j, 0, 0))],
        out_shape=[jax.ShapeDtypeStruct((T, dm.AW), BF16), jax.ShapeDtypeStruct((T, dm.AW), F32),
                   jax.ShapeDtypeStruct((H, T // C, HEAD, HEAD), F32)],
        scratch_shapes=[pltpu.VMEM((HEAD, HEAD), F32)],
        compiler_params=_cparams(("parallel", "arbitrary")),
    )(main, main, main, main, lb_logits, norm_w, w2, masks)


def _hgrn_bwd(main, lb_logits, norm_w, o_saved, states, dya, dm):
    T, H, C = dm.T, dm.H, CHUNK
    R = GROUP * C
    nj = T // R
    nl = len(LEVELS) + 1
    w2, w2t, masks = _hgrn_consts()

    def body(q_ref, z_ref, v_ref, g_ref, lg_ref, nw_ref, w2_ref, w2t_ref, masks_ref, o_ref, sp_ref, dya_ref,
             d_ref, dlb_ref, dnw_ref, ds_ref, stash):
        j = pl.program_id(1)
        sec = pl.program_id(2)

        @pl.when((j == 0) & (sec == 0))
        def _():
            ds_ref[...] = jnp.zeros_like(ds_ref)
            dlb_ref[...] = jnp.zeros_like(dlb_ref)
            dnw_ref[...] = jnp.zeros_like(dnw_ref)

        @pl.when(sec == 0)
        def _():
            q = q_ref[...]
            v = v_ref[...]
            gv = g_ref[...]
            vb = v.astype(BF16)
            lb, om, sg, sgm, f, lf, kk = _hgrn_gates(z_ref[...], lg_ref)
            args, exps, mts, qf, kf = _hgrn_levels(q, kk, lf, w2_ref)
            qb = [t.astype(BF16) for t in qf]
            kb = [t.astype(BF16) for t in kf]
            p = _hgrn_scores(qf, kf, masks_ref)
            o = o_ref[...]
            nw = nw_ref[...]
            r = _rstd(o)
            n = o * r
            sgg = _sigmoid(gv)
            dya_v = dya_ref[...]
            d_on = dya_v * (gv * sgg)
            dg = dya_v * (n * nw) * (sgg * (1.0 + gv * (1.0 - sgg)))
            dnw_ref[0] += _colsum(d_on * n)
            do = _rms_bwd(d_on * nw, n, r)
            dob = do.astype(BF16)
            dp = _nt(dob, vb)
            dv = _tn(p.astype(BF16), dob)
            dq = jnp.zeros_like(q)
            dkk = jnp.zeros_like(q)
            dargs = [None] * nl
            for l in range(nl):
                dpl = (dp * masks_ref[l]).astype(BF16)
                dql = _nn(dpl, kb[l])
                dkl = _tn(dpl, qb[l])
                if l == 0:
                    dq += dql
                    dkk += dkl
                else:
                    dq += dql * exps[l] * mts[l]
                    dkk += dkl * exps[l] * (1.0 - mts[l])
                    dargs[l] = dql * qf[l] + dkl * kf[l]
            b, eb = args[0], exps[0]
            row = lax.broadcasted_iota(jnp.int32, (C, HEAD), 0)
            dq_i, dkk_i, dv_i, db_i = [None] * GROUP, [None] * GROUP, [None] * GROUP, [None] * GROUP
            for c in reversed(range(GROUP)):
                sl = slice(c * C, (c + 1) * C)
                st = sp_ref[0, c]
                ds = ds_ref[...]
                dsb = ds.astype(BF16)
                blast = b[c * C + C - 1:c * C + C, :]
                ebl = jnp.exp(blast)
                el = jnp.exp(blast - b[sl])
                qe = q[sl] * eb[sl]
                kd = kk[sl] * el
                dqe = _nn(dob[sl], st.astype(BF16))
                dkd = _nn(vb[sl], dsb)
                t = dkd * kd
                dblast = _colsum(t) + _colsum(ds * st) * ebl
                dq_i[c] = dqe * eb[sl]
                dkk_i[c] = dkd * el
                dv_i[c] = _nt(kd.astype(BF16), dsb)
                db_i[c] = dqe * qe - t + jnp.where(row == C - 1, dblast, 0.0)
                ds_ref[...] = ds * ebl + _tn(dob[sl], qe.astype(BF16))
            dq = dq + jnp.concatenate(dq_i, axis=0)
            dkk = dkk + jnp.concatenate(dkk_i, axis=0)
            dv = dv + jnp.concatenate(dv_i, axis=0)
            dlf_c = []
            for c in range(GROUP):
                sl = slice(c * C, (c + 1) * C)
                stack = jnp.concatenate([db_i[c]] + [dargs[l][sl] for l in range(1, nl)], axis=0)
                hi, lo = _split2(stack)
                dlf_c.append(_nn(w2t_ref[...], jnp.concatenate([hi, lo], axis=0)))
            dlf = jnp.concatenate(dlf_c, axis=0)
            dz = dlf * (om * sg * (1.0 - sg) / f) - dkk * (om * sgm * (1.0 - sgm))
            dlb_ref[...] += _colsum(dlf * (1.0 - sg) / f - dkk * sgm)
            stash[0] = dq.astype(BF16)
            stash[1] = dz.astype(BF16)
            stash[2] = dv.astype(BF16)
            stash[3] = dg.astype(BF16)

        d_ref[...] = stash[sec]

    def col(sec):
        return pl.BlockSpec((R, HEAD), lambda h, j, s: (nj - 1 - j, sec * H + h))

    return pl.pallas_call(
        body, name="hgrn_bwd", grid=(H, nj, 4),
        in_specs=[col(0), col(1), col(2), col(3),
                  pl.BlockSpec((2, HEAD), lambda h, j, s: (0, h)),
                  pl.BlockSpec((1, HEAD), lambda h, j, s: (0, 0)),
                  pl.BlockSpec(w2.shape, lambda h, j, s: (0, 0)),
                  pl.BlockSpec(w2t.shape, lambda h, j, s: (0, 0)),
                  pl.BlockSpec(masks.shape, lambda h, j, s: (0, 0, 0)),
                  pl.BlockSpec((R, HEAD), lambda h, j, s: (nj - 1 - j, h)),
                  pl.BlockSpec((1, GROUP, HEAD, HEAD), lambda h, j, s: (h, nj - 1 - j, 0, 0)),
                  pl.BlockSpec((R, HEAD), lambda h, j, s: (nj - 1 - j, h))],
        out_specs=[pl.BlockSpec((R, HEAD), lambda h, j, s: (nj - 1 - j, s * H + h)),
                   pl.BlockSpec((1, HEAD), lambda h, j, s: (0, h)),
                   pl.BlockSpec((1, 1, HEAD), lambda h, j, s: (h, 0, 0))],
        out_shape=[jax.ShapeDtypeStruct((T, 4 * dm.AW), BF16), jax.ShapeDtypeStruct((1, dm.AW), F32),
                   jax.ShapeDtypeStruct((H, 1, HEAD), F32)],
        scratch_shapes=[pltpu.VMEM((HEAD, HEAD), F32), pltpu.VMEM((4, R, HEAD), BF16)],
        compiler_params=_cparams(("parallel", "arbitrary", "arbitrary")),
    )(main, main, main, main, lb_logits, norm_w, w2, w2t, masks, o_saved, states, dya)


def _log_sigmoid(x):
    return jnp.minimum(x, 0.0) - jnp.log(1.0 + jnp.exp(-jnp.abs(x)))


def _tri(n):
    return jnp.asarray(np.tril(np.ones((n, n), np.float32)), BF16)


def _cum_fwd(rest, bias, dm, tb=256):
    T = dm.T
    tb = min(tb, T)
    cb = 2 * dm.D // HEAD
    tri = _tri(tb)

    def body(x_ref, b_ref, tri_ref, o_ref, carry):
        i = pl.program_id(0)

        @pl.when(i == 0)
        def _():
            carry[...] = jnp.zeros_like(carry)

        lf = _log_sigmoid(x_ref[...] + b_ref[...])
        hi, mid, lo = _split3(lf)
        tr_ = tri_ref[...]
        c = _nn(tr_, hi) + _nn(tr_, mid) + _nn(tr_, lo) + carry[...]
        o_ref[...] = c
        carry[...] = c[tb - 1:tb, :]

    return pl.pallas_call(
        body, name="cum_fwd", grid=(T // tb,),
        in_specs=[_rows(tb, HEAD, cb), _vec(HEAD), pl.BlockSpec((tb, tb), lambda i: (0, 0))],
        out_specs=_rows(tb, HEAD),
        out_shape=jax.ShapeDtypeStruct((T, HEAD), F32),
        scratch_shapes=[pltpu.VMEM((1, HEAD), F32)],
        compiler_params=_cparams(("arbitrary",)),
    )(rest, bias, tri)


def _cum_bwd(dcum, rest, bias, dm, tb=256):
    T = dm.T
    tb = min(tb, T)
    nb = T // tb
    cb = 2 * dm.D // HEAD
    tri = _tri(tb)

    def body(d_ref, x_ref, b_ref, tri_ref, o_ref, db_ref, carry):
        i = pl.program_id(0)

        @pl.when(i == 0)
        def _():
            carry[...] = jnp.zeros_like(carry)
            db_ref[...] = jnp.zeros_like(db_ref)

        hi, mid, lo = _split3(d_ref[...])
        tr_ = tri_ref[...]
        dlf = _tn(tr_, hi) + _tn(tr_, mid) + _tn(tr_, lo) + carry[...]
        carry[...] = dlf[0:1, :]
        dx = dlf * _sigmoid(-(x_ref[...] + b_ref[...]))
        o_ref[...] = dx.astype(BF16)
        db_ref[...] += _colsum(dx)

    return pl.pallas_call(
        body, name="cum_bwd", grid=(nb,),
        in_specs=[pl.BlockSpec((tb, HEAD), lambda i: (nb - 1 - i, 0)),
                  pl.BlockSpec((tb, HEAD), lambda i: (nb - 1 - i, cb)),
                  _vec(HEAD), pl.BlockSpec((tb, tb), lambda i: (0, 0))],
        out_specs=[pl.BlockSpec((tb, HEAD), lambda i: (nb - 1 - i, 0)), _vec(HEAD)],
        out_shape=[jax.ShapeDtypeStruct((T, HEAD), BF16), jax.ShapeDtypeStruct((1, HEAD), F32)],
        scratch_shapes=[pltpu.VMEM((1, HEAD), F32)],
        compiler_params=_cparams(("arbitrary",)),
    )(dcum, rest, bias, tri)


def _fox_scores(q_ref, k_ref, ck_ref, qi, kj, tq):
    qs = (q_ref[...] * (HEAD ** -0.5)).astype(BF16)
    s = _nt(qs, k_ref[...].astype(BF16)) - ck_ref[0]
    row = qi * tq + lax.broadcasted_iota(jnp.int32, s.shape, 0)
    colp = kj * tq + lax.broadcasted_iota(jnp.int32, s.shape, 1)
    return jnp.where(row >= colp, s, NEG), qs


def _fox_fwd(main, cum_row, dm, tq=512):
    T, H = dm.T, dm.H
    tq = min(tq, T)
    nq = T // tq

    def body(q_ref, k_ref, v_ref, ck_ref, o_ref, lse_ref, m_ref, l_ref, acc_ref):
        i = pl.program_id(1)
        j = pl.program_id(2)

        @pl.when(j == 0)
        def _():
            m_ref[...] = jnp.full_like(m_ref, NEG)
            l_ref[...] = jnp.zeros_like(l_ref)
            acc_ref[...] = jnp.zeros_like(acc_ref)

        @pl.when(j <= i)
        def _():
            s, _ = _fox_scores(q_ref, k_ref, ck_ref, i, j, tq)
            m_old = m_ref[...]
            m_new = jnp.maximum(m_old, jnp.max(s, axis=-1, keepdims=True))
            a = jnp.exp(m_old - m_new)
            p = jnp.exp(s - m_new)
            l_ref[...] = a * l_ref[...] + jnp.sum(p, axis=-1, keepdims=True)
            acc_ref[...] = a * acc_ref[...] + _nn(p.astype(BF16), v_ref[...].astype(BF16))
            m_ref[...] = m_new

        @pl.when(j == nq - 1)
        def _():
            l = l_ref[...]
            o_ref[...] = acc_ref[...] / l
            lse_ref[0] = m_ref[...] + jnp.log(l)

    return pl.pallas_call(
        body, name="fox_fwd", grid=(H, nq, nq),
        in_specs=[pl.BlockSpec((tq, HEAD), lambda h, i, j: (i, 4 * H + h)),
                  pl.BlockSpec((tq, HEAD), lambda h, i, j: (jnp.minimum(i, j), 5 * H + h)),
                  pl.BlockSpec((tq, HEAD), lambda h, i, j: (jnp.minimum(i, j), 6 * H + h)),
                  pl.BlockSpec((1, 1, tq), lambda h, i, j: (h, 0, jnp.minimum(i, j)))],
        out_specs=[pl.BlockSpec((tq, HEAD), lambda h, i, j: (i, h)),
                   pl.BlockSpec((1, tq, 1), lambda h, i, j: (h, i, 0))],
        out_shape=[jax.ShapeDtypeStruct((T, dm.AW), F32), jax.ShapeDtypeStruct((H, T, 1), F32)],
        scratch_shapes=[pltpu.VMEM((tq, 1), F32), pltpu.VMEM((tq, 1), F32), pltpu.VMEM((tq, HEAD), F32)],
        compiler_params=_cparams(("parallel", "parallel", "arbitrary")),
    )(main, main, main, cum_row)


def _fox_bwd(main, cum_row, o_saved, lse, do, dm, tq=512):
    T, H = dm.T, dm.H
    tq = min(tq, T)
    nq = T // tq

    def body(q_ref, k_ref, v_ref, ck_ref, o_ref, lse_ref, do_ref, dq_ref, dk_ref, dv_ref, dc_ref, dr_ref,
             dk_acc, dv_acc, dc_acc):
        kj = pl.program_id(1)
        i = pl.program_id(2)

        @pl.when((kj == 0) & (i == 0))
        def _():
            dq_ref[...] = jnp.zeros_like(dq_ref)
            dr_ref[...] = jnp.zeros_like(dr_ref)

        @pl.when(i == 0)
        def _():
            dk_acc[...] = jnp.zeros_like(dk_acc)
            dv_acc[...] = jnp.zeros_like(dv_acc)
            dc_acc[...] = jnp.zeros_like(dc_acc)

        @pl.when(i >= kj)
        def _():
            s, qs = _fox_scores(q_ref, k_ref, ck_ref, i, kj, tq)
            kb = k_ref[...].astype(BF16)
            dov = do_ref[...]
            dob = dov.astype(BF16)
            p = jnp.exp(s - lse_ref[0])
            dp = _nt(dob, v_ref[...].astype(BF16))
            delta = jnp.sum(dov * o_ref[...], axis=-1, keepdims=True)
            ds = p * (dp - delta)
            dsb = ds.astype(BF16)
            dv_acc[...] += _tn(p.astype(BF16), dob)
            dk_acc[...] += _tn(dsb, qs)
            rows = pl.ds(pl.multiple_of(i * tq, tq), tq)
            dq_ref[rows, :] += _nn(dsb, kb) * (HEAD ** -0.5)
            dr_ref[0, rows, :] += jnp.sum(ds, axis=-1, keepdims=True)
            dc_acc[...] -= _colsum(ds)

        @pl.when(i == nq - 1)
        def _():
            dk_ref[...] = dk_acc[...].astype(BF16)
            dv_ref[...] = dv_acc[...].astype(BF16)
            dc_ref[0] = dc_acc[...]

    qrow = lambda h, kj, i: jnp.maximum(i, kj)
    return pl.pallas_call(
        body, name="fox_bwd", grid=(H, nq, nq),
        in_specs=[pl.BlockSpec((tq, HEAD), lambda h, kj, i: (qrow(h, kj, i), 4 * H + h)),
                  pl.BlockSpec((tq, HEAD), lambda h, kj, i: (kj, 5 * H + h)),
                  pl.BlockSpec((tq, HEAD), lambda h, kj, i: (kj, 6 * H + h)),
                  pl.BlockSpec((1, 1, tq), lambda h, kj, i: (h, 0, kj)),
                  pl.BlockSpec((tq, HEAD), lambda h, kj, i: (qrow(h, kj, i), h)),
                  pl.BlockSpec((1, tq, 1), lambda h, kj, i: (h, qrow(h, kj, i), 0)),
                  pl.BlockSpec((tq, HEAD), lambda h, kj, i: (qrow(h, kj, i), h))],
        out_specs=[pl.BlockSpec((T, HEAD), lambda h, kj, i: (0, h)),
                   pl.BlockSpec((tq, HEAD), lambda h, kj, i: (kj, h)),
                   pl.BlockSpec((tq, HEAD), lambda h, kj, i: (kj, h)),
                   pl.BlockSpec((1, 1, tq), lambda h, kj, i: (h, 0, kj)),
                   pl.BlockSpec((1, T, 1), lambda h, kj, i: (h, 0, 0))],
        out_shape=[jax.ShapeDtypeStruct((T, dm.AW), F32), jax.ShapeDtypeStruct((T, dm.AW), BF16),
                   jax.ShapeDtypeStruct((T, dm.AW), BF16), jax.ShapeDtypeStruct((H, 1, T), F32),
                   jax.ShapeDtypeStruct((H, T, 1), F32)],
        scratch_shapes=[pltpu.VMEM((tq, HEAD), F32), pltpu.VMEM((tq, HEAD), F32), pltpu.VMEM((1, tq), F32)],
        compiler_params=_cparams(("parallel", "arbitrary", "arbitrary")),
    )(main, main, main, cum_row, o_saved, lse, do)


def _local_step(dm, x, tgt, w_main, w_rest, w_up_a, w_up_b, w_o, w_ffn_in, w_ffn_down,
                bias_p, lb_logits, norm_w, g1, g2, g3, g4):
    T, D, H = dm.T, dm.D, dm.H
    h1, r1 = _rms_fwd(x, g1)
    main = _mm(h1, w_main, "nn", F32, "proj_main")
    rest = _mm(h1, w_rest, "nn", F32, "proj_rest", tn=1408)
    ya, o_a, states = _hgrn_fwd(main, lb_logits, norm_w, dm)
    cum = _cum_fwd(rest, bias_p, dm)
    cum_row = cum[:, :H].T.reshape(H, 1, T)
    o_b, lse = _fox_fwd(main, cum_row, dm)
    y_a = _mm(ya, w_up_a, "nn", F32, "up_a")
    y_b = _mm(o_b, w_up_b, "nn", F32, "up_b")
    merged = _merge_fwd(rest, y_a, y_b)
    u = _mm(merged, w_o, "nn", F32, "w_o")
    x1, r2, h3, r3 = _post_pre(x, u, g2, g3)
    gu = _mm(h3, w_ffn_in, "nn", F32, "ffn_in")
    act = _swiglu_fwd(gu)
    w = _mm(act, w_ffn_down, "nn", F32, "ffn_down")
    loss, dy, dw, dg4 = _loss_bwd(x1, w, g4, tgt)
    dact = _mm(dw, w_ffn_down, "nt", BF16, "d_act")
    d_ffn_down = _mm(act, dw, "tn", F32, "dw_ffn_down")
    dgu = _swiglu_bwd(gu, dact)
    dh3 = _mm(dgu, w_ffn_in, "nt", F32, "d_h3")
    d_ffn_in = _mm(h3, dgu, "tn", F32, "dw_ffn_in")
    dx1, du, dg3, dg2 = _rms_bwd2(dy, dh3, x1, r3, g3, u, r2, g2)
    dmerged = _mm(du, w_o, "nt", F32, "d_merged")
    d_w_o = _mm(merged, du, "tn", F32, "dw_o")
    dy_a, dy_b, dgates = _merge_bwd(dmerged, rest, y_a, y_b)
    dya = _mm(dy_a, w_up_a, "nt", F32, "d_ya")
    d_up_a = _mm(ya, dy_a, "tn", F32, "dw_up_a")
    dob = _mm(dy_b, w_up_b, "nt", F32, "d_ob")
    d_up_b = _mm(o_b, dy_b, "tn", F32, "dw_up_b")
    d_a, dlb, dnw_h = _hgrn_bwd(main, lb_logits, norm_w, o_a, states, dya, dm)
    dq_b, dk_b, dv_b, dcum_row, dcum_col = _fox_bwd(main, cum_row, o_b, lse, dob, dm)
    dcum = jnp.pad((dcum_row.reshape(H, T) + dcum_col.reshape(H, T)).T, ((0, 0), (0, HEAD - H)))
    dbf, dbias = _cum_bwd(dcum, rest, bias_p, dm)
    dmain = jnp.concatenate([d_a, dq_b.astype(BF16), dk_b, dv_b], axis=1)
    drest = jnp.concatenate([dgates, dbf], axis=1)
    dh1 = _mm(dmain, w_main, "nt", F32, "d_h1_main")
    dh1 = _mm(drest, w_rest, "nt", F32, "d_h1_rest", add=dh1, tk=1408)
    d_main = _mm(h1, dmain, "tn", F32, "dw_main")
    d_rest = _mm(h1, drest, "tn", F32, "dw_rest", tn=1408)
    dx, dg1 = _rms_bwd1(dx1, dh1, x, r1, g1)
    big = dict(main=d_main, rest=d_rest, up_a=d_up_a, up_b=d_up_b, w_o=d_w_o, ffn_in=d_ffn_in, ffn_down=d_ffn_down)
    small = dict(loss=loss, bias=dbias, norm_w=jnp.sum(dnw_h, axis=0), lb=dlb, g1=dg1, g2=dg2, g3=dg3, g4=dg4)
    return dx, big, small


HBM = pl.BlockSpec(memory_space=pltpu.HBM)


def _place():
    x, y, c = lax.axis_index("x"), lax.axis_index("y"), lax.axis_index("c")
    chips = [(1 - x, y), (x, 1 - y), (1 - x, 1 - y)]
    return x, y, c, chips


def _gather_weights(wflat):
    _, nr, fw = wflat.shape

    def body(w_ref, g_ref, send_sems, recv_sems, local_sem):
        x, y, c, chips = _place()
        sibling = (x, y, 1 - c)
        own = pltpu.make_async_copy(w_ref, g_ref.at[2 * x + y], local_sem)
        own.start()

        def copy(k, chip, half, to, src=None):
            blk = g_ref.at[2 * chip[0] + chip[1], half]
            return pltpu.make_async_remote_copy(
                src_ref=blk if src is None else src, dst_ref=blk,
                send_sem=send_sems.at[k], recv_sem=recv_sems.at[k], device_id=to, device_id_type=MESH)

        first = [copy(j, (x, y), c, (*chip, c), src=w_ref.at[c]) for j, chip in enumerate(chips)]
        for cp in first:
            cp.start()
        passed = [copy(3 + j, chip, c, sibling) for j, chip in enumerate(chips)]
        for j, chip in enumerate(chips):
            copy(j, chip, c, sibling).wait_recv()
            passed[j].start()
        for j, chip in enumerate(chips):
            copy(3 + j, chip, 1 - c, sibling).wait_recv()
        for cp in first + passed:
            cp.wait_send()
        own.wait()

    return pl.pallas_call(
        body, name="gather_weights",
        in_specs=[HBM], out_specs=HBM,
        out_shape=jax.ShapeDtypeStruct((4, 2, nr, fw), wflat.dtype),
        scratch_shapes=[pltpu.SemaphoreType.DMA((6,)), pltpu.SemaphoreType.DMA((6,)), pltpu.SemaphoreType.DMA],
    )(wflat)


def _swap_halves(g):
    n, _, nr, fw = g.shape

    def body(g_ref, o_ref, send_sems, recv_sems):
        x, y, c, _ = _place()
        cps = [pltpu.make_async_remote_copy(
            src_ref=g_ref.at[s, 1 - c], dst_ref=o_ref.at[s], send_sem=send_sems.at[s], recv_sem=recv_sems.at[s],
            device_id=(x, y, 1 - c), device_id_type=MESH) for s in range(n)]
        for cp in cps:
            cp.start()
        for cp in cps:
            cp.wait_recv()
        for cp in cps:
            cp.wait_send()

    return pl.pallas_call(
        body, name="swap_halves",
        in_specs=[HBM], out_specs=HBM,
        out_shape=jax.ShapeDtypeStruct((n, nr, fw), g.dtype),
        scratch_shapes=[pltpu.SemaphoreType.DMA((n,)), pltpu.SemaphoreType.DMA((n,))],
    )(g)


def _add_sibling(g, got, c_idx, tr=512):
    n, _, nr, fw = g.shape

    def body(c_ref, g_ref, r_ref, of_ref, ob_ref):
        s = g_ref[0, 0] + r_ref[0]
        of_ref[0] = s
        ob_ref[0] = s.astype(BF16)

    blk = pl.BlockSpec((1, tr, fw), lambda s, i, c_ref: (s, i, 0))
    return pl.pallas_call(
        body, name="add_sibling",
        grid_spec=pltpu.PrefetchScalarGridSpec(
            num_scalar_prefetch=1, grid=(n, nr // tr),
            in_specs=[pl.BlockSpec((1, 1, tr, fw), lambda s, i, c_ref: (s, c_ref[0], i, 0)), blk],
            out_specs=[blk, blk]),
        out_shape=[jax.ShapeDtypeStruct((n, nr, fw), F32), jax.ShapeDtypeStruct((n, nr, fw), BF16)],
        compiler_params=_cparams(("parallel", "parallel")),
    )(c_idx, g, got)


def _scatter_chips(a):
    _, nr, fw = a.shape

    def body(a_ref, o_ref, send_sems, recv_sems):
        x, y, c, chips = _place()
        cps = [pltpu.make_async_remote_copy(
            src_ref=a_ref.at[2 * chip[0] + chip[1]], dst_ref=o_ref.at[j],
            send_sem=send_sems.at[j], recv_sem=recv_sems.at[j],
            device_id=(*chip, c), device_id_type=MESH) for j, chip in enumerate(chips)]
        for cp in cps:
            cp.start()
        for cp in cps:
            cp.wait_recv()
        for cp in cps:
            cp.wait_send()

    return pl.pallas_call(
        body, name="scatter_chips",
        in_specs=[HBM], out_specs=HBM,
        out_shape=jax.ShapeDtypeStruct((3, nr, fw), a.dtype),
        scratch_shapes=[pltpu.SemaphoreType.DMA((3,)), pltpu.SemaphoreType.DMA((3,))],
    )(a)


def _add_chips(a, got, q_idx, tr=512):
    _, nr, fw = a.shape

    def body(q_ref, a_ref, r_ref, o_ref):
        o_ref[...] = ((a_ref[0] + r_ref[0].astype(F32)) + r_ref[1].astype(F32)) + r_ref[2].astype(F32)

    return pl.pallas_call(
        body, name="add_chips",
        grid_spec=pltpu.PrefetchScalarGridSpec(
            num_scalar_prefetch=1, grid=(nr // tr,),
            in_specs=[pl.BlockSpec((1, tr, fw), lambda i, q_ref: (q_ref[0], i, 0)),
                      pl.BlockSpec((3, tr, fw), lambda i, q_ref: (0, i, 0))],
            out_specs=pl.BlockSpec((tr, fw), lambda i, q_ref: (i, 0))),
        out_shape=jax.ShapeDtypeStruct((nr, fw), F32),
        compiler_params=_cparams(("parallel",)),
    )(q_idx, a, got)


def _share_halves(r):
    nr, fw = r.shape

    def body(r_ref, o_ref, send_sem, recv_sem, local_sem):
        x, y, c, _ = _place()
        own = pltpu.make_async_copy(r_ref, o_ref.at[c], local_sem)
        own.start()
        cp = pltpu.make_async_remote_copy(
            src_ref=r_ref, dst_ref=o_ref.at[c], send_sem=send_sem, recv_sem=recv_sem,
            device_id=(x, y, 1 - c), device_id_type=MESH)
        cp.start()
        pltpu.make_async_remote_copy(
            src_ref=r_ref, dst_ref=o_ref.at[1 - c], send_sem=send_sem, recv_sem=recv_sem,
            device_id=(x, y, 1 - c), device_id_type=MESH).wait_recv()
        cp.wait_send()
        own.wait()

    return pl.pallas_call(
        body, name="share_halves",
        in_specs=[HBM], out_specs=HBM,
        out_shape=jax.ShapeDtypeStruct((2, nr, fw), r.dtype),
        scratch_shapes=[pltpu.SemaphoreType.DMA, pltpu.SemaphoreType.DMA, pltpu.SemaphoreType.DMA],
    )(r)


def _sum_small(vec):
    rows, w = vec.shape

    def body(v_ref, o_ref, buf, send_sems, recv_sems):
        x, y, c, _ = _place()
        me = 4 * x + 2 * y + c
        buf[me] = v_ref[...]
        cps = []
        for k in range(1, 8):
            to = (x ^ (k >> 2), y ^ ((k >> 1) & 1), c ^ (k & 1))
            cps.append(pltpu.make_async_remote_copy(
                src_ref=v_ref, dst_ref=buf.at[me], send_sem=send_sems.at[k - 1], recv_sem=recv_sems.at[k - 1],
                device_id=to, device_id_type=MESH))
        for cp in cps:
            cp.start()
        for k in range(1, 8):
            pltpu.make_async_remote_copy(
                src_ref=v_ref, dst_ref=buf.at[me ^ k], send_sem=send_sems.at[k - 1], recv_sem=recv_sems.at[k - 1],
                device_id=(x, y, c), device_id_type=MESH).wait_recv()
        for cp in cps:
            cp.wait_send()
        total = buf[0]
        for d in range(1, 8):
            total = total + buf[d]
        o_ref[...] = total

    return pl.pallas_call(
        body, name="sum_small",
        in_specs=[pl.BlockSpec(memory_space=pltpu.VMEM)], out_specs=pl.BlockSpec(memory_space=pltpu.VMEM),
        out_shape=jax.ShapeDtypeStruct((rows, w), F32),
        scratch_shapes=[pltpu.VMEM((8, rows, w), F32), pltpu.SemaphoreType.DMA((7,)), pltpu.SemaphoreType.DMA((7,))],
    )(vec)


def _adam_math(w, g, m, v):
    m = ADAM_B1 * m + (1.0 - ADAM_B1) * g
    v = ADAM_B2 * v + (1.0 - ADAM_B2) * (g * g)
    m_hat = m / (1.0 - ADAM_B1 ** ADAM_STEP)
    v_hat = v / (1.0 - ADAM_B2 ** ADAM_STEP)
    delta = -ADAM_LR * (m_hat / (jnp.sqrt(v_hat) + ADAM_EPS) + ADAM_WD * w)
    return delta, m, v


def _adamw(w, g, m, v, name, tr=128):
    R, Cn = w.shape
    tr = min(tr, R)
    assert R % tr == 0

    def body(w_ref, g_ref, m_ref, v_ref, d_ref, mo_ref, vo_ref):
        d, mn, vn = _adam_math(w_ref[...], g_ref[...], m_ref[...], v_ref[...])
        d_ref[...] = d
        mo_ref[...] = mn
        vo_ref[...] = vn

    blk = pl.BlockSpec((tr, Cn), lambda i: (i, 0))
    sds = jax.ShapeDtypeStruct((R, Cn), F32)
    return pl.pallas_call(
        body, name=name, grid=(R // tr,),
        in_specs=[blk] * 4, out_specs=[blk] * 3, out_shape=[sds] * 3,
        compiler_params=_cparams(("parallel",)),
    )(w, g, m, v)


ROW_LOSS, ROW_BIAS, ROW_NORM, ROW_LB0, ROW_G1, ROW_G2, ROW_G3, ROW_G4, ROW_LB1 = range(9)
SMALL_ROWS = 16


def _small_update(gsum, wp, mp, vp):
    _, w = gsum.shape

    def body(g_ref, w_ref, m_ref, v_ref, go_ref, d_ref, mo_ref, vo_ref):
        wv = w_ref[...]
        l0 = wv[ROW_LB0:ROW_LB0 + 1, :]
        l1 = wv[ROW_LB1:ROW_LB1 + 1, :]
        mx = jnp.maximum(l0, l1)
        e0 = jnp.exp(l0 - mx)
        e1 = jnp.exp(l1 - mx)
        p0 = e0 / (e0 + e1)
        gs = g_ref[...]
        dl0 = gs[ROW_LB0:ROW_LB0 + 1, :] * p0 * (1.0 - p0)
        row8 = lax.broadcasted_iota(jnp.int32, gs.shape, 0)
        top = jnp.where(row8 == ROW_LB0, dl0, jnp.where(row8 == ROW_LOSS, 0.0, gs))
        bot = jnp.where(row8 == ROW_LB1 - 8, -dl0, 0.0)
        g16 = jnp.concatenate([top, bot], axis=0)
        d, mn, vn = _adam_math(wv, g16, m_ref[...], v_ref[...])
        go_ref[...] = g16
        d_ref[...] = d
        mo_ref[...] = mn
        vo_ref[...] = vn

    sds = jax.ShapeDtypeStruct((SMALL_ROWS, w), F32)
    full = pl.BlockSpec(memory_space=pltpu.VMEM)
    return pl.pallas_call(
        body, name="small_update", in_specs=[full] * 4, out_specs=[full] * 4, out_shape=[sds] * 4,
    )(gsum, wp, mp, vp)


def _shard_shapes(dm):
    D, AW, FF = dm.D, dm.AW, dm.FF
    return [("w_in", (D, dm.NIN // 4)), ("w_up_a", (AW, D // 4)), ("w_up_b", (AW, D // 4)),
            ("w_o", (D // 4, D)), ("w_ffn_in", (D, 2 * FF // 4)), ("w_ffn_down", (FF // 4, D))]


def _flat_rows(dm):
    n = sum(r * c for _, (r, c) in _shard_shapes(dm))
    return FLAT_ROWS * (-(-n // (2 * FLAT_ROWS * FLAT_W))), n


def _flatten(parts, dm, dtype):
    nr, n = _flat_rows(dm)
    lead = parts[0].shape[:-2]
    flat = jnp.concatenate([p.reshape(lead + (-1,)).astype(dtype) for p in parts], axis=-1)
    flat = jnp.pad(flat, [(0, 0)] * len(lead) + [(0, 2 * nr * FLAT_W - n)])
    return flat.reshape(lead + (2, nr, FLAT_W))


def _unflatten(buf, dm):
    lead = buf.shape[:-3]
    flat = buf.reshape(lead + (-1,))
    out, off = [], 0
    for _, (r, c) in _shard_shapes(dm):
        out.append(flat[..., off:off + r * c].reshape(lead + (r, c)))
        off += r * c
    return out


def _cols_to_full(g):
    n, r, c = g.shape
    return jnp.transpose(g, (1, 0, 2)).reshape(r, n * c)


def _full_to_cols(a):
    r, c4 = a.shape
    return jnp.transpose(a.reshape(r, 4, c4 // 4), (1, 0, 2))


def _split_w_in(w_in, dm):
    nm, H = dm.NMAIN, dm.H
    pad = jnp.zeros((dm.D, HEAD - H), w_in.dtype)
    return w_in[:, :nm], jnp.concatenate([w_in[:, nm + H:], w_in[:, nm:nm + H], pad], axis=1)


def _join_w_in(d_main, d_rest, dm):
    H, D2 = dm.H, 2 * dm.D
    return jnp.concatenate([d_main, d_rest[:, D2:D2 + H], d_rest[:, :D2]], axis=1)


def _pack_small(dm, bias, norm_w, lb_logits, g1, g2, g3, g4):
    D = dm.D
    row = lambda v: jnp.pad(v.reshape(1, -1), ((0, 0), (0, D - v.size)))
    rows = [jnp.zeros((1, D), F32), row(bias), row(norm_w), row(lb_logits[0]), row(g1), row(g2), row(g3), row(g4),
            row(lb_logits[1]), jnp.zeros((SMALL_ROWS - 9, D), F32)]
    return jnp.concatenate(rows, axis=0)


def _unpack_small(p, dm):
    H, AW = dm.H, dm.AW
    return (p[ROW_BIAS:ROW_BIAS + 1, :H], jnp.concatenate([p[ROW_LB0:ROW_LB0 + 1, :AW], p[ROW_LB1:ROW_LB1 + 1, :AW]], axis=0),
            p[ROW_NORM:ROW_NORM + 1, :HEAD], p[ROW_G1:ROW_G1 + 1], p[ROW_G2:ROW_G2 + 1], p[ROW_G3:ROW_G3 + 1],
            p[ROW_G4:ROW_G4 + 1])


def _step(dm, x, w_in, b_fox_f, hgrn_lb_logits, hgrn_norm_w, w_up_a, w_up_b, w_o, norm_mix_pre, norm_mix_post,
          norm_ffn_pre, norm_ffn_post, w_ffn_in, w_ffn_down, loss_target, moments_m, moments_v):
    xi, yi, ci = lax.axis_index("x"), lax.axis_index("y"), lax.axis_index("c")
    c_idx = jnp.reshape(ci, (1,)).astype(jnp.int32)
    q_idx = jnp.reshape(2 * xi + yi, (1,)).astype(jnp.int32)
    shards = [w_in[0], w_up_a[0], w_up_b[0], w_o[0], w_ffn_in[0], w_ffn_down[0]]

    gathered = _gather_weights(_flatten(shards, dm, BF16))
    g_in, g_up_a, g_up_b, g_o, g_ffn_in, g_ffn_down = _unflatten(gathered, dm)
    w_main, w_rest = _split_w_in(_cols_to_full(g_in), dm)
    full_o = g_o.reshape(dm.D, dm.D)
    full_down = g_ffn_down.reshape(dm.FF, dm.D)

    bias_p = jnp.pad(b_fox_f, ((0, 0), (0, HEAD - dm.H)))
    dx, big, small = _local_step(
        dm, x[0], loss_target[0], w_main, w_rest, _cols_to_full(g_up_a), _cols_to_full(g_up_b), full_o,
        _cols_to_full(g_ffn_in), full_down, bias_p, hgrn_lb_logits, hgrn_norm_w,
        norm_mix_pre, norm_mix_post, norm_ffn_pre, norm_ffn_post)

    parts = [_full_to_cols(_join_w_in(big["main"], big["rest"], dm)), _full_to_cols(big["up_a"]),
             _full_to_cols(big["up_b"]), big["w_o"].reshape(4, dm.D // 4, dm.D),
             _full_to_cols(big["ffn_in"]), big["ffn_down"].reshape(4, dm.FF // 4, dm.D)]
    gflat = _flatten(parts, dm, F32)
    pair_f32, pair_bf16 = _add_sibling(gflat, _swap_halves(gflat), c_idx)
    half = _add_chips(pair_f32, _scatter_chips(pair_bf16), q_idx)
    grads = _unflatten(_share_halves(half), dm)

    D = dm.D
    row = lambda v: jnp.pad(v.reshape(1, -1), ((0, 0), (0, D - v.size)))
    vec = jnp.concatenate([row(small["loss"][:, :1]), row(small["bias"][:, :dm.H]), row(small["norm_w"]),
                           row(small["lb"]), small["g1"], small["g2"], small["g3"], small["g4"]], axis=0)
    gsum = _sum_small(vec)
    loss = gsum[ROW_LOSS, 0]

    smalls = lambda t: (t["b_fox_f"], t["hgrn_norm_w"], t["hgrn_lb_logits"], t["norm_mix_pre"], t["norm_mix_post"],
                        t["norm_ffn_pre"], t["norm_ffn_post"])
    params = dict(b_fox_f=b_fox_f, hgrn_norm_w=hgrn_norm_w, hgrn_lb_logits=hgrn_lb_logits, norm_mix_pre=norm_mix_pre,
                  norm_mix_post=norm_mix_post, norm_ffn_pre=norm_ffn_pre, norm_ffn_post=norm_ffn_post)
    sg, sd, sm, sv = _small_update(gsum, _pack_small(dm, *smalls(params)), _pack_small(dm, *smalls(moments_m)),
                                   _pack_small(dm, *smalls(moments_v)))
    big_names = ["w_in", "w_up_a", "w_up_b", "w_o", "w_ffn_in", "w_ffn_down"]
    big_out = {}
    for name, wsh, g in zip(big_names, shards, grads):
        d, mn, vn = _adamw(wsh, g, moments_m[name][0], moments_v[name][0], "adamw_" + name)
        big_out[name] = (g[None], d[None], mn[None], vn[None])

    order = ["w_in", "b_fox_f", "hgrn_lb_logits", "hgrn_norm_w", "w_up_a", "w_up_b", "w_o", "norm_mix_pre",
             "norm_mix_post", "norm_ffn_pre", "norm_ffn_post", "w_ffn_in", "w_ffn_down"]
    outs = []
    for kind, packed in enumerate([sg, sd, sm, sv]):
        b, lbl, nw, p1, p2, p3, p4 = _unpack_small(packed, dm)
        sm_map = dict(b_fox_f=b, hgrn_lb_logits=lbl, hgrn_norm_w=nw, norm_mix_pre=p1, norm_mix_post=p2,
                      norm_ffn_pre=p3, norm_ffn_post=p4)
        outs.append([big_out[n][kind] if n in big_out else sm_map[n] for n in order])
    return (loss, dx[None], *outs[0], *outs[1], *outs[2], *outs[3])


def kernel(x, w_in, b_fox_f, hgrn_lb_logits, hgrn_norm_w, w_up_a, w_up_b, w_o, norm_mix_pre, norm_mix_post, norm_ffn_pre, norm_ffn_post, w_ffn_in, w_ffn_down, loss_target, m_w_in, m_b_fox_f, m_hgrn_lb_logits, m_hgrn_norm_w, m_w_up_a, m_w_up_b, m_w_o, m_norm_mix_pre, m_norm_mix_post, m_norm_ffn_pre, m_norm_ffn_post, m_w_ffn_in, m_w_ffn_down, v_w_in, v_b_fox_f, v_hgrn_lb_logits, v_hgrn_norm_w, v_w_up_a, v_w_up_b, v_w_o, v_norm_mix_pre, v_norm_mix_post, v_norm_ffn_pre, v_norm_ffn_post, v_w_ffn_in, v_w_ffn_down):
    dm = Dims(T=x.shape[1], D=x.shape[2], FF=w_ffn_down.shape[1] * 4)
    moments_m = dict(w_in=m_w_in, b_fox_f=m_b_fox_f, hgrn_lb_logits=m_hgrn_lb_logits, hgrn_norm_w=m_hgrn_norm_w,
                     w_up_a=m_w_up_a, w_up_b=m_w_up_b, w_o=m_w_o, norm_mix_pre=m_norm_mix_pre,
                     norm_mix_post=m_norm_mix_post, norm_ffn_pre=m_norm_ffn_pre, norm_ffn_post=m_norm_ffn_post,
                     w_ffn_in=m_w_ffn_in, w_ffn_down=m_w_ffn_down)
    moments_v = dict(w_in=v_w_in, b_fox_f=v_b_fox_f, hgrn_lb_logits=v_hgrn_lb_logits, hgrn_norm_w=v_hgrn_norm_w,
                     w_up_a=v_w_up_a, w_up_b=v_w_up_b, w_o=v_w_o, norm_mix_pre=v_norm_mix_pre,
                     norm_mix_post=v_norm_mix_post, norm_ffn_pre=v_norm_ffn_pre, norm_ffn_post=v_norm_ffn_post,
                     w_ffn_in=v_w_ffn_in, w_ffn_down=v_w_ffn_down)
    return _step(dm, x, w_in, b_fox_f, hgrn_lb_logits, hgrn_norm_w, w_up_a, w_up_b, w_o, norm_mix_pre, norm_mix_post,
                 norm_ffn_pre, norm_ffn_post, w_ffn_in, w_ffn_down, loss_target, moments_m, moments_v)
```

```python
from typing import NamedTuple

import numpy as np
import jax
import jax.numpy as jnp
from jax import lax
from jax.experimental import pallas as pl
from jax.experimental.pallas import tpu as pltpu

F32 = jnp.float32
BF16 = jnp.bfloat16
MESH = pl.DeviceIdType.MESH

RMS_EPS = 1e-6
HEAD = 128
CHUNK = 64
GROUP = 4
LEVELS = (32, 16, 8, 4, 2, 1)
NEG = -1e30
FLAT_W = 1024
FLAT_ROWS = 256

ADAM_LR = 0.001
ADAM_B1 = 0.9
ADAM_B2 = 0.999
ADAM_EPS = 1e-08
ADAM_WD = 0.01
ADAM_STEP = 10

VMEM_LIMIT = 56 * 1024 * 1024


class Dims(NamedTuple):
    T: int
    D: int
    FF: int

    @property
    def AW(self):
        return self.D // 2

    @property
    def H(self):
        return self.AW // HEAD

    @property
    def NMAIN(self):
        return 7 * self.AW

    @property
    def NREST(self):
        return 2 * self.D + HEAD

    @property
    def NIN(self):
        return 7 * self.AW + self.H + 2 * self.D


def _cparams(sem, vmem=VMEM_LIMIT, **kw):
    return pltpu.CompilerParams(dimension_semantics=sem, vmem_limit_bytes=vmem, **kw)


def _tile(n, target):
    if n <= target:
        return n
    t = (target // 128) * 128
    while t >= 128:
        if n % t == 0:
            return t
        t -= 128
    raise ValueError(f"no tile for {n}")


def _dot(a, b, dims):
    return lax.dot_general(a, b, (dims, ((), ())), preferred_element_type=F32)


def _nn(a, b):
    return _dot(a, b, ((1,), (0,)))


def _nt(a, b):
    return _dot(a, b, ((1,), (1,)))


def _tn(a, b):
    return _dot(a, b, ((0,), (0,)))


def _sigmoid(x):
    return jax.nn.sigmoid(x)


def _split2(x):
    hi = x.astype(BF16)
    lo = (x - hi.astype(F32)).astype(BF16)
    return hi, lo


def _split3(x):
    hi = x.astype(BF16)
    r = x - hi.astype(F32)
    mid = r.astype(BF16)
    lo = (r - mid.astype(F32)).astype(BF16)
    return hi, mid, lo


def _mm(a, b, mode, out_dtype, name, add=None, tm=1024, tn=1024, tk=None):
    if mode == "nn":
        (M, K), (K2, N) = a.shape, b.shape
    elif mode == "nt":
        (M, K), (N, K2) = a.shape, b.shape
    else:
        (K, M), (K2, N) = a.shape, b.shape
    assert K == K2, (a.shape, b.shape, mode)
    if tk is None:
        tk = 1024 if mode == "tn" else 2048
    tm, tn, tk = _tile(M, tm), _tile(N, tn), _tile(K, tk)
    nk = K // tk
    isz = lambda t: jnp.dtype(t.dtype).itemsize
    vmem = 2 * (tm * tk * isz(a) + tk * tn * isz(b) + tm * tn * jnp.dtype(out_dtype).itemsize)
    vmem += (tm * tn * 4 if nk > 1 else 0) + (2 * tm * tn * isz(add) if add is not None else 0)
    assert vmem <= VMEM_LIMIT - 8 * 1024 * 1024, (name, vmem)
    if mode == "nn":
        a_spec = pl.BlockSpec((tm, tk), lambda i, j, k: (i, k))
        b_spec = pl.BlockSpec((tk, tn), lambda i, j, k: (k, j))
        op = _nn
    elif mode == "nt":
        a_spec = pl.BlockSpec((tm, tk), lambda i, j, k: (i, k))
        b_spec = pl.BlockSpec((tn, tk), lambda i, j, k: (j, k))
        op = _nt
    else:
        a_spec = pl.BlockSpec((tk, tm), lambda i, j, k: (k, i))
        b_spec = pl.BlockSpec((tk, tn), lambda i, j, k: (k, j))
        op = _tn
    o_spec = pl.BlockSpec((tm, tn), lambda i, j, k: (i, j))
    has_add = add is not None

    def body(*refs):
        a_ref, b_ref = refs[:2]
        add_ref = refs[2] if has_add else None
        o_ref = refs[3] if has_add else refs[2]

        def finish(r):
            if has_add:
                r = r + add_ref[...].astype(F32)
            o_ref[...] = r.astype(out_dtype)

        part = op(a_ref[...].astype(BF16), b_ref[...].astype(BF16))
        if nk == 1:
            finish(part)
            return
        acc = refs[-1]
        k = pl.program_id(2)

        @pl.when(k == 0)
        def _():
            acc[...] = part

        @pl.when((k > 0) & (k < nk - 1))
        def _():
            acc[...] += part

        @pl.when(k == nk - 1)
        def _():
            finish(acc[...] + part)

    in_specs = [a_spec, b_spec] + ([o_spec] if has_add else [])
    args = (a, b) + ((add,) if has_add else ())
    return pl.pallas_call(
        body, name=name, grid=(M // tm, N // tn, nk),
        in_specs=in_specs, out_specs=o_spec,
        out_shape=jax.ShapeDtypeStruct((M, N), out_dtype),
        scratch_shapes=[pltpu.VMEM((tm, tn), F32)] if nk > 1 else [],
        compiler_params=_cparams(("parallel", "parallel", "arbitrary")),
    )(*args)


def _rows(tr, w, col=0):
    return pl.BlockSpec((tr, w), lambda i, *_: (i, col))


def _vec(w):
    return pl.BlockSpec((1, w), lambda i, *_: (0, 0))


def _rstd(v):
    return lax.rsqrt(jnp.mean(v * v, axis=-1, keepdims=True) + RMS_EPS)


def _rms_bwd(dn, n, r):
    return r * (dn - n * jnp.mean(dn * n, axis=-1, keepdims=True))


def _colsum(v):
    return jnp.sum(v, axis=0, keepdims=True)


def _rms_fwd(x, g, tr=256):
    T, D = x.shape

    def body(x_ref, g_ref, h_ref, r_ref):
        xv = x_ref[...]
        r = _rstd(xv)
        h_ref[...] = (xv * r * g_ref[...]).astype(BF16)
        r_ref[...] = r

    return pl.pallas_call(
        body, name="rms_fwd", grid=(T // tr,),
        in_specs=[_rows(tr, D), _vec(D)],
        out_specs=[_rows(tr, D), _rows(tr, 1)],
        out_shape=[jax.ShapeDtypeStruct((T, D), BF16), jax.ShapeDtypeStruct((T, 1), F32)],
        compiler_params=_cparams(("parallel",)),
    )(x, g)


def _merge_fwd(rest, y_a, y_b, tr=256):
    T, D = y_a.shape

    def body(ga_ref, gb_ref, ya_ref, yb_ref, o_ref):
        o_ref[...] = (_sigmoid(ga_ref[...]) * ya_ref[...] + _sigmoid(gb_ref[...]) * yb_ref[...]).astype(BF16)

    return pl.pallas_call(
        body, name="merge_fwd", grid=(T // tr,),
        in_specs=[_rows(tr, D, 0), _rows(tr, D, 1), _rows(tr, D), _rows(tr, D)],
        out_specs=_rows(tr, D),
        out_shape=jax.ShapeDtypeStruct((T, D), BF16),
        compiler_params=_cparams(("parallel",)),
    )(rest, rest, y_a, y_b)


def _post_pre(x, u, g2, g3, tr=256):
    T, D = x.shape

    def body(x_ref, u_ref, g2_ref, g3_ref, x1_ref, r2_ref, h3_ref, r3_ref):
        uv = u_ref[...]
        r2 = _rstd(uv)
        x1 = x_ref[...] + uv * r2 * g2_ref[...]
        r3 = _rstd(x1)
        x1_ref[...] = x1
        r2_ref[...] = r2
        h3_ref[...] = (x1 * r3 * g3_ref[...]).astype(BF16)
        r3_ref[...] = r3

    return pl.pallas_call(
        body, name="post_pre", grid=(T // tr,),
        in_specs=[_rows(tr, D), _rows(tr, D), _vec(D), _vec(D)],
        out_specs=[_rows(tr, D), _rows(tr, 1), _rows(tr, D), _rows(tr, 1)],
        out_shape=[jax.ShapeDtypeStruct((T, D), F32), jax.ShapeDtypeStruct((T, 1), F32),
                   jax.ShapeDtypeStruct((T, D), BF16), jax.ShapeDtypeStruct((T, 1), F32)],
        compiler_params=_cparams(("parallel",)),
    )(x, u, g2, g3)


def _swiglu_fwd(gu, tr=256):
    T, FF2 = gu.shape
    FF = FF2 // 2
    tc = _tile(FF, 1024)
    nc = FF // tc

    def body(g_ref, u_ref, o_ref):
        gv = g_ref[...]
        o_ref[...] = (gv * _sigmoid(gv) * u_ref[...]).astype(BF16)

    return pl.pallas_call(
        body, name="swiglu_fwd", grid=(T // tr, nc),
        in_specs=[pl.BlockSpec((tr, tc), lambda i, j: (i, j)),
                  pl.BlockSpec((tr, tc), lambda i, j: (i, j + nc))],
        out_specs=pl.BlockSpec((tr, tc), lambda i, j: (i, j)),
        out_shape=jax.ShapeDtypeStruct((T, FF), BF16),
        compiler_params=_cparams(("parallel", "parallel")),
    )(gu, gu)


def _loss_bwd(x1, w, g4, tgt, tr=256):
    T, D = x1.shape

    def body(x1_ref, w_ref, g4_ref, t_ref, loss_ref, dy_ref, dw_ref, dg_ref):
        i = pl.program_id(0)

        @pl.when(i == 0)
        def _():
            loss_ref[...] = jnp.zeros_like(loss_ref)
            dg_ref[...] = jnp.zeros_like(dg_ref)

        wv = w_ref[...]
        g4v = g4_ref[...]
        r4 = _rstd(wv)
        n4 = wv * r4
        e = x1_ref[...] + n4 * g4v - t_ref[...]
        loss_ref[...] += 0.5 * jnp.sum(jnp.mean(e * e, axis=-1, keepdims=True), axis=0, keepdims=True)
        dy = e * (1.0 / D)
        dy_ref[...] = dy
        dg_ref[...] += _colsum(dy * n4)
        dw_ref[...] = _rms_bwd(dy * g4v, n4, r4).astype(BF16)

    return pl.pallas_call(
        body, name="loss_bwd", grid=(T // tr,),
        in_specs=[_rows(tr, D), _rows(tr, D), _vec(D), _rows(tr, D)],
        out_specs=[_vec(HEAD), _rows(tr, D), _rows(tr, D), _vec(D)],
        out_shape=[jax.ShapeDtypeStruct((1, HEAD), F32), jax.ShapeDtypeStruct((T, D), F32),
                   jax.ShapeDtypeStruct((T, D), BF16), jax.ShapeDtypeStruct((1, D), F32)],
        compiler_params=_cparams(("arbitrary",)),
    )(x1, w, g4, tgt)


def _swiglu_bwd(gu, dact, tr=128):
    T, FF2 = gu.shape
    FF = FF2 // 2

    def body(g_ref, u_ref, d_ref, o_ref):
        h = pl.program_id(1)
        gv = g_ref[...]
        s = _sigmoid(gv)
        dv = d_ref[...].astype(F32)

        @pl.when(h == 0)
        def _():
            o_ref[...] = (dv * u_ref[...] * (s * (1.0 + gv * (1.0 - s)))).astype(BF16)

        @pl.when(h == 1)
        def _():
            o_ref[...] = (dv * (gv * s)).astype(BF16)

    return pl.pallas_call(
        body, name="swiglu_bwd", grid=(T // tr, 2),
        in_specs=[pl.BlockSpec((tr, FF), lambda i, h: (i, 0)),
                  pl.BlockSpec((tr, FF), lambda i, h: (i, 1)),
                  pl.BlockSpec((tr, FF), lambda i, h: (i, 0))],
        out_specs=pl.BlockSpec((tr, FF), lambda i, h: (i, h)),
        out_shape=jax.ShapeDtypeStruct((T, FF2), BF16),
        compiler_params=_cparams(("parallel", "arbitrary")),
    )(gu, gu, dact)


def _rms_bwd2(dy, dh3, x1, r3, g3, u, r2, g2, tr=256):
    T, D = dy.shape

    def body(dy_ref, dh_ref, x1_ref, r3_ref, g3_ref, u_ref, r2_ref, g2_ref, dx1_ref, du_ref, dg3_ref, dg2_ref):
        i = pl.program_id(0)

        @pl.when(i == 0)
        def _():
            dg3_ref[...] = jnp.zeros_like(dg3_ref)
            dg2_ref[...] = jnp.zeros_like(dg2_ref)

        r3v, r2v = r3_ref[...], r2_ref[...]
        dh = dh_ref[...]
        n3 = x1_ref[...] * r3v
        dg3_ref[...] += _colsum(dh * n3)
        dx1 = dy_ref[...] + _rms_bwd(dh * g3_ref[...], n3, r3v)
        dx1_ref[...] = dx1
        n2 = u_ref[...] * r2v
        dg2_ref[...] += _colsum(dx1 * n2)
        du_ref[...] = _rms_bwd(dx1 * g2_ref[...], n2, r2v).astype(BF16)

    return pl.pallas_call(
        body, name="rms_bwd2", grid=(T // tr,),
        in_specs=[_rows(tr, D), _rows(tr, D), _rows(tr, D), _rows(tr, 1), _vec(D),
                  _rows(tr, D), _rows(tr, 1), _vec(D)],
        out_specs=[_rows(tr, D), _rows(tr, D), _vec(D), _vec(D)],
        out_shape=[jax.ShapeDtypeStruct((T, D), F32), jax.ShapeDtypeStruct((T, D), BF16),
                   jax.ShapeDtypeStruct((1, D), F32), jax.ShapeDtypeStruct((1, D), F32)],
        compiler_params=_cparams(("arbitrary",)),
    )(dy, dh3, x1, r3, g3, u, r2, g2)


def _merge_bwd(dmerged, rest, y_a, y_b, tr=256):
    T, D = dmerged.shape

    def body(dm_ref, ga_ref, gb_ref, ya_ref, yb_ref, dya_ref, dyb_ref, dg_ref):
        h = pl.program_id(1)
        dm = dm_ref[...]

        @pl.when(h == 0)
        def _():
            s = _sigmoid(ga_ref[...])
            dya_ref[...] = (dm * s).astype(BF16)
            dg_ref[...] = (dm * ya_ref[...] * s * (1.0 - s)).astype(BF16)

        @pl.when(h == 1)
        def _():
            s = _sigmoid(gb_ref[...])
            dyb_ref[...] = (dm * s).astype(BF16)
            dg_ref[...] = (dm * yb_ref[...] * s * (1.0 - s)).astype(BF16)

    blk = lambda col: pl.BlockSpec((tr, D), lambda i, h: (i, col))
    return pl.pallas_call(
        body, name="merge_bwd", grid=(T // tr, 2),
        in_specs=[blk(0), blk(0), blk(1), blk(0), blk(0)],
        out_specs=[blk(0), blk(0), pl.BlockSpec((tr, D), lambda i, h: (i, h))],
        out_shape=[jax.ShapeDtypeStruct((T, D), BF16), jax.ShapeDtypeStruct((T, D), BF16),
                   jax.ShapeDtypeStruct((T, 2 * D), BF16)],
        compiler_params=_cparams(("parallel", "arbitrary")),
    )(dmerged, rest, rest, y_a, y_b)


def _rms_bwd1(dx1, dh1, x, r1, g1, tr=256):
    T, D = x.shape

    def body(dx1_ref, dh_ref, x_ref, r_ref, g_ref, dx_ref, dg_ref):
        i = pl.program_id(0)

        @pl.when(i == 0)
        def _():
            dg_ref[...] = jnp.zeros_like(dg_ref)

        rv = r_ref[...]
        dh = dh_ref[...]
        n = x_ref[...] * rv
        dg_ref[...] += _colsum(dh * n)
        dx_ref[...] = dx1_ref[...] + _rms_bwd(dh * g_ref[...], n, rv)

    return pl.pallas_call(
        body, name="rms_bwd1", grid=(T // tr,),
        in_specs=[_rows(tr, D), _rows(tr, D), _rows(tr, D), _rows(tr, 1), _vec(D)],
        out_specs=[_rows(tr, D), _vec(D)],
        out_shape=[jax.ShapeDtypeStruct((T, D), F32), jax.ShapeDtypeStruct((1, D), F32)],
        compiler_params=_cparams(("arbitrary",)),
    )(dx1, dh1, x, r1, g1)


def _hgrn_consts():
    C = CHUNK
    nl = len(LEVELS) + 1
    w = np.zeros((nl, C, C), np.float32)
    w[0] = np.tril(np.ones((C, C), np.float32))
    for li, m in enumerate(LEVELS, start=1):
        for r in range(C):
            mid = (r // (2 * m)) * 2 * m + m
            if r >= mid:
                w[li, r, mid:r + 1] = 1.0
            else:
                w[li, r, r + 1:mid] = 1.0
    w_all = w.reshape(nl * C, C)
    w2 = np.concatenate([w_all, w_all], axis=1)
    w2t = np.concatenate([w_all.T, w_all.T], axis=1)
    R = GROUP * C
    t = np.arange(R)[:, None]
    s = np.arange(R)[None, :]
    masks = np.zeros((nl, R, R), np.float32)
    masks[0] = (t == s)
    for li, m in enumerate(LEVELS, start=1):
        masks[li] = ((t ^ s) < 2 * m)
    return jnp.asarray(w2, BF16), jnp.asarray(w2t, BF16), jnp.asarray(masks, F32)


def _hgrn_gates(z, lg_ref):
    l0 = lg_ref[0:1, :]
    l1 = lg_ref[1:2, :]
    mx = jnp.maximum(l0, l1)
    e0 = jnp.exp(l0 - mx)
    e1 = jnp.exp(l1 - mx)
    lb = e0 / (e0 + e1)
    om = 1.0 - lb
    sg = _sigmoid(z)
    sgm = _sigmoid(-z)
    f = lb + om * sg
    return lb, om, sg, sgm, f, jnp.log(f), om * sgm


def _hgrn_levels(q, kk, lf, w2_ref):
    C = CHUNK
    nl = len(LEVELS) + 1
    lf_hi, lf_lo = _split2(lf)
    per_chunk = []
    for c in range(GROUP):
        rhs = jnp.concatenate([lf_hi[c * C:(c + 1) * C], lf_lo[c * C:(c + 1) * C]], axis=0)
        per_chunk.append(_nn(w2_ref[...], rhs))
    args = [jnp.concatenate([per_chunk[c][l * C:(l + 1) * C] for c in range(GROUP)], axis=0) for l in range(nl)]
    exps = [jnp.exp(a) for a in args]
    row = lax.broadcasted_iota(jnp.int32, q.shape, 0)
    qf, kf, mts = [q], [kk], [None]
    for li, m in enumerate(LEVELS, start=1):
        mt = jnp.where((row & m) != 0, 1.0, 0.0).astype(F32)
        mts.append(mt)
        qf.append(q * exps[li] * mt)
        kf.append(kk * exps[li] * (1.0 - mt))
    return args, exps, mts, qf, kf


def _hgrn_scores(qf, kf, masks_ref):
    p = None
    for l in range(len(qf)):
        pl_ = _nt(qf[l].astype(BF16), kf[l].astype(BF16)) * masks_ref[l]
        p = pl_ if p is None else p + pl_
    return p


def _hgrn_fwd(main, lb_logits, norm_w, dm):
    T, H, C = dm.T, dm.H, CHUNK
    R = GROUP * C
    nj = T // R
    w2, _, masks = _hgrn_consts()

    def body(q_ref, z_ref, v_ref, g_ref, lg_ref, nw_ref, w2_ref, masks_ref, ya_ref, o_ref, sp_ref, st_ref):
        j = pl.program_id(1)

        @pl.when(j == 0)
        def _():
            st_ref[...] = jnp.zeros_like(st_ref)

        q = q_ref[...]
        v = v_ref[...]
        vb = v.astype(BF16)
        _, _, _, _, _, lf, kk = _hgrn_gates(z_ref[...], lg_ref)
        args, exps, _, qf, kf = _hgrn_levels(q, kk, lf, w2_ref)
        p = _hgrn_scores(qf, kf, masks_ref)
        o_intra = _nn(p.astype(BF16), vb)
        b, eb = args[0], exps[0]
        o_inter = []
        for c in range(GROUP):
            sl = slice(c * C, (c + 1) * C)
            st = st_ref[...]
            sp_ref[0, c] = st
            blast = b[c * C + C - 1:c * C + C, :]
            o_inter.append(_nt((q[sl] * eb[sl]).astype(BF16), st.astype(BF16)))
            kd = (kk[sl] * jnp.exp(blast - b[sl])).astype(BF16)
            st_ref[...] = st * jnp.exp(blast) + _tn(vb[sl], kd)
        o = o_intra + jnp.concatenate(o_inter, axis=0)
        o_ref[...] = o
        gv = g_ref[...]
        ya_ref[...] = (o * _rstd(o) * nw_ref[...] * (gv * _sigmoid(gv))).astype(BF16)

    def col(sec):
        return pl.BlockSpec((R, HEAD), lambda h, j: (j, sec * H + h))

    return pl.pallas_call(
        body, name="hgrn_fwd", grid=(H, nj),
        in_specs=[col(0), col(1), col(2), col(3),
                  pl.BlockSpec((2, HEAD), lambda h, j: (0, h)),
                  pl.BlockSpec((1, HEAD), lambda h, j: (0, 0)),
                  pl.BlockSpec(w2.shape, lambda h, j: (0, 0)),
                  pl.BlockSpec(masks.shape, lambda h, j: (0, 0, 0))],
        out_specs=[pl.BlockSpec((R, HEAD), lambda h, j: (j, h)),
                   pl.BlockSpec((R, HEAD), lambda h, j: (j, h)),
                   pl.BlockSpec((1, GROUP, HEAD, HEAD), lambda h, j: (h, j, 0, 0))],
        out_shape=[jax.ShapeDtypeStruct((T, dm.AW), BF16), jax.ShapeDtypeStruct((T, dm.AW), F32),
                   jax.ShapeDtypeStruct((H, T // C, HEAD, HEAD), F32)],
        scratch_shapes=[pltpu.VMEM((HEAD, HEAD), F32)],
        compiler_params=_cparams(("parallel", "arbitrary")),
    )(main, main, main, main, lb_logits, norm_w, w2, masks)


def _hgrn_bwd(main, lb_logits, norm_w, o_saved, states, dya, dm):
    T, H, C = dm.T, dm.H, CHUNK
    R = GROUP * C
    nj = T // R
    nl = len(LEVELS) + 1
    w2, w2t, masks = _hgrn_consts()

    def body(q_ref, z_ref, v_ref, g_ref, lg_ref, nw_ref, w2_ref, w2t_ref, masks_ref, o_ref, sp_ref, dya_ref,
             d_ref, dlb_ref, dnw_ref, ds_ref, stash):
        j = pl.program_id(1)
        sec = pl.program_id(2)

        @pl.when((j == 0) & (sec == 0))
        def _():
            ds_ref[...] = jnp.zeros_like(ds_ref)
            dlb_ref[...] = jnp.zeros_like(dlb_ref)
            dnw_ref[...] = jnp.zeros_like(dnw_ref)

        @pl.when(sec == 0)
        def _():
            q = q_ref[...]
            v = v_ref[...]
            gv = g_ref[...]
            vb = v.astype(BF16)
            lb, om, sg, sgm, f, lf, kk = _hgrn_gates(z_ref[...], lg_ref)
            args, exps, mts, qf, kf = _hgrn_levels(q, kk, lf, w2_ref)
            qb = [t.astype(BF16) for t in qf]
            kb = [t.astype(BF16) for t in kf]
            p = _hgrn_scores(qf, kf, masks_ref)
            o = o_ref[...]
            nw = nw_ref[...]
            r = _rstd(o)
            n = o * r
            sgg = _sigmoid(gv)
            dya_v = dya_ref[...]
            d_on = dya_v * (gv * sgg)
            dg = dya_v * (n * nw) * (sgg * (1.0 + gv * (1.0 - sgg)))
            dnw_ref[0] += _colsum(d_on * n)
            do = _rms_bwd(d_on * nw, n, r)
            dob = do.astype(BF16)
            dp = _nt(dob, vb)
            dv = _tn(p.astype(BF16), dob)
            dq = jnp.zeros_like(q)
            dkk = jnp.zeros_like(q)
            dargs = [None] * nl
            for l in range(nl):
                dpl = (dp * masks_ref[l]).astype(BF16)
                dql = _nn(dpl, kb[l])
                dkl = _tn(dpl, qb[l])
                if l == 0:
                    dq += dql
                    dkk += dkl
                else:
                    dq += dql * exps[l] * mts[l]
                    dkk += dkl * exps[l] * (1.0 - mts[l])
                    dargs[l] = dql * qf[l] + dkl * kf[l]
            b, eb = args[0], exps[0]
            row = lax.broadcasted_iota(jnp.int32, (C, HEAD), 0)
            dq_i, dkk_i, dv_i, db_i = [None] * GROUP, [None] * GROUP, [None] * GROUP, [None] * GROUP
            for c in reversed(range(GROUP)):
                sl = slice(c * C, (c + 1) * C)
                st = sp_ref[0, c]
                ds = ds_ref[...]
                dsb = ds.astype(BF16)
                blast = b[c * C + C - 1:c * C + C, :]
                ebl = jnp.exp(blast)
                el = jnp.exp(blast - b[sl])
                qe = q[sl] * eb[sl]
                kd = kk[sl] * el
                dqe = _nn(dob[sl], st.astype(BF16))
                dkd = _nn(vb[sl], dsb)
                t = dkd * kd
                dblast = _colsum(t) + _colsum(ds * st) * ebl
                dq_i[c] = dqe * eb[sl]
                dkk_i[c] = dkd * el
                dv_i[c] = _nt(kd.astype(BF16), dsb)
                db_i[c] = dqe * qe - t + jnp.where(row == C - 1, dblast, 0.0)
                ds_ref[...] = ds * ebl + _tn(dob[sl], qe.astype(BF16))
            dq = dq + jnp.concatenate(dq_i, axis=0)
            dkk = dkk + jnp.concatenate(dkk_i, axis=0)
            dv = dv + jnp.concatenate(dv_i, axis=0)
            dlf_c = []
            for c in range(GROUP):
                sl = slice(c * C, (c + 1) * C)
                stack = jnp.concatenate([db_i[c]] + [dargs[l][sl] for l in range(1, nl)], axis=0)
                hi, lo = _split2(stack)
                dlf_c.append(_nn(w2t_ref[...], jnp.concatenate([hi, lo], axis=0)))
            dlf = jnp.concatenate(dlf_c, axis=0)
            dz = dlf * (om * sg * (1.0 - sg) / f) - dkk * (om * sgm * (1.0 - sgm))
            dlb_ref[...] += _colsum(dlf * (1.0 - sg) / f - dkk * sgm)
            stash[0] = dq.astype(BF16)
            stash[1] = dz.astype(BF16)
            stash[2] = dv.astype(BF16)
            stash[3] = dg.astype(BF16)

        d_ref[...] = stash[sec]

    def col(sec):
        return pl.BlockSpec((R, HEAD), lambda h, j, s: (nj - 1 - j, sec * H + h))

    return pl.pallas_call(
        body, name="hgrn_bwd", grid=(H, nj, 4),
        in_specs=[col(0), col(1), col(2), col(3),
                  pl.BlockSpec((2, HEAD), lambda h, j, s: (0, h)),
                  pl.BlockSpec((1, HEAD), lambda h, j, s: (0, 0)),
                  pl.BlockSpec(w2.shape, lambda h, j, s: (0, 0)),
                  pl.BlockSpec(w2t.shape, lambda h, j, s: (0, 0)),
                  pl.BlockSpec(masks.shape, lambda h, j, s: (0, 0, 0)),
                  pl.BlockSpec((R, HEAD), lambda h, j, s: (nj - 1 - j, h)),
                  pl.BlockSpec((1, GROUP, HEAD, HEAD), lambda h, j, s: (h, nj - 1 - j, 0, 0)),
                  pl.BlockSpec((R, HEAD), lambda h, j, s: (nj - 1 - j, h))],
        out_specs=[pl.BlockSpec((R, HEAD), lambda h, j, s: (nj - 1 - j, s * H + h)),
                   pl.BlockSpec((1, HEAD), lambda h, j, s: (0, h)),
                   pl.BlockSpec((1, 1, HEAD), lambda h, j, s: (h, 0, 0))],
        out_shape=[jax.ShapeDtypeStruct((T, 4 * dm.AW), BF16), jax.ShapeDtypeStruct((1, dm.AW), F32),
                   jax.ShapeDtypeStruct((H, 1, HEAD), F32)],
        scratch_shapes=[pltpu.VMEM((HEAD, HEAD), F32), pltpu.VMEM((4, R, HEAD), BF16)],
        compiler_params=_cparams(("parallel", "arbitrary", "arbitrary")),
    )(main, main, main, main, lb_logits, norm_w, w2, w2t, masks, o_saved, states, dya)


def _log_sigmoid(x):
    return jnp.minimum(x, 0.0) - jnp.log(1.0 + jnp.exp(-jnp.abs(x)))


def _tri(n):
    return jnp.asarray(np.tril(np.ones((n, n), np.float32)), BF16)


def _cum_fwd(rest, bias, dm, tb=256):
    T = dm.T
    tb = min(tb, T)
    cb = 2 * dm.D // HEAD
    tri = _tri(tb)

    def body(x_ref, b_ref, tri_ref, o_ref, carry):
        i = pl.program_id(0)

        @pl.when(i == 0)
        def _():
            carry[...] = jnp.zeros_like(carry)

        lf = _log_sigmoid(x_ref[...] + b_ref[...])
        hi, mid, lo = _split3(lf)
        tr_ = tri_ref[...]
        c = _nn(tr_, hi) + _nn(tr_, mid) + _nn(tr_, lo) + carry[...]
        o_ref[...] = c
        carry[...] = c[tb - 1:tb, :]

    return pl.pallas_call(
        body, name="cum_fwd", grid=(T // tb,),
        in_specs=[_rows(tb, HEAD, cb), _vec(HEAD), pl.BlockSpec((tb, tb), lambda i: (0, 0))],
        out_specs=_rows(tb, HEAD),
        out_shape=jax.ShapeDtypeStruct((T, HEAD), F32),
        scratch_shapes=[pltpu.VMEM((1, HEAD), F32)],
        compiler_params=_cparams(("arbitrary",)),
    )(rest, bias, tri)


def _cum_bwd(dcum, rest, bias, dm, tb=256):
    T = dm.T
    tb = min(tb, T)
    nb = T // tb
    cb = 2 * dm.D // HEAD
    tri = _tri(tb)

    def body(d_ref, x_ref, b_ref, tri_ref, o_ref, db_ref, carry):
        i = pl.program_id(0)

        @pl.when(i == 0)
        def _():
            carry[...] = jnp.zeros_like(carry)
            db_ref[...] = jnp.zeros_like(db_ref)

        hi, mid, lo = _split3(d_ref[...])
        tr_ = tri_ref[...]
        dlf = _tn(tr_, hi) + _tn(tr_, mid) + _tn(tr_, lo) + carry[...]
        carry[...] = dlf[0:1, :]
        dx = dlf * _sigmoid(-(x_ref[...] + b_ref[...]))
        o_ref[...] = dx.astype(BF16)
        db_ref[...] += _colsum(dx)

    return pl.pallas_call(
        body, name="cum_bwd", grid=(nb,),
        in_specs=[pl.BlockSpec((tb, HEAD), lambda i: (nb - 1 - i, 0)),
                  pl.BlockSpec((tb, HEAD), lambda i: (nb - 1 - i, cb)),
                  _vec(HEAD), pl.BlockSpec((tb, tb), lambda i: (0, 0))],
        out_specs=[pl.BlockSpec((tb, HEAD), lambda i: (nb - 1 - i, 0)), _vec(HEAD)],
        out_shape=[jax.ShapeDtypeStruct((T, HEAD), BF16), jax.ShapeDtypeStruct((1, HEAD), F32)],
        scratch_shapes=[pltpu.VMEM((1, HEAD), F32)],
        compiler_params=_cparams(("arbitrary",)),
    )(dcum, rest, bias, tri)


def _fox_scores(q_ref, k_ref, ck_ref, on_diagonal):
    qs = (q_ref[...] * (HEAD ** -0.5)).astype(BF16)
    s = _nt(qs, k_ref[...].astype(BF16)) - ck_ref[0]
    if on_diagonal:
        row = lax.broadcasted_iota(jnp.int32, s.shape, 0)
        colp = lax.broadcasted_iota(jnp.int32, s.shape, 1)
        s = jnp.where(row >= colp, s, NEG)
    return s, qs


def _fox_pairs(nq, kv_major):
    if kv_major:
        pairs = [(i, j) for j in range(nq) for i in range(j, nq)]
    else:
        pairs = [(i, j) for i in range(nq) for j in range(i + 1)]
    qi = jnp.asarray(np.array([p[0] for p in pairs], np.int32))
    kj = jnp.asarray(np.array([p[1] for p in pairs], np.int32))
    return qi, kj


def _fox_fwd(main, cum_row, dm, tq=512):
    T, H = dm.T, dm.H
    tq = min(tq, T)
    nq = T // tq
    qi_tab, kj_tab = _fox_pairs(nq, kv_major=False)

    def body(qi_ref, kj_ref, q_ref, k_ref, v_ref, ck_ref, o_ref, lse_ref, m_ref, l_ref, acc_ref):
        t = pl.program_id(1)
        i = qi_ref[t]
        j = kj_ref[t]

        @pl.when(j == 0)
        def _():
            m_ref[...] = jnp.full_like(m_ref, NEG)
            l_ref[...] = jnp.zeros_like(l_ref)
            acc_ref[...] = jnp.zeros_like(acc_ref)

        def step(on_diagonal):
            s, _ = _fox_scores(q_ref, k_ref, ck_ref, on_diagonal)
            m_old = m_ref[...]
            m_new = jnp.maximum(m_old, jnp.max(s, axis=-1, keepdims=True))
            a = jnp.exp(m_old - m_new)
            p = jnp.exp(s - m_new)
            l_ref[...] = a * l_ref[...] + jnp.sum(p, axis=-1, keepdims=True)
            acc_ref[...] = a * acc_ref[...] + _nn(p.astype(BF16), v_ref[...].astype(BF16))
            m_ref[...] = m_new

        @pl.when(j < i)
        def _():
            step(False)

        @pl.when(j == i)
        def _():
            step(True)
            l = l_ref[...]
            o_ref[...] = acc_ref[...] / l
            lse_ref[0] = m_ref[...] + jnp.log(l)

    return pl.pallas_call(
        body, name="fox_fwd",
        grid_spec=pltpu.PrefetchScalarGridSpec(
            num_scalar_prefetch=2, grid=(H, int(qi_tab.shape[0])),
            in_specs=[pl.BlockSpec((tq, HEAD), lambda h, t, qi, kj: (qi[t], 4 * H + h)),
                      pl.BlockSpec((tq, HEAD), lambda h, t, qi, kj: (kj[t], 5 * H + h)),
                      pl.BlockSpec((tq, HEAD), lambda h, t, qi, kj: (kj[t], 6 * H + h)),
                      pl.BlockSpec((1, 1, tq), lambda h, t, qi, kj: (h, 0, kj[t]))],
            out_specs=[pl.BlockSpec((tq, HEAD), lambda h, t, qi, kj: (qi[t], h)),
                       pl.BlockSpec((1, tq, 1), lambda h, t, qi, kj: (h, qi[t], 0))],
            scratch_shapes=[pltpu.VMEM((tq, 1), F32), pltpu.VMEM((tq, 1), F32), pltpu.VMEM((tq, HEAD), F32)]),
        out_shape=[jax.ShapeDtypeStruct((T, dm.AW), F32), jax.ShapeDtypeStruct((H, T, 1), F32)],
        compiler_params=_cparams(("parallel", "arbitrary")),
    )(qi_tab, kj_tab, main, main, main, cum_row)


def _fox_bwd(main, cum_row, o_saved, lse, do, dm, tq=512):
    T, H = dm.T, dm.H
    tq = min(tq, T)
    nq = T // tq
    qi_tab, kj_tab = _fox_pairs(nq, kv_major=True)

    def body(qi_ref, kj_ref, q_ref, k_ref, v_ref, ck_ref, o_ref, lse_ref, do_ref,
             dq_ref, dk_ref, dv_ref, dc_ref, dr_ref, dk_acc, dv_acc, dc_acc):
        t = pl.program_id(1)
        i = qi_ref[t]
        kj = kj_ref[t]

        @pl.when(t == 0)
        def _():
            dq_ref[...] = jnp.zeros_like(dq_ref)
            dr_ref[...] = jnp.zeros_like(dr_ref)

        @pl.when(i == kj)
        def _():
            dk_acc[...] = jnp.zeros_like(dk_acc)
            dv_acc[...] = jnp.zeros_like(dv_acc)
            dc_acc[...] = jnp.zeros_like(dc_acc)

        def step(on_diagonal):
            s, qs = _fox_scores(q_ref, k_ref, ck_ref, on_diagonal)
            kb = k_ref[...].astype(BF16)
            dov = do_ref[...]
            dob = dov.astype(BF16)
            p = jnp.exp(s - lse_ref[0])
            dp = _nt(dob, v_ref[...].astype(BF16))
            delta = jnp.sum(dov * o_ref[...], axis=-1, keepdims=True)
            ds = p * (dp - delta)
            dsb = ds.astype(BF16)
            dv_acc[...] += _tn(p.astype(BF16), dob)
            dk_acc[...] += _tn(dsb, qs)
            rows = pl.ds(pl.multiple_of(i * tq, tq), tq)
            dq_ref[rows, :] += _nn(dsb, kb) * (HEAD ** -0.5)
            dr_ref[0, rows, :] += jnp.sum(ds, axis=-1, keepdims=True)
            dc_acc[...] -= _colsum(ds)

        @pl.when(i == kj)
        def _():
            step(True)

        @pl.when(i > kj)
        def _():
            step(False)

        @pl.when(i == nq - 1)
        def _():
            dk_ref[...] = dk_acc[...].astype(BF16)
            dv_ref[...] = dv_acc[...].astype(BF16)
            dc_ref[0] = dc_acc[...]

    return pl.pallas_call(
        body, name="fox_bwd",
        grid_spec=pltpu.PrefetchScalarGridSpec(
            num_scalar_prefetch=2, grid=(H, int(qi_tab.shape[0])),
            in_specs=[pl.BlockSpec((tq, HEAD), lambda h, t, qi, kj: (qi[t], 4 * H + h)),
                      pl.BlockSpec((tq, HEAD), lambda h, t, qi, kj: (kj[t], 5 * H + h)),
                      pl.BlockSpec((tq, HEAD), lambda h, t, qi, kj: (kj[t], 6 * H + h)),
                      pl.BlockSpec((1, 1, tq), lambda h, t, qi, kj: (h, 0, kj[t])),
                      pl.BlockSpec((tq, HEAD), lambda h, t, qi, kj: (qi[t], h)),
                      pl.BlockSpec((1, tq, 1), lambda h, t, qi, kj: (h, qi[t], 0)),
                      pl.BlockSpec((tq, HEAD), lambda h, t, qi, kj: (qi[t], h))],
            out_specs=[pl.BlockSpec((T, HEAD), lambda h, t, qi, kj: (0, h)),
                       pl.BlockSpec((tq, HEAD), lambda h, t, qi, kj: (kj[t], h)),
                       pl.BlockSpec((tq, HEAD), lambda h, t, qi, kj: (kj[t], h)),
                       pl.BlockSpec((1, 1, tq), lambda h, t, qi, kj: (h, 0, kj[t])),
                       pl.BlockSpec((1, T, 1), lambda h, t, qi, kj: (h, 0, 0))],
            scratch_shapes=[pltpu.VMEM((tq, HEAD), F32), pltpu.VMEM((tq, HEAD), F32), pltpu.VMEM((1, tq), F32)]),
        out_shape=[jax.ShapeDtypeStruct((T, dm.AW), F32), jax.ShapeDtypeStruct((T, dm.AW), BF16),
                   jax.ShapeDtypeStruct((T, dm.AW), BF16), jax.ShapeDtypeStruct((H, 1, T), F32),
                   jax.ShapeDtypeStruct((H, T, 1), F32)],
        compiler_params=_cparams(("parallel", "arbitrary")),
    )(qi_tab, kj_tab, main, main, main, cum_row, o_saved, lse, do)


def _local_step(dm, x, tgt, w_main, w_rest, w_up_a, w_up_b, w_o, w_ffn_in, w_ffn_down,
                bias_p, lb_logits, norm_w, g1, g2, g3, g4):
    T, D, H = dm.T, dm.D, dm.H
    h1, r1 = _rms_fwd(x, g1)
    main = _mm(h1, w_main, "nn", F32, "proj_main")
    rest = _mm(h1, w_rest, "nn", F32, "proj_rest", tn=1408)
    ya, o_a, states = _hgrn_fwd(main, lb_logits, norm_w, dm)
    cum = _cum_fwd(rest, bias_p, dm)
    cum_row = cum[:, :H].T.reshape(H, 1, T)
    o_b, lse = _fox_fwd(main, cum_row, dm)
    y_a = _mm(ya, w_up_a, "nn", F32, "up_a")
    y_b = _mm(o_b, w_up_b, "nn", F32, "up_b")
    merged = _merge_fwd(rest, y_a, y_b)
    u = _mm(merged, w_o, "nn", F32, "w_o")
    x1, r2, h3, r3 = _post_pre(x, u, g2, g3)
    gu = _mm(h3, w_ffn_in, "nn", F32, "ffn_in")
    act = _swiglu_fwd(gu)
    w = _mm(act, w_ffn_down, "nn", F32, "ffn_down")
    loss, dy, dw, dg4 = _loss_bwd(x1, w, g4, tgt)
    dact = _mm(dw, w_ffn_down, "nt", BF16, "d_act", tn=1408)
    d_ffn_down = _mm(act, dw, "tn", F32, "dw_ffn_down", tm=1408)
    dgu = _swiglu_bwd(gu, dact)
    dh3 = _mm(dgu, w_ffn_in, "nt", F32, "d_h3")
    d_ffn_in = _mm(h3, dgu, "tn", F32, "dw_ffn_in")
    dx1, du, dg3, dg2 = _rms_bwd2(dy, dh3, x1, r3, g3, u, r2, g2)
    dmerged = _mm(du, w_o, "nt", F32, "d_merged")
    d_w_o = _mm(merged, du, "tn", F32, "dw_o")
    dy_a, dy_b, dgates = _merge_bwd(dmerged, rest, y_a, y_b)
    dya = _mm(dy_a, w_up_a, "nt", F32, "d_ya")
    d_up_a = _mm(ya, dy_a, "tn", F32, "dw_up_a")
    dob = _mm(dy_b, w_up_b, "nt", F32, "d_ob")
    d_up_b = _mm(o_b, dy_b, "tn", F32, "dw_up_b")
    d_a, dlb, dnw_h = _hgrn_bwd(main, lb_logits, norm_w, o_a, states, dya, dm)
    dq_b, dk_b, dv_b, dcum_row, dcum_col = _fox_bwd(main, cum_row, o_b, lse, dob, dm)
    dcum = jnp.pad((dcum_row.reshape(H, T) + dcum_col.reshape(H, T)).T, ((0, 0), (0, HEAD - H)))
    dbf, dbias = _cum_bwd(dcum, rest, bias_p, dm)
    dmain = jnp.concatenate([d_a, dq_b.astype(BF16), dk_b, dv_b], axis=1)
    drest = jnp.concatenate([dgates, dbf], axis=1)
    dh1 = _mm(dmain, w_main, "nt", F32, "d_h1_main")
    dh1 = _mm(drest, w_rest, "nt", F32, "d_h1_rest", add=dh1)
    d_main = _mm(h1, dmain, "tn", F32, "dw_main")
    d_rest = _mm(h1, drest, "tn", F32, "dw_rest", tn=1408)
    dx, dg1 = _rms_bwd1(dx1, dh1, x, r1, g1)
    big = dict(main=d_main, rest=d_rest, up_a=d_up_a, up_b=d_up_b, w_o=d_w_o, ffn_in=d_ffn_in, ffn_down=d_ffn_down)
    small = dict(loss=loss, bias=dbias, norm_w=jnp.sum(dnw_h, axis=0), lb=dlb, g1=dg1, g2=dg2, g3=dg3, g4=dg4)
    return dx, big, small


HBM = pl.BlockSpec(memory_space=pltpu.HBM)


def _place():
    x, y, c = lax.axis_index("x"), lax.axis_index("y"), lax.axis_index("c")
    chips = [(1 - x, y), (x, 1 - y), (1 - x, 1 - y)]
    return x, y, c, chips


class Sharded(NamedTuple):
    kind: str
    r: int
    c: int

    @property
    def full(self):
        return (self.r, 4 * self.c) if self.kind == "col" else (4 * self.r, self.c)

    @property
    def half(self):
        return (self.r // 2, self.c) if self.kind == "col" else (self.r, self.c // 2)

    @property
    def half_of_full(self):
        return (self.r // 2, 4 * self.c) if self.kind == "col" else (4 * self.r, self.c // 2)

    def shard_window(self, ref, s):
        if self.kind == "col":
            return ref.at[:, pl.ds(pl.multiple_of(s * self.c, 128), self.c)]
        return ref.at[pl.ds(pl.multiple_of(s * self.r, 16), self.r), :]

    def half_window(self, ref, s, h):
        if self.kind == "col":
            return ref.at[pl.ds(pl.multiple_of(h * (self.r // 2), 16), self.r // 2),
                          pl.ds(pl.multiple_of(s * self.c, 128), self.c)]
        return ref.at[pl.ds(pl.multiple_of(s * self.r, 16), self.r),
                      pl.ds(pl.multiple_of(h * (self.c // 2), 128), self.c // 2)]

    def half_of(self, ref, h):
        if self.kind == "col":
            n = ref.shape[0] // 2
            return ref.at[pl.ds(pl.multiple_of(h * n, 16), n), :]
        n = ref.shape[1] // 2
        return ref.at[:, pl.ds(pl.multiple_of(h * n, 128), n)]

    def window_of_half(self, ref, s):
        if self.kind == "col":
            return ref.at[:, pl.ds(pl.multiple_of(s * self.c, 128), self.c)]
        return ref.at[pl.ds(pl.multiple_of(s * self.r, 16), self.r), :]


def _gather_weights(shards, specs):
    n = len(shards)

    def body(*refs):
        w_refs, f_refs = refs[:n], refs[n:2 * n]
        send_sems, recv_sems, local_sems = refs[2 * n:]
        x, y, c, chips = _place()
        q = 2 * x + y
        sibling = (x, y, 1 - c)
        own, first, passed = [], [], []

        def copy(t, k, chip, half, to, src=None):
            dst = specs[t].half_window(f_refs[t], 2 * chip[0] + chip[1], half)
            return pltpu.make_async_remote_copy(
                src_ref=dst if src is None else src, dst_ref=dst,
                send_sem=send_sems.at[6 * t + k], recv_sem=recv_sems.at[6 * t + k], device_id=to, device_id_type=MESH)

        for t in range(n):
            own.append(pltpu.make_async_copy(w_refs[t], specs[t].shard_window(f_refs[t], q), local_sems.at[t]))
            own[-1].start()
            for j, chip in enumerate(chips):
                first.append(copy(t, j, (x, y), c, (*chip, c), src=specs[t].half_of(w_refs[t], c)))
                first[-1].start()
        for t in range(n):
            for j, chip in enumerate(chips):
                copy(t, j, chip, c, sibling).wait_recv()
                passed.append(copy(t, 3 + j, chip, c, sibling))
                passed[-1].start()
        for t in range(n):
            for j, chip in enumerate(chips):
                copy(t, 3 + j, chip, 1 - c, sibling).wait_recv()
        for cp in first + passed:
            cp.wait_send()
        for cp in own:
            cp.wait()

    return pl.pallas_call(
        body, name="gather_weights",
        in_specs=[HBM] * n, out_specs=[HBM] * n,
        out_shape=[jax.ShapeDtypeStruct(sp.full, w.dtype) for sp, w in zip(specs, shards)],
        scratch_shapes=[pltpu.SemaphoreType.DMA((6 * n,)), pltpu.SemaphoreType.DMA((6 * n,)),
                        pltpu.SemaphoreType.DMA((n,))],
    )(*shards)


def _swap_halves(fulls, specs):
    n = len(fulls)

    def body(*refs):
        g_refs, o_refs = refs[:n], refs[n:2 * n]
        send_sems, recv_sems = refs[2 * n:]
        x, y, c, _ = _place()
        cps = [pltpu.make_async_remote_copy(
            src_ref=specs[t].half_of(g_refs[t], 1 - c), dst_ref=o_refs[t],
            send_sem=send_sems.at[t], recv_sem=recv_sems.at[t],
            device_id=(x, y, 1 - c), device_id_type=MESH) for t in range(n)]
        for cp in cps:
            cp.start()
        for cp in cps:
            cp.wait_recv()
        for cp in cps:
            cp.wait_send()

    return pl.pallas_call(
        body, name="swap_halves",
        in_specs=[HBM] * n, out_specs=[HBM] * n,
        out_shape=[jax.ShapeDtypeStruct(sp.half_of_full, g.dtype) for sp, g in zip(specs, fulls)],
        scratch_shapes=[pltpu.SemaphoreType.DMA((n,)), pltpu.SemaphoreType.DMA((n,))],
    )(*fulls)


def _rtile(n, target):
    t = min(n, (target // 16) * 16)
    while n % t:
        t -= 16
    return t


def _add_sibling(full, got, sp, cq_idx, name):
    hr, hc = sp.half
    tr = _rtile(hr, 256)
    nrt = hr // tr

    def body(cq_ref, g_ref, r_ref, ob_ref, of_ref):
        s = pl.program_id(1)
        v = g_ref[...] + r_ref[...]
        ob_ref[...] = v.astype(BF16)

        @pl.when(s == cq_ref[1])
        def _():
            of_ref[...] = v

    if sp.kind == "col":
        g_spec = pl.BlockSpec((tr, hc), lambda i, s, cq: (cq[0] * nrt + i, s))
        r_spec = pl.BlockSpec((tr, hc), lambda i, s, cq: (i, s))
    else:
        g_spec = pl.BlockSpec((tr, hc), lambda i, s, cq: (s * nrt + i, cq[0]))
        r_spec = pl.BlockSpec((tr, hc), lambda i, s, cq: (s * nrt + i, 0))
    return pl.pallas_call(
        body, name=name,
        grid_spec=pltpu.PrefetchScalarGridSpec(
            num_scalar_prefetch=1, grid=(nrt, 4),
            in_specs=[g_spec, r_spec],
            out_specs=[r_spec, pl.BlockSpec((tr, hc), lambda i, s, cq: (i, 0))]),
        out_shape=[jax.ShapeDtypeStruct(sp.half_of_full, BF16), jax.ShapeDtypeStruct(sp.half, F32)],
        compiler_params=_cparams(("parallel", "arbitrary")),
    )(cq_idx, full, got)


def _scatter_chips(sums, specs):
    n = len(sums)

    def body(*refs):
        a_refs, o_refs = refs[:n], refs[n:2 * n]
        send_sems, recv_sems = refs[2 * n:]
        x, y, c, chips = _place()
        cps = [pltpu.make_async_remote_copy(
            src_ref=specs[t].window_of_half(a_refs[t], 2 * chip[0] + chip[1]), dst_ref=o_refs[t].at[j],
            send_sem=send_sems.at[3 * t + j], recv_sem=recv_sems.at[3 * t + j],
            device_id=(*chip, c), device_id_type=MESH) for t in range(n) for j, chip in enumerate(chips)]
        for cp in cps:
            cp.start()
        for cp in cps:
            cp.wait_recv()
        for cp in cps:
            cp.wait_send()

    return pl.pallas_call(
        body, name="scatter_chips",
        in_specs=[HBM] * n, out_specs=[HBM] * n,
        out_shape=[jax.ShapeDtypeStruct((3,) + sp.half, a.dtype) for sp, a in zip(specs, sums)],
        scratch_shapes=[pltpu.SemaphoreType.DMA((3 * n,)), pltpu.SemaphoreType.DMA((3 * n,))],
    )(*sums)


def _add_chips(own, got, name):
    hr, hc = own.shape
    tr = _rtile(hr, 256)

    def body(a_ref, r_ref, o_ref):
        o_ref[...] = ((a_ref[...] + r_ref[0].astype(F32)) + r_ref[1].astype(F32)) + r_ref[2].astype(F32)

    return pl.pallas_call(
        body, name=name, grid=(hr // tr,),
        in_specs=[pl.BlockSpec((tr, hc), lambda i: (i, 0)), pl.BlockSpec((3, tr, hc), lambda i: (0, i, 0))],
        out_specs=pl.BlockSpec((tr, hc), lambda i: (i, 0)),
        out_shape=jax.ShapeDtypeStruct((hr, hc), F32),
        compiler_params=_cparams(("parallel",)),
    )(own, got)


def _share_halves(halves, specs):
    n = len(halves)

    def body(*refs):
        r_refs, o_refs = refs[:n], refs[n:2 * n]
        send_sems, recv_sems, local_sems = refs[2 * n:]
        x, y, c, _ = _place()
        sibling = (x, y, 1 - c)
        own, cps = [], []
        for t in range(n):
            own.append(pltpu.make_async_copy(r_refs[t], specs[t].half_of(o_refs[t], c), local_sems.at[t]))
            own[-1].start()
            cps.append(pltpu.make_async_remote_copy(
                src_ref=r_refs[t], dst_ref=specs[t].half_of(o_refs[t], c),
                send_sem=send_sems.at[t], recv_sem=recv_sems.at[t], device_id=sibling, device_id_type=MESH))
            cps[-1].start()
        for t in range(n):
            pltpu.make_async_remote_copy(
                src_ref=r_refs[t], dst_ref=specs[t].half_of(o_refs[t], 1 - c),
                send_sem=send_sems.at[t], recv_sem=recv_sems.at[t], device_id=sibling, device_id_type=MESH).wait_recv()
        for cp in cps:
            cp.wait_send()
        for cp in own:
            cp.wait()

    return pl.pallas_call(
        body, name="share_halves",
        in_specs=[HBM] * n, out_specs=[HBM] * n,
        out_shape=[jax.ShapeDtypeStruct((sp.r, sp.c), F32) for sp in specs],
        scratch_shapes=[pltpu.SemaphoreType.DMA((n,)), pltpu.SemaphoreType.DMA((n,)), pltpu.SemaphoreType.DMA((n,))],
    )(*halves)


def _sum_small(vec):
    rows, w = vec.shape

    def body(v_ref, o_ref, buf, send_sems, recv_sems):
        x, y, c, _ = _place()
        me = 4 * x + 2 * y + c
        buf[me] = v_ref[...]
        cps = []
        for k in range(1, 8):
            to = (x ^ (k >> 2), y ^ ((k >> 1) & 1), c ^ (k & 1))
            cps.append(pltpu.make_async_remote_copy(
                src_ref=v_ref, dst_ref=buf.at[me], send_sem=send_sems.at[k - 1], recv_sem=recv_sems.at[k - 1],
                device_id=to, device_id_type=MESH))
        for cp in cps:
            cp.start()
        for k in range(1, 8):
            pltpu.make_async_remote_copy(
                src_ref=v_ref, dst_ref=buf.at[me ^ k], send_sem=send_sems.at[k - 1], recv_sem=recv_sems.at[k - 1],
                device_id=(x, y, c), device_id_type=MESH).wait_recv()
        for cp in cps:
            cp.wait_send()
        total = buf[0]
        for d in range(1, 8):
            total = total + buf[d]
        o_ref[...] = total

    return pl.pallas_call(
        body, name="sum_small",
        in_specs=[pl.BlockSpec(memory_space=pltpu.VMEM)], out_specs=pl.BlockSpec(memory_space=pltpu.VMEM),
        out_shape=jax.ShapeDtypeStruct((rows, w), F32),
        scratch_shapes=[pltpu.VMEM((8, rows, w), F32), pltpu.SemaphoreType.DMA((7,)), pltpu.SemaphoreType.DMA((7,))],
    )(vec)


def _adam_math(w, g, m, v):
    m = ADAM_B1 * m + (1.0 - ADAM_B1) * g
    v = ADAM_B2 * v + (1.0 - ADAM_B2) * (g * g)
    m_hat = m / (1.0 - ADAM_B1 ** ADAM_STEP)
    v_hat = v / (1.0 - ADAM_B2 ** ADAM_STEP)
    delta = -ADAM_LR * (m_hat / (jnp.sqrt(v_hat) + ADAM_EPS) + ADAM_WD * w)
    return delta, m, v


def _adamw(w, g, m, v, name, tr=128):
    R, Cn = w.shape
    tr = min(tr, R)
    assert R % tr == 0

    def body(w_ref, g_ref, m_ref, v_ref, d_ref, mo_ref, vo_ref):
        d, mn, vn = _adam_math(w_ref[...], g_ref[...], m_ref[...], v_ref[...])
        d_ref[...] = d
        mo_ref[...] = mn
        vo_ref[...] = vn

    blk = pl.BlockSpec((tr, Cn), lambda i: (i, 0))
    sds = jax.ShapeDtypeStruct((R, Cn), F32)
    return pl.pallas_call(
        body, name=name, grid=(R // tr,),
        in_specs=[blk] * 4, out_specs=[blk] * 3, out_shape=[sds] * 3,
        compiler_params=_cparams(("parallel",)),
    )(w, g, m, v)


ROW_LOSS, ROW_BIAS, ROW_NORM, ROW_LB0, ROW_G1, ROW_G2, ROW_G3, ROW_G4, ROW_LB1 = range(9)
SMALL_ROWS = 16


def _small_update(gsum, wp, mp, vp):
    _, w = gsum.shape

    def body(g_ref, w_ref, m_ref, v_ref, go_ref, d_ref, mo_ref, vo_ref):
        wv = w_ref[...]
        l0 = wv[ROW_LB0:ROW_LB0 + 1, :]
        l1 = wv[ROW_LB1:ROW_LB1 + 1, :]
        mx = jnp.maximum(l0, l1)
        e0 = jnp.exp(l0 - mx)
        e1 = jnp.exp(l1 - mx)
        p0 = e0 / (e0 + e1)
        gs = g_ref[...]
        dl0 = gs[ROW_LB0:ROW_LB0 + 1, :] * p0 * (1.0 - p0)
        row8 = lax.broadcasted_iota(jnp.int32, gs.shape, 0)
        top = jnp.where(row8 == ROW_LB0, dl0, jnp.where(row8 == ROW_LOSS, 0.0, gs))
        bot = jnp.where(row8 == ROW_LB1 - 8, -dl0, 0.0)
        g16 = jnp.concatenate([top, bot], axis=0)
        d, mn, vn = _adam_math(wv, g16, m_ref[...], v_ref[...])
        go_ref[...] = g16
        d_ref[...] = d
        mo_ref[...] = mn
        vo_ref[...] = vn

    sds = jax.ShapeDtypeStruct((SMALL_ROWS, w), F32)
    full = pl.BlockSpec(memory_space=pltpu.VMEM)
    return pl.pallas_call(
        body, name="small_update", in_specs=[full] * 4, out_specs=[full] * 4, out_shape=[sds] * 4,
    )(gsum, wp, mp, vp)


def _w_in_layout(dm):
    cs = dm.NIN // 4
    place = [((cs * q) // 128, (cs * q) % 128) for q in range(4)]
    cp = -(-(cs + max(sh for _, sh in place)) // 128) * 128
    return cs, cp, place


def _zeros_cols(rows, n, dtype):
    return jnp.zeros((rows, n), dtype)


def _unshuffle_w_in(wp, dm, tr=128):
    D, H = dm.D, dm.H
    cs, cp, place = _w_in_layout(dm)
    nm, nrest = dm.NMAIN, dm.NREST
    ng = nm + nrest
    tr = min(tr, D)

    def body(x_ref, main_ref, rest_ref):
        g = None
        for q, (t0, sh) in enumerate(place):
            xq = x_ref[:, q * cp:(q + 1) * cp]
            yq = pltpu.roll(xq, sh, axis=1) if sh else xq
            width = min(cp, ng - t0 * 128)
            parts = [_zeros_cols(tr, t0 * 128, wp.dtype)] if t0 else []
            parts.append(yq[:, :width])
            if ng - t0 * 128 - width:
                parts.append(_zeros_cols(tr, ng - t0 * 128 - width, wp.dtype))
            placed = jnp.concatenate(parts, axis=1)
            g = placed if g is None else g + placed
        main_ref[...] = g[:, :nm]
        tail = g[:, nm:]
        gates = pltpu.roll(tail, nrest - H, axis=1)[:, :2 * D]
        lane = lax.broadcasted_iota(jnp.int32, (tr, HEAD), 1)
        rest_ref[...] = jnp.concatenate([gates, jnp.where(lane < H, tail[:, :HEAD], 0)], axis=1)

    return pl.pallas_call(
        body, name="unshuffle_w_in", grid=(D // tr,),
        in_specs=[pl.BlockSpec((tr, 4 * cp), lambda i: (i, 0))],
        out_specs=[pl.BlockSpec((tr, nm), lambda i: (i, 0)), pl.BlockSpec((tr, nrest), lambda i: (i, 0))],
        out_shape=[jax.ShapeDtypeStruct((D, nm), wp.dtype), jax.ShapeDtypeStruct((D, nrest), wp.dtype)],
        compiler_params=_cparams(("parallel",)),
    )(wp)


def _shuffle_w_in(d_main, d_rest, dm, tr=64):
    D, H = dm.D, dm.H
    cs, cp, place = _w_in_layout(dm)
    nm, nrest = dm.NMAIN, dm.NREST
    ng = nm + nrest
    tr = min(tr, D)

    def body(m_ref, r_ref, o_ref):
        rv = r_ref[...]
        lane = lax.broadcasted_iota(jnp.int32, (tr, HEAD), 1)
        tail = pltpu.roll(jnp.concatenate([rv[:, :2 * D], _zeros_cols(tr, HEAD, F32)], axis=1), H, axis=1)
        head = jnp.where(lane < H, rv[:, 2 * D:], 0.0) + tail[:, :HEAD]
        g = jnp.concatenate([m_ref[...], head, tail[:, HEAD:]], axis=1)
        lanes = lax.broadcasted_iota(jnp.int32, (tr, cp), 1)
        outs = []
        for q, (t0, sh) in enumerate(place):
            width = min(cp, ng - t0 * 128)
            win = g[:, t0 * 128:t0 * 128 + width]
            if width < cp:
                win = jnp.concatenate([win, _zeros_cols(tr, cp - width, F32)], axis=1)
            xq = pltpu.roll(win, cp - sh, axis=1) if sh else win
            outs.append(jnp.where(lanes < cs, xq, 0.0))
        o_ref[...] = jnp.concatenate(outs, axis=1)

    return pl.pallas_call(
        body, name="shuffle_w_in", grid=(D // tr,),
        in_specs=[pl.BlockSpec((tr, nm), lambda i: (i, 0)), pl.BlockSpec((tr, nrest), lambda i: (i, 0))],
        out_specs=pl.BlockSpec((tr, 4 * cp), lambda i: (i, 0)),
        out_shape=jax.ShapeDtypeStruct((D, 4 * cp), F32),
        compiler_params=_cparams(("parallel",)),
    )(d_main, d_rest)


def _pack_small(dm, bias, norm_w, lb_logits, g1, g2, g3, g4):
    D = dm.D
    row = lambda v: jnp.pad(v.reshape(1, -1), ((0, 0), (0, D - v.size)))
    rows = [jnp.zeros((1, D), F32), row(bias), row(norm_w), row(lb_logits[0]), row(g1), row(g2), row(g3), row(g4),
            row(lb_logits[1]), jnp.zeros((SMALL_ROWS - 9, D), F32)]
    return jnp.concatenate(rows, axis=0)


def _unpack_small(p, dm):
    H, AW = dm.H, dm.AW
    return (p[ROW_BIAS:ROW_BIAS + 1, :H], jnp.concatenate([p[ROW_LB0:ROW_LB0 + 1, :AW], p[ROW_LB1:ROW_LB1 + 1, :AW]], axis=0),
            p[ROW_NORM:ROW_NORM + 1, :HEAD], p[ROW_G1:ROW_G1 + 1], p[ROW_G2:ROW_G2 + 1], p[ROW_G3:ROW_G3 + 1],
            p[ROW_G4:ROW_G4 + 1])


def _step(dm, x, w_in, b_fox_f, hgrn_lb_logits, hgrn_norm_w, w_up_a, w_up_b, w_o, norm_mix_pre, norm_mix_post,
          norm_ffn_pre, norm_ffn_post, w_ffn_in, w_ffn_down, loss_target, moments_m, moments_v):
    xi, yi, ci = lax.axis_index("x"), lax.axis_index("y"), lax.axis_index("c")
    cq_idx = jnp.stack([ci, 2 * xi + yi]).astype(jnp.int32)
    D, AW, FF = dm.D, dm.AW, dm.FF
    cs, cp, _ = _w_in_layout(dm)
    big_names = ["w_in", "w_up_a", "w_up_b", "w_o", "w_ffn_in", "w_ffn_down"]
    specs = [Sharded("col", D, cp), Sharded("col", AW, D // 4), Sharded("col", AW, D // 4),
             Sharded("row", D // 4, D), Sharded("col", D, 2 * FF // 4), Sharded("row", FF // 4, D)]
    shards = [w_in[0], w_up_a[0], w_up_b[0], w_o[0], w_ffn_in[0], w_ffn_down[0]]

    sent = [jnp.pad(shards[0].astype(BF16), ((0, 0), (0, cp - cs)))] + [w.astype(BF16) for w in shards[1:]]
    f_in, f_up_a, f_up_b, f_o, f_ffn_in, f_ffn_down = _gather_weights(sent, specs)
    w_main, w_rest = _unshuffle_w_in(f_in, dm)

    bias_p = jnp.pad(b_fox_f, ((0, 0), (0, HEAD - dm.H)))
    dx, big, small = _local_step(
        dm, x[0], loss_target[0], w_main, w_rest, f_up_a, f_up_b, f_o, f_ffn_in, f_ffn_down, bias_p,
        hgrn_lb_logits, hgrn_norm_w, norm_mix_pre, norm_mix_post, norm_ffn_pre, norm_ffn_post)

    fulls = [_shuffle_w_in(big["main"], big["rest"], dm), big["up_a"], big["up_b"], big["w_o"], big["ffn_in"],
             big["ffn_down"]]
    from_sibling = _swap_halves(fulls, specs)
    pair = [_add_sibling(g, r, sp, cq_idx, "add_sibling_" + n)
            for g, r, sp, n in zip(fulls, from_sibling, specs, big_names)]
    from_chips = _scatter_chips([p[0] for p in pair], specs)
    halves = [_add_chips(p[1], r, "add_chips_" + n) for p, r, n in zip(pair, from_chips, big_names)]
    grads = list(_share_halves(halves, specs))
    grads[0] = grads[0][:, :cs]

    row = lambda v: jnp.pad(v.reshape(1, -1), ((0, 0), (0, D - v.size)))
    vec = jnp.concatenate([row(small["loss"][:, :1]), row(small["bias"][:, :dm.H]), row(small["norm_w"]),
                           row(small["lb"]), small["g1"], small["g2"], small["g3"], small["g4"]], axis=0)
    gsum = _sum_small(vec)
    loss = gsum[ROW_LOSS, 0]

    smalls = lambda t: (t["b_fox_f"], t["hgrn_norm_w"], t["hgrn_lb_logits"], t["norm_mix_pre"], t["norm_mix_post"],
                        t["norm_ffn_pre"], t["norm_ffn_post"])
    params = dict(b_fox_f=b_fox_f, hgrn_norm_w=hgrn_norm_w, hgrn_lb_logits=hgrn_lb_logits, norm_mix_pre=norm_mix_pre,
                  norm_mix_post=norm_mix_post, norm_ffn_pre=norm_ffn_pre, norm_ffn_post=norm_ffn_post)
    sg, sd, sm, sv = _small_update(gsum, _pack_small(dm, *smalls(params)), _pack_small(dm, *smalls(moments_m)),
                                   _pack_small(dm, *smalls(moments_v)))
    big_out = {}
    for name, wsh, g in zip(big_names, shards, grads):
        d, mn, vn = _adamw(wsh, g, moments_m[name][0], moments_v[name][0], "adamw_" + name)
        big_out[name] = (g[None], d[None], mn[None], vn[None])

    order = ["w_in", "b_fox_f", "hgrn_lb_logits", "hgrn_norm_w", "w_up_a", "w_up_b", "w_o", "norm_mix_pre",
             "norm_mix_post", "norm_ffn_pre", "norm_ffn_post", "w_ffn_in", "w_ffn_down"]
    outs = []
    for kind, packed in enumerate([sg, sd, sm, sv]):
        b, lbl, nw, p1, p2, p3, p4 = _unpack_small(packed, dm)
        sm_map = dict(b_fox_f=b, hgrn_lb_logits=lbl, hgrn_norm_w=nw, norm_mix_pre=p1, norm_mix_post=p2,
                      norm_ffn_pre=p3, norm_ffn_post=p4)
        outs.append([big_out[n][kind] if n in big_out else sm_map[n] for n in order])
    return (loss, dx[None], *outs[0], *outs[1], *outs[2], *outs[3])


def kernel(x, w_in, b_fox_f, hgrn_lb_logits, hgrn_norm_w, w_up_a, w_up_b, w_o, norm_mix_pre, norm_mix_post, norm_ffn_pre, norm_ffn_post, w_ffn_in, w_ffn_down, loss_target, m_w_in, m_b_fox_f, m_hgrn_lb_logits, m_hgrn_norm_w, m_w_up_a, m_w_up_b, m_w_o, m_norm_mix_pre, m_norm_mix_post, m_norm_ffn_pre, m_norm_ffn_post, m_w_ffn_in, m_w_ffn_down, v_w_in, v_b_fox_f, v_hgrn_lb_logits, v_hgrn_norm_w, v_w_up_a, v_w_up_b, v_w_o, v_norm_mix_pre, v_norm_mix_post, v_norm_ffn_pre, v_norm_ffn_post, v_w_ffn_in, v_w_ffn_down):
    dm = Dims(T=x.shape[1], D=x.shape[2], FF=w_ffn_down.shape[1] * 4)
    moments_m = dict(w_in=m_w_in, b_fox_f=m_b_fox_f, hgrn_lb_logits=m_hgrn_lb_logits, hgrn_norm_w=m_hgrn_norm_w,
                     w_up_a=m_w_up_a, w_up_b=m_w_up_b, w_o=m_w_o, norm_mix_pre=m_norm_mix_pre,
                     norm_mix_post=m_norm_mix_post, norm_ffn_pre=m_norm_ffn_pre, norm_ffn_post=m_norm_ffn_post,
                     w_ffn_in=m_w_ffn_in, w_ffn_down=m_w_ffn_down)
    moments_v = dict(w_in=v_w_in, b_fox_f=v_b_fox_f, hgrn_lb_logits=v_hgrn_lb_logits, hgrn_norm_w=v_hgrn_norm_w,
                     w_up_a=v_w_up_a, w_up_b=v_w_up_b, w_o=v_w_o, norm_mix_pre=v_norm_mix_pre,
                     norm_mix_post=v_norm_mix_post, norm_ffn_pre=v_norm_ffn_pre, norm_ffn_post=v_norm_ffn_post,
                     w_ffn_in=v_w_ffn_in, w_ffn_down=v_w_ffn_down)
    return _step(dm, x, w_in, b_fox_f, hgrn_lb_logits, hgrn_norm_w, w_up_a, w_up_b, w_o, norm_mix_pre, norm_mix_post,
                 norm_ffn_pre, norm_ffn_post, w_ffn_in, w_ffn_down, loss_target, moments_m, moments_v)
```

```python
from typing import NamedTuple

import numpy as np
import jax
import jax.numpy as jnp
from jax import lax
from jax.experimental import pallas as pl
from jax.experimental.pallas import tpu as pltpu

F32 = jnp.float32
BF16 = jnp.bfloat16
MESH = pl.DeviceIdType.MESH

RMS_EPS = 1e-6
HEAD = 128
CHUNK = 64
GROUP = 4
LEVELS = (32, 16, 8, 4, 2, 1)
NEG = -1e30

ADAM_LR = 0.001
ADAM_B1 = 0.9
ADAM_B2 = 0.999
ADAM_EPS = 1e-08
ADAM_WD = 0.01
ADAM_STEP = 10

VMEM_LIMIT = 56 * 1024 * 1024


class Dims(NamedTuple):
    T: int
    D: int
    FF: int

    @property
    def AW(self):
        return self.D // 2

    @property
    def H(self):
        return self.AW // HEAD

    @property
    def NMAIN(self):
        return 7 * self.AW

    @property
    def NREST(self):
        return 2 * self.D + HEAD

    @property
    def NIN(self):
        return 7 * self.AW + self.H + 2 * self.D


def _cparams(sem, vmem=VMEM_LIMIT, **kw):
    return pltpu.CompilerParams(dimension_semantics=sem, vmem_limit_bytes=vmem, **kw)


def _tile(n, target):
    if n <= target:
        return n
    t = (target // 128) * 128
    while t >= 128:
        if n % t == 0:
            return t
        t -= 128
    raise ValueError(f"no tile for {n}")


def _dot(a, b, dims):
    return lax.dot_general(a, b, (dims, ((), ())), preferred_element_type=F32)


def _nn(a, b):
    return _dot(a, b, ((1,), (0,)))


def _nt(a, b):
    return _dot(a, b, ((1,), (1,)))


def _tn(a, b):
    return _dot(a, b, ((0,), (0,)))


def _sigmoid(x):
    return jax.nn.sigmoid(x)


def _split2(x):
    hi = x.astype(BF16)
    lo = (x - hi.astype(F32)).astype(BF16)
    return hi, lo


def _split3(x):
    hi = x.astype(BF16)
    r = x - hi.astype(F32)
    mid = r.astype(BF16)
    lo = (r - mid.astype(F32)).astype(BF16)
    return hi, mid, lo


def _mm(a, b, mode, out_dtype, name, add=None, tm=1024, tn=1024, tk=None):
    if mode == "nn":
        (M, K), (K2, N) = a.shape, b.shape
    elif mode == "nt":
        (M, K), (N, K2) = a.shape, b.shape
    else:
        (K, M), (K2, N) = a.shape, b.shape
    assert K == K2, (a.shape, b.shape, mode)
    if tk is None:
        tk = 1024 if mode == "tn" else 2048
    tm, tn, tk = _tile(M, tm), _tile(N, tn), _tile(K, tk)
    nk = K // tk
    isz = lambda t: jnp.dtype(t.dtype).itemsize
    vmem = 2 * (tm * tk * isz(a) + tk * tn * isz(b) + tm * tn * jnp.dtype(out_dtype).itemsize)
    vmem += (tm * tn * 4 if nk > 1 else 0) + (2 * tm * tn * isz(add) if add is not None else 0)
    assert vmem <= VMEM_LIMIT - 8 * 1024 * 1024, (name, vmem)
    if mode == "nn":
        a_spec = pl.BlockSpec((tm, tk), lambda i, j, k: (i, k))
        b_spec = pl.BlockSpec((tk, tn), lambda i, j, k: (k, j))
        op = _nn
    elif mode == "nt":
        a_spec = pl.BlockSpec((tm, tk), lambda i, j, k: (i, k))
        b_spec = pl.BlockSpec((tn, tk), lambda i, j, k: (j, k))
        op = _nt
    else:
        a_spec = pl.BlockSpec((tk, tm), lambda i, j, k: (k, i))
        b_spec = pl.BlockSpec((tk, tn), lambda i, j, k: (k, j))
        op = _tn
    o_spec = pl.BlockSpec((tm, tn), lambda i, j, k: (i, j))
    has_add = add is not None

    def body(*refs):
        a_ref, b_ref = refs[:2]
        add_ref = refs[2] if has_add else None
        o_ref = refs[3] if has_add else refs[2]

        def finish(r):
            if has_add:
                r = r + add_ref[...].astype(F32)
            o_ref[...] = r.astype(out_dtype)

        part = op(a_ref[...].astype(BF16), b_ref[...].astype(BF16))
        if nk == 1:
            finish(part)
            return
        acc = refs[-1]
        k = pl.program_id(2)

        @pl.when(k == 0)
        def _():
            acc[...] = part

        @pl.when((k > 0) & (k < nk - 1))
        def _():
            acc[...] += part

        @pl.when(k == nk - 1)
        def _():
            finish(acc[...] + part)

    in_specs = [a_spec, b_spec] + ([o_spec] if has_add else [])
    args = (a, b) + ((add,) if has_add else ())
    return pl.pallas_call(
        body, name=name, grid=(M // tm, N // tn, nk),
        in_specs=in_specs, out_specs=o_spec,
        out_shape=jax.ShapeDtypeStruct((M, N), out_dtype),
        scratch_shapes=[pltpu.VMEM((tm, tn), F32)] if nk > 1 else [],
        compiler_params=_cparams(("parallel", "parallel", "arbitrary")),
    )(*args)


def _rows(tr, w, col=0):
    return pl.BlockSpec((tr, w), lambda i, *_: (i, col))


def _vec(w):
    return pl.BlockSpec((1, w), lambda i, *_: (0, 0))


def _rstd(v):
    return lax.rsqrt(jnp.mean(v * v, axis=-1, keepdims=True) + RMS_EPS)


def _rms_bwd(dn, n, r):
    return r * (dn - n * jnp.mean(dn * n, axis=-1, keepdims=True))


def _colsum(v):
    return jnp.sum(v, axis=0, keepdims=True)


def _rms_fwd(x, g, tr=256):
    T, D = x.shape

    def body(x_ref, g_ref, h_ref, r_ref):
        xv = x_ref[...]
        r = _rstd(xv)
        h_ref[...] = (xv * r * g_ref[...]).astype(BF16)
        r_ref[...] = r

    return pl.pallas_call(
        body, name="rms_fwd", grid=(T // tr,),
        in_specs=[_rows(tr, D), _vec(D)],
        out_specs=[_rows(tr, D), _rows(tr, 1)],
        out_shape=[jax.ShapeDtypeStruct((T, D), BF16), jax.ShapeDtypeStruct((T, 1), F32)],
        compiler_params=_cparams(("parallel",)),
    )(x, g)


def _merge_fwd(rest, y_a, y_b, tr=256):
    T, D = y_a.shape

    def body(ga_ref, gb_ref, ya_ref, yb_ref, o_ref):
        o_ref[...] = (_sigmoid(ga_ref[...]) * ya_ref[...] + _sigmoid(gb_ref[...]) * yb_ref[...]).astype(BF16)

    return pl.pallas_call(
        body, name="merge_fwd", grid=(T // tr,),
        in_specs=[_rows(tr, D, 0), _rows(tr, D, 1), _rows(tr, D), _rows(tr, D)],
        out_specs=_rows(tr, D),
        out_shape=jax.ShapeDtypeStruct((T, D), BF16),
        compiler_params=_cparams(("parallel",)),
    )(rest, rest, y_a, y_b)


def _post_pre(x, u, g2, g3, tr=256):
    T, D = x.shape

    def body(x_ref, u_ref, g2_ref, g3_ref, x1_ref, r2_ref, h3_ref, r3_ref):
        uv = u_ref[...]
        r2 = _rstd(uv)
        x1 = x_ref[...] + uv * r2 * g2_ref[...]
        r3 = _rstd(x1)
        x1_ref[...] = x1
        r2_ref[...] = r2
        h3_ref[...] = (x1 * r3 * g3_ref[...]).astype(BF16)
        r3_ref[...] = r3

    return pl.pallas_call(
        body, name="post_pre", grid=(T // tr,),
        in_specs=[_rows(tr, D), _rows(tr, D), _vec(D), _vec(D)],
        out_specs=[_rows(tr, D), _rows(tr, 1), _rows(tr, D), _rows(tr, 1)],
        out_shape=[jax.ShapeDtypeStruct((T, D), F32), jax.ShapeDtypeStruct((T, 1), F32),
                   jax.ShapeDtypeStruct((T, D), BF16), jax.ShapeDtypeStruct((T, 1), F32)],
        compiler_params=_cparams(("parallel",)),
    )(x, u, g2, g3)


def _swiglu_fwd(gu, tr=256):
    T, FF2 = gu.shape
    FF = FF2 // 2
    tc = _tile(FF, 1024)
    nc = FF // tc

    def body(g_ref, u_ref, o_ref):
        gv = g_ref[...]
        o_ref[...] = (gv * _sigmoid(gv) * u_ref[...]).astype(BF16)

    return pl.pallas_call(
        body, name="swiglu_fwd", grid=(T // tr, nc),
        in_specs=[pl.BlockSpec((tr, tc), lambda i, j: (i, j)),
                  pl.BlockSpec((tr, tc), lambda i, j: (i, j + nc))],
        out_specs=pl.BlockSpec((tr, tc), lambda i, j: (i, j)),
        out_shape=jax.ShapeDtypeStruct((T, FF), BF16),
        compiler_params=_cparams(("parallel", "parallel")),
    )(gu, gu)


def _loss_bwd(x1, w, g4, tgt, tr=256):
    T, D = x1.shape

    def body(x1_ref, w_ref, g4_ref, t_ref, loss_ref, dy_ref, dw_ref, dg_ref):
        i = pl.program_id(0)

        @pl.when(i == 0)
        def _():
            loss_ref[...] = jnp.zeros_like(loss_ref)
            dg_ref[...] = jnp.zeros_like(dg_ref)

        wv = w_ref[...]
        g4v = g4_ref[...]
        r4 = _rstd(wv)
        n4 = wv * r4
        e = x1_ref[...] + n4 * g4v - t_ref[...]
        loss_ref[...] += 0.5 * jnp.sum(jnp.mean(e * e, axis=-1, keepdims=True), axis=0, keepdims=True)
        dy = e * (1.0 / D)
        dy_ref[...] = dy
        dg_ref[...] += _colsum(dy * n4)
        dw_ref[...] = _rms_bwd(dy * g4v, n4, r4).astype(BF16)

    return pl.pallas_call(
        body, name="loss_bwd", grid=(T // tr,),
        in_specs=[_rows(tr, D), _rows(tr, D), _vec(D), _rows(tr, D)],
        out_specs=[_vec(HEAD), _rows(tr, D), _rows(tr, D), _vec(D)],
        out_shape=[jax.ShapeDtypeStruct((1, HEAD), F32), jax.ShapeDtypeStruct((T, D), F32),
                   jax.ShapeDtypeStruct((T, D), BF16), jax.ShapeDtypeStruct((1, D), F32)],
        compiler_params=_cparams(("arbitrary",)),
    )(x1, w, g4, tgt)


def _swiglu_bwd(gu, dact, tr=128):
    T, FF2 = gu.shape
    FF = FF2 // 2

    def body(g_ref, u_ref, d_ref, o_ref):
        h = pl.program_id(1)
        gv = g_ref[...]
        s = _sigmoid(gv)
        dv = d_ref[...].astype(F32)

        @pl.when(h == 0)
        def _():
            o_ref[...] = (dv * u_ref[...] * (s * (1.0 + gv * (1.0 - s)))).astype(BF16)

        @pl.when(h == 1)
        def _():
            o_ref[...] = (dv * (gv * s)).astype(BF16)

    return pl.pallas_call(
        body, name="swiglu_bwd", grid=(T // tr, 2),
        in_specs=[pl.BlockSpec((tr, FF), lambda i, h: (i, 0)),
                  pl.BlockSpec((tr, FF), lambda i, h: (i, 1)),
                  pl.BlockSpec((tr, FF), lambda i, h: (i, 0))],
        out_specs=pl.BlockSpec((tr, FF), lambda i, h: (i, h)),
        out_shape=jax.ShapeDtypeStruct((T, FF2), BF16),
        compiler_params=_cparams(("parallel", "arbitrary")),
    )(gu, gu, dact)


def _rms_bwd2(dy, dh3, x1, r3, g3, u, r2, g2, tr=256):
    T, D = dy.shape

    def body(dy_ref, dh_ref, x1_ref, r3_ref, g3_ref, u_ref, r2_ref, g2_ref, dx1_ref, du_ref, dg3_ref, dg2_ref):
        i = pl.program_id(0)

        @pl.when(i == 0)
        def _():
            dg3_ref[...] = jnp.zeros_like(dg3_ref)
            dg2_ref[...] = jnp.zeros_like(dg2_ref)

        r3v, r2v = r3_ref[...], r2_ref[...]
        dh = dh_ref[...]
        n3 = x1_ref[...] * r3v
        dg3_ref[...] += _colsum(dh * n3)
        dx1 = dy_ref[...] + _rms_bwd(dh * g3_ref[...], n3, r3v)
        dx1_ref[...] = dx1
        n2 = u_ref[...] * r2v
        dg2_ref[...] += _colsum(dx1 * n2)
        du_ref[...] = _rms_bwd(dx1 * g2_ref[...], n2, r2v).astype(BF16)

    return pl.pallas_call(
        body, name="rms_bwd2", grid=(T // tr,),
        in_specs=[_rows(tr, D), _rows(tr, D), _rows(tr, D), _rows(tr, 1), _vec(D),
                  _rows(tr, D), _rows(tr, 1), _vec(D)],
        out_specs=[_rows(tr, D), _rows(tr, D), _vec(D), _vec(D)],
        out_shape=[jax.ShapeDtypeStruct((T, D), F32), jax.ShapeDtypeStruct((T, D), BF16),
                   jax.ShapeDtypeStruct((1, D), F32), jax.ShapeDtypeStruct((1, D), F32)],
        compiler_params=_cparams(("arbitrary",)),
    )(dy, dh3, x1, r3, g3, u, r2, g2)


def _merge_bwd(dmerged, rest, y_a, y_b, tr=256):
    T, D = dmerged.shape

    def body(dm_ref, ga_ref, gb_ref, ya_ref, yb_ref, dya_ref, dyb_ref, dg_ref):
        h = pl.program_id(1)
        dm = dm_ref[...]

        @pl.when(h == 0)
        def _():
            s = _sigmoid(ga_ref[...])
            dya_ref[...] = (dm * s).astype(BF16)
            dg_ref[...] = (dm * ya_ref[...] * s * (1.0 - s)).astype(BF16)

        @pl.when(h == 1)
        def _():
            s = _sigmoid(gb_ref[...])
            dyb_ref[...] = (dm * s).astype(BF16)
            dg_ref[...] = (dm * yb_ref[...] * s * (1.0 - s)).astype(BF16)

    blk = lambda col: pl.BlockSpec((tr, D), lambda i, h: (i, col))
    return pl.pallas_call(
        body, name="merge_bwd", grid=(T // tr, 2),
        in_specs=[blk(0), blk(0), blk(1), blk(0), blk(0)],
        out_specs=[blk(0), blk(0), pl.BlockSpec((tr, D), lambda i, h: (i, h))],
        out_shape=[jax.ShapeDtypeStruct((T, D), BF16), jax.ShapeDtypeStruct((T, D), BF16),
                   jax.ShapeDtypeStruct((T, 2 * D), BF16)],
        compiler_params=_cparams(("parallel", "arbitrary")),
    )(dmerged, rest, rest, y_a, y_b)


def _rms_bwd1(dx1, dh1, x, r1, g1, tr=256):
    T, D = x.shape

    def body(dx1_ref, dh_ref, x_ref, r_ref, g_ref, dx_ref, dg_ref):
        i = pl.program_id(0)

        @pl.when(i == 0)
        def _():
            dg_ref[...] = jnp.zeros_like(dg_ref)

        rv = r_ref[...]
        dh = dh_ref[...]
        n = x_ref[...] * rv
        dg_ref[...] += _colsum(dh * n)
        dx_ref[...] = dx1_ref[...] + _rms_bwd(dh * g_ref[...], n, rv)

    return pl.pallas_call(
        body, name="rms_bwd1", grid=(T // tr,),
        in_specs=[_rows(tr, D), _rows(tr, D), _rows(tr, D), _rows(tr, 1), _vec(D)],
        out_specs=[_rows(tr, D), _vec(D)],
        out_shape=[jax.ShapeDtypeStruct((T, D), F32), jax.ShapeDtypeStruct((1, D), F32)],
        compiler_params=_cparams(("arbitrary",)),
    )(dx1, dh1, x, r1, g1)


def _hgrn_consts():
    C = CHUNK
    nl = len(LEVELS) + 1
    w = np.zeros((nl, C, C), np.float32)
    w[0] = np.tril(np.ones((C, C), np.float32))
    for li, m in enumerate(LEVELS, start=1):
        for r in range(C):
            mid = (r // (2 * m)) * 2 * m + m
            if r >= mid:
                w[li, r, mid:r + 1] = 1.0
            else:
                w[li, r, r + 1:mid] = 1.0
    w_all = w.reshape(nl * C, C)
    w2 = np.concatenate([w_all, w_all], axis=1)
    w2t = np.concatenate([w_all.T, w_all.T], axis=1)
    R = GROUP * C
    t = np.arange(R)[:, None]
    s = np.arange(R)[None, :]
    masks = np.zeros((nl, R, R), np.float32)
    masks[0] = (t == s)
    for li, m in enumerate(LEVELS, start=1):
        masks[li] = ((t ^ s) < 2 * m)
    return jnp.asarray(w2, BF16), jnp.asarray(w2t, BF16), jnp.asarray(masks, F32)


def _hgrn_gates(z, lg_ref):
    l0 = lg_ref[0:1, :]
    l1 = lg_ref[1:2, :]
    mx = jnp.maximum(l0, l1)
    e0 = jnp.exp(l0 - mx)
    e1 = jnp.exp(l1 - mx)
    lb = e0 / (e0 + e1)
    om = 1.0 - lb
    sg = _sigmoid(z)
    sgm = _sigmoid(-z)
    f = lb + om * sg
    return lb, om, sg, sgm, f, jnp.log(f), om * sgm


def _hgrn_levels(q, kk, lf, w2_ref):
    C = CHUNK
    nl = len(LEVELS) + 1
    lf_hi, lf_lo = _split2(lf)
    per_chunk = []
    for c in range(GROUP):
        rhs = jnp.concatenate([lf_hi[c * C:(c + 1) * C], lf_lo[c * C:(c + 1) * C]], axis=0)
        per_chunk.append(_nn(w2_ref[...], rhs))
    args = [jnp.concatenate([per_chunk[c][l * C:(l + 1) * C] for c in range(GROUP)], axis=0) for l in range(nl)]
    exps = [jnp.exp(a) for a in args]
    row = lax.broadcasted_iota(jnp.int32, q.shape, 0)
    qf, kf, mts = [q], [kk], [None]
    for li, m in enumerate(LEVELS, start=1):
        mt = jnp.where((row & m) != 0, 1.0, 0.0).astype(F32)
        mts.append(mt)
        qf.append(q * exps[li] * mt)
        kf.append(kk * exps[li] * (1.0 - mt))
    return args, exps, mts, qf, kf


def _hgrn_scores(qf, kf, masks_ref):
    p = None
    for l in range(len(qf)):
        pl_ = _nt(qf[l].astype(BF16), kf[l].astype(BF16)) * masks_ref[l]
        p = pl_ if p is None else p + pl_
    return p


def _hgrn_fwd(main, lb_logits, norm_w, dm):
    T, H, C = dm.T, dm.H, CHUNK
    R = GROUP * C
    nj = T // R
    w2, _, masks = _hgrn_consts()

    def body(q_ref, z_ref, v_ref, g_ref, lg_ref, nw_ref, w2_ref, masks_ref, ya_ref, o_ref, sp_ref, st_ref):
        j = pl.program_id(1)

        @pl.when(j == 0)
        def _():
            st_ref[...] = jnp.zeros_like(st_ref)

        q = q_ref[...]
        v = v_ref[...]
        vb = v.astype(BF16)
        _, _, _, _, _, lf, kk = _hgrn_gates(z_ref[...], lg_ref)
        args, exps, _, qf, kf = _hgrn_levels(q, kk, lf, w2_ref)
        p = _hgrn_scores(qf, kf, masks_ref)
        o_intra = _nn(p.astype(BF16), vb)
        b, eb = args[0], exps[0]
        o_inter = []
        for c in range(GROUP):
            sl = slice(c * C, (c + 1) * C)
            st = st_ref[...]
            sp_ref[0, c] = st
            blast = b[c * C + C - 1:c * C + C, :]
            o_inter.append(_nt((q[sl] * eb[sl]).astype(BF16), st.astype(BF16)))
            kd = (kk[sl] * jnp.exp(blast - b[sl])).astype(BF16)
            st_ref[...] = st * jnp.exp(blast) + _tn(vb[sl], kd)
        o = o_intra + jnp.concatenate(o_inter, axis=0)
        o_ref[...] = o
        gv = g_ref[...]
        ya_ref[...] = (o * _rstd(o) * nw_ref[...] * (gv * _sigmoid(gv))).astype(BF16)

    def col(sec):
        return pl.BlockSpec((R, HEAD), lambda h, j: (j, sec * H + h))

    return pl.pallas_call(
        body, name="hgrn_fwd", grid=(H, nj),
        in_specs=[col(0), col(1), col(2), col(3),
                  pl.BlockSpec((2, HEAD), lambda h, j: (0, h)),
                  pl.BlockSpec((1, HEAD), lambda h, j: (0, 0)),
                  pl.BlockSpec(w2.shape, lambda h, j: (0, 0)),
                  pl.BlockSpec(masks.shape, lambda h, j: (0, 0, 0))],
        out_specs=[pl.BlockSpec((R, HEAD), lambda h, j: (j, h)),
                   pl.BlockSpec((R, HEAD), lambda h, j: (j, h)),
                   pl.BlockSpec((1, GROUP, HEAD, HEAD), lambda h, j: (h, j, 0, 0))],
        out_shape=[jax.ShapeDtypeStruct((T, dm.AW), BF16), jax.ShapeDtypeStruct((T, dm.AW), F32),
                   jax.ShapeDtypeStruct((H, T // C, HEAD, HEAD), F32)],
        scratch_shapes=[pltpu.VMEM((HEAD, HEAD), F32)],
        compiler_params=_cparams(("parallel", "arbitrary")),
    )(main, main, main, main, lb_logits, norm_w, w2, masks)


def _hgrn_bwd(main, lb_logits, norm_w, o_saved, states, dya, dm):
    T, H, C = dm.T, dm.H, CHUNK
    R = GROUP * C
    nj = T // R
    nl = len(LEVELS) + 1
    w2, w2t, masks = _hgrn_consts()

    def body(q_ref, z_ref, v_ref, g_ref, lg_ref, nw_ref, w2_ref, w2t_ref, masks_ref, o_ref, sp_ref, dya_ref,
             d_ref, dlb_ref, dnw_ref, ds_ref, stash):
        j = pl.program_id(1)
        sec = pl.program_id(2)

        @pl.when((j == 0) & (sec == 0))
        def _():
            ds_ref[...] = jnp.zeros_like(ds_ref)
            dlb_ref[...] = jnp.zeros_like(dlb_ref)
            dnw_ref[...] = jnp.zeros_like(dnw_ref)

        @pl.when(sec == 0)
        def _():
            q = q_ref[...]
            v = v_ref[...]
            gv = g_ref[...]
            vb = v.astype(BF16)
            lb, om, sg, sgm, f, lf, kk = _hgrn_gates(z_ref[...], lg_ref)
            args, exps, mts, qf, kf = _hgrn_levels(q, kk, lf, w2_ref)
            qb = [t.astype(BF16) for t in qf]
            kb = [t.astype(BF16) for t in kf]
            p = _hgrn_scores(qf, kf, masks_ref)
            o = o_ref[...]
            nw = nw_ref[...]
            r = _rstd(o)
            n = o * r
            sgg = _sigmoid(gv)
            dya_v = dya_ref[...]
            d_on = dya_v * (gv * sgg)
            dg = dya_v * (n * nw) * (sgg * (1.0 + gv * (1.0 - sgg)))
            dnw_ref[0] += _colsum(d_on * n)
            do = _rms_bwd(d_on * nw, n, r)
            dob = do.astype(BF16)
            dp = _nt(dob, vb)
            dv = _tn(p.astype(BF16), dob)
            dq = jnp.zeros_like(q)
            dkk = jnp.zeros_like(q)
            dargs = [None] * nl
            for l in range(nl):
                dpl = (dp * masks_ref[l]).astype(BF16)
                dql = _nn(dpl, kb[l])
                dkl = _tn(dpl, qb[l])
                if l == 0:
                    dq += dql
                    dkk += dkl
                else:
                    dq += dql * exps[l] * mts[l]
                    dkk += dkl * exps[l] * (1.0 - mts[l])
                    dargs[l] = dql * qf[l] + dkl * kf[l]
            b, eb = args[0], exps[0]
            row = lax.broadcasted_iota(jnp.int32, (C, HEAD), 0)
            dq_i, dkk_i, dv_i, db_i = [None] * GROUP, [None] * GROUP, [None] * GROUP, [None] * GROUP
            for c in reversed(range(GROUP)):
                sl = slice(c * C, (c + 1) * C)
                st = sp_ref[0, c]
                ds = ds_ref[...]
                dsb = ds.astype(BF16)
                blast = b[c * C + C - 1:c * C + C, :]
                ebl = jnp.exp(blast)
                el = jnp.exp(blast - b[sl])
                qe = q[sl] * eb[sl]
                kd = kk[sl] * el
                dqe = _nn(dob[sl], st.astype(BF16))
                dkd = _nn(vb[sl], dsb)
                t = dkd * kd
                dblast = _colsum(t) + _colsum(ds * st) * ebl
                dq_i[c] = dqe * eb[sl]
                dkk_i[c] = dkd * el
                dv_i[c] = _nt(kd.astype(BF16), dsb)
                db_i[c] = dqe * qe - t + jnp.where(row == C - 1, dblast, 0.0)
                ds_ref[...] = ds * ebl + _tn(dob[sl], qe.astype(BF16))
            dq = dq + jnp.concatenate(dq_i, axis=0)
            dkk = dkk + jnp.concatenate(dkk_i, axis=0)
            dv = dv + jnp.concatenate(dv_i, axis=0)
            dlf_c = []
            for c in range(GROUP):
                sl = slice(c * C, (c + 1) * C)
                stack = jnp.concatenate([db_i[c]] + [dargs[l][sl] for l in range(1, nl)], axis=0)
                hi, lo = _split2(stack)
                dlf_c.append(_nn(w2t_ref[...], jnp.concatenate([hi, lo], axis=0)))
            dlf = jnp.concatenate(dlf_c, axis=0)
            dz = dlf * (om * sg * (1.0 - sg) / f) - dkk * (om * sgm * (1.0 - sgm))
            dlb_ref[...] += _colsum(dlf * (1.0 - sg) / f - dkk * sgm)
            stash[0] = dq.astype(BF16)
            stash[1] = dz.astype(BF16)
            stash[2] = dv.astype(BF16)
            stash[3] = dg.astype(BF16)

        d_ref[...] = stash[sec]

    def col(sec):
        return pl.BlockSpec((R, HEAD), lambda h, j, s: (nj - 1 - j, sec * H + h))

    return pl.pallas_call(
        body, name="hgrn_bwd", grid=(H, nj, 4),
        in_specs=[col(0), col(1), col(2), col(3),
                  pl.BlockSpec((2, HEAD), lambda h, j, s: (0, h)),
                  pl.BlockSpec((1, HEAD), lambda h, j, s: (0, 0)),
                  pl.BlockSpec(w2.shape, lambda h, j, s: (0, 0)),
                  pl.BlockSpec(w2t.shape, lambda h, j, s: (0, 0)),
                  pl.BlockSpec(masks.shape, lambda h, j, s: (0, 0, 0)),
                  pl.BlockSpec((R, HEAD), lambda h, j, s: (nj - 1 - j, h)),
                  pl.BlockSpec((1, GROUP, HEAD, HEAD), lambda h, j, s: (h, nj - 1 - j, 0, 0)),
                  pl.BlockSpec((R, HEAD), lambda h, j, s: (nj - 1 - j, h))],
        out_specs=[pl.BlockSpec((R, HEAD), lambda h, j, s: (nj - 1 - j, s * H + h)),
                   pl.BlockSpec((1, HEAD), lambda h, j, s: (0, h)),
                   pl.BlockSpec((1, 1, HEAD), lambda h, j, s: (h, 0, 0))],
        out_shape=[jax.ShapeDtypeStruct((T, 4 * dm.AW), BF16), jax.ShapeDtypeStruct((1, dm.AW), F32),
                   jax.ShapeDtypeStruct((H, 1, HEAD), F32)],
        scratch_shapes=[pltpu.VMEM((HEAD, HEAD), F32), pltpu.VMEM((4, R, HEAD), BF16)],
        compiler_params=_cparams(("parallel", "arbitrary", "arbitrary")),
    )(main, main, main, main, lb_logits, norm_w, w2, w2t, masks, o_saved, states, dya)


def _log_sigmoid(x):
    return jnp.minimum(x, 0.0) - jnp.log(1.0 + jnp.exp(-jnp.abs(x)))


def _tri(n):
    return jnp.asarray(np.tril(np.ones((n, n), np.float32)), BF16)


def _cum_fwd(rest, bias, dm, tb=256):
    T = dm.T
    tb = min(tb, T)
    cb = 2 * dm.D // HEAD
    tri = _tri(tb)

    def body(x_ref, b_ref, tri_ref, o_ref, carry):
        i = pl.program_id(0)

        @pl.when(i == 0)
        def _():
            carry[...] = jnp.zeros_like(carry)

        lf = _log_sigmoid(x_ref[...] + b_ref[...])
        hi, mid, lo = _split3(lf)
        tr_ = tri_ref[...]
        c = _nn(tr_, hi) + _nn(tr_, mid) + _nn(tr_, lo) + carry[...]
        o_ref[...] = c
        carry[...] = c[tb - 1:tb, :]

    return pl.pallas_call(
        body, name="cum_fwd", grid=(T // tb,),
        in_specs=[_rows(tb, HEAD, cb), _vec(HEAD), pl.BlockSpec((tb, tb), lambda i: (0, 0))],
        out_specs=_rows(tb, HEAD),
        out_shape=jax.ShapeDtypeStruct((T, HEAD), F32),
        scratch_shapes=[pltpu.VMEM((1, HEAD), F32)],
        compiler_params=_cparams(("arbitrary",)),
    )(rest, bias, tri)


def _cum_bwd(dcum, rest, bias, dm, tb=256):
    T = dm.T
    tb = min(tb, T)
    nb = T // tb
    cb = 2 * dm.D // HEAD
    tri = _tri(tb)

    def body(d_ref, x_ref, b_ref, tri_ref, o_ref, db_ref, carry):
        i = pl.program_id(0)

        @pl.when(i == 0)
        def _():
            carry[...] = jnp.zeros_like(carry)
            db_ref[...] = jnp.zeros_like(db_ref)

        hi, mid, lo = _split3(d_ref[...])
        tr_ = tri_ref[...]
        dlf = _tn(tr_, hi) + _tn(tr_, mid) + _tn(tr_, lo) + carry[...]
        carry[...] = dlf[0:1, :]
        dx = dlf * _sigmoid(-(x_ref[...] + b_ref[...]))
        o_ref[...] = dx.astype(BF16)
        db_ref[...] += _colsum(dx)

    return pl.pallas_call(
        body, name="cum_bwd", grid=(nb,),
        in_specs=[pl.BlockSpec((tb, HEAD), lambda i: (nb - 1 - i, 0)),
                  pl.BlockSpec((tb, HEAD), lambda i: (nb - 1 - i, cb)),
                  _vec(HEAD), pl.BlockSpec((tb, tb), lambda i: (0, 0))],
        out_specs=[pl.BlockSpec((tb, HEAD), lambda i: (nb - 1 - i, 0)), _vec(HEAD)],
        out_shape=[jax.ShapeDtypeStruct((T, HEAD), BF16), jax.ShapeDtypeStruct((1, HEAD), F32)],
        scratch_shapes=[pltpu.VMEM((1, HEAD), F32)],
        compiler_params=_cparams(("arbitrary",)),
    )(dcum, rest, bias, tri)


def _fox_pairs(nq, kv_major):
    if kv_major:
        pairs = [(i, j) for j in range(nq) for i in range(j, nq)]
    else:
        pairs = [(i, j) for i in range(nq) for j in range(i + 1)]
    qi = jnp.asarray(np.array([p[0] for p in pairs], np.int32))
    kj = jnp.asarray(np.array([p[1] for p in pairs], np.int32))
    return qi, kj


class Fused(NamedTuple):
    ins: tuple
    outs: tuple
    sems: tuple
    hooks: object


def _fused_parts(fused):
    if fused is None:
        return (), (), (), 0, 0
    return tuple(fused.ins), tuple(fused.outs), tuple(fused.sems), len(fused.ins), len(fused.outs)


FOX_FWD_SPLIT = 4
FOX_BWD_SPLIT = 2


def _fox_fwd(main, cum_row, dm, fused=None, tq=512):
    T, H = dm.T, dm.H
    tq = min(tq, T)
    nq = T // tq
    qi_tab, kj_tab = _fox_pairs(nq, kv_major=False)
    npairs = int(qi_tab.shape[0])
    x_ins, x_outs, x_sems, n_in, n_out = _fused_parts(fused)
    ns = FOX_FWD_SPLIT if tq % (8 * FOX_FWD_SPLIT) == 0 else 1
    rq = tq // ns

    def body(qi_ref, kj_ref, q_ref, k_ref, v_ref, ck_ref, *rest):
        x_in, (o_ref, lse_ref) = rest[:n_in], rest[n_in:n_in + 2]
        x_out = rest[n_in + 2:n_in + 2 + n_out]
        m_ref, l_ref, acc_ref = rest[n_in + 2 + n_out:n_in + 5 + n_out]
        hd = pl.program_id(0)
        t = pl.program_id(1)
        i = qi_ref[t]
        j = kj_ref[t]
        if fused is not None:
            start, middle, finish = fused.hooks(x_in, x_out, *rest[n_in + 5 + n_out:])
            pl.when((hd == 0) & (t == 0))(start)
            pl.when((hd == H // 2) & (t == 0))(middle)

        @pl.when(j == 0)
        def _():
            m_ref[...] = jnp.full_like(m_ref, NEG)
            l_ref[...] = jnp.zeros_like(l_ref)
            acc_ref[...] = jnp.zeros_like(acc_ref)

        def step(on_diagonal):
            kb = k_ref[...].astype(BF16)
            vb = v_ref[...].astype(BF16)
            ck = ck_ref[0]
            for g in range(ns):
                rows = slice(g * rq, (g + 1) * rq)
                qs = (q_ref[rows, :] * (HEAD ** -0.5)).astype(BF16)
                s = _nt(qs, kb) - ck
                if on_diagonal:
                    row = g * rq + lax.broadcasted_iota(jnp.int32, s.shape, 0)
                    s = jnp.where(row >= lax.broadcasted_iota(jnp.int32, s.shape, 1), s, NEG)
                m_old = m_ref[rows, :]
                m_new = jnp.maximum(m_old, jnp.max(s, axis=-1, keepdims=True))
                a = jnp.exp(m_old - m_new)
                p = jnp.exp(s - m_new)
                l_ref[rows, :] = a * l_ref[rows, :] + jnp.sum(p, axis=-1, keepdims=True)
                acc_ref[rows, :] = a * acc_ref[rows, :] + _nn(p.astype(BF16), vb)
                m_ref[rows, :] = m_new

        @pl.when(j < i)
        def _():
            step(False)

        @pl.when(j == i)
        def _():
            step(True)
            l = l_ref[...]
            o_ref[...] = acc_ref[...] / l
            lse_ref[0] = m_ref[...] + jnp.log(l)

        if fused is not None:
            pl.when((hd == H - 1) & (t == npairs - 1))(finish)

    outs = pl.pallas_call(
        body, name="fox_fwd",
        grid_spec=pltpu.PrefetchScalarGridSpec(
            num_scalar_prefetch=2, grid=(H, npairs),
            in_specs=[pl.BlockSpec((tq, HEAD), lambda h, t, qi, kj: (qi[t], 4 * H + h)),
                      pl.BlockSpec((tq, HEAD), lambda h, t, qi, kj: (kj[t], 5 * H + h)),
                      pl.BlockSpec((tq, HEAD), lambda h, t, qi, kj: (kj[t], 6 * H + h)),
                      pl.BlockSpec((1, 1, tq), lambda h, t, qi, kj: (h, 0, kj[t]))] + [HBM] * n_in,
            out_specs=[pl.BlockSpec((tq, HEAD), lambda h, t, qi, kj: (qi[t], h)),
                       pl.BlockSpec((1, tq, 1), lambda h, t, qi, kj: (h, qi[t], 0))] + [HBM] * n_out,
            scratch_shapes=[pltpu.VMEM((tq, 1), F32), pltpu.VMEM((tq, 1), F32), pltpu.VMEM((tq, HEAD), F32)]
            + list(x_sems)),
        out_shape=[jax.ShapeDtypeStruct((T, dm.AW), F32), jax.ShapeDtypeStruct((H, T, 1), F32)] + list(x_outs),
        compiler_params=_cparams(("arbitrary", "arbitrary")),
    )(qi_tab, kj_tab, main, main, main, cum_row, *x_ins)
    return outs[0], outs[1], tuple(outs[2:])


def _fox_delta(do, o, dm, tr=256):
    T, H = dm.T, dm.H
    tr = min(tr, T)

    def body(do_ref, o_ref, d_ref):
        d_ref[0] = jnp.sum(do_ref[...] * o_ref[...], axis=-1, keepdims=True)

    return pl.pallas_call(
        body, name="fox_delta", grid=(H, T // tr),
        in_specs=[pl.BlockSpec((tr, HEAD), lambda h, i: (i, h)), pl.BlockSpec((tr, HEAD), lambda h, i: (i, h))],
        out_specs=pl.BlockSpec((1, tr, 1), lambda h, i: (h, i, 0)),
        out_shape=jax.ShapeDtypeStruct((H, T, 1), F32),
        compiler_params=_cparams(("parallel", "parallel")),
    )(do, o)


def _fox_bwd(main, cum_row, lse, delta, do, dm, fused=None, tq=512):
    T, H = dm.T, dm.H
    tq = min(tq, T)
    nq = T // tq
    qi_tab, kj_tab = _fox_pairs(nq, kv_major=True)
    npairs = int(qi_tab.shape[0])
    x_ins, x_outs, x_sems, n_in, n_out = _fused_parts(fused)
    ns = FOX_BWD_SPLIT if tq % (16 * FOX_BWD_SPLIT) == 0 else 1
    rq = tq // ns

    def body(qi_ref, kj_ref, q_ref, k_ref, v_ref, ck_ref, lse_ref, dl_ref, do_ref, *rest):
        x_in = rest[:n_in]
        dq_ref, dk_ref, dv_ref, dc_ref, dr_ref = rest[n_in:n_in + 5]
        x_out = rest[n_in + 5:n_in + 5 + n_out]
        dk_acc, dv_acc, dc_acc = rest[n_in + 5 + n_out:n_in + 8 + n_out]
        hd = pl.program_id(0)
        t = pl.program_id(1)
        i = qi_ref[t]
        kj = kj_ref[t]
        if fused is not None:
            start, middle, finish = fused.hooks(x_in, x_out, *rest[n_in + 8 + n_out:])
            pl.when((hd == 0) & (t == 0))(start)
            pl.when((hd == H // 2) & (t == 0))(middle)

        @pl.when(t == 0)
        def _():
            dq_ref[...] = jnp.zeros_like(dq_ref)
            dr_ref[...] = jnp.zeros_like(dr_ref)

        @pl.when(i == kj)
        def _():
            dk_acc[...] = jnp.zeros_like(dk_acc)
            dv_acc[...] = jnp.zeros_like(dv_acc)
            dc_acc[...] = jnp.zeros_like(dc_acc)

        def step(on_diagonal):
            kb = k_ref[...].astype(BF16)
            vb = v_ref[...].astype(BF16)
            ck = ck_ref[0]
            for g in range(ns):
                rows = slice(g * rq, (g + 1) * rq)
                qs = (q_ref[rows, :] * (HEAD ** -0.5)).astype(BF16)
                s = _nt(qs, kb) - ck
                if on_diagonal:
                    row = g * rq + lax.broadcasted_iota(jnp.int32, s.shape, 0)
                    s = jnp.where(row >= lax.broadcasted_iota(jnp.int32, s.shape, 1), s, NEG)
                dob = do_ref[rows, :].astype(BF16)
                p = jnp.exp(s - lse_ref[0, rows, :])
                ds = p * (_nt(dob, vb) - dl_ref[0, rows, :])
                dsb = ds.astype(BF16)
                dv_acc[...] += _tn(p.astype(BF16), dob)
                dk_acc[...] += _tn(dsb, qs)
                out_rows = pl.ds(pl.multiple_of(i * tq + g * rq, rq), rq)
                dq_ref[out_rows, :] += _nn(dsb, kb) * (HEAD ** -0.5)
                dr_ref[0, out_rows, :] += jnp.sum(ds, axis=-1, keepdims=True)
                dc_acc[...] -= _colsum(ds)

        @pl.when(i == kj)
        def _():
            step(True)

        @pl.when(i > kj)
        def _():
            step(False)

        @pl.when(i == nq - 1)
        def _():
            dk_ref[...] = dk_acc[...].astype(BF16)
            dv_ref[...] = dv_acc[...].astype(BF16)
            dc_ref[0] = dc_acc[...]

        if fused is not None:
            pl.when((hd == H - 1) & (t == npairs - 1))(finish)

    qcol = pl.BlockSpec((1, tq, 1), lambda h, t, qi, kj: (h, qi[t], 0))
    outs = pl.pallas_call(
        body, name="fox_bwd",
        grid_spec=pltpu.PrefetchScalarGridSpec(
            num_scalar_prefetch=2, grid=(H, npairs),
            in_specs=[pl.BlockSpec((tq, HEAD), lambda h, t, qi, kj: (qi[t], 4 * H + h)),
                      pl.BlockSpec((tq, HEAD), lambda h, t, qi, kj: (kj[t], 5 * H + h)),
                      pl.BlockSpec((tq, HEAD), lambda h, t, qi, kj: (kj[t], 6 * H + h)),
                      pl.BlockSpec((1, 1, tq), lambda h, t, qi, kj: (h, 0, kj[t])),
                      qcol, qcol,
                      pl.BlockSpec((tq, HEAD), lambda h, t, qi, kj: (qi[t], h))] + [HBM] * n_in,
            out_specs=[pl.BlockSpec((T, HEAD), lambda h, t, qi, kj: (0, h)),
                       pl.BlockSpec((tq, HEAD), lambda h, t, qi, kj: (kj[t], h)),
                       pl.BlockSpec((tq, HEAD), lambda h, t, qi, kj: (kj[t], h)),
                       pl.BlockSpec((1, 1, tq), lambda h, t, qi, kj: (h, 0, kj[t])),
                       pl.BlockSpec((1, T, 1), lambda h, t, qi, kj: (h, 0, 0))] + [HBM] * n_out,
            scratch_shapes=[pltpu.VMEM((tq, HEAD), F32), pltpu.VMEM((tq, HEAD), F32), pltpu.VMEM((1, tq), F32)]
            + list(x_sems)),
        out_shape=[jax.ShapeDtypeStruct((T, dm.AW), F32), jax.ShapeDtypeStruct((T, dm.AW), BF16),
                   jax.ShapeDtypeStruct((T, dm.AW), BF16), jax.ShapeDtypeStruct((H, 1, T), F32),
                   jax.ShapeDtypeStruct((H, T, 1), F32)] + list(x_outs),
        compiler_params=_cparams(("arbitrary", "arbitrary")),
    )(qi_tab, kj_tab, main, main, main, cum_row, lse, delta, do, *x_ins)
    return outs[:5], tuple(outs[5:])


def _local_step(dm, x, tgt, w_main, w_rest, later_weights, bias_p, lb_logits, norm_w, g1, g2, g3, g4,
                early_reduce=None):
    T, D, H = dm.T, dm.D, dm.H
    h1, r1 = _rms_fwd(x, g1)
    main = _mm(h1, w_main, "nn", F32, "proj_main")
    rest = _mm(h1, w_rest, "nn", F32, "proj_rest", tn=1408)
    ya, o_a, states = _hgrn_fwd(main, lb_logits, norm_w, dm)
    cum = _cum_fwd(rest, bias_p, dm)
    cum_row = cum[:, :H].T.reshape(H, 1, T)
    if isinstance(later_weights, Fused):
        o_b, lse, later_weights = _fox_fwd(main, cum_row, dm, fused=later_weights)
    else:
        o_b, lse, _ = _fox_fwd(main, cum_row, dm)
    w_up_a, w_up_b, w_o, w_ffn_in, w_ffn_down = later_weights
    y_a = _mm(ya, w_up_a, "nn", F32, "up_a")
    y_b = _mm(o_b, w_up_b, "nn", F32, "up_b")
    merged = _merge_fwd(rest, y_a, y_b)
    u = _mm(merged, w_o, "nn", F32, "w_o")
    x1, r2, h3, r3 = _post_pre(x, u, g2, g3)
    gu = _mm(h3, w_ffn_in, "nn", F32, "ffn_in")
    act = _swiglu_fwd(gu)
    w = _mm(act, w_ffn_down, "nn", F32, "ffn_down")
    loss, dy, dw, dg4 = _loss_bwd(x1, w, g4, tgt)
    dact = _mm(dw, w_ffn_down, "nt", BF16, "d_act", tn=1408)
    d_ffn_down = _mm(act, dw, "tn", F32, "dw_ffn_down", tm=1408)
    dgu = _swiglu_bwd(gu, dact)
    dh3 = _mm(dgu, w_ffn_in, "nt", F32, "d_h3")
    d_ffn_in = _mm(h3, dgu, "tn", F32, "dw_ffn_in")
    dx1, du, dg3, dg2 = _rms_bwd2(dy, dh3, x1, r3, g3, u, r2, g2)
    dmerged = _mm(du, w_o, "nt", F32, "d_merged")
    d_w_o = _mm(merged, du, "tn", F32, "dw_o")
    dy_a, dy_b, dgates = _merge_bwd(dmerged, rest, y_a, y_b)
    dya = _mm(dy_a, w_up_a, "nt", F32, "d_ya")
    d_up_a = _mm(ya, dy_a, "tn", F32, "dw_up_a")
    dob = _mm(dy_b, w_up_b, "nt", F32, "d_ob")
    d_up_b = _mm(o_b, dy_b, "tn", F32, "dw_up_b")
    d_a, dlb, dnw_h = _hgrn_bwd(main, lb_logits, norm_w, o_a, states, dya, dm)
    fused = early_reduce([d_up_a, d_up_b, d_w_o, d_ffn_in, d_ffn_down]) if early_reduce is not None else None
    (dq_b, dk_b, dv_b, dcum_row, dcum_col), early = _fox_bwd(
        main, cum_row, lse, _fox_delta(dob, o_b, dm), dob, dm, fused=fused)
    dcum = jnp.pad((dcum_row.reshape(H, T) + dcum_col.reshape(H, T)).T, ((0, 0), (0, HEAD - H)))
    dbf, dbias = _cum_bwd(dcum, rest, bias_p, dm)
    dmain = jnp.concatenate([d_a, dq_b.astype(BF16), dk_b, dv_b], axis=1)
    drest = jnp.concatenate([dgates, dbf], axis=1)
    dh1 = _mm(dmain, w_main, "nt", F32, "d_h1_main")
    dh1 = _mm(drest, w_rest, "nt", F32, "d_h1_rest", add=dh1)
    d_main = _mm(h1, dmain, "tn", F32, "dw_main")
    d_rest = _mm(h1, drest, "tn", F32, "dw_rest", tn=1408)
    dx, dg1 = _rms_bwd1(dx1, dh1, x, r1, g1)
    big = dict(main=d_main, rest=d_rest, up_a=d_up_a, up_b=d_up_b, w_o=d_w_o, ffn_in=d_ffn_in, ffn_down=d_ffn_down)
    small = dict(loss=loss, bias=dbias, norm_w=jnp.sum(dnw_h, axis=0), lb=dlb, g1=dg1, g2=dg2, g3=dg3, g4=dg4)
    return dx, big, small, early


HBM = pl.BlockSpec(memory_space=pltpu.HBM)


def _place():
    x, y, c = lax.axis_index("x"), lax.axis_index("y"), lax.axis_index("c")
    chips = [(1 - x, y), (x, 1 - y), (1 - x, 1 - y)]
    return x, y, c, chips


class Sharded(NamedTuple):
    kind: str
    r: int
    c: int

    @property
    def full(self):
        return (self.r, 4 * self.c) if self.kind == "col" else (4 * self.r, self.c)

    @property
    def half(self):
        return (self.r // 2, self.c) if self.kind == "col" else (self.r, self.c // 2)

    @property
    def half_of_full(self):
        return (self.r // 2, 4 * self.c) if self.kind == "col" else (4 * self.r, self.c // 2)

    def shard_window(self, ref, s):
        if self.kind == "col":
            return ref.at[:, pl.ds(pl.multiple_of(s * self.c, 128), self.c)]
        return ref.at[pl.ds(pl.multiple_of(s * self.r, 16), self.r), :]

    def half_window(self, ref, s, h):
        if self.kind == "col":
            return ref.at[pl.ds(pl.multiple_of(h * (self.r // 2), 16), self.r // 2),
                          pl.ds(pl.multiple_of(s * self.c, 128), self.c)]
        return ref.at[pl.ds(pl.multiple_of(s * self.r, 16), self.r),
                      pl.ds(pl.multiple_of(h * (self.c // 2), 128), self.c // 2)]

    def half_of(self, ref, h):
        if self.kind == "col":
            n = ref.shape[0] // 2
            return ref.at[pl.ds(pl.multiple_of(h * n, 16), n), :]
        n = ref.shape[1] // 2
        return ref.at[:, pl.ds(pl.multiple_of(h * n, 128), n)]

    def window_of_half(self, ref, s):
        if self.kind == "col":
            return ref.at[:, pl.ds(pl.multiple_of(s * self.c, 128), self.c)]
        return ref.at[pl.ds(pl.multiple_of(s * self.r, 16), self.r), :]


def _gather_hooks(specs):
    n = len(specs)

    def hooks(w_refs, f_refs, send_sems, recv_sems):
        x, y, c, chips = _place()
        q = 2 * x + y
        sibling = (x, y, 1 - c)

        def copy(t, k, src, dst, to):
            return pltpu.make_async_remote_copy(
                src_ref=src, dst_ref=dst, send_sem=send_sems.at[7 * t + k], recv_sem=recv_sems.at[7 * t + k],
                device_id=to, device_id_type=MESH)

        def over_ici(t, j, chip_from, to):
            src = specs[t].half_of(w_refs[t], c)
            return copy(t, j, src, specs[t].half_window(f_refs[t], 2 * chip_from[0] + chip_from[1], c), to)

        def passed_on(t, j, chip_from, half):
            win = specs[t].half_window(f_refs[t], 2 * chip_from[0] + chip_from[1], half)
            return copy(t, 3 + j, win, win, sibling)

        def own_shard(t):
            return copy(t, 6, w_refs[t], specs[t].shard_window(f_refs[t], q), sibling)

        def start():
            for t in range(n):
                own_shard(t).start()
                for j, chip in enumerate(chips):
                    over_ici(t, j, (x, y), (*chip, c)).start()

        def middle():
            for t in range(n):
                for j, chip in enumerate(chips):
                    over_ici(t, j, chip, sibling).wait_recv()
                    passed_on(t, j, chip, c).start()

        def finish():
            for t in range(n):
                own_shard(t).wait_recv()
                for j, chip in enumerate(chips):
                    passed_on(t, j, chip, 1 - c).wait_recv()
            for t in range(n):
                own_shard(t).wait_send()
                for j, chip in enumerate(chips):
                    over_ici(t, j, (x, y), (*chip, c)).wait_send()
                    passed_on(t, j, chip, c).wait_send()

        return start, middle, finish

    sems = (pltpu.SemaphoreType.DMA((7 * n,)), pltpu.SemaphoreType.DMA((7 * n,)))
    return hooks, sems


def _gather_fused(shards, specs):
    hooks, sems = _gather_hooks(specs)
    outs = tuple(jax.ShapeDtypeStruct(sp.full, w.dtype) for sp, w in zip(specs, shards))
    return Fused(ins=tuple(shards), outs=outs, sems=sems, hooks=hooks)


def _run_fused(fused, name):
    n_in, n_out = len(fused.ins), len(fused.outs)

    def body(*refs):
        start, middle, finish = fused.hooks(refs[:n_in], refs[n_in:n_in + n_out], *refs[n_in + n_out:])
        start()
        middle()
        finish()

    return pl.pallas_call(
        body, name=name, in_specs=[HBM] * n_in, out_specs=[HBM] * n_out,
        out_shape=list(fused.outs), scratch_shapes=list(fused.sems),
    )(*fused.ins)


def _swap_halves(fulls, specs, name):
    n = len(fulls)

    def body(*refs):
        g_refs, o_refs = refs[:n], refs[n:2 * n]
        send_sems, recv_sems = refs[2 * n:]
        x, y, c, _ = _place()
        cps = [pltpu.make_async_remote_copy(
            src_ref=specs[t].half_of(g_refs[t], 1 - c), dst_ref=o_refs[t],
            send_sem=send_sems.at[t], recv_sem=recv_sems.at[t],
            device_id=(x, y, 1 - c), device_id_type=MESH) for t in range(n)]
        for cp in cps:
            cp.start()
        for cp in cps:
            cp.wait_recv()
        for cp in cps:
            cp.wait_send()

    return pl.pallas_call(
        body, name=name,
        in_specs=[HBM] * n, out_specs=[HBM] * n,
        out_shape=[jax.ShapeDtypeStruct(sp.half_of_full, g.dtype) for sp, g in zip(specs, fulls)],
        scratch_shapes=[pltpu.SemaphoreType.DMA((n,)), pltpu.SemaphoreType.DMA((n,))],
    )(*fulls)


def _rtile(n, target):
    t = min(n, (target // 16) * 16)
    while n % t:
        t -= 16
    return t


def _add_sibling(full, got, sp, cq_idx, name):
    hr, hc = sp.half
    tr = _rtile(hr, 256)
    nrt = hr // tr

    def body(cq_ref, g_ref, r_ref, ob_ref, of_ref):
        s = pl.program_id(1)
        v = g_ref[...] + r_ref[...]
        ob_ref[...] = v.astype(BF16)

        @pl.when(s == cq_ref[1])
        def _():
            of_ref[...] = v

    if sp.kind == "col":
        g_spec = pl.BlockSpec((tr, hc), lambda i, s, cq: (cq[0] * nrt + i, s))
        r_spec = pl.BlockSpec((tr, hc), lambda i, s, cq: (i, s))
    else:
        g_spec = pl.BlockSpec((tr, hc), lambda i, s, cq: (s * nrt + i, cq[0]))
        r_spec = pl.BlockSpec((tr, hc), lambda i, s, cq: (s * nrt + i, 0))
    return pl.pallas_call(
        body, name=name,
        grid_spec=pltpu.PrefetchScalarGridSpec(
            num_scalar_prefetch=1, grid=(nrt, 4),
            in_specs=[g_spec, r_spec],
            out_specs=[r_spec, pl.BlockSpec((tr, hc), lambda i, s, cq: (i, 0))]),
        out_shape=[jax.ShapeDtypeStruct(sp.half_of_full, BF16), jax.ShapeDtypeStruct(sp.half, F32)],
        compiler_params=_cparams(("parallel", "arbitrary")),
    )(cq_idx, full, got)


def _scatter_fused(sums, specs):
    n = len(sums)

    def hooks(a_refs, o_refs, send_sems, recv_sems):
        x, y, c, chips = _place()

        def copies():
            return [pltpu.make_async_remote_copy(
                src_ref=specs[t].window_of_half(a_refs[t], 2 * chip[0] + chip[1]), dst_ref=o_refs[t].at[j],
                send_sem=send_sems.at[3 * t + j], recv_sem=recv_sems.at[3 * t + j],
                device_id=(*chip, c), device_id_type=MESH) for t in range(n) for j, chip in enumerate(chips)]

        def start():
            for cp in copies():
                cp.start()

        def finish():
            for cp in copies():
                cp.wait_recv()
            for cp in copies():
                cp.wait_send()

        return start, lambda: None, finish

    outs = tuple(jax.ShapeDtypeStruct((3,) + sp.half, a.dtype) for sp, a in zip(specs, sums))
    sems = (pltpu.SemaphoreType.DMA((3 * n,)), pltpu.SemaphoreType.DMA((3 * n,)))
    return Fused(ins=tuple(sums), outs=outs, sems=sems, hooks=hooks)


def _add_chips(own, got, sp, cq_idx, name):
    hr, hc = sp.half
    tr = _rtile(hr, 256)
    nrt = hr // tr

    def body(cq_ref, a_ref, r_ref, o_ref):
        o_ref[...] = ((a_ref[...] + r_ref[0].astype(F32)) + r_ref[1].astype(F32)) + r_ref[2].astype(F32)

    if sp.kind == "col":
        o_spec = pl.BlockSpec((tr, hc), lambda i, cq: (cq[0] * nrt + i, 0))
    else:
        o_spec = pl.BlockSpec((tr, hc), lambda i, cq: (i, cq[0]))
    return pl.pallas_call(
        body, name=name,
        grid_spec=pltpu.PrefetchScalarGridSpec(
            num_scalar_prefetch=1, grid=(nrt,),
            in_specs=[pl.BlockSpec((tr, hc), lambda i, cq: (i, 0)), pl.BlockSpec((3, tr, hc), lambda i, cq: (0, i, 0))],
            out_specs=o_spec),
        out_shape=jax.ShapeDtypeStruct((sp.r, sp.c), F32),
        compiler_params=_cparams(("parallel",)),
    )(cq_idx, own, got)


def _share_halves(shards, specs):
    n = len(shards)

    def body(*refs):
        o_refs = refs[n:2 * n]
        send_sems, recv_sems = refs[2 * n:]
        x, y, c, _ = _place()

        def copy(t, half):
            win = specs[t].half_of(o_refs[t], half)
            return pltpu.make_async_remote_copy(
                src_ref=win, dst_ref=win, send_sem=send_sems.at[t], recv_sem=recv_sems.at[t],
                device_id=(x, y, 1 - c), device_id_type=MESH)

        for t in range(n):
            copy(t, c).start()
        for t in range(n):
            copy(t, 1 - c).wait_recv()
        for t in range(n):
            copy(t, c).wait_send()

    return pl.pallas_call(
        body, name="share_halves",
        in_specs=[HBM] * n, out_specs=[HBM] * n,
        out_shape=[jax.ShapeDtypeStruct((sp.r, sp.c), F32) for sp in specs],
        input_output_aliases={t: t for t in range(n)},
        scratch_shapes=[pltpu.SemaphoreType.DMA((n,)), pltpu.SemaphoreType.DMA((n,))],
    )(*shards)


def _sum_small(vec):
    rows, w = vec.shape

    def body(v_ref, o_ref, buf, send_sems, recv_sems):
        x, y, c, _ = _place()
        me = 4 * x + 2 * y + c
        buf[me] = v_ref[...]
        cps = []
        for k in range(1, 8):
            to = (x ^ (k >> 2), y ^ ((k >> 1) & 1), c ^ (k & 1))
            cps.append(pltpu.make_async_remote_copy(
                src_ref=v_ref, dst_ref=buf.at[me], send_sem=send_sems.at[k - 1], recv_sem=recv_sems.at[k - 1],
                device_id=to, device_id_type=MESH))
        for cp in cps:
            cp.start()
        for k in range(1, 8):
            pltpu.make_async_remote_copy(
                src_ref=v_ref, dst_ref=buf.at[me ^ k], send_sem=send_sems.at[k - 1], recv_sem=recv_sems.at[k - 1],
                device_id=(x, y, c), device_id_type=MESH).wait_recv()
        for cp in cps:
            cp.wait_send()
        total = buf[0]
        for d in range(1, 8):
            total = total + buf[d]
        o_ref[...] = total

    return pl.pallas_call(
        body, name="sum_small",
        in_specs=[pl.BlockSpec(memory_space=pltpu.VMEM)], out_specs=pl.BlockSpec(memory_space=pltpu.VMEM),
        out_shape=jax.ShapeDtypeStruct((rows, w), F32),
        scratch_shapes=[pltpu.VMEM((8, rows, w), F32), pltpu.SemaphoreType.DMA((7,)), pltpu.SemaphoreType.DMA((7,))],
    )(vec)


def _adam_math(w, g, m, v):
    m = ADAM_B1 * m + (1.0 - ADAM_B1) * g
    v = ADAM_B2 * v + (1.0 - ADAM_B2) * (g * g)
    m_hat = m / (1.0 - ADAM_B1 ** ADAM_STEP)
    v_hat = v / (1.0 - ADAM_B2 ** ADAM_STEP)
    delta = -ADAM_LR * (m_hat / (jnp.sqrt(v_hat) + ADAM_EPS) + ADAM_WD * w)
    return delta, m, v


def _adamw(w, g, m, v, name, tr=128):
    R, Cn = w.shape
    tr = min(tr, R)
    assert R % tr == 0

    def body(w_ref, g_ref, m_ref, v_ref, go_ref, d_ref, mo_ref, vo_ref):
        gv = g_ref[...]
        d, mn, vn = _adam_math(w_ref[...], gv, m_ref[...], v_ref[...])
        go_ref[...] = gv
        d_ref[...] = d
        mo_ref[...] = mn
        vo_ref[...] = vn

    blk = pl.BlockSpec((tr, Cn), lambda i: (i, 0))
    sds = jax.ShapeDtypeStruct((R, Cn), F32)
    return pl.pallas_call(
        body, name=name, grid=(R // tr,),
        in_specs=[blk] * 4, out_specs=[blk] * 4, out_shape=[sds] * 4,
        compiler_params=_cparams(("parallel",)),
    )(w, g, m, v)


ROW_LOSS, ROW_BIAS, ROW_NORM, ROW_LB0, ROW_G1, ROW_G2, ROW_G3, ROW_G4, ROW_LB1 = range(9)
SMALL_ROWS = 16


def _small_update(gsum, wp, mp, vp):
    _, w = gsum.shape

    def body(g_ref, w_ref, m_ref, v_ref, go_ref, d_ref, mo_ref, vo_ref):
        wv = w_ref[...]
        l0 = wv[ROW_LB0:ROW_LB0 + 1, :]
        l1 = wv[ROW_LB1:ROW_LB1 + 1, :]
        mx = jnp.maximum(l0, l1)
        e0 = jnp.exp(l0 - mx)
        e1 = jnp.exp(l1 - mx)
        p0 = e0 / (e0 + e1)
        gs = g_ref[...]
        dl0 = gs[ROW_LB0:ROW_LB0 + 1, :] * p0 * (1.0 - p0)
        row8 = lax.broadcasted_iota(jnp.int32, gs.shape, 0)
        top = jnp.where(row8 == ROW_LB0, dl0, jnp.where(row8 == ROW_LOSS, 0.0, gs))
        bot = jnp.where(row8 == ROW_LB1 - 8, -dl0, 0.0)
        g16 = jnp.concatenate([top, bot], axis=0)
        d, mn, vn = _adam_math(wv, g16, m_ref[...], v_ref[...])
        go_ref[...] = g16
        d_ref[...] = d
        mo_ref[...] = mn
        vo_ref[...] = vn

    sds = jax.ShapeDtypeStruct((SMALL_ROWS, w), F32)
    full = pl.BlockSpec(memory_space=pltpu.VMEM)
    return pl.pallas_call(
        body, name="small_update", in_specs=[full] * 4, out_specs=[full] * 4, out_shape=[sds] * 4,
    )(gsum, wp, mp, vp)


def _w_in_layout(dm):
    cs = dm.NIN // 4
    place = [((cs * q) // 128, (cs * q) % 128) for q in range(4)]
    cp = -(-(cs + max(sh for _, sh in place)) // 128) * 128
    return cs, cp, place


def _zeros_cols(rows, n, dtype):
    return jnp.zeros((rows, n), dtype)


def _unshuffle_w_in(wp, dm, tr=128):
    D, H = dm.D, dm.H
    cs, cp, place = _w_in_layout(dm)
    nm, nrest = dm.NMAIN, dm.NREST
    ng = nm + nrest
    tr = min(tr, D)

    def body(x_ref, main_ref, rest_ref):
        g = None
        for q, (t0, sh) in enumerate(place):
            xq = x_ref[:, q * cp:(q + 1) * cp]
            yq = pltpu.roll(xq, sh, axis=1) if sh else xq
            width = min(cp, ng - t0 * 128)
            parts = [_zeros_cols(tr, t0 * 128, wp.dtype)] if t0 else []
            parts.append(yq[:, :width])
            if ng - t0 * 128 - width:
                parts.append(_zeros_cols(tr, ng - t0 * 128 - width, wp.dtype))
            placed = jnp.concatenate(parts, axis=1)
            g = placed if g is None else g + placed
        main_ref[...] = g[:, :nm]
        tail = g[:, nm:]
        gates = pltpu.roll(tail, nrest - H, axis=1)[:, :2 * D]
        lane = lax.broadcasted_iota(jnp.int32, (tr, HEAD), 1)
        rest_ref[...] = jnp.concatenate([gates, jnp.where(lane < H, tail[:, :HEAD], 0)], axis=1)

    return pl.pallas_call(
        body, name="unshuffle_w_in", grid=(D // tr,),
        in_specs=[pl.BlockSpec((tr, 4 * cp), lambda i: (i, 0))],
        out_specs=[pl.BlockSpec((tr, nm), lambda i: (i, 0)), pl.BlockSpec((tr, nrest), lambda i: (i, 0))],
        out_shape=[jax.ShapeDtypeStruct((D, nm), wp.dtype), jax.ShapeDtypeStruct((D, nrest), wp.dtype)],
        compiler_params=_cparams(("parallel",)),
    )(wp)


def _shuffle_w_in(d_main, d_rest, dm, tr=64):
    D, H = dm.D, dm.H
    cs, cp, place = _w_in_layout(dm)
    nm, nrest = dm.NMAIN, dm.NREST
    ng = nm + nrest
    tr = min(tr, D)

    def body(m_ref, r_ref, o_ref):
        rv = r_ref[...]
        lane = lax.broadcasted_iota(jnp.int32, (tr, HEAD), 1)
        tail = pltpu.roll(jnp.concatenate([rv[:, :2 * D], _zeros_cols(tr, HEAD, F32)], axis=1), H, axis=1)
        head = jnp.where(lane < H, rv[:, 2 * D:], 0.0) + tail[:, :HEAD]
        g = jnp.concatenate([m_ref[...], head, tail[:, HEAD:]], axis=1)
        lanes = lax.broadcasted_iota(jnp.int32, (tr, cp), 1)
        outs = []
        for q, (t0, sh) in enumerate(place):
            width = min(cp, ng - t0 * 128)
            win = g[:, t0 * 128:t0 * 128 + width]
            if width < cp:
                win = jnp.concatenate([win, _zeros_cols(tr, cp - width, F32)], axis=1)
            xq = pltpu.roll(win, cp - sh, axis=1) if sh else win
            outs.append(jnp.where(lanes < cs, xq, 0.0))
        o_ref[...] = jnp.concatenate(outs, axis=1)

    return pl.pallas_call(
        body, name="shuffle_w_in", grid=(D // tr,),
        in_specs=[pl.BlockSpec((tr, nm), lambda i: (i, 0)), pl.BlockSpec((tr, nrest), lambda i: (i, 0))],
        out_specs=pl.BlockSpec((tr, 4 * cp), lambda i: (i, 0)),
        out_shape=jax.ShapeDtypeStruct((D, 4 * cp), F32),
        compiler_params=_cparams(("parallel",)),
    )(d_main, d_rest)


def _pack_small(dm, bias, norm_w, lb_logits, g1, g2, g3, g4):
    D = dm.D
    row = lambda v: jnp.pad(v.reshape(1, -1), ((0, 0), (0, D - v.size)))
    rows = [jnp.zeros((1, D), F32), row(bias), row(norm_w), row(lb_logits[0]), row(g1), row(g2), row(g3), row(g4),
            row(lb_logits[1]), jnp.zeros((SMALL_ROWS - 9, D), F32)]
    return jnp.concatenate(rows, axis=0)


def _unpack_small(p, dm):
    H, AW = dm.H, dm.AW
    return (p[ROW_BIAS:ROW_BIAS + 1, :H], jnp.concatenate([p[ROW_LB0:ROW_LB0 + 1, :AW], p[ROW_LB1:ROW_LB1 + 1, :AW]], axis=0),
            p[ROW_NORM:ROW_NORM + 1, :HEAD], p[ROW_G1:ROW_G1 + 1], p[ROW_G2:ROW_G2 + 1], p[ROW_G3:ROW_G3 + 1],
            p[ROW_G4:ROW_G4 + 1])


def _step(dm, x, w_in, b_fox_f, hgrn_lb_logits, hgrn_norm_w, w_up_a, w_up_b, w_o, norm_mix_pre, norm_mix_post,
          norm_ffn_pre, norm_ffn_post, w_ffn_in, w_ffn_down, loss_target, moments_m, moments_v):
    xi, yi, ci = lax.axis_index("x"), lax.axis_index("y"), lax.axis_index("c")
    cq_idx = jnp.stack([ci, 2 * xi + yi]).astype(jnp.int32)
    D, AW, FF = dm.D, dm.AW, dm.FF
    cs, cp, _ = _w_in_layout(dm)
    big_names = ["w_in", "w_up_a", "w_up_b", "w_o", "w_ffn_in", "w_ffn_down"]
    specs = [Sharded("col", D, cp), Sharded("col", AW, D // 4), Sharded("col", AW, D // 4),
             Sharded("row", D // 4, D), Sharded("col", D, 2 * FF // 4), Sharded("row", FF // 4, D)]
    shards = [w_in[0], w_up_a[0], w_up_b[0], w_o[0], w_ffn_in[0], w_ffn_down[0]]

    sent = [jnp.pad(shards[0].astype(BF16), ((0, 0), (0, cp - cs)))] + [w.astype(BF16) for w in shards[1:]]
    (f_in,) = _run_fused(_gather_fused(sent[:1], specs[:1]), "gather_w_in")
    w_main, w_rest = _unshuffle_w_in(f_in, dm)

    def pair_sums(fulls, sps, names, tag):
        from_sibling = _swap_halves(fulls, sps, "swap_halves_" + tag)
        return [_add_sibling(g, r, sp, cq_idx, "add_sibling_" + n)
                for g, r, sp, n in zip(fulls, from_sibling, sps, names)]

    early_pairs = []

    def early_reduce(fulls):
        early_pairs.extend(pair_sums(fulls, specs[1:], big_names[1:], "early"))
        return _scatter_fused([p[0] for p in early_pairs], specs[1:])

    bias_p = jnp.pad(b_fox_f, ((0, 0), (0, HEAD - dm.H)))
    dx, big, small, early_from_chips = _local_step(
        dm, x[0], loss_target[0], w_main, w_rest, _gather_fused(sent[1:], specs[1:]), bias_p,
        hgrn_lb_logits, hgrn_norm_w, norm_mix_pre, norm_mix_post, norm_ffn_pre, norm_ffn_post,
        early_reduce=early_reduce)

    late_pairs = pair_sums([_shuffle_w_in(big["main"], big["rest"], dm)], specs[:1], big_names[:1], "late")
    late_from_chips = _run_fused(_scatter_fused([p[0] for p in late_pairs], specs[:1]), "scatter_w_in")
    pairs = late_pairs + early_pairs
    from_chips = list(late_from_chips) + list(early_from_chips)
    halves = [_add_chips(p[1], r, sp, cq_idx, "add_chips_" + n)
              for p, r, sp, n in zip(pairs, from_chips, specs, big_names)]
    grads = list(_share_halves(halves, specs))
    grads[0] = grads[0][:, :cs]

    row = lambda v: jnp.pad(v.reshape(1, -1), ((0, 0), (0, D - v.size)))
    vec = jnp.concatenate([row(small["loss"][:, :1]), row(small["bias"][:, :dm.H]), row(small["norm_w"]),
                           row(small["lb"]), small["g1"], small["g2"], small["g3"], small["g4"]], axis=0)
    gsum = _sum_small(vec)
    loss = gsum[ROW_LOSS, 0]

    smalls = lambda t: (t["b_fox_f"], t["hgrn_norm_w"], t["hgrn_lb_logits"], t["norm_mix_pre"], t["norm_mix_post"],
                        t["norm_ffn_pre"], t["norm_ffn_post"])
    params = dict(b_fox_f=b_fox_f, hgrn_norm_w=hgrn_norm_w, hgrn_lb_logits=hgrn_lb_logits, norm_mix_pre=norm_mix_pre,
                  norm_mix_post=norm_mix_post, norm_ffn_pre=norm_ffn_pre, norm_ffn_post=norm_ffn_post)
    sg, sd, sm, sv = _small_update(gsum, _pack_small(dm, *smalls(params)), _pack_small(dm, *smalls(moments_m)),
                                   _pack_small(dm, *smalls(moments_v)))
    big_out = {}
    for name, wsh, g in zip(big_names, shards, grads):
        go, d, mn, vn = _adamw(wsh, g, moments_m[name][0], moments_v[name][0], "adamw_" + name)
        big_out[name] = (go[None], d[None], mn[None], vn[None])

    order = ["w_in", "b_fox_f", "hgrn_lb_logits", "hgrn_norm_w", "w_up_a", "w_up_b", "w_o", "norm_mix_pre",
             "norm_mix_post", "norm_ffn_pre", "norm_ffn_post", "w_ffn_in", "w_ffn_down"]
    outs = []
    for kind, packed in enumerate([sg, sd, sm, sv]):
        b, lbl, nw, p1, p2, p3, p4 = _unpack_small(packed, dm)
        sm_map = dict(b_fox_f=b, hgrn_lb_logits=lbl, hgrn_norm_w=nw, norm_mix_pre=p1, norm_mix_post=p2,
                      norm_ffn_pre=p3, norm_ffn_post=p4)
        outs.append([big_out[n][kind] if n in big_out else sm_map[n] for n in order])
    return (loss, dx[None], *outs[0], *outs[1], *outs[2], *outs[3])


def kernel(x, w_in, b_fox_f, hgrn_lb_logits, hgrn_norm_w, w_up_a, w_up_b, w_o, norm_mix_pre, norm_mix_post, norm_ffn_pre, norm_ffn_post, w_ffn_in, w_ffn_down, loss_target, m_w_in, m_b_fox_f, m_hgrn_lb_logits, m_hgrn_norm_w, m_w_up_a, m_w_up_b, m_w_o, m_norm_mix_pre, m_norm_mix_post, m_norm_ffn_pre, m_norm_ffn_post, m_w_ffn_in, m_w_ffn_down, v_w_in, v_b_fox_f, v_hgrn_lb_logits, v_hgrn_norm_w, v_w_up_a, v_w_up_b, v_w_o, v_norm_mix_pre, v_norm_mix_post, v_norm_ffn_pre, v_norm_ffn_post, v_w_ffn_in, v_w_ffn_down):
    dm = Dims(T=x.shape[1], D=x.shape[2], FF=w_ffn_down.shape[1] * 4)
    moments_m = dict(w_in=m_w_in, b_fox_f=m_b_fox_f, hgrn_lb_logits=m_hgrn_lb_logits, hgrn_norm_w=m_hgrn_norm_w,
                     w_up_a=m_w_up_a, w_up_b=m_w_up_b, w_o=m_w_o, norm_mix_pre=m_norm_mix_pre,
                     norm_mix_post=m_norm_mix_post, norm_ffn_pre=m_norm_ffn_pre, norm_ffn_post=m_norm_ffn_post,
                     w_ffn_in=m_w_ffn_in, w_ffn_down=m_w_ffn_down)
    moments_v = dict(w_in=v_w_in, b_fox_f=v_b_fox_f, hgrn_lb_logits=v_hgrn_lb_logits, hgrn_norm_w=v_hgrn_norm_w,
                     w_up_a=v_w_up_a, w_up_b=v_w_up_b, w_o=v_w_o, norm_mix_pre=v_norm_mix_pre,
                     norm_mix_post=v_norm_mix_post, norm_ffn_pre=v_norm_ffn_pre, norm_ffn_post=v_norm_ffn_post,
                     w_ffn_in=v_w_ffn_in, w_ffn_down=v_w_ffn_down)
    return _step(dm, x, w_in, b_fox_f, hgrn_lb_logits, hgrn_norm_w, w_up_a, w_up_b, w_o, norm_mix_pre, norm_mix_post,
                 norm_ffn_pre, norm_ffn_post, w_ffn_in, w_ffn_down, loss_target, moments_m, moments_v)
```

```python
from typing import NamedTuple

import numpy as np
import jax
import jax.numpy as jnp
from jax import lax
from jax.experimental import pallas as pl
from jax.experimental.pallas import tpu as pltpu

F32 = jnp.float32
BF16 = jnp.bfloat16
MESH = pl.DeviceIdType.MESH

RMS_EPS = 1e-6
HEAD = 128
CHUNK = 64
GROUP = 4
LEVELS = (32, 16, 8, 4, 2, 1)
NEG = -1e30

ADAM_LR = 0.001
ADAM_B1 = 0.9
ADAM_B2 = 0.999
ADAM_EPS = 1e-08
ADAM_WD = 0.01
ADAM_STEP = 10

VMEM_LIMIT = 56 * 1024 * 1024


class Dims(NamedTuple):
    T: int
    D: int
    FF: int

    @property
    def AW(self):
        return self.D // 2

    @property
    def H(self):
        return self.AW // HEAD

    @property
    def NMAIN(self):
        return 7 * self.AW

    @property
    def NREST(self):
        return 2 * self.D + HEAD

    @property
    def NIN(self):
        return 7 * self.AW + self.H + 2 * self.D


def _cparams(sem, vmem=VMEM_LIMIT, **kw):
    return pltpu.CompilerParams(dimension_semantics=sem, vmem_limit_bytes=vmem, **kw)


def _tile(n, target):
    if n <= target:
        return n
    t = (target // 128) * 128
    while t >= 128:
        if n % t == 0:
            return t
        t -= 128
    raise ValueError(f"no tile for {n}")


def _dot(a, b, dims):
    return lax.dot_general(a, b, (dims, ((), ())), preferred_element_type=F32)


def _nn(a, b):
    return _dot(a, b, ((1,), (0,)))


def _nt(a, b):
    return _dot(a, b, ((1,), (1,)))


def _tn(a, b):
    return _dot(a, b, ((0,), (0,)))


def _sigmoid(x):
    return jax.nn.sigmoid(x)


def _split2(x):
    hi = x.astype(BF16)
    lo = (x - hi.astype(F32)).astype(BF16)
    return hi, lo


def _split3(x):
    hi = x.astype(BF16)
    r = x - hi.astype(F32)
    mid = r.astype(BF16)
    lo = (r - mid.astype(F32)).astype(BF16)
    return hi, mid, lo


def _mm(a, b, mode, out_dtype, name, add=None, tm=1024, tn=1024, tk=None):
    if mode == "nn":
        (M, K), (K2, N) = a.shape, b.shape
    elif mode == "nt":
        (M, K), (N, K2) = a.shape, b.shape
    else:
        (K, M), (K2, N) = a.shape, b.shape
    assert K == K2, (a.shape, b.shape, mode)
    if tk is None:
        tk = 1024 if mode == "tn" else 2048
    tm, tn, tk = _tile(M, tm), _tile(N, tn), _tile(K, tk)
    nk = K // tk
    isz = lambda t: jnp.dtype(t.dtype).itemsize
    vmem = 2 * (tm * tk * isz(a) + tk * tn * isz(b) + tm * tn * jnp.dtype(out_dtype).itemsize)
    vmem += (tm * tn * 4 if nk > 1 else 0) + (2 * tm * tn * isz(add) if add is not None else 0)
    assert vmem <= VMEM_LIMIT - 8 * 1024 * 1024, (name, vmem)
    if mode == "nn":
        a_spec = pl.BlockSpec((tm, tk), lambda i, j, k: (i, k))
        b_spec = pl.BlockSpec((tk, tn), lambda i, j, k: (k, j))
        op = _nn
    elif mode == "nt":
        a_spec = pl.BlockSpec((tm, tk), lambda i, j, k: (i, k))
        b_spec = pl.BlockSpec((tn, tk), lambda i, j, k: (j, k))
        op = _nt
    else:
        a_spec = pl.BlockSpec((tk, tm), lambda i, j, k: (k, i))
        b_spec = pl.BlockSpec((tk, tn), lambda i, j, k: (k, j))
        op = _tn
    o_spec = pl.BlockSpec((tm, tn), lambda i, j, k: (i, j))
    has_add = add is not None

    def body(*refs):
        a_ref, b_ref = refs[:2]
        add_ref = refs[2] if has_add else None
        o_ref = refs[3] if has_add else refs[2]

        def finish(r):
            if has_add:
                r = r + add_ref[...].astype(F32)
            o_ref[...] = r.astype(out_dtype)

        part = op(a_ref[...].astype(BF16), b_ref[...].astype(BF16))
        if nk == 1:
            finish(part)
            return
        acc = refs[-1]
        k = pl.program_id(2)

        @pl.when(k == 0)
        def _():
            acc[...] = part

        @pl.when((k > 0) & (k < nk - 1))
        def _():
            acc[...] += part

        @pl.when(k == nk - 1)
        def _():
            finish(acc[...] + part)

    in_specs = [a_spec, b_spec] + ([o_spec] if has_add else [])
    args = (a, b) + ((add,) if has_add else ())
    return pl.pallas_call(
        body, name=name, grid=(M // tm, N // tn, nk),
        in_specs=in_specs, out_specs=o_spec,
        out_shape=jax.ShapeDtypeStruct((M, N), out_dtype),
        scratch_shapes=[pltpu.VMEM((tm, tn), F32)] if nk > 1 else [],
        compiler_params=_cparams(("parallel", "parallel", "arbitrary")),
    )(*args)


def _rows(tr, w, col=0):
    return pl.BlockSpec((tr, w), lambda i, *_: (i, col))


def _vec(w):
    return pl.BlockSpec((1, w), lambda i, *_: (0, 0))


def _rstd(v):
    return lax.rsqrt(jnp.mean(v * v, axis=-1, keepdims=True) + RMS_EPS)


def _rms_bwd(dn, n, r):
    return r * (dn - n * jnp.mean(dn * n, axis=-1, keepdims=True))


def _colsum(v):
    return jnp.sum(v, axis=0, keepdims=True)


def _rms_fwd(x, g, tr=256):
    T, D = x.shape

    def body(x_ref, g_ref, h_ref, r_ref):
        xv = x_ref[...]
        r = _rstd(xv)
        h_ref[...] = (xv * r * g_ref[...]).astype(BF16)
        r_ref[...] = r

    return pl.pallas_call(
        body, name="rms_fwd", grid=(T // tr,),
        in_specs=[_rows(tr, D), _vec(D)],
        out_specs=[_rows(tr, D), _rows(tr, 1)],
        out_shape=[jax.ShapeDtypeStruct((T, D), BF16), jax.ShapeDtypeStruct((T, 1), F32)],
        compiler_params=_cparams(("parallel",)),
    )(x, g)


def _merge_fwd(rest, y_a, y_b, tr=256):
    T, D = y_a.shape

    def body(ga_ref, gb_ref, ya_ref, yb_ref, o_ref):
        o_ref[...] = (_sigmoid(ga_ref[...]) * ya_ref[...] + _sigmoid(gb_ref[...]) * yb_ref[...]).astype(BF16)

    return pl.pallas_call(
        body, name="merge_fwd", grid=(T // tr,),
        in_specs=[_rows(tr, D, 0), _rows(tr, D, 1), _rows(tr, D), _rows(tr, D)],
        out_specs=_rows(tr, D),
        out_shape=jax.ShapeDtypeStruct((T, D), BF16),
        compiler_params=_cparams(("parallel",)),
    )(rest, rest, y_a, y_b)


def _post_pre(x, u, g2, g3, tr=256):
    T, D = x.shape

    def body(x_ref, u_ref, g2_ref, g3_ref, x1_ref, r2_ref, h3_ref, r3_ref):
        uv = u_ref[...]
        r2 = _rstd(uv)
        x1 = x_ref[...] + uv * r2 * g2_ref[...]
        r3 = _rstd(x1)
        x1_ref[...] = x1
        r2_ref[...] = r2
        h3_ref[...] = (x1 * r3 * g3_ref[...]).astype(BF16)
        r3_ref[...] = r3

    return pl.pallas_call(
        body, name="post_pre", grid=(T // tr,),
        in_specs=[_rows(tr, D), _rows(tr, D), _vec(D), _vec(D)],
        out_specs=[_rows(tr, D), _rows(tr, 1), _rows(tr, D), _rows(tr, 1)],
        out_shape=[jax.ShapeDtypeStruct((T, D), F32), jax.ShapeDtypeStruct((T, 1), F32),
                   jax.ShapeDtypeStruct((T, D), BF16), jax.ShapeDtypeStruct((T, 1), F32)],
        compiler_params=_cparams(("parallel",)),
    )(x, u, g2, g3)


def _swiglu_fwd(gu, tr=256):
    T, FF2 = gu.shape
    FF = FF2 // 2
    tc = _tile(FF, 1024)
    nc = FF // tc

    def body(g_ref, u_ref, o_ref):
        gv = g_ref[...]
        o_ref[...] = (gv * _sigmoid(gv) * u_ref[...]).astype(BF16)

    return pl.pallas_call(
        body, name="swiglu_fwd", grid=(T // tr, nc),
        in_specs=[pl.BlockSpec((tr, tc), lambda i, j: (i, j)),
                  pl.BlockSpec((tr, tc), lambda i, j: (i, j + nc))],
        out_specs=pl.BlockSpec((tr, tc), lambda i, j: (i, j)),
        out_shape=jax.ShapeDtypeStruct((T, FF), BF16),
        compiler_params=_cparams(("parallel", "parallel")),
    )(gu, gu)


def _loss_bwd(x1, w, g4, tgt, tr=256):
    T, D = x1.shape

    def body(x1_ref, w_ref, g4_ref, t_ref, loss_ref, dy_ref, dw_ref, dg_ref):
        i = pl.program_id(0)

        @pl.when(i == 0)
        def _():
            loss_ref[...] = jnp.zeros_like(loss_ref)
            dg_ref[...] = jnp.zeros_like(dg_ref)

        wv = w_ref[...]
        g4v = g4_ref[...]
        r4 = _rstd(wv)
        n4 = wv * r4
        e = x1_ref[...] + n4 * g4v - t_ref[...]
        loss_ref[...] += 0.5 * jnp.sum(jnp.mean(e * e, axis=-1, keepdims=True), axis=0, keepdims=True)
        dy = e * (1.0 / D)
        dy_ref[...] = dy
        dg_ref[...] += _colsum(dy * n4)
        dw_ref[...] = _rms_bwd(dy * g4v, n4, r4).astype(BF16)

    return pl.pallas_call(
        body, name="loss_bwd", grid=(T // tr,),
        in_specs=[_rows(tr, D), _rows(tr, D), _vec(D), _rows(tr, D)],
        out_specs=[_vec(HEAD), _rows(tr, D), _rows(tr, D), _vec(D)],
        out_shape=[jax.ShapeDtypeStruct((1, HEAD), F32), jax.ShapeDtypeStruct((T, D), F32),
                   jax.ShapeDtypeStruct((T, D), BF16), jax.ShapeDtypeStruct((1, D), F32)],
        compiler_params=_cparams(("arbitrary",)),
    )(x1, w, g4, tgt)


def _swiglu_bwd(gu, dact, tr=128):
    T, FF2 = gu.shape
    FF = FF2 // 2

    def body(g_ref, u_ref, d_ref, o_ref):
        h = pl.program_id(1)
        gv = g_ref[...]
        s = _sigmoid(gv)
        dv = d_ref[...].astype(F32)

        @pl.when(h == 0)
        def _():
            o_ref[...] = (dv * u_ref[...] * (s * (1.0 + gv * (1.0 - s)))).astype(BF16)

        @pl.when(h == 1)
        def _():
            o_ref[...] = (dv * (gv * s)).astype(BF16)

    return pl.pallas_call(
        body, name="swiglu_bwd", grid=(T // tr, 2),
        in_specs=[pl.BlockSpec((tr, FF), lambda i, h: (i, 0)),
                  pl.BlockSpec((tr, FF), lambda i, h: (i, 1)),
                  pl.BlockSpec((tr, FF), lambda i, h: (i, 0))],
        out_specs=pl.BlockSpec((tr, FF), lambda i, h: (i, h)),
        out_shape=jax.ShapeDtypeStruct((T, FF2), BF16),
        compiler_params=_cparams(("parallel", "arbitrary")),
    )(gu, gu, dact)


def _rms_bwd2(dy, dh3, x1, r3, g3, u, r2, g2, tr=256):
    T, D = dy.shape

    def body(dy_ref, dh_ref, x1_ref, r3_ref, g3_ref, u_ref, r2_ref, g2_ref, dx1_ref, du_ref, dg3_ref, dg2_ref):
        i = pl.program_id(0)

        @pl.when(i == 0)
        def _():
            dg3_ref[...] = jnp.zeros_like(dg3_ref)
            dg2_ref[...] = jnp.zeros_like(dg2_ref)

        r3v, r2v = r3_ref[...], r2_ref[...]
        dh = dh_ref[...]
        n3 = x1_ref[...] * r3v
        dg3_ref[...] += _colsum(dh * n3)
        dx1 = dy_ref[...] + _rms_bwd(dh * g3_ref[...], n3, r3v)
        dx1_ref[...] = dx1
        n2 = u_ref[...] * r2v
        dg2_ref[...] += _colsum(dx1 * n2)
        du_ref[...] = _rms_bwd(dx1 * g2_ref[...], n2, r2v).astype(BF16)

    return pl.pallas_call(
        body, name="rms_bwd2", grid=(T // tr,),
        in_specs=[_rows(tr, D), _rows(tr, D), _rows(tr, D), _rows(tr, 1), _vec(D),
                  _rows(tr, D), _rows(tr, 1), _vec(D)],
        out_specs=[_rows(tr, D), _rows(tr, D), _vec(D), _vec(D)],
        out_shape=[jax.ShapeDtypeStruct((T, D), F32), jax.ShapeDtypeStruct((T, D), BF16),
                   jax.ShapeDtypeStruct((1, D), F32), jax.ShapeDtypeStruct((1, D), F32)],
        compiler_params=_cparams(("arbitrary",)),
    )(dy, dh3, x1, r3, g3, u, r2, g2)


def _merge_bwd(dmerged, rest, y_a, y_b, tr=256):
    T, D = dmerged.shape

    def body(dm_ref, ga_ref, gb_ref, ya_ref, yb_ref, dya_ref, dyb_ref, dg_ref):
        h = pl.program_id(1)
        dm = dm_ref[...]

        @pl.when(h == 0)
        def _():
            s = _sigmoid(ga_ref[...])
            dya_ref[...] = (dm * s).astype(BF16)
            dg_ref[...] = (dm * ya_ref[...] * s * (1.0 - s)).astype(BF16)

        @pl.when(h == 1)
        def _():
            s = _sigmoid(gb_ref[...])
            dyb_ref[...] = (dm * s).astype(BF16)
            dg_ref[...] = (dm * yb_ref[...] * s * (1.0 - s)).astype(BF16)

    blk = lambda col: pl.BlockSpec((tr, D), lambda i, h: (i, col))
    return pl.pallas_call(
        body, name="merge_bwd", grid=(T // tr, 2),
        in_specs=[blk(0), blk(0), blk(1), blk(0), blk(0)],
        out_specs=[blk(0), blk(0), pl.BlockSpec((tr, D), lambda i, h: (i, h))],
        out_shape=[jax.ShapeDtypeStruct((T, D), BF16), jax.ShapeDtypeStruct((T, D), BF16),
                   jax.ShapeDtypeStruct((T, 2 * D), BF16)],
        compiler_params=_cparams(("parallel", "arbitrary")),
    )(dmerged, rest, rest, y_a, y_b)


def _rms_bwd1(dx1, dh1, x, r1, g1, tr=256):
    T, D = x.shape

    def body(dx1_ref, dh_ref, x_ref, r_ref, g_ref, dx_ref, dg_ref):
        i = pl.program_id(0)

        @pl.when(i == 0)
        def _():
            dg_ref[...] = jnp.zeros_like(dg_ref)

        rv = r_ref[...]
        dh = dh_ref[...]
        n = x_ref[...] * rv
        dg_ref[...] += _colsum(dh * n)
        dx_ref[...] = dx1_ref[...] + _rms_bwd(dh * g_ref[...], n, rv)

    return pl.pallas_call(
        body, name="rms_bwd1", grid=(T // tr,),
        in_specs=[_rows(tr, D), _rows(tr, D), _rows(tr, D), _rows(tr, 1), _vec(D)],
        out_specs=[_rows(tr, D), _vec(D)],
        out_shape=[jax.ShapeDtypeStruct((T, D), F32), jax.ShapeDtypeStruct((1, D), F32)],
        compiler_params=_cparams(("arbitrary",)),
    )(dx1, dh1, x, r1, g1)


def _hgrn_consts():
    C = CHUNK
    nl = len(LEVELS) + 1
    w = np.zeros((nl, C, C), np.float32)
    w[0] = np.tril(np.ones((C, C), np.float32))
    for li, m in enumerate(LEVELS, start=1):
        for r in range(C):
            mid = (r // (2 * m)) * 2 * m + m
            if r >= mid:
                w[li, r, mid:r + 1] = 1.0
            else:
                w[li, r, r + 1:mid] = 1.0
    w_all = w.reshape(nl * C, C)
    w2 = np.concatenate([w_all, w_all], axis=1)
    w2t = np.concatenate([w_all.T, w_all.T], axis=1)
    R = GROUP * C
    t = np.arange(R)[:, None]
    s = np.arange(R)[None, :]
    masks = np.zeros((nl, R, R), np.float32)
    masks[0] = (t == s)
    for li, m in enumerate(LEVELS, start=1):
        masks[li] = ((t ^ s) < 2 * m)
    return jnp.asarray(w2, BF16), jnp.asarray(w2t, BF16), jnp.asarray(masks, F32)


def _hgrn_gates(z, lg_ref):
    l0 = lg_ref[0:1, :]
    l1 = lg_ref[1:2, :]
    mx = jnp.maximum(l0, l1)
    e0 = jnp.exp(l0 - mx)
    e1 = jnp.exp(l1 - mx)
    lb = e0 / (e0 + e1)
    om = 1.0 - lb
    sg = _sigmoid(z)
    sgm = _sigmoid(-z)
    f = lb + om * sg
    return lb, om, sg, sgm, f, jnp.log(f), om * sgm


def _hgrn_levels(q, kk, lf, w2_ref):
    C = CHUNK
    nl = len(LEVELS) + 1
    lf_hi, lf_lo = _split2(lf)
    per_chunk = []
    for c in range(GROUP):
        rhs = jnp.concatenate([lf_hi[c * C:(c + 1) * C], lf_lo[c * C:(c + 1) * C]], axis=0)
        per_chunk.append(_nn(w2_ref[...], rhs))
    args = [jnp.concatenate([per_chunk[c][l * C:(l + 1) * C] for c in range(GROUP)], axis=0) for l in range(nl)]
    exps = [jnp.exp(a) for a in args]
    row = lax.broadcasted_iota(jnp.int32, q.shape, 0)
    qf, kf, mts = [q], [kk], [None]
    for li, m in enumerate(LEVELS, start=1):
        mt = jnp.where((row & m) != 0, 1.0, 0.0).astype(F32)
        mts.append(mt)
        qf.append(q * exps[li] * mt)
        kf.append(kk * exps[li] * (1.0 - mt))
    return args, exps, mts, qf, kf


def _hgrn_scores(qf, kf, masks_ref):
    p = None
    for l in range(len(qf)):
        pl_ = _nt(qf[l].astype(BF16), kf[l].astype(BF16)) * masks_ref[l]
        p = pl_ if p is None else p + pl_
    return p


def _hgrn_fwd(main, lb_logits, norm_w, dm):
    T, H, C = dm.T, dm.H, CHUNK
    R = GROUP * C
    nj = T // R
    w2, _, masks = _hgrn_consts()

    def body(q_ref, z_ref, v_ref, g_ref, lg_ref, nw_ref, w2_ref, masks_ref, ya_ref, o_ref, sp_ref, st_ref):
        j = pl.program_id(1)

        @pl.when(j == 0)
        def _():
            st_ref[...] = jnp.zeros_like(st_ref)

        q = q_ref[...]
        v = v_ref[...]
        vb = v.astype(BF16)
        _, _, _, _, _, lf, kk = _hgrn_gates(z_ref[...], lg_ref)
        args, exps, _, qf, kf = _hgrn_levels(q, kk, lf, w2_ref)
        p = _hgrn_scores(qf, kf, masks_ref)
        o_intra = _nn(p.astype(BF16), vb)
        b, eb = args[0], exps[0]
        o_inter = []
        for c in range(GROUP):
            sl = slice(c * C, (c + 1) * C)
            st = st_ref[...]
            sp_ref[0, c] = st
            blast = b[c * C + C - 1:c * C + C, :]
            o_inter.append(_nt((q[sl] * eb[sl]).astype(BF16), st.astype(BF16)))
            kd = (kk[sl] * jnp.exp(blast - b[sl])).astype(BF16)
            st_ref[...] = st * jnp.exp(blast) + _tn(vb[sl], kd)
        o = o_intra + jnp.concatenate(o_inter, axis=0)
        o_ref[...] = o
        gv = g_ref[...]
        ya_ref[...] = (o * _rstd(o) * nw_ref[...] * (gv * _sigmoid(gv))).astype(BF16)

    def col(sec):
        return pl.BlockSpec((R, HEAD), lambda h, j: (j, sec * H + h))

    return pl.pallas_call(
        body, name="hgrn_fwd", grid=(H, nj),
        in_specs=[col(0), col(1), col(2), col(3),
                  pl.BlockSpec((2, HEAD), lambda h, j: (0, h)),
                  pl.BlockSpec((1, HEAD), lambda h, j: (0, 0)),
                  pl.BlockSpec(w2.shape, lambda h, j: (0, 0)),
                  pl.BlockSpec(masks.shape, lambda h, j: (0, 0, 0))],
        out_specs=[pl.BlockSpec((R, HEAD), lambda h, j: (j, h)),
                   pl.BlockSpec((R, HEAD), lambda h, j: (j, h)),
                   pl.BlockSpec((1, GROUP, HEAD, HEAD), lambda h, j: (h, j, 0, 0))],
        out_shape=[jax.ShapeDtypeStruct((T, dm.AW), BF16), jax.ShapeDtypeStruct((T, dm.AW), F32),
                   jax.ShapeDtypeStruct((H, T // C, HEAD, HEAD), F32)],
        scratch_shapes=[pltpu.VMEM((HEAD, HEAD), F32)],
        compiler_params=_cparams(("parallel", "arbitrary")),
    )(main, main, main, main, lb_logits, norm_w, w2, masks)


def _hgrn_bwd(main, lb_logits, norm_w, o_saved, states, dya, dm):
    T, H, C = dm.T, dm.H, CHUNK
    R = GROUP * C
    nj = T // R
    nl = len(LEVELS) + 1
    w2, w2t, masks = _hgrn_consts()

    def body(q_ref, z_ref, v_ref, g_ref, lg_ref, nw_ref, w2_ref, w2t_ref, masks_ref, o_ref, sp_ref, dya_ref,
             d_ref, dlb_ref, dnw_ref, ds_ref, stash):
        j = pl.program_id(1)
        sec = pl.program_id(2)

        @pl.when((j == 0) & (sec == 0))
        def _():
            ds_ref[...] = jnp.zeros_like(ds_ref)
            dlb_ref[...] = jnp.zeros_like(dlb_ref)
            dnw_ref[...] = jnp.zeros_like(dnw_ref)

        @pl.when(sec == 0)
        def _():
            q = q_ref[...]
            v = v_ref[...]
            gv = g_ref[...]
            vb = v.astype(BF16)
            lb, om, sg, sgm, f, lf, kk = _hgrn_gates(z_ref[...], lg_ref)
            args, exps, mts, qf, kf = _hgrn_levels(q, kk, lf, w2_ref)
            qb = [t.astype(BF16) for t in qf]
            kb = [t.astype(BF16) for t in kf]
            p = _hgrn_scores(qf, kf, masks_ref)
            o = o_ref[...]
            nw = nw_ref[...]
            r = _rstd(o)
            n = o * r
            sgg = _sigmoid(gv)
            dya_v = dya_ref[...]
            d_on = dya_v * (gv * sgg)
            dg = dya_v * (n * nw) * (sgg * (1.0 + gv * (1.0 - sgg)))
            dnw_ref[0] += _colsum(d_on * n)
            do = _rms_bwd(d_on * nw, n, r)
            dob = do.astype(BF16)
            dp = _nt(dob, vb)
            dv = _tn(p.astype(BF16), dob)
            dq = jnp.zeros_like(q)
            dkk = jnp.zeros_like(q)
            dargs = [None] * nl
            for l in range(nl):
                dpl = (dp * masks_ref[l]).astype(BF16)
                dql = _nn(dpl, kb[l])
                dkl = _tn(dpl, qb[l])
                if l == 0:
                    dq += dql
                    dkk += dkl
                else:
                    dq += dql * exps[l] * mts[l]
                    dkk += dkl * exps[l] * (1.0 - mts[l])
                    dargs[l] = dql * qf[l] + dkl * kf[l]
            b, eb = args[0], exps[0]
            row = lax.broadcasted_iota(jnp.int32, (C, HEAD), 0)
            dq_i, dkk_i, dv_i, db_i = [None] * GROUP, [None] * GROUP, [None] * GROUP, [None] * GROUP
            for c in reversed(range(GROUP)):
                sl = slice(c * C, (c + 1) * C)
                st = sp_ref[0, c]
                ds = ds_ref[...]
                dsb = ds.astype(BF16)
                blast = b[c * C + C - 1:c * C + C, :]
                ebl = jnp.exp(blast)
                el = jnp.exp(blast - b[sl])
                qe = q[sl] * eb[sl]
                kd = kk[sl] * el
                dqe = _nn(dob[sl], st.astype(BF16))
                dkd = _nn(vb[sl], dsb)
                t = dkd * kd
                dblast = _colsum(t) + _colsum(ds * st) * ebl
                dq_i[c] = dqe * eb[sl]
                dkk_i[c] = dkd * el
                dv_i[c] = _nt(kd.astype(BF16), dsb)
                db_i[c] = dqe * qe - t + jnp.where(row == C - 1, dblast, 0.0)
                ds_ref[...] = ds * ebl + _tn(dob[sl], qe.astype(BF16))
            dq = dq + jnp.concatenate(dq_i, axis=0)
            dkk = dkk + jnp.concatenate(dkk_i, axis=0)
            dv = dv + jnp.concatenate(dv_i, axis=0)
            dlf_c = []
            for c in range(GROUP):
                sl = slice(c * C, (c + 1) * C)
                stack = jnp.concatenate([db_i[c]] + [dargs[l][sl] for l in range(1, nl)], axis=0)
                hi, lo = _split2(stack)
                dlf_c.append(_nn(w2t_ref[...], jnp.concatenate([hi, lo], axis=0)))
            dlf = jnp.concatenate(dlf_c, axis=0)
            dz = dlf * (om * sg * (1.0 - sg) / f) - dkk * (om * sgm * (1.0 - sgm))
            dlb_ref[...] += _colsum(dlf * (1.0 - sg) / f - dkk * sgm)
            stash[0] = dq.astype(BF16)
            stash[1] = dz.astype(BF16)
            stash[2] = dv.astype(BF16)
            stash[3] = dg.astype(BF16)

        d_ref[...] = stash[sec]

    def col(sec):
        return pl.BlockSpec((R, HEAD), lambda h, j, s: (nj - 1 - j, sec * H + h))

    return pl.pallas_call(
        body, name="hgrn_bwd", grid=(H, nj, 4),
        in_specs=[col(0), col(1), col(2), col(3),
                  pl.BlockSpec((2, HEAD), lambda h, j, s: (0, h)),
                  pl.BlockSpec((1, HEAD), lambda h, j, s: (0, 0)),
                  pl.BlockSpec(w2.shape, lambda h, j, s: (0, 0)),
                  pl.BlockSpec(w2t.shape, lambda h, j, s: (0, 0)),
                  pl.BlockSpec(masks.shape, lambda h, j, s: (0, 0, 0)),
                  pl.BlockSpec((R, HEAD), lambda h, j, s: (nj - 1 - j, h)),
                  pl.BlockSpec((1, GROUP, HEAD, HEAD), lambda h, j, s: (h, nj - 1 - j, 0, 0)),
                  pl.BlockSpec((R, HEAD), lambda h, j, s: (nj - 1 - j, h))],
        out_specs=[pl.BlockSpec((R, HEAD), lambda h, j, s: (nj - 1 - j, s * H + h)),
                   pl.BlockSpec((1, HEAD), lambda h, j, s: (0, h)),
                   pl.BlockSpec((1, 1, HEAD), lambda h, j, s: (h, 0, 0))],
        out_shape=[jax.ShapeDtypeStruct((T, 4 * dm.AW), BF16), jax.ShapeDtypeStruct((1, dm.AW), F32),
                   jax.ShapeDtypeStruct((H, 1, HEAD), F32)],
        scratch_shapes=[pltpu.VMEM((HEAD, HEAD), F32), pltpu.VMEM((4, R, HEAD), BF16)],
        compiler_params=_cparams(("parallel", "arbitrary", "arbitrary")),
    )(main, main, main, main, lb_logits, norm_w, w2, w2t, masks, o_saved, states, dya)


def _log_sigmoid(x):
    return jnp.minimum(x, 0.0) - jnp.log(1.0 + jnp.exp(-jnp.abs(x)))


def _tri(n):
    return jnp.asarray(np.tril(np.ones((n, n), np.float32)), BF16)


def _cum_fwd(rest, bias, dm, tb=256):
    T = dm.T
    tb = min(tb, T)
    cb = 2 * dm.D // HEAD
    tri = _tri(tb)

    def body(x_ref, b_ref, tri_ref, o_ref, carry):
        i = pl.program_id(0)

        @pl.when(i == 0)
        def _():
            carry[...] = jnp.zeros_like(carry)

        lf = _log_sigmoid(x_ref[...] + b_ref[...])
        hi, mid, lo = _split3(lf)
        tr_ = tri_ref[...]
        c = _nn(tr_, hi) + _nn(tr_, mid) + _nn(tr_, lo) + carry[...]
        o_ref[...] = c
        carry[...] = c[tb - 1:tb, :]

    return pl.pallas_call(
        body, name="cum_fwd", grid=(T // tb,),
        in_specs=[_rows(tb, HEAD, cb), _vec(HEAD), pl.BlockSpec((tb, tb), lambda i: (0, 0))],
        out_specs=_rows(tb, HEAD),
        out_shape=jax.ShapeDtypeStruct((T, HEAD), F32),
        scratch_shapes=[pltpu.VMEM((1, HEAD), F32)],
        compiler_params=_cparams(("arbitrary",)),
    )(rest, bias, tri)


def _cum_bwd(dcum, rest, bias, dm, tb=256):
    T = dm.T
    tb = min(tb, T)
    nb = T // tb
    cb = 2 * dm.D // HEAD
    tri = _tri(tb)

    def body(d_ref, x_ref, b_ref, tri_ref, o_ref, db_ref, carry):
        i = pl.program_id(0)

        @pl.when(i == 0)
        def _():
            carry[...] = jnp.zeros_like(carry)
            db_ref[...] = jnp.zeros_like(db_ref)

        hi, mid, lo = _split3(d_ref[...])
        tr_ = tri_ref[...]
        dlf = _tn(tr_, hi) + _tn(tr_, mid) + _tn(tr_, lo) + carry[...]
        carry[...] = dlf[0:1, :]
        dx = dlf * _sigmoid(-(x_ref[...] + b_ref[...]))
        o_ref[...] = dx.astype(BF16)
        db_ref[...] += _colsum(dx)

    return pl.pallas_call(
        body, name="cum_bwd", grid=(nb,),
        in_specs=[pl.BlockSpec((tb, HEAD), lambda i: (nb - 1 - i, 0)),
                  pl.BlockSpec((tb, HEAD), lambda i: (nb - 1 - i, cb)),
                  _vec(HEAD), pl.BlockSpec((tb, tb), lambda i: (0, 0))],
        out_specs=[pl.BlockSpec((tb, HEAD), lambda i: (nb - 1 - i, 0)), _vec(HEAD)],
        out_shape=[jax.ShapeDtypeStruct((T, HEAD), BF16), jax.ShapeDtypeStruct((1, HEAD), F32)],
        scratch_shapes=[pltpu.VMEM((1, HEAD), F32)],
        compiler_params=_cparams(("arbitrary",)),
    )(dcum, rest, bias, tri)


def _fox_pairs(nq, kv_major):
    if kv_major:
        pairs = [(i, j) for j in range(nq) for i in range(j, nq)]
    else:
        pairs = [(i, j) for i in range(nq) for j in range(i + 1)]
    qi = jnp.asarray(np.array([p[0] for p in pairs], np.int32))
    kj = jnp.asarray(np.array([p[1] for p in pairs], np.int32))
    return qi, kj


class Fused(NamedTuple):
    ins: tuple
    outs: tuple
    sems: tuple
    hooks: object


def _fused_parts(fused):
    if fused is None:
        return (), (), (), 0, 0
    return tuple(fused.ins), tuple(fused.outs), tuple(fused.sems), len(fused.ins), len(fused.outs)


FOX_FWD_SPLIT = 4
FOX_BWD_SPLIT = 2


def _fox_fwd(main, cum_row, dm, fused=None, tq=512):
    T, H = dm.T, dm.H
    tq = min(tq, T)
    nq = T // tq
    qi_tab, kj_tab = _fox_pairs(nq, kv_major=False)
    npairs = int(qi_tab.shape[0])
    x_ins, x_outs, x_sems, n_in, n_out = _fused_parts(fused)
    ns = FOX_FWD_SPLIT if tq % (8 * FOX_FWD_SPLIT) == 0 else 1
    rq = tq // ns

    def body(qi_ref, kj_ref, q_ref, k_ref, v_ref, ck_ref, *rest):
        x_in, (o_ref, lse_ref) = rest[:n_in], rest[n_in:n_in + 2]
        x_out = rest[n_in + 2:n_in + 2 + n_out]
        m_ref, l_ref, acc_ref = rest[n_in + 2 + n_out:n_in + 5 + n_out]
        hd = pl.program_id(0)
        t = pl.program_id(1)
        i = qi_ref[t]
        j = kj_ref[t]
        if fused is not None:
            start, middle, finish = fused.hooks(x_in, x_out, *rest[n_in + 5 + n_out:])
            pl.when((hd == 0) & (t == 0))(start)
            pl.when((hd == H // 2) & (t == 0))(middle)

        @pl.when(j == 0)
        def _():
            m_ref[...] = jnp.full_like(m_ref, NEG)
            l_ref[...] = jnp.zeros_like(l_ref)
            acc_ref[...] = jnp.zeros_like(acc_ref)

        def step(on_diagonal):
            kb = k_ref[...].astype(BF16)
            vb = v_ref[...].astype(BF16)
            ck = ck_ref[0]
            q_all, m_all, l_all, acc_all = q_ref[...], m_ref[...], l_ref[...], acc_ref[...]
            m_out, l_out, acc_out = [], [], []
            for g in range(ns):
                rows = slice(g * rq, (g + 1) * rq)
                qs = (q_all[rows] * (HEAD ** -0.5)).astype(BF16)
                s = _nt(qs, kb) - ck
                if on_diagonal:
                    row = g * rq + lax.broadcasted_iota(jnp.int32, s.shape, 0)
                    s = jnp.where(row >= lax.broadcasted_iota(jnp.int32, s.shape, 1), s, NEG)
                m_new = jnp.maximum(m_all[rows], jnp.max(s, axis=-1, keepdims=True))
                a = jnp.exp(m_all[rows] - m_new)
                p = jnp.exp(s - m_new)
                m_out.append(m_new)
                l_out.append(a * l_all[rows] + jnp.sum(p, axis=-1, keepdims=True))
                acc_out.append(a * acc_all[rows] + _nn(p.astype(BF16), vb))
            m_ref[...] = jnp.concatenate(m_out, axis=0)
            l_ref[...] = jnp.concatenate(l_out, axis=0)
            acc_ref[...] = jnp.concatenate(acc_out, axis=0)

        @pl.when(j < i)
        def _():
            step(False)

        @pl.when(j == i)
        def _():
            step(True)
            l = l_ref[...]
            o_ref[...] = acc_ref[...] / l
            lse_ref[0] = m_ref[...] + jnp.log(l)

        if fused is not None:
            pl.when((hd == H - 1) & (t == npairs - 1))(finish)

    outs = pl.pallas_call(
        body, name="fox_fwd",
        grid_spec=pltpu.PrefetchScalarGridSpec(
            num_scalar_prefetch=2, grid=(H, npairs),
            in_specs=[pl.BlockSpec((tq, HEAD), lambda h, t, qi, kj: (qi[t], 4 * H + h)),
                      pl.BlockSpec((tq, HEAD), lambda h, t, qi, kj: (kj[t], 5 * H + h)),
                      pl.BlockSpec((tq, HEAD), lambda h, t, qi, kj: (kj[t], 6 * H + h)),
                      pl.BlockSpec((1, 1, tq), lambda h, t, qi, kj: (h, 0, kj[t]))] + [HBM] * n_in,
            out_specs=[pl.BlockSpec((tq, HEAD), lambda h, t, qi, kj: (qi[t], h)),
                       pl.BlockSpec((1, tq, 1), lambda h, t, qi, kj: (h, qi[t], 0))] + [HBM] * n_out,
            scratch_shapes=[pltpu.VMEM((tq, 1), F32), pltpu.VMEM((tq, 1), F32), pltpu.VMEM((tq, HEAD), F32)]
            + list(x_sems)),
        out_shape=[jax.ShapeDtypeStruct((T, dm.AW), F32), jax.ShapeDtypeStruct((H, T, 1), F32)] + list(x_outs),
        compiler_params=_cparams(("arbitrary", "arbitrary")),
    )(qi_tab, kj_tab, main, main, main, cum_row, *x_ins)
    return outs[0], outs[1], tuple(outs[2:])


def _fox_delta(do, o, dm, tr=256):
    T, H = dm.T, dm.H
    tr = min(tr, T)

    def body(do_ref, o_ref, d_ref):
        d_ref[0] = jnp.sum(do_ref[...] * o_ref[...], axis=-1, keepdims=True)

    return pl.pallas_call(
        body, name="fox_delta", grid=(H, T // tr),
        in_specs=[pl.BlockSpec((tr, HEAD), lambda h, i: (i, h)), pl.BlockSpec((tr, HEAD), lambda h, i: (i, h))],
        out_specs=pl.BlockSpec((1, tr, 1), lambda h, i: (h, i, 0)),
        out_shape=jax.ShapeDtypeStruct((H, T, 1), F32),
        compiler_params=_cparams(("parallel", "parallel")),
    )(do, o)


def _fox_bwd(main, cum_row, lse, delta, do, dm, fused=None, tq=512):
    T, H = dm.T, dm.H
    tq = min(tq, T)
    nq = T // tq
    qi_tab, kj_tab = _fox_pairs(nq, kv_major=True)
    npairs = int(qi_tab.shape[0])
    x_ins, x_outs, x_sems, n_in, n_out = _fused_parts(fused)
    ns = FOX_BWD_SPLIT if tq % (16 * FOX_BWD_SPLIT) == 0 else 1
    rq = tq // ns

    def body(qi_ref, kj_ref, q_ref, k_ref, v_ref, ck_ref, lse_ref, dl_ref, do_ref, *rest):
        x_in = rest[:n_in]
        dq_ref, dk_ref, dv_ref, dc_ref, dr_ref = rest[n_in:n_in + 5]
        x_out = rest[n_in + 5:n_in + 5 + n_out]
        dk_acc, dv_acc, dc_acc = rest[n_in + 5 + n_out:n_in + 8 + n_out]
        hd = pl.program_id(0)
        t = pl.program_id(1)
        i = qi_ref[t]
        kj = kj_ref[t]
        if fused is not None:
            start, middle, finish = fused.hooks(x_in, x_out, *rest[n_in + 8 + n_out:])
            pl.when((hd == 0) & (t == 0))(start)
            pl.when((hd == H // 2) & (t == 0))(middle)

        @pl.when(t == 0)
        def _():
            dq_ref[...] = jnp.zeros_like(dq_ref)
            dr_ref[...] = jnp.zeros_like(dr_ref)

        @pl.when(i == kj)
        def _():
            dk_acc[...] = jnp.zeros_like(dk_acc)
            dv_acc[...] = jnp.zeros_like(dv_acc)
            dc_acc[...] = jnp.zeros_like(dc_acc)

        def step(on_diagonal):
            kb = k_ref[...].astype(BF16)
            vb = v_ref[...].astype(BF16)
            ck = ck_ref[0]
            q_all, do_all, lse_all, dl_all = q_ref[...], do_ref[...], lse_ref[0], dl_ref[0]
            dq_g, dr_g, dv_c, dk_c, dc_c = [], [], None, None, None
            for g in range(ns):
                rows = slice(g * rq, (g + 1) * rq)
                qs = (q_all[rows] * (HEAD ** -0.5)).astype(BF16)
                s = _nt(qs, kb) - ck
                if on_diagonal:
                    row = g * rq + lax.broadcasted_iota(jnp.int32, s.shape, 0)
                    s = jnp.where(row >= lax.broadcasted_iota(jnp.int32, s.shape, 1), s, NEG)
                dob = do_all[rows].astype(BF16)
                p = jnp.exp(s - lse_all[rows])
                ds = p * (_nt(dob, vb) - dl_all[rows])
                dsb = ds.astype(BF16)
                dq_g.append(_nn(dsb, kb) * (HEAD ** -0.5))
                dr_g.append(jnp.sum(ds, axis=-1, keepdims=True))
                dv_g, dk_g, dc_g = _tn(p.astype(BF16), dob), _tn(dsb, qs), _colsum(ds)
                dv_c = dv_g if dv_c is None else dv_c + dv_g
                dk_c = dk_g if dk_c is None else dk_c + dk_g
                dc_c = dc_g if dc_c is None else dc_c + dc_g
            dv_acc[...] += dv_c
            dk_acc[...] += dk_c
            dc_acc[...] -= dc_c
            out_rows = pl.ds(pl.multiple_of(i * tq, tq), tq)
            dq_ref[out_rows, :] += jnp.concatenate(dq_g, axis=0)
            dr_ref[0, out_rows, :] += jnp.concatenate(dr_g, axis=0)

        @pl.when(i == kj)
        def _():
            step(True)

        @pl.when(i > kj)
        def _():
            step(False)

        @pl.when(i == nq - 1)
        def _():
            dk_ref[...] = dk_acc[...].astype(BF16)
            dv_ref[...] = dv_acc[...].astype(BF16)
            dc_ref[0] = dc_acc[...]

        if fused is not None:
            pl.when((hd == H - 1) & (t == npairs - 1))(finish)

    qcol = pl.BlockSpec((1, tq, 1), lambda h, t, qi, kj: (h, qi[t], 0))
    outs = pl.pallas_call(
        body, name="fox_bwd",
        grid_spec=pltpu.PrefetchScalarGridSpec(
            num_scalar_prefetch=2, grid=(H, npairs),
            in_specs=[pl.BlockSpec((tq, HEAD), lambda h, t, qi, kj: (qi[t], 4 * H + h)),
                      pl.BlockSpec((tq, HEAD), lambda h, t, qi, kj: (kj[t], 5 * H + h)),
                      pl.BlockSpec((tq, HEAD), lambda h, t, qi, kj: (kj[t], 6 * H + h)),
                      pl.BlockSpec((1, 1, tq), lambda h, t, qi, kj: (h, 0, kj[t])),
                      qcol, qcol,
                      pl.BlockSpec((tq, HEAD), lambda h, t, qi, kj: (qi[t], h))] + [HBM] * n_in,
            out_specs=[pl.BlockSpec((T, HEAD), lambda h, t, qi, kj: (0, h)),
                       pl.BlockSpec((tq, HEAD), lambda h, t, qi, kj: (kj[t], h)),
                       pl.BlockSpec((tq, HEAD), lambda h, t, qi, kj: (kj[t], h)),
                       pl.BlockSpec((1, 1, tq), lambda h, t, qi, kj: (h, 0, kj[t])),
                       pl.BlockSpec((1, T, 1), lambda h, t, qi, kj: (h, 0, 0))] + [HBM] * n_out,
            scratch_shapes=[pltpu.VMEM((tq, HEAD), F32), pltpu.VMEM((tq, HEAD), F32), pltpu.VMEM((1, tq), F32)]
            + list(x_sems)),
        out_shape=[jax.ShapeDtypeStruct((T, dm.AW), F32), jax.ShapeDtypeStruct((T, dm.AW), BF16),
                   jax.ShapeDtypeStruct((T, dm.AW), BF16), jax.ShapeDtypeStruct((H, 1, T), F32),
                   jax.ShapeDtypeStruct((H, T, 1), F32)] + list(x_outs),
        compiler_params=_cparams(("arbitrary", "arbitrary")),
    )(qi_tab, kj_tab, main, main, main, cum_row, lse, delta, do, *x_ins)
    return outs[:5], tuple(outs[5:])


FOX_QG = 256


def _fox_sT(kb, q_rows, ck, g, on_diagonal):
    qs = (q_rows * (HEAD ** -0.5)).astype(BF16)
    reps = q_rows.shape[0] // HEAD
    sT = _nt(kb, qs) - (jnp.concatenate([ck] * reps, axis=1) if reps > 1 else ck)
    if on_diagonal:
        key = lax.broadcasted_iota(jnp.int32, sT.shape, 0)
        qry = g * q_rows.shape[0] + lax.broadcasted_iota(jnp.int32, sT.shape, 1)
        sT = jnp.where(key <= qry, sT, NEG)
    return sT, qs


def _foxt_fwd(main, ckb, dm, fused=None, tq=512):
    T, H = dm.T, dm.H
    tq = min(tq, T)
    nq = T // tq
    qg = min(FOX_QG, tq)
    ns = tq // qg
    qi_tab, kj_tab = _fox_pairs(nq, kv_major=False)
    npairs = int(qi_tab.shape[0])
    x_ins, x_outs, x_sems, n_in, n_out = _fused_parts(fused)

    def body(qi_ref, kj_ref, q_ref, k_ref, v_ref, ck_ref, *rest):
        x_in, (o_ref, lse_ref) = rest[:n_in], rest[n_in:n_in + 2]
        x_out = rest[n_in + 2:n_in + 2 + n_out]
        m_ref, l_ref, acc_ref = rest[n_in + 2 + n_out:n_in + 5 + n_out]
        hd = pl.program_id(0)
        t = pl.program_id(1)
        i = qi_ref[t]
        j = kj_ref[t]
        if fused is not None:
            start, middle, finish = fused.hooks(x_in, x_out, *rest[n_in + 5 + n_out:])
            pl.when((hd == 0) & (t == 0))(start)
            pl.when((hd == H // 2) & (t == 0))(middle)

        @pl.when(j == 0)
        def _():
            m_ref[...] = jnp.full_like(m_ref, NEG)
            l_ref[...] = jnp.zeros_like(l_ref)
            acc_ref[...] = jnp.zeros_like(acc_ref)

        def step(on_diagonal):
            kb = k_ref[...].astype(BF16)
            vb = v_ref[...].astype(BF16)
            ck = ck_ref[0]
            scores = [_fox_sT(kb, q_ref[g * qg:(g + 1) * qg, :], ck, g, on_diagonal)[0] for g in range(ns)]
            for g in range(ns):
                cols = slice(g * qg, (g + 1) * qg)
                sT = scores[g]
                m_old = m_ref[:, cols]
                m_new = jnp.maximum(m_old, jnp.max(sT, axis=0, keepdims=True))
                a = jnp.exp(m_old - m_new)
                pT = jnp.exp(sT - m_new)
                l_ref[:, cols] = a * l_ref[:, cols] + jnp.sum(pT, axis=0, keepdims=True)
                acc_ref[:, cols] = a * acc_ref[:, cols] + _tn(vb, pT.astype(BF16))
                m_ref[:, cols] = m_new

        @pl.when(j < i)
        def _():
            step(False)

        @pl.when(j == i)
        def _():
            step(True)
            l = l_ref[...]
            o_ref[...] = acc_ref[...] / l
            lse_ref[0] = m_ref[...] + jnp.log(l)

        if fused is not None:
            pl.when((hd == H - 1) & (t == npairs - 1))(finish)

    outs = pl.pallas_call(
        body, name="fox_fwd",
        grid_spec=pltpu.PrefetchScalarGridSpec(
            num_scalar_prefetch=2, grid=(H, npairs),
            in_specs=[pl.BlockSpec((tq, HEAD), lambda h, t, qi, kj: (qi[t], 4 * H + h)),
                      pl.BlockSpec((tq, HEAD), lambda h, t, qi, kj: (kj[t], 5 * H + h)),
                      pl.BlockSpec((tq, HEAD), lambda h, t, qi, kj: (kj[t], 6 * H + h)),
                      pl.BlockSpec((1, tq, HEAD), lambda h, t, qi, kj: (h, kj[t], 0))] + [HBM] * n_in,
            out_specs=[pl.BlockSpec((HEAD, tq), lambda h, t, qi, kj: (h, qi[t])),
                       pl.BlockSpec((1, 1, tq), lambda h, t, qi, kj: (h, 0, qi[t]))] + [HBM] * n_out,
            scratch_shapes=[pltpu.VMEM((1, tq), F32), pltpu.VMEM((1, tq), F32), pltpu.VMEM((HEAD, tq), F32)]
            + list(x_sems)),
        out_shape=[jax.ShapeDtypeStruct((dm.AW, T), F32), jax.ShapeDtypeStruct((H, 1, T), F32)] + list(x_outs),
        compiler_params=_cparams(("arbitrary", "arbitrary")),
    )(qi_tab, kj_tab, main, main, main, ckb, *x_ins)
    return outs[0], outs[1], tuple(outs[2:])


def _foxt_delta(doT, oT, dm, tc=1024):
    T, H = dm.T, dm.H
    tc = min(tc, T)

    def body(do_ref, o_ref, d_ref):
        d_ref[0] = jnp.sum(do_ref[...] * o_ref[...], axis=0, keepdims=True)

    blk = pl.BlockSpec((HEAD, tc), lambda h, i: (h, i))
    return pl.pallas_call(
        body, name="fox_delta", grid=(H, T // tc),
        in_specs=[blk, blk], out_specs=pl.BlockSpec((1, 1, tc), lambda h, i: (h, 0, i)),
        out_shape=jax.ShapeDtypeStruct((H, 1, T), F32),
        compiler_params=_cparams(("parallel", "parallel")),
    )(doT, oT)


def _foxt_bwd(main, ckb, lse, delta, doT, dm, fused=None, tq=512):
    T, H = dm.T, dm.H
    tq = min(tq, T)
    nq = T // tq
    qg = min(FOX_QG, tq)
    ns = tq // qg
    qi_tab, kj_tab = _fox_pairs(nq, kv_major=True)
    npairs = int(qi_tab.shape[0])
    x_ins, x_outs, x_sems, n_in, n_out = _fused_parts(fused)

    def body(qi_ref, kj_ref, q_ref, k_ref, v_ref, ck_ref, lse_ref, dl_ref, do_ref, *rest):
        x_in = rest[:n_in]
        dq_ref, dk_ref, dv_ref, dr_ref, dc_ref = rest[n_in:n_in + 5]
        x_out = rest[n_in + 5:n_in + 5 + n_out]
        dq_acc, dk_acc, dv_acc = rest[n_in + 5 + n_out:n_in + 8 + n_out]
        hd = pl.program_id(0)
        t = pl.program_id(1)
        i = qi_ref[t]
        kj = kj_ref[t]
        if fused is not None:
            start, middle, finish = fused.hooks(x_in, x_out, *rest[n_in + 8 + n_out:])
            pl.when((hd == 0) & (t == 0))(start)
            pl.when((hd == H // 2) & (t == 0))(middle)

        @pl.when(t == 0)
        def _():
            dq_acc[...] = jnp.zeros_like(dq_acc)
            dr_ref[...] = jnp.zeros_like(dr_ref)

        @pl.when(i == kj)
        def _():
            dk_acc[...] = jnp.zeros_like(dk_acc)
            dv_acc[...] = jnp.zeros_like(dv_acc)

        def step(on_diagonal):
            kb = k_ref[...].astype(BF16)
            vb = v_ref[...].astype(BF16)
            ck = ck_ref[0]
            ones = jnp.ones((qg, HEAD), BF16)
            dq_g, dr_g = [], []
            ahead = []
            for g in range(ns):
                cols = slice(g * qg, (g + 1) * qg)
                sT, qs = _fox_sT(kb, q_ref[cols, :], ck, g, on_diagonal)
                dob = do_ref[:, cols].astype(BF16)
                ahead.append((sT, qs, dob, _nn(vb, dob)))
            for g in range(ns):
                cols = slice(g * qg, (g + 1) * qg)
                sT, qs, dob, dpT = ahead[g]
                pT = jnp.exp(sT - lse_ref[0, :, cols])
                dsT = pT * (dpT - dl_ref[0, :, cols])
                dsb = dsT.astype(BF16)
                dv_acc[...] += _nt(pT.astype(BF16), dob)
                dk_acc[...] += _nn(dsb, jnp.concatenate([qs, ones], axis=1))
                dq_g.append(_tn(kb, dsb) * (HEAD ** -0.5))
                dr_g.append(_nn(jnp.ones((8, tq), BF16), dsb)[0:1])
            dq_acc[i] += jnp.concatenate(dq_g, axis=1)
            dr_ref[0, pl.ds(i, 1), :] += jnp.concatenate(dr_g, axis=1)

        @pl.when(i == kj)
        def _():
            step(True)

        @pl.when(i > kj)
        def _():
            step(False)

        @pl.when(i == nq - 1)
        def _():
            acc = dk_acc[...]
            dk_ref[...] = acc[:, :HEAD].astype(BF16)
            dc_ref[0] = acc[:, HEAD:]
            dv_ref[...] = dv_acc[...].astype(BF16)

        @pl.when(t == npairs - 1)
        def _():
            for b in range(nq):
                dq_ref[b * tq:(b + 1) * tq, :] = dq_acc[b].T.astype(BF16)

        if fused is not None:
            pl.when((hd == H - 1) & (t == npairs - 1))(finish)

    qrow = pl.BlockSpec((1, 1, tq), lambda h, t, qi, kj: (h, 0, qi[t]))
    outs = pl.pallas_call(
        body, name="fox_bwd",
        grid_spec=pltpu.PrefetchScalarGridSpec(
            num_scalar_prefetch=2, grid=(H, npairs),
            in_specs=[pl.BlockSpec((tq, HEAD), lambda h, t, qi, kj: (qi[t], 4 * H + h)),
                      pl.BlockSpec((tq, HEAD), lambda h, t, qi, kj: (kj[t], 5 * H + h)),
                      pl.BlockSpec((tq, HEAD), lambda h, t, qi, kj: (kj[t], 6 * H + h)),
                      pl.BlockSpec((1, tq, HEAD), lambda h, t, qi, kj: (h, kj[t], 0)),
                      qrow, qrow,
                      pl.BlockSpec((HEAD, tq), lambda h, t, qi, kj: (h, qi[t]))] + [HBM] * n_in,
            out_specs=[pl.BlockSpec((T, HEAD), lambda h, t, qi, kj: (0, h)),
                       pl.BlockSpec((tq, HEAD), lambda h, t, qi, kj: (kj[t], h)),
                       pl.BlockSpec((tq, HEAD), lambda h, t, qi, kj: (kj[t], h)),
                       pl.BlockSpec((1, nq, tq), lambda h, t, qi, kj: (h, 0, 0)),
                       pl.BlockSpec((1, tq, HEAD), lambda h, t, qi, kj: (h, kj[t], 0))] + [HBM] * n_out,
            scratch_shapes=[pltpu.VMEM((nq, HEAD, tq), F32), pltpu.VMEM((tq, 2 * HEAD), F32),
                            pltpu.VMEM((tq, HEAD), F32)] + list(x_sems)),
        out_shape=[jax.ShapeDtypeStruct((T, dm.AW), BF16), jax.ShapeDtypeStruct((T, dm.AW), BF16),
                   jax.ShapeDtypeStruct((T, dm.AW), BF16), jax.ShapeDtypeStruct((H, nq, tq), F32),
                   jax.ShapeDtypeStruct((H, T, HEAD), F32)] + list(x_outs),
        compiler_params=_cparams(("arbitrary", "arbitrary")),
    )(qi_tab, kj_tab, main, main, main, ckb, lse, delta, doT, *x_ins)
    return outs[:5], tuple(outs[5:])


def _local_step(dm, x, tgt, w_main, w_rest, later_weights, bias_p, lb_logits, norm_w, g1, g2, g3, g4,
                early_reduce=None):
    T, D, H = dm.T, dm.D, dm.H
    h1, r1 = _rms_fwd(x, g1)
    main = _mm(h1, w_main, "nn", F32, "proj_main")
    rest = _mm(h1, w_rest, "nn", F32, "proj_rest", tn=1408)
    ya, o_a, states = _hgrn_fwd(main, lb_logits, norm_w, dm)
    cum = _cum_fwd(rest, bias_p, dm)
    ckb = jnp.broadcast_to(cum[:, :H].T[:, :, None], (H, T, HEAD))
    if isinstance(later_weights, Fused):
        oT_b, lse, later_weights = _foxt_fwd(main, ckb, dm, fused=later_weights)
    else:
        oT_b, lse, _ = _foxt_fwd(main, ckb, dm)
    w_up_a, w_up_b, w_o, w_ffn_in, w_ffn_down = later_weights
    y_a = _mm(ya, w_up_a, "nn", F32, "up_a")
    y_b = _mm(oT_b, w_up_b, "tn", F32, "up_b")
    merged = _merge_fwd(rest, y_a, y_b)
    u = _mm(merged, w_o, "nn", F32, "w_o")
    x1, r2, h3, r3 = _post_pre(x, u, g2, g3)
    gu = _mm(h3, w_ffn_in, "nn", F32, "ffn_in")
    act = _swiglu_fwd(gu)
    w = _mm(act, w_ffn_down, "nn", F32, "ffn_down")
    loss, dy, dw, dg4 = _loss_bwd(x1, w, g4, tgt)
    dact = _mm(dw, w_ffn_down, "nt", BF16, "d_act", tn=1408)
    d_ffn_down = _mm(act, dw, "tn", F32, "dw_ffn_down", tm=1408)
    dgu = _swiglu_bwd(gu, dact)
    dh3 = _mm(dgu, w_ffn_in, "nt", F32, "d_h3")
    d_ffn_in = _mm(h3, dgu, "tn", F32, "dw_ffn_in")
    dx1, du, dg3, dg2 = _rms_bwd2(dy, dh3, x1, r3, g3, u, r2, g2)
    dmerged = _mm(du, w_o, "nt", F32, "d_merged")
    d_w_o = _mm(merged, du, "tn", F32, "dw_o")
    dy_a, dy_b, dgates = _merge_bwd(dmerged, rest, y_a, y_b)
    dya = _mm(dy_a, w_up_a, "nt", F32, "d_ya")
    d_up_a = _mm(ya, dy_a, "tn", F32, "dw_up_a")
    doT = _mm(w_up_b, dy_b, "nt", F32, "d_ob")
    d_up_b = _mm(oT_b, dy_b, "nn", F32, "dw_up_b")
    d_a, dlb, dnw_h = _hgrn_bwd(main, lb_logits, norm_w, o_a, states, dya, dm)
    fused = early_reduce([d_up_a, d_up_b, d_w_o, d_ffn_in, d_ffn_down]) if early_reduce is not None else None
    (dq_b, dk_b, dv_b, d_over_keys, d_over_queries), early = _foxt_bwd(
        main, ckb, lse, _foxt_delta(doT, oT_b, dm), doT, dm, fused=fused)
    dcum = jnp.pad((d_over_keys.reshape(H, T) - d_over_queries[:, :, 0]).T, ((0, 0), (0, HEAD - H)))
    dbf, dbias = _cum_bwd(dcum, rest, bias_p, dm)
    dmain = jnp.concatenate([d_a, dq_b, dk_b, dv_b], axis=1)
    drest = jnp.concatenate([dgates, dbf], axis=1)
    dh1 = _mm(dmain, w_main, "nt", F32, "d_h1_main")
    dh1 = _mm(drest, w_rest, "nt", F32, "d_h1_rest", add=dh1)
    d_main = _mm(h1, dmain, "tn", F32, "dw_main")
    d_rest = _mm(h1, drest, "tn", F32, "dw_rest", tn=1408)
    dx, dg1 = _rms_bwd1(dx1, dh1, x, r1, g1)
    big = dict(main=d_main, rest=d_rest, up_a=d_up_a, up_b=d_up_b, w_o=d_w_o, ffn_in=d_ffn_in, ffn_down=d_ffn_down)
    small = dict(loss=loss, bias=dbias, norm_w=jnp.sum(dnw_h, axis=0), lb=dlb, g1=dg1, g2=dg2, g3=dg3, g4=dg4)
    return dx, big, small, early


HBM = pl.BlockSpec(memory_space=pltpu.HBM)


def _place():
    x, y, c = lax.axis_index("x"), lax.axis_index("y"), lax.axis_index("c")
    chips = [(1 - x, y), (x, 1 - y), (1 - x, 1 - y)]
    return x, y, c, chips


class Sharded(NamedTuple):
    kind: str
    r: int
    c: int

    @property
    def full(self):
        return (self.r, 4 * self.c) if self.kind == "col" else (4 * self.r, self.c)

    @property
    def half(self):
        return (self.r // 2, self.c) if self.kind == "col" else (self.r, self.c // 2)

    @property
    def half_of_full(self):
        return (self.r // 2, 4 * self.c) if self.kind == "col" else (4 * self.r, self.c // 2)

    def shard_window(self, ref, s):
        if self.kind == "col":
            return ref.at[:, pl.ds(pl.multiple_of(s * self.c, 128), self.c)]
        return ref.at[pl.ds(pl.multiple_of(s * self.r, 16), self.r), :]

    def half_window(self, ref, s, h):
        if self.kind == "col":
            return ref.at[pl.ds(pl.multiple_of(h * (self.r // 2), 16), self.r // 2),
                          pl.ds(pl.multiple_of(s * self.c, 128), self.c)]
        return ref.at[pl.ds(pl.multiple_of(s * self.r, 16), self.r),
                      pl.ds(pl.multiple_of(h * (self.c // 2), 128), self.c // 2)]

    def half_of(self, ref, h):
        if self.kind == "col":
            n = ref.shape[0] // 2
            return ref.at[pl.ds(pl.multiple_of(h * n, 16), n), :]
        n = ref.shape[1] // 2
        return ref.at[:, pl.ds(pl.multiple_of(h * n, 128), n)]

    def window_of_half(self, ref, s):
        if self.kind == "col":
            return ref.at[:, pl.ds(pl.multiple_of(s * self.c, 128), self.c)]
        return ref.at[pl.ds(pl.multiple_of(s * self.r, 16), self.r), :]


def _gather_hooks(specs):
    n = len(specs)

    def hooks(w_refs, f_refs, send_sems, recv_sems):
        x, y, c, chips = _place()
        q = 2 * x + y
        sibling = (x, y, 1 - c)

        def copy(t, k, src, dst, to):
            return pltpu.make_async_remote_copy(
                src_ref=src, dst_ref=dst, send_sem=send_sems.at[7 * t + k], recv_sem=recv_sems.at[7 * t + k],
                device_id=to, device_id_type=MESH)

        def over_ici(t, j, chip_from, to):
            src = specs[t].half_of(w_refs[t], c)
            return copy(t, j, src, specs[t].half_window(f_refs[t], 2 * chip_from[0] + chip_from[1], c), to)

        def passed_on(t, j, chip_from, half):
            win = specs[t].half_window(f_refs[t], 2 * chip_from[0] + chip_from[1], half)
            return copy(t, 3 + j, win, win, sibling)

        def own_shard(t):
            return copy(t, 6, w_refs[t], specs[t].shard_window(f_refs[t], q), sibling)

        def start():
            for t in range(n):
                own_shard(t).start()
                for j, chip in enumerate(chips):
                    over_ici(t, j, (x, y), (*chip, c)).start()

        def middle():
            for t in range(n):
                for j, chip in enumerate(chips):
                    over_ici(t, j, chip, sibling).wait_recv()
                    passed_on(t, j, chip, c).start()

        def finish():
            for t in range(n):
                own_shard(t).wait_recv()
                for j, chip in enumerate(chips):
                    passed_on(t, j, chip, 1 - c).wait_recv()
            for t in range(n):
                own_shard(t).wait_send()
                for j, chip in enumerate(chips):
                    over_ici(t, j, (x, y), (*chip, c)).wait_send()
                    passed_on(t, j, chip, c).wait_send()

        return start, middle, finish

    sems = (pltpu.SemaphoreType.DMA((7 * n,)), pltpu.SemaphoreType.DMA((7 * n,)))
    return hooks, sems


def _gather_fused(shards, specs):
    hooks, sems = _gather_hooks(specs)
    outs = tuple(jax.ShapeDtypeStruct(sp.full, w.dtype) for sp, w in zip(specs, shards))
    return Fused(ins=tuple(shards), outs=outs, sems=sems, hooks=hooks)


def _run_fused(fused, name):
    n_in, n_out = len(fused.ins), len(fused.outs)

    def body(*refs):
        start, middle, finish = fused.hooks(refs[:n_in], refs[n_in:n_in + n_out], *refs[n_in + n_out:])
        start()
        middle()
        finish()

    return pl.pallas_call(
        body, name=name, in_specs=[HBM] * n_in, out_specs=[HBM] * n_out,
        out_shape=list(fused.outs), scratch_shapes=list(fused.sems),
    )(*fused.ins)


def _swap_halves(fulls, specs, name):
    n = len(fulls)

    def body(*refs):
        g_refs, o_refs = refs[:n], refs[n:2 * n]
        send_sems, recv_sems = refs[2 * n:]
        x, y, c, _ = _place()
        cps = [pltpu.make_async_remote_copy(
            src_ref=specs[t].half_of(g_refs[t], 1 - c), dst_ref=o_refs[t],
            send_sem=send_sems.at[t], recv_sem=recv_sems.at[t],
            device_id=(x, y, 1 - c), device_id_type=MESH) for t in range(n)]
        for cp in cps:
            cp.start()
        for cp in cps:
            cp.wait_recv()
        for cp in cps:
            cp.wait_send()

    return pl.pallas_call(
        body, name=name,
        in_specs=[HBM] * n, out_specs=[HBM] * n,
        out_shape=[jax.ShapeDtypeStruct(sp.half_of_full, g.dtype) for sp, g in zip(specs, fulls)],
        scratch_shapes=[pltpu.SemaphoreType.DMA((n,)), pltpu.SemaphoreType.DMA((n,))],
    )(*fulls)


def _rtile(n, target):
    t = min(n, (target // 16) * 16)
    while n % t:
        t -= 16
    return t


def _add_sibling(full, got, sp, cq_idx, name):
    hr, hc = sp.half
    tr = _rtile(hr, 256)
    nrt = hr // tr

    def body(cq_ref, g_ref, r_ref, ob_ref, of_ref):
        s = pl.program_id(1)
        v = g_ref[...] + r_ref[...]
        ob_ref[...] = v.astype(BF16)

        @pl.when(s == cq_ref[1])
        def _():
            of_ref[...] = v

    if sp.kind == "col":
        g_spec = pl.BlockSpec((tr, hc), lambda i, s, cq: (cq[0] * nrt + i, s))
        r_spec = pl.BlockSpec((tr, hc), lambda i, s, cq: (i, s))
    else:
        g_spec = pl.BlockSpec((tr, hc), lambda i, s, cq: (s * nrt + i, cq[0]))
        r_spec = pl.BlockSpec((tr, hc), lambda i, s, cq: (s * nrt + i, 0))
    return pl.pallas_call(
        body, name=name,
        grid_spec=pltpu.PrefetchScalarGridSpec(
            num_scalar_prefetch=1, grid=(nrt, 4),
            in_specs=[g_spec, r_spec],
            out_specs=[r_spec, pl.BlockSpec((tr, hc), lambda i, s, cq: (i, 0))]),
        out_shape=[jax.ShapeDtypeStruct(sp.half_of_full, BF16), jax.ShapeDtypeStruct(sp.half, F32)],
        compiler_params=_cparams(("parallel", "arbitrary")),
    )(cq_idx, full, got)


def _scatter_fused(sums, specs):
    n = len(sums)

    def hooks(a_refs, o_refs, send_sems, recv_sems):
        x, y, c, chips = _place()

        def copies():
            return [pltpu.make_async_remote_copy(
                src_ref=specs[t].window_of_half(a_refs[t], 2 * chip[0] + chip[1]), dst_ref=o_refs[t].at[j],
                send_sem=send_sems.at[3 * t + j], recv_sem=recv_sems.at[3 * t + j],
                device_id=(*chip, c), device_id_type=MESH) for t in range(n) for j, chip in enumerate(chips)]

        def start():
            for cp in copies():
                cp.start()

        def finish():
            for cp in copies():
                cp.wait_recv()
            for cp in copies():
                cp.wait_send()

        return start, lambda: None, finish

    outs = tuple(jax.ShapeDtypeStruct((3,) + sp.half, a.dtype) for sp, a in zip(specs, sums))
    sems = (pltpu.SemaphoreType.DMA((3 * n,)), pltpu.SemaphoreType.DMA((3 * n,)))
    return Fused(ins=tuple(sums), outs=outs, sems=sems, hooks=hooks)


def _add_chips(own, got, sp, cq_idx, name):
    hr, hc = sp.half
    tr = _rtile(hr, 256)
    nrt = hr // tr

    def body(cq_ref, a_ref, r_ref, o_ref):
        o_ref[...] = ((a_ref[...] + r_ref[0].astype(F32)) + r_ref[1].astype(F32)) + r_ref[2].astype(F32)

    if sp.kind == "col":
        o_spec = pl.BlockSpec((tr, hc), lambda i, cq: (cq[0] * nrt + i, 0))
    else:
        o_spec = pl.BlockSpec((tr, hc), lambda i, cq: (i, cq[0]))
    return pl.pallas_call(
        body, name=name,
        grid_spec=pltpu.PrefetchScalarGridSpec(
            num_scalar_prefetch=1, grid=(nrt,),
            in_specs=[pl.BlockSpec((tr, hc), lambda i, cq: (i, 0)), pl.BlockSpec((3, tr, hc), lambda i, cq: (0, i, 0))],
            out_specs=o_spec),
        out_shape=jax.ShapeDtypeStruct((sp.r, sp.c), F32),
        compiler_params=_cparams(("parallel",)),
    )(cq_idx, own, got)


def _share_halves(shards, specs):
    n = len(shards)

    def body(*refs):
        o_refs = refs[n:2 * n]
        send_sems, recv_sems = refs[2 * n:]
        x, y, c, _ = _place()

        def copy(t, half):
            win = specs[t].half_of(o_refs[t], half)
            return pltpu.make_async_remote_copy(
                src_ref=win, dst_ref=win, send_sem=send_sems.at[t], recv_sem=recv_sems.at[t],
                device_id=(x, y, 1 - c), device_id_type=MESH)

        for t in range(n):
            copy(t, c).start()
        for t in range(n):
            copy(t, 1 - c).wait_recv()
        for t in range(n):
            copy(t, c).wait_send()

    return pl.pallas_call(
        body, name="share_halves",
        in_specs=[HBM] * n, out_specs=[HBM] * n,
        out_shape=[jax.ShapeDtypeStruct((sp.r, sp.c), F32) for sp in specs],
        input_output_aliases={t: t for t in range(n)},
        scratch_shapes=[pltpu.SemaphoreType.DMA((n,)), pltpu.SemaphoreType.DMA((n,))],
    )(*shards)


def _sum_small(vec):
    rows, w = vec.shape

    def body(v_ref, o_ref, buf, send_sems, recv_sems):
        x, y, c, _ = _place()
        me = 4 * x + 2 * y + c
        buf[me] = v_ref[...]
        cps = []
        for k in range(1, 8):
            to = (x ^ (k >> 2), y ^ ((k >> 1) & 1), c ^ (k & 1))
            cps.append(pltpu.make_async_remote_copy(
                src_ref=v_ref, dst_ref=buf.at[me], send_sem=send_sems.at[k - 1], recv_sem=recv_sems.at[k - 1],
                device_id=to, device_id_type=MESH))
        for cp in cps:
            cp.start()
        for k in range(1, 8):
            pltpu.make_async_remote_copy(
                src_ref=v_ref, dst_ref=buf.at[me ^ k], send_sem=send_sems.at[k - 1], recv_sem=recv_sems.at[k - 1],
                device_id=(x, y, c), device_id_type=MESH).wait_recv()
        for cp in cps:
            cp.wait_send()
        total = buf[0]
        for d in range(1, 8):
            total = total + buf[d]
        o_ref[...] = total

    return pl.pallas_call(
        body, name="sum_small",
        in_specs=[pl.BlockSpec(memory_space=pltpu.VMEM)], out_specs=pl.BlockSpec(memory_space=pltpu.VMEM),
        out_shape=jax.ShapeDtypeStruct((rows, w), F32),
        scratch_shapes=[pltpu.VMEM((8, rows, w), F32), pltpu.SemaphoreType.DMA((7,)), pltpu.SemaphoreType.DMA((7,))],
    )(vec)


def _adam_math(w, g, m, v):
    m = ADAM_B1 * m + (1.0 - ADAM_B1) * g
    v = ADAM_B2 * v + (1.0 - ADAM_B2) * (g * g)
    m_hat = m / (1.0 - ADAM_B1 ** ADAM_STEP)
    v_hat = v / (1.0 - ADAM_B2 ** ADAM_STEP)
    delta = -ADAM_LR * (m_hat / (jnp.sqrt(v_hat) + ADAM_EPS) + ADAM_WD * w)
    return delta, m, v


def _adamw(w, g, m, v, name, tr=128):
    R, Cn = w.shape
    tr = min(tr, R)
    assert R % tr == 0

    def body(w_ref, g_ref, m_ref, v_ref, go_ref, d_ref, mo_ref, vo_ref):
        gv = g_ref[...]
        d, mn, vn = _adam_math(w_ref[...], gv, m_ref[...], v_ref[...])
        go_ref[...] = gv
        d_ref[...] = d
        mo_ref[...] = mn
        vo_ref[...] = vn

    blk = pl.BlockSpec((tr, Cn), lambda i: (i, 0))
    sds = jax.ShapeDtypeStruct((R, Cn), F32)
    return pl.pallas_call(
        body, name=name, grid=(R // tr,),
        in_specs=[blk] * 4, out_specs=[blk] * 4, out_shape=[sds] * 4,
        compiler_params=_cparams(("parallel",)),
    )(w, g, m, v)


ROW_LOSS, ROW_BIAS, ROW_NORM, ROW_LB0, ROW_G1, ROW_G2, ROW_G3, ROW_G4, ROW_LB1 = range(9)
SMALL_ROWS = 16


def _small_update(gsum, wp, mp, vp):
    _, w = gsum.shape

    def body(g_ref, w_ref, m_ref, v_ref, go_ref, d_ref, mo_ref, vo_ref):
        wv = w_ref[...]
        l0 = wv[ROW_LB0:ROW_LB0 + 1, :]
        l1 = wv[ROW_LB1:ROW_LB1 + 1, :]
        mx = jnp.maximum(l0, l1)
        e0 = jnp.exp(l0 - mx)
        e1 = jnp.exp(l1 - mx)
        p0 = e0 / (e0 + e1)
        gs = g_ref[...]
        dl0 = gs[ROW_LB0:ROW_LB0 + 1, :] * p0 * (1.0 - p0)
        row8 = lax.broadcasted_iota(jnp.int32, gs.shape, 0)
        top = jnp.where(row8 == ROW_LB0, dl0, jnp.where(row8 == ROW_LOSS, 0.0, gs))
        bot = jnp.where(row8 == ROW_LB1 - 8, -dl0, 0.0)
        g16 = jnp.concatenate([top, bot], axis=0)
        d, mn, vn = _adam_math(wv, g16, m_ref[...], v_ref[...])
        go_ref[...] = g16
        d_ref[...] = d
        mo_ref[...] = mn
        vo_ref[...] = vn

    sds = jax.ShapeDtypeStruct((SMALL_ROWS, w), F32)
    full = pl.BlockSpec(memory_space=pltpu.VMEM)
    return pl.pallas_call(
        body, name="small_update", in_specs=[full] * 4, out_specs=[full] * 4, out_shape=[sds] * 4,
    )(gsum, wp, mp, vp)


def _w_in_layout(dm):
    cs = dm.NIN // 4
    place = [((cs * q) // 128, (cs * q) % 128) for q in range(4)]
    cp = -(-(cs + max(sh for _, sh in place)) // 128) * 128
    return cs, cp, place


def _zeros_cols(rows, n, dtype):
    return jnp.zeros((rows, n), dtype)


def _unshuffle_w_in(wp, dm, tr=128):
    D, H = dm.D, dm.H
    cs, cp, place = _w_in_layout(dm)
    nm, nrest = dm.NMAIN, dm.NREST
    ng = nm + nrest
    tr = min(tr, D)

    def body(x_ref, main_ref, rest_ref):
        g = None
        for q, (t0, sh) in enumerate(place):
            xq = x_ref[:, q * cp:(q + 1) * cp]
            yq = pltpu.roll(xq, sh, axis=1) if sh else xq
            width = min(cp, ng - t0 * 128)
            parts = [_zeros_cols(tr, t0 * 128, wp.dtype)] if t0 else []
            parts.append(yq[:, :width])
            if ng - t0 * 128 - width:
                parts.append(_zeros_cols(tr, ng - t0 * 128 - width, wp.dtype))
            placed = jnp.concatenate(parts, axis=1)
            g = placed if g is None else g + placed
        main_ref[...] = g[:, :nm]
        tail = g[:, nm:]
        gates = pltpu.roll(tail, nrest - H, axis=1)[:, :2 * D]
        lane = lax.broadcasted_iota(jnp.int32, (tr, HEAD), 1)
        rest_ref[...] = jnp.concatenate([gates, jnp.where(lane < H, tail[:, :HEAD], 0)], axis=1)

    return pl.pallas_call(
        body, name="unshuffle_w_in", grid=(D // tr,),
        in_specs=[pl.BlockSpec((tr, 4 * cp), lambda i: (i, 0))],
        out_specs=[pl.BlockSpec((tr, nm), lambda i: (i, 0)), pl.BlockSpec((tr, nrest), lambda i: (i, 0))],
        out_shape=[jax.ShapeDtypeStruct((D, nm), wp.dtype), jax.ShapeDtypeStruct((D, nrest), wp.dtype)],
        compiler_params=_cparams(("parallel",)),
    )(wp)


def _shuffle_w_in(d_main, d_rest, dm, tr=64):
    D, H = dm.D, dm.H
    cs, cp, place = _w_in_layout(dm)
    nm, nrest = dm.NMAIN, dm.NREST
    ng = nm + nrest
    tr = min(tr, D)

    def body(m_ref, r_ref, o_ref):
        rv = r_ref[...]
        lane = lax.broadcasted_iota(jnp.int32, (tr, HEAD), 1)
        tail = pltpu.roll(jnp.concatenate([rv[:, :2 * D], _zeros_cols(tr, HEAD, F32)], axis=1), H, axis=1)
        head = jnp.where(lane < H, rv[:, 2 * D:], 0.0) + tail[:, :HEAD]
        g = jnp.concatenate([m_ref[...], head, tail[:, HEAD:]], axis=1)
        lanes = lax.broadcasted_iota(jnp.int32, (tr, cp), 1)
        outs = []
        for q, (t0, sh) in enumerate(place):
            width = min(cp, ng - t0 * 128)
            win = g[:, t0 * 128:t0 * 128 + width]
            if width < cp:
                win = jnp.concatenate([win, _zeros_cols(tr, cp - width, F32)], axis=1)
            xq = pltpu.roll(win, cp - sh, axis=1) if sh else win
            outs.append(jnp.where(lanes < cs, xq, 0.0))
        o_ref[...] = jnp.concatenate(outs, axis=1)

    return pl.pallas_call(
        body, name="shuffle_w_in", grid=(D // tr,),
        in_specs=[pl.BlockSpec((tr, nm), lambda i: (i, 0)), pl.BlockSpec((tr, nrest), lambda i: (i, 0))],
        out_specs=pl.BlockSpec((tr, 4 * cp), lambda i: (i, 0)),
        out_shape=jax.ShapeDtypeStruct((D, 4 * cp), F32),
        compiler_params=_cparams(("parallel",)),
    )(d_main, d_rest)


def _pack_small(dm, bias, norm_w, lb_logits, g1, g2, g3, g4):
    D = dm.D
    row = lambda v: jnp.pad(v.reshape(1, -1), ((0, 0), (0, D - v.size)))
    rows = [jnp.zeros((1, D), F32), row(bias), row(norm_w), row(lb_logits[0]), row(g1), row(g2), row(g3), row(g4),
            row(lb_logits[1]), jnp.zeros((SMALL_ROWS - 9, D), F32)]
    return jnp.concatenate(rows, axis=0)


def _unpack_small(p, dm):
    H, AW = dm.H, dm.AW
    return (p[ROW_BIAS:ROW_BIAS + 1, :H], jnp.concatenate([p[ROW_LB0:ROW_LB0 + 1, :AW], p[ROW_LB1:ROW_LB1 + 1, :AW]], axis=0),
            p[ROW_NORM:ROW_NORM + 1, :HEAD], p[ROW_G1:ROW_G1 + 1], p[ROW_G2:ROW_G2 + 1], p[ROW_G3:ROW_G3 + 1],
            p[ROW_G4:ROW_G4 + 1])


def _step(dm, x, w_in, b_fox_f, hgrn_lb_logits, hgrn_norm_w, w_up_a, w_up_b, w_o, norm_mix_pre, norm_mix_post,
          norm_ffn_pre, norm_ffn_post, w_ffn_in, w_ffn_down, loss_target, moments_m, moments_v):
    xi, yi, ci = lax.axis_index("x"), lax.axis_index("y"), lax.axis_index("c")
    cq_idx = jnp.stack([ci, 2 * xi + yi]).astype(jnp.int32)
    D, AW, FF = dm.D, dm.AW, dm.FF
    cs, cp, _ = _w_in_layout(dm)
    big_names = ["w_in", "w_up_a", "w_up_b", "w_o", "w_ffn_in", "w_ffn_down"]
    specs = [Sharded("col", D, cp), Sharded("col", AW, D // 4), Sharded("col", AW, D // 4),
             Sharded("row", D // 4, D), Sharded("col", D, 2 * FF // 4), Sharded("row", FF // 4, D)]
    shards = [w_in[0], w_up_a[0], w_up_b[0], w_o[0], w_ffn_in[0], w_ffn_down[0]]

    sent = [jnp.pad(shards[0].astype(BF16), ((0, 0), (0, cp - cs)))] + [w.astype(BF16) for w in shards[1:]]
    (f_in,) = _run_fused(_gather_fused(sent[:1], specs[:1]), "gather_w_in")
    w_main, w_rest = _unshuffle_w_in(f_in, dm)

    def pair_sums(fulls, sps, names, tag):
        from_sibling = _swap_halves(fulls, sps, "swap_halves_" + tag)
        return [_add_sibling(g, r, sp, cq_idx, "add_sibling_" + n)
                for g, r, sp, n in zip(fulls, from_sibling, sps, names)]

    early_pairs = []

    def early_reduce(fulls):
        early_pairs.extend(pair_sums(fulls, specs[1:], big_names[1:], "early"))
        return _scatter_fused([p[0] for p in early_pairs], specs[1:])

    bias_p = jnp.pad(b_fox_f, ((0, 0), (0, HEAD - dm.H)))
    dx, big, small, early_from_chips = _local_step(
        dm, x[0], loss_target[0], w_main, w_rest, _gather_fused(sent[1:], specs[1:]), bias_p,
        hgrn_lb_logits, hgrn_norm_w, norm_mix_pre, norm_mix_post, norm_ffn_pre, norm_ffn_post,
        early_reduce=early_reduce)

    late_pairs = pair_sums([_shuffle_w_in(big["main"], big["rest"], dm)], specs[:1], big_names[:1], "late")
    late_from_chips = _run_fused(_scatter_fused([p[0] for p in late_pairs], specs[:1]), "scatter_w_in")
    pairs = late_pairs + early_pairs
    from_chips = list(late_from_chips) + list(early_from_chips)
    halves = [_add_chips(p[1], r, sp, cq_idx, "add_chips_" + n)
              for p, r, sp, n in zip(pairs, from_chips, specs, big_names)]
    grads = list(_share_halves(halves, specs))
    grads[0] = grads[0][:, :cs]

    row = lambda v: jnp.pad(v.reshape(1, -1), ((0, 0), (0, D - v.size)))
    vec = jnp.concatenate([row(small["loss"][:, :1]), row(small["bias"][:, :dm.H]), row(small["norm_w"]),
                           row(small["lb"]), small["g1"], small["g2"], small["g3"], small["g4"]], axis=0)
    gsum = _sum_small(vec)
    loss = gsum[ROW_LOSS, 0]

    smalls = lambda t: (t["b_fox_f"], t["hgrn_norm_w"], t["hgrn_lb_logits"], t["norm_mix_pre"], t["norm_mix_post"],
                        t["norm_ffn_pre"], t["norm_ffn_post"])
    params = dict(b_fox_f=b_fox_f, hgrn_norm_w=hgrn_norm_w, hgrn_lb_logits=hgrn_lb_logits, norm_mix_pre=norm_mix_pre,
                  norm_mix_post=norm_mix_post, norm_ffn_pre=norm_ffn_pre, norm_ffn_post=norm_ffn_post)
    sg, sd, sm, sv = _small_update(gsum, _pack_small(dm, *smalls(params)), _pack_small(dm, *smalls(moments_m)),
                                   _pack_small(dm, *smalls(moments_v)))
    big_out = {}
    for name, wsh, g in zip(big_names, shards, grads):
        go, d, mn, vn = _adamw(wsh, g, moments_m[name][0], moments_v[name][0], "adamw_" + name)
        big_out[name] = (go[None], d[None], mn[None], vn[None])

    order = ["w_in", "b_fox_f", "hgrn_lb_logits", "hgrn_norm_w", "w_up_a", "w_up_b", "w_o", "norm_mix_pre",
             "norm_mix_post", "norm_ffn_pre", "norm_ffn_post", "w_ffn_in", "w_ffn_down"]
    outs = []
    for kind, packed in enumerate([sg, sd, sm, sv]):
        b, lbl, nw, p1, p2, p3, p4 = _unpack_small(packed, dm)
        sm_map = dict(b_fox_f=b, hgrn_lb_logits=lbl, hgrn_norm_w=nw, norm_mix_pre=p1, norm_mix_post=p2,
                      norm_ffn_pre=p3, norm_ffn_post=p4)
        outs.append([big_out[n][kind] if n in big_out else sm_map[n] for n in order])
    return (loss, dx[None], *outs[0], *outs[1], *outs[2], *outs[3])


def kernel(x, w_in, b_fox_f, hgrn_lb_logits, hgrn_norm_w, w_up_a, w_up_b, w_o, norm_mix_pre, norm_mix_post, norm_ffn_pre, norm_ffn_post, w_ffn_in, w_ffn_down, loss_target, m_w_in, m_b_fox_f, m_hgrn_lb_logits, m_hgrn_norm_w, m_w_up_a, m_w_up_b, m_w_o, m_norm_mix_pre, m_norm_mix_post, m_norm_ffn_pre, m_norm_ffn_post, m_w_ffn_in, m_w_ffn_down, v_w_in, v_b_fox_f, v_hgrn_lb_logits, v_hgrn_norm_w, v_w_up_a, v_w_up_b, v_w_o, v_norm_mix_pre, v_norm_mix_post, v_norm_ffn_pre, v_norm_ffn_post, v_w_ffn_in, v_w_ffn_down):
    dm = Dims(T=x.shape[1], D=x.shape[2], FF=w_ffn_down.shape[1] * 4)
    moments_m = dict(w_in=m_w_in, b_fox_f=m_b_fox_f, hgrn_lb_logits=m_hgrn_lb_logits, hgrn_norm_w=m_hgrn_norm_w,
                     w_up_a=m_w_up_a, w_up_b=m_w_up_b, w_o=m_w_o, norm_mix_pre=m_norm_mix_pre,
                     norm_mix_post=m_norm_mix_post, norm_ffn_pre=m_norm_ffn_pre, norm_ffn_post=m_norm_ffn_post,
                     w_ffn_in=m_w_ffn_in, w_ffn_down=m_w_ffn_down)
    moments_v = dict(w_in=v_w_in, b_fox_f=v_b_fox_f, hgrn_lb_logits=v_hgrn_lb_logits, hgrn_norm_w=v_hgrn_norm_w,
                     w_up_a=v_w_up_a, w_up_b=v_w_up_b, w_o=v_w_o, norm_mix_pre=v_norm_mix_pre,
                     norm_mix_post=v_norm_mix_post, norm_ffn_pre=v_norm_ffn_pre, norm_ffn_post=v_norm_ffn_post,
                     w_ffn_in=v_w_ffn_in, w_ffn_down=v_w_ffn_down)
    return _step(dm, x, w_in, b_fox_f, hgrn_lb_logits, hgrn_norm_w, w_up_a, w_up_b, w_o, norm_mix_pre, norm_mix_post,
                 norm_ffn_pre, norm_ffn_post, w_ffn_in, w_ffn_down, loss_target, moments_m, moments_v)
```

```python
from typing import NamedTuple

import numpy as np
import jax
import jax.numpy as jnp
from jax import lax
from jax.experimental import pallas as pl
from jax.experimental.pallas import tpu as pltpu

F32 = jnp.float32
BF16 = jnp.bfloat16
MESH = pl.DeviceIdType.MESH

RMS_EPS = 1e-6
HEAD = 128
CHUNK = 64
GROUP = 4
LEVELS = (32, 16, 8, 4, 2, 1)
NEG = -1e30

ADAM_LR = 0.001
ADAM_B1 = 0.9
ADAM_B2 = 0.999
ADAM_EPS = 1e-08
ADAM_WD = 0.01
ADAM_STEP = 10

VMEM_LIMIT = 56 * 1024 * 1024


class Dims(NamedTuple):
    T: int
    D: int
    FF: int

    @property
    def AW(self):
        return self.D // 2

    @property
    def H(self):
        return self.AW // HEAD

    @property
    def NMAIN(self):
        return 7 * self.AW

    @property
    def NREST(self):
        return 2 * self.D + HEAD

    @property
    def NIN(self):
        return 7 * self.AW + self.H + 2 * self.D


def _cparams(sem, vmem=VMEM_LIMIT, **kw):
    return pltpu.CompilerParams(dimension_semantics=sem, vmem_limit_bytes=vmem, **kw)


def _tile(n, target):
    if n <= target:
        return n
    t = (target // 128) * 128
    while t >= 128:
        if n % t == 0:
            return t
        t -= 128
    raise ValueError(f"no tile for {n}")


def _dot(a, b, dims):
    return lax.dot_general(a, b, (dims, ((), ())), preferred_element_type=F32)


def _nn(a, b):
    return _dot(a, b, ((1,), (0,)))


def _nt(a, b):
    return _dot(a, b, ((1,), (1,)))


def _tn(a, b):
    return _dot(a, b, ((0,), (0,)))


def _sigmoid(x):
    return jax.nn.sigmoid(x)


def _split2(x):
    hi = x.astype(BF16)
    lo = (x - hi.astype(F32)).astype(BF16)
    return hi, lo


def _split3(x):
    hi = x.astype(BF16)
    r = x - hi.astype(F32)
    mid = r.astype(BF16)
    lo = (r - mid.astype(F32)).astype(BF16)
    return hi, mid, lo


def _mm(a, b, mode, out_dtype, name, add=None, tm=1024, tn=1024, tk=None, fused=None):
    if mode == "nn":
        (M, K), (K2, N) = a.shape, b.shape
    elif mode == "nt":
        (M, K), (N, K2) = a.shape, b.shape
    else:
        (K, M), (K2, N) = a.shape, b.shape
    assert K == K2, (a.shape, b.shape, mode)
    if tk is None:
        tk = 1024 if mode == "tn" else 2048
    tm, tn, tk = _tile(M, tm), _tile(N, tn), _tile(K, tk)
    nk = K // tk
    isz = lambda t: jnp.dtype(t.dtype).itemsize
    vmem = 2 * (tm * tk * isz(a) + tk * tn * isz(b) + tm * tn * jnp.dtype(out_dtype).itemsize)
    vmem += (tm * tn * 4 if nk > 1 else 0) + (2 * tm * tn * isz(add) if add is not None else 0)
    assert vmem <= VMEM_LIMIT - 8 * 1024 * 1024, (name, vmem)
    if mode == "nn":
        a_spec = pl.BlockSpec((tm, tk), lambda i, j, k: (i, k))
        b_spec = pl.BlockSpec((tk, tn), lambda i, j, k: (k, j))
        op = _nn
    elif mode == "nt":
        a_spec = pl.BlockSpec((tm, tk), lambda i, j, k: (i, k))
        b_spec = pl.BlockSpec((tn, tk), lambda i, j, k: (j, k))
        op = _nt
    else:
        a_spec = pl.BlockSpec((tk, tm), lambda i, j, k: (k, i))
        b_spec = pl.BlockSpec((tk, tn), lambda i, j, k: (k, j))
        op = _tn
    o_spec = pl.BlockSpec((tm, tn), lambda i, j, k: (i, j))
    has_add = add is not None
    n_own = 3 if has_add else 2
    x_ins, x_outs, x_sems, n_in, n_out = _fused_parts(fused)
    grid = (M // tm, N // tn, nk)

    def body(*refs):
        a_ref, b_ref = refs[:2]
        add_ref = refs[2] if has_add else None
        o_ref = refs[n_own + n_in]
        acc = refs[n_own + n_in + 1 + n_out] if nk > 1 else None
        k = pl.program_id(2)
        if fused is not None:
            step = (pl.program_id(0) * grid[1] + pl.program_id(1)) * nk + k
            start, middle, finish_x = fused.hooks(
                refs[n_own:n_own + n_in], refs[n_own + n_in + 1:n_own + n_in + 1 + n_out],
                *refs[n_own + n_in + 1 + n_out + (1 if nk > 1 else 0):])
            pl.when(step == 0)(start)
            pl.when(step == (grid[0] * grid[1] * nk) // 2)(middle)

        def finish(r):
            if has_add:
                r = r + add_ref[...].astype(F32)
            o_ref[...] = r.astype(out_dtype)

        part = op(a_ref[...].astype(BF16), b_ref[...].astype(BF16))
        if nk == 1:
            finish(part)
        else:
            @pl.when(k == 0)
            def _():
                acc[...] = part

            @pl.when((k > 0) & (k < nk - 1))
            def _():
                acc[...] += part

            @pl.when(k == nk - 1)
            def _():
                finish(acc[...] + part)

        if fused is not None:
            pl.when(step == grid[0] * grid[1] * nk - 1)(finish_x)

    in_specs = [a_spec, b_spec] + ([o_spec] if has_add else []) + [HBM] * n_in
    args = (a, b) + ((add,) if has_add else ()) + x_ins
    outs = pl.pallas_call(
        body, name=name, grid=grid,
        in_specs=in_specs, out_specs=[o_spec] + [HBM] * n_out,
        out_shape=[jax.ShapeDtypeStruct((M, N), out_dtype)] + list(x_outs),
        scratch_shapes=([pltpu.VMEM((tm, tn), F32)] if nk > 1 else []) + list(x_sems),
        compiler_params=_cparams(("parallel", "parallel", "arbitrary") if fused is None else ("arbitrary",) * 3),
    )(*args)
    return outs[0] if fused is None else (outs[0], tuple(outs[1:]))


def _rows(tr, w, col=0):
    return pl.BlockSpec((tr, w), lambda i, *_: (i, col))


def _vec(w):
    return pl.BlockSpec((1, w), lambda i, *_: (0, 0))


def _rstd(v):
    return lax.rsqrt(jnp.mean(v * v, axis=-1, keepdims=True) + RMS_EPS)


def _rms_bwd(dn, n, r):
    return r * (dn - n * jnp.mean(dn * n, axis=-1, keepdims=True))


def _colsum(v):
    return jnp.sum(v, axis=0, keepdims=True)


def _rms_fwd(x, g, tr=256):
    T, D = x.shape

    def body(x_ref, g_ref, h_ref, r_ref):
        xv = x_ref[...]
        r = _rstd(xv)
        h_ref[...] = (xv * r * g_ref[...]).astype(BF16)
        r_ref[...] = r

    return pl.pallas_call(
        body, name="rms_fwd", grid=(T // tr,),
        in_specs=[_rows(tr, D), _vec(D)],
        out_specs=[_rows(tr, D), _rows(tr, 1)],
        out_shape=[jax.ShapeDtypeStruct((T, D), BF16), jax.ShapeDtypeStruct((T, 1), F32)],
        compiler_params=_cparams(("parallel",)),
    )(x, g)


def _merge_fwd(rest, y_a, y_b, tr=256):
    T, D = y_a.shape

    def body(ga_ref, gb_ref, ya_ref, yb_ref, o_ref):
        o_ref[...] = (_sigmoid(ga_ref[...]) * ya_ref[...] + _sigmoid(gb_ref[...]) * yb_ref[...]).astype(BF16)

    return pl.pallas_call(
        body, name="merge_fwd", grid=(T // tr,),
        in_specs=[_rows(tr, D, 0), _rows(tr, D, 1), _rows(tr, D), _rows(tr, D)],
        out_specs=_rows(tr, D),
        out_shape=jax.ShapeDtypeStruct((T, D), BF16),
        compiler_params=_cparams(("parallel",)),
    )(rest, rest, y_a, y_b)


def _post_pre(x, u, g2, g3, tr=256):
    T, D = x.shape

    def body(x_ref, u_ref, g2_ref, g3_ref, x1_ref, r2_ref, h3_ref, r3_ref):
        uv = u_ref[...]
        r2 = _rstd(uv)
        x1 = x_ref[...] + uv * r2 * g2_ref[...]
        r3 = _rstd(x1)
        x1_ref[...] = x1
        r2_ref[...] = r2
        h3_ref[...] = (x1 * r3 * g3_ref[...]).astype(BF16)
        r3_ref[...] = r3

    return pl.pallas_call(
        body, name="post_pre", grid=(T // tr,),
        in_specs=[_rows(tr, D), _rows(tr, D), _vec(D), _vec(D)],
        out_specs=[_rows(tr, D), _rows(tr, 1), _rows(tr, D), _rows(tr, 1)],
        out_shape=[jax.ShapeDtypeStruct((T, D), F32), jax.ShapeDtypeStruct((T, 1), F32),
                   jax.ShapeDtypeStruct((T, D), BF16), jax.ShapeDtypeStruct((T, 1), F32)],
        compiler_params=_cparams(("parallel",)),
    )(x, u, g2, g3)


def _swiglu_fwd(gu, tr=256):
    T, FF2 = gu.shape
    FF = FF2 // 2
    tc = _tile(FF, 1024)
    nc = FF // tc

    def body(g_ref, u_ref, o_ref):
        gv = g_ref[...]
        o_ref[...] = (gv * _sigmoid(gv) * u_ref[...]).astype(BF16)

    return pl.pallas_call(
        body, name="swiglu_fwd", grid=(T // tr, nc),
        in_specs=[pl.BlockSpec((tr, tc), lambda i, j: (i, j)),
                  pl.BlockSpec((tr, tc), lambda i, j: (i, j + nc))],
        out_specs=pl.BlockSpec((tr, tc), lambda i, j: (i, j)),
        out_shape=jax.ShapeDtypeStruct((T, FF), BF16),
        compiler_params=_cparams(("parallel", "parallel")),
    )(gu, gu)


def _loss_bwd(x1, w, g4, tgt, tr=256):
    T, D = x1.shape

    def body(x1_ref, w_ref, g4_ref, t_ref, loss_ref, dy_ref, dw_ref, dg_ref):
        i = pl.program_id(0)

        @pl.when(i == 0)
        def _():
            loss_ref[...] = jnp.zeros_like(loss_ref)
            dg_ref[...] = jnp.zeros_like(dg_ref)

        wv = w_ref[...]
        g4v = g4_ref[...]
        r4 = _rstd(wv)
        n4 = wv * r4
        e = x1_ref[...] + n4 * g4v - t_ref[...]
        loss_ref[...] += 0.5 * jnp.sum(jnp.mean(e * e, axis=-1, keepdims=True), axis=0, keepdims=True)
        dy = e * (1.0 / D)
        dy_ref[...] = dy
        dg_ref[...] += _colsum(dy * n4)
        dw_ref[...] = _rms_bwd(dy * g4v, n4, r4).astype(BF16)

    return pl.pallas_call(
        body, name="loss_bwd", grid=(T // tr,),
        in_specs=[_rows(tr, D), _rows(tr, D), _vec(D), _rows(tr, D)],
        out_specs=[_vec(HEAD), _rows(tr, D), _rows(tr, D), _vec(D)],
        out_shape=[jax.ShapeDtypeStruct((1, HEAD), F32), jax.ShapeDtypeStruct((T, D), F32),
                   jax.ShapeDtypeStruct((T, D), BF16), jax.ShapeDtypeStruct((1, D), F32)],
        compiler_params=_cparams(("arbitrary",)),
    )(x1, w, g4, tgt)


def _swiglu_bwd(gu, dact, tr=128):
    T, FF2 = gu.shape
    FF = FF2 // 2

    def body(g_ref, u_ref, d_ref, o_ref):
        h = pl.program_id(1)
        gv = g_ref[...]
        s = _sigmoid(gv)
        dv = d_ref[...].astype(F32)

        @pl.when(h == 0)
        def _():
            o_ref[...] = (dv * u_ref[...] * (s * (1.0 + gv * (1.0 - s)))).astype(BF16)

        @pl.when(h == 1)
        def _():
            o_ref[...] = (dv * (gv * s)).astype(BF16)

    return pl.pallas_call(
        body, name="swiglu_bwd", grid=(T // tr, 2),
        in_specs=[pl.BlockSpec((tr, FF), lambda i, h: (i, 0)),
                  pl.BlockSpec((tr, FF), lambda i, h: (i, 1)),
                  pl.BlockSpec((tr, FF), lambda i, h: (i, 0))],
        out_specs=pl.BlockSpec((tr, FF), lambda i, h: (i, h)),
        out_shape=jax.ShapeDtypeStruct((T, FF2), BF16),
        compiler_params=_cparams(("parallel", "arbitrary")),
    )(gu, gu, dact)


def _rms_bwd2(dy, dh3, x1, r3, g3, u, r2, g2, tr=256):
    T, D = dy.shape

    def body(dy_ref, dh_ref, x1_ref, r3_ref, g3_ref, u_ref, r2_ref, g2_ref, dx1_ref, du_ref, dg3_ref, dg2_ref):
        i = pl.program_id(0)

        @pl.when(i == 0)
        def _():
            dg3_ref[...] = jnp.zeros_like(dg3_ref)
            dg2_ref[...] = jnp.zeros_like(dg2_ref)

        r3v, r2v = r3_ref[...], r2_ref[...]
        dh = dh_ref[...]
        n3 = x1_ref[...] * r3v
        dg3_ref[...] += _colsum(dh * n3)
        dx1 = dy_ref[...] + _rms_bwd(dh * g3_ref[...], n3, r3v)
        dx1_ref[...] = dx1
        n2 = u_ref[...] * r2v
        dg2_ref[...] += _colsum(dx1 * n2)
        du_ref[...] = _rms_bwd(dx1 * g2_ref[...], n2, r2v).astype(BF16)

    return pl.pallas_call(
        body, name="rms_bwd2", grid=(T // tr,),
        in_specs=[_rows(tr, D), _rows(tr, D), _rows(tr, D), _rows(tr, 1), _vec(D),
                  _rows(tr, D), _rows(tr, 1), _vec(D)],
        out_specs=[_rows(tr, D), _rows(tr, D), _vec(D), _vec(D)],
        out_shape=[jax.ShapeDtypeStruct((T, D), F32), jax.ShapeDtypeStruct((T, D), BF16),
                   jax.ShapeDtypeStruct((1, D), F32), jax.ShapeDtypeStruct((1, D), F32)],
        compiler_params=_cparams(("arbitrary",)),
    )(dy, dh3, x1, r3, g3, u, r2, g2)


def _merge_bwd(dmerged, rest, y_a, y_b, tr=256):
    T, D = dmerged.shape

    def body(dm_ref, ga_ref, gb_ref, ya_ref, yb_ref, dya_ref, dyb_ref, dg_ref):
        h = pl.program_id(1)
        dm = dm_ref[...]

        @pl.when(h == 0)
        def _():
            s = _sigmoid(ga_ref[...])
            dya_ref[...] = (dm * s).astype(BF16)
            dg_ref[...] = (dm * ya_ref[...] * s * (1.0 - s)).astype(BF16)

        @pl.when(h == 1)
        def _():
            s = _sigmoid(gb_ref[...])
            dyb_ref[...] = (dm * s).astype(BF16)
            dg_ref[...] = (dm * yb_ref[...] * s * (1.0 - s)).astype(BF16)

    blk = lambda col: pl.BlockSpec((tr, D), lambda i, h: (i, col))
    return pl.pallas_call(
        body, name="merge_bwd", grid=(T // tr, 2),
        in_specs=[blk(0), blk(0), blk(1), blk(0), blk(0)],
        out_specs=[blk(0), blk(0), pl.BlockSpec((tr, D), lambda i, h: (i, h))],
        out_shape=[jax.ShapeDtypeStruct((T, D), BF16), jax.ShapeDtypeStruct((T, D), BF16),
                   jax.ShapeDtypeStruct((T, 2 * D), BF16)],
        compiler_params=_cparams(("parallel", "arbitrary")),
    )(dmerged, rest, rest, y_a, y_b)


def _rms_bwd1(dx1, dh1, x, r1, g1, tr=256):
    T, D = x.shape

    def body(dx1_ref, dh_ref, x_ref, r_ref, g_ref, dx_ref, dg_ref):
        i = pl.program_id(0)

        @pl.when(i == 0)
        def _():
            dg_ref[...] = jnp.zeros_like(dg_ref)

        rv = r_ref[...]
        dh = dh_ref[...]
        n = x_ref[...] * rv
        dg_ref[...] += _colsum(dh * n)
        dx_ref[...] = dx1_ref[...] + _rms_bwd(dh * g_ref[...], n, rv)

    return pl.pallas_call(
        body, name="rms_bwd1", grid=(T // tr,),
        in_specs=[_rows(tr, D), _rows(tr, D), _rows(tr, D), _rows(tr, 1), _vec(D)],
        out_specs=[_rows(tr, D), _vec(D)],
        out_shape=[jax.ShapeDtypeStruct((T, D), F32), jax.ShapeDtypeStruct((1, D), F32)],
        compiler_params=_cparams(("arbitrary",)),
    )(dx1, dh1, x, r1, g1)


def _hgrn_consts():
    C = CHUNK
    nl = len(LEVELS) + 1
    w = np.zeros((nl, C, C), np.float32)
    w[0] = np.tril(np.ones((C, C), np.float32))
    for li, m in enumerate(LEVELS, start=1):
        for r in range(C):
            mid = (r // (2 * m)) * 2 * m + m
            if r >= mid:
                w[li, r, mid:r + 1] = 1.0
            else:
                w[li, r, r + 1:mid] = 1.0
    w_all = w.reshape(nl * C, C)
    w2 = np.concatenate([w_all, w_all], axis=1)
    w2t = np.concatenate([w_all.T, w_all.T], axis=1)
    R = GROUP * C
    t = np.arange(R)[:, None]
    s = np.arange(R)[None, :]
    masks = np.zeros((nl, R, R), np.float32)
    masks[0] = (t == s)
    for li, m in enumerate(LEVELS, start=1):
        masks[li] = ((t ^ s) < 2 * m)
    return jnp.asarray(w2, BF16), jnp.asarray(w2t, BF16), jnp.asarray(masks, F32)


def _hgrn_gates(z, lg_ref):
    l0 = lg_ref[0:1, :]
    l1 = lg_ref[1:2, :]
    mx = jnp.maximum(l0, l1)
    e0 = jnp.exp(l0 - mx)
    e1 = jnp.exp(l1 - mx)
    lb = e0 / (e0 + e1)
    om = 1.0 - lb
    sg = _sigmoid(z)
    sgm = _sigmoid(-z)
    f = lb + om * sg
    return lb, om, sg, sgm, f, jnp.log(f), om * sgm


def _hgrn_levels(q, kk, lf, w2_ref):
    C = CHUNK
    nl = len(LEVELS) + 1
    lf_hi, lf_lo = _split2(lf)
    per_chunk = []
    for c in range(GROUP):
        rhs = jnp.concatenate([lf_hi[c * C:(c + 1) * C], lf_lo[c * C:(c + 1) * C]], axis=0)
        per_chunk.append(_nn(w2_ref[...], rhs))
    args = [jnp.concatenate([per_chunk[c][l * C:(l + 1) * C] for c in range(GROUP)], axis=0) for l in range(nl)]
    exps = [jnp.exp(a) for a in args]
    row = lax.broadcasted_iota(jnp.int32, q.shape, 0)
    qf, kf, mts = [q], [kk], [None]
    for li, m in enumerate(LEVELS, start=1):
        mt = jnp.where((row & m) != 0, 1.0, 0.0).astype(F32)
        mts.append(mt)
        qf.append(q * exps[li] * mt)
        kf.append(kk * exps[li] * (1.0 - mt))
    return args, exps, mts, qf, kf


def _hgrn_scores(qf, kf, masks_ref):
    p = None
    for l in range(len(qf)):
        pl_ = _nt(qf[l].astype(BF16), kf[l].astype(BF16)) * masks_ref[l]
        p = pl_ if p is None else p + pl_
    return p


def _hgrn_fwd(main, lb_logits, norm_w, dm):
    T, H, C = dm.T, dm.H, CHUNK
    R = GROUP * C
    nj = T // R
    w2, _, masks = _hgrn_consts()

    def body(q_ref, z_ref, v_ref, g_ref, lg_ref, nw_ref, w2_ref, masks_ref, ya_ref, o_ref, sp_ref, st_ref):
        j = pl.program_id(1)

        @pl.when(j == 0)
        def _():
            st_ref[...] = jnp.zeros_like(st_ref)

        q = q_ref[...]
        v = v_ref[...]
        vb = v.astype(BF16)
        _, _, _, _, _, lf, kk = _hgrn_gates(z_ref[...], lg_ref)
        args, exps, _, qf, kf = _hgrn_levels(q, kk, lf, w2_ref)
        p = _hgrn_scores(qf, kf, masks_ref)
        o_intra = _nn(p.astype(BF16), vb)
        b, eb = args[0], exps[0]
        o_inter = []
        for c in range(GROUP):
            sl = slice(c * C, (c + 1) * C)
            st = st_ref[...]
            sp_ref[0, c] = st
            blast = b[c * C + C - 1:c * C + C, :]
            o_inter.append(_nt((q[sl] * eb[sl]).astype(BF16), st.astype(BF16)))
            kd = (kk[sl] * jnp.exp(blast - b[sl])).astype(BF16)
            st_ref[...] = st * jnp.exp(blast) + _tn(vb[sl], kd)
        o = o_intra + jnp.concatenate(o_inter, axis=0)
        o_ref[...] = o
        gv = g_ref[...]
        ya_ref[...] = (o * _rstd(o) * nw_ref[...] * (gv * _sigmoid(gv))).astype(BF16)

    def col(sec):
        return pl.BlockSpec((R, HEAD), lambda h, j: (j, sec * H + h))

    return pl.pallas_call(
        body, name="hgrn_fwd", grid=(H, nj),
        in_specs=[col(0), col(1), col(2), col(3),
                  pl.BlockSpec((2, HEAD), lambda h, j: (0, h)),
                  pl.BlockSpec((1, HEAD), lambda h, j: (0, 0)),
                  pl.BlockSpec(w2.shape, lambda h, j: (0, 0)),
                  pl.BlockSpec(masks.shape, lambda h, j: (0, 0, 0))],
        out_specs=[pl.BlockSpec((R, HEAD), lambda h, j: (j, h)),
                   pl.BlockSpec((R, HEAD), lambda h, j: (j, h)),
                   pl.BlockSpec((1, GROUP, HEAD, HEAD), lambda h, j: (h, j, 0, 0))],
        out_shape=[jax.ShapeDtypeStruct((T, dm.AW), BF16), jax.ShapeDtypeStruct((T, dm.AW), F32),
                   jax.ShapeDtypeStruct((H, T // C, HEAD, HEAD), F32)],
        scratch_shapes=[pltpu.VMEM((HEAD, HEAD), F32)],
        compiler_params=_cparams(("parallel", "arbitrary")),
    )(main, main, main, main, lb_logits, norm_w, w2, masks)


def _hgrn_bwd(main, lb_logits, norm_w, o_saved, states, dya, dm, fused=None):
    T, H, C = dm.T, dm.H, CHUNK
    R = GROUP * C
    nj = T // R
    nl = len(LEVELS) + 1
    w2, w2t, masks = _hgrn_consts()
    x_ins, x_outs, x_sems, n_in, n_out = _fused_parts(fused)

    def body(q_ref, z_ref, v_ref, g_ref, lg_ref, nw_ref, w2_ref, w2t_ref, masks_ref, o_ref, sp_ref, dya_ref, *rest):
        d_ref, dlb_ref, dnw_ref = rest[n_in:n_in + 3]
        ds_ref, stash = rest[n_in + 3 + n_out:n_in + 5 + n_out]
        j = pl.program_id(1)
        sec = pl.program_id(2)
        if fused is not None:
            hd = pl.program_id(0)
            start, middle, finish_x = fused.hooks(rest[:n_in], rest[n_in + 3:n_in + 3 + n_out],
                                                  *rest[n_in + 5 + n_out:])
            pl.when((hd == 0) & (j == 0) & (sec == 0))(start)
            pl.when((hd == H // 2) & (j == 0) & (sec == 0))(middle)
            pl.when((hd == H - 1) & (j == nj - 1) & (sec == 3))(finish_x)

        @pl.when((j == 0) & (sec == 0))
        def _():
            ds_ref[...] = jnp.zeros_like(ds_ref)
            dlb_ref[...] = jnp.zeros_like(dlb_ref)
            dnw_ref[...] = jnp.zeros_like(dnw_ref)

        @pl.when(sec == 0)
        def _():
            q = q_ref[...]
            v = v_ref[...]
            gv = g_ref[...]
            vb = v.astype(BF16)
            lb, om, sg, sgm, f, lf, kk = _hgrn_gates(z_ref[...], lg_ref)
            args, exps, mts, qf, kf = _hgrn_levels(q, kk, lf, w2_ref)
            qb = [t.astype(BF16) for t in qf]
            kb = [t.astype(BF16) for t in kf]
            p = _hgrn_scores(qf, kf, masks_ref)
            o = o_ref[...]
            nw = nw_ref[...]
            r = _rstd(o)
            n = o * r
            sgg = _sigmoid(gv)
            dya_v = dya_ref[...]
            d_on = dya_v * (gv * sgg)
            dg = dya_v * (n * nw) * (sgg * (1.0 + gv * (1.0 - sgg)))
            dnw_ref[0] += _colsum(d_on * n)
            do = _rms_bwd(d_on * nw, n, r)
            dob = do.astype(BF16)
            dp = _nt(dob, vb)
            dv = _tn(p.astype(BF16), dob)
            dq = jnp.zeros_like(q)
            dkk = jnp.zeros_like(q)
            dargs = [None] * nl
            for l in range(nl):
                dpl = (dp * masks_ref[l]).astype(BF16)
                dql = _nn(dpl, kb[l])
                dkl = _tn(dpl, qb[l])
                if l == 0:
                    dq += dql
                    dkk += dkl
                else:
                    dq += dql * exps[l] * mts[l]
                    dkk += dkl * exps[l] * (1.0 - mts[l])
                    dargs[l] = dql * qf[l] + dkl * kf[l]
            b, eb = args[0], exps[0]
            row = lax.broadcasted_iota(jnp.int32, (C, HEAD), 0)
            dq_i, dkk_i, dv_i, db_i = [None] * GROUP, [None] * GROUP, [None] * GROUP, [None] * GROUP
            for c in reversed(range(GROUP)):
                sl = slice(c * C, (c + 1) * C)
                st = sp_ref[0, c]
                ds = ds_ref[...]
                dsb = ds.astype(BF16)
                blast = b[c * C + C - 1:c * C + C, :]
                ebl = jnp.exp(blast)
                el = jnp.exp(blast - b[sl])
                qe = q[sl] * eb[sl]
                kd = kk[sl] * el
                dqe = _nn(dob[sl], st.astype(BF16))
                dkd = _nn(vb[sl], dsb)
                t = dkd * kd
                dblast = _colsum(t) + _colsum(ds * st) * ebl
                dq_i[c] = dqe * eb[sl]
                dkk_i[c] = dkd * el
                dv_i[c] = _nt(kd.astype(BF16), dsb)
                db_i[c] = dqe * qe - t + jnp.where(row == C - 1, dblast, 0.0)
                ds_ref[...] = ds * ebl + _tn(dob[sl], qe.astype(BF16))
            dq = dq + jnp.concatenate(dq_i, axis=0)
            dkk = dkk + jnp.concatenate(dkk_i, axis=0)
            dv = dv + jnp.concatenate(dv_i, axis=0)
            dlf_c = []
            for c in range(GROUP):
                sl = slice(c * C, (c + 1) * C)
                stack = jnp.concatenate([db_i[c]] + [dargs[l][sl] for l in range(1, nl)], axis=0)
                hi, lo = _split2(stack)
                dlf_c.append(_nn(w2t_ref[...], jnp.concatenate([hi, lo], axis=0)))
            dlf = jnp.concatenate(dlf_c, axis=0)
            dz = dlf * (om * sg * (1.0 - sg) / f) - dkk * (om * sgm * (1.0 - sgm))
            dlb_ref[...] += _colsum(dlf * (1.0 - sg) / f - dkk * sgm)
            stash[0] = dq.astype(BF16)
            stash[1] = dz.astype(BF16)
            stash[2] = dv.astype(BF16)
            stash[3] = dg.astype(BF16)

        d_ref[...] = stash[sec]

    def col(sec):
        return pl.BlockSpec((R, HEAD), lambda h, j, s: (nj - 1 - j, sec * H + h))

    outs = pl.pallas_call(
        body, name="hgrn_bwd", grid=(H, nj, 4),
        in_specs=[col(0), col(1), col(2), col(3),
                  pl.BlockSpec((2, HEAD), lambda h, j, s: (0, h)),
                  pl.BlockSpec((1, HEAD), lambda h, j, s: (0, 0)),
                  pl.BlockSpec(w2.shape, lambda h, j, s: (0, 0)),
                  pl.BlockSpec(w2t.shape, lambda h, j, s: (0, 0)),
                  pl.BlockSpec(masks.shape, lambda h, j, s: (0, 0, 0)),
                  pl.BlockSpec((R, HEAD), lambda h, j, s: (nj - 1 - j, h)),
                  pl.BlockSpec((1, GROUP, HEAD, HEAD), lambda h, j, s: (h, nj - 1 - j, 0, 0)),
                  pl.BlockSpec((R, HEAD), lambda h, j, s: (nj - 1 - j, h))] + [HBM] * n_in,
        out_specs=[pl.BlockSpec((R, HEAD), lambda h, j, s: (nj - 1 - j, s * H + h)),
                   pl.BlockSpec((1, HEAD), lambda h, j, s: (0, h)),
                   pl.BlockSpec((1, 1, HEAD), lambda h, j, s: (h, 0, 0))] + [HBM] * n_out,
        out_shape=[jax.ShapeDtypeStruct((T, 4 * dm.AW), BF16), jax.ShapeDtypeStruct((1, dm.AW), F32),
                   jax.ShapeDtypeStruct((H, 1, HEAD), F32)] + list(x_outs),
        scratch_shapes=[pltpu.VMEM((HEAD, HEAD), F32), pltpu.VMEM((4, R, HEAD), BF16)] + list(x_sems),
        compiler_params=_cparams(("arbitrary", "arbitrary", "arbitrary")),
    )(main, main, main, main, lb_logits, norm_w, w2, w2t, masks, o_saved, states, dya, *x_ins)
    return outs[0], outs[1], outs[2], tuple(outs[3:])


def _log_sigmoid(x):
    return jnp.minimum(x, 0.0) - jnp.log(1.0 + jnp.exp(-jnp.abs(x)))


def _tri(n):
    return jnp.asarray(np.tril(np.ones((n, n), np.float32)), BF16)


def _cum_fwd(rest, bias, dm, tb=256):
    T = dm.T
    tb = min(tb, T)
    cb = 2 * dm.D // HEAD
    tri = _tri(tb)

    def body(x_ref, b_ref, tri_ref, o_ref, carry):
        i = pl.program_id(0)

        @pl.when(i == 0)
        def _():
            carry[...] = jnp.zeros_like(carry)

        lf = _log_sigmoid(x_ref[...] + b_ref[...])
        hi, mid, lo = _split3(lf)
        tr_ = tri_ref[...]
        c = _nn(tr_, hi) + _nn(tr_, mid) + _nn(tr_, lo) + carry[...]
        o_ref[...] = c
        carry[...] = c[tb - 1:tb, :]

    return pl.pallas_call(
        body, name="cum_fwd", grid=(T // tb,),
        in_specs=[_rows(tb, HEAD, cb), _vec(HEAD), pl.BlockSpec((tb, tb), lambda i: (0, 0))],
        out_specs=_rows(tb, HEAD),
        out_shape=jax.ShapeDtypeStruct((T, HEAD), F32),
        scratch_shapes=[pltpu.VMEM((1, HEAD), F32)],
        compiler_params=_cparams(("arbitrary",)),
    )(rest, bias, tri)


def _cum_bwd(dcum, rest, bias, dm, tb=256):
    T = dm.T
    tb = min(tb, T)
    nb = T // tb
    cb = 2 * dm.D // HEAD
    tri = _tri(tb)

    def body(d_ref, x_ref, b_ref, tri_ref, o_ref, db_ref, carry):
        i = pl.program_id(0)

        @pl.when(i == 0)
        def _():
            carry[...] = jnp.zeros_like(carry)
            db_ref[...] = jnp.zeros_like(db_ref)

        hi, mid, lo = _split3(d_ref[...])
        tr_ = tri_ref[...]
        dlf = _tn(tr_, hi) + _tn(tr_, mid) + _tn(tr_, lo) + carry[...]
        carry[...] = dlf[0:1, :]
        dx = dlf * _sigmoid(-(x_ref[...] + b_ref[...]))
        o_ref[...] = dx.astype(BF16)
        db_ref[...] += _colsum(dx)

    return pl.pallas_call(
        body, name="cum_bwd", grid=(nb,),
        in_specs=[pl.BlockSpec((tb, HEAD), lambda i: (nb - 1 - i, 0)),
                  pl.BlockSpec((tb, HEAD), lambda i: (nb - 1 - i, cb)),
                  _vec(HEAD), pl.BlockSpec((tb, tb), lambda i: (0, 0))],
        out_specs=[pl.BlockSpec((tb, HEAD), lambda i: (nb - 1 - i, 0)), _vec(HEAD)],
        out_shape=[jax.ShapeDtypeStruct((T, HEAD), BF16), jax.ShapeDtypeStruct((1, HEAD), F32)],
        scratch_shapes=[pltpu.VMEM((1, HEAD), F32)],
        compiler_params=_cparams(("arbitrary",)),
    )(dcum, rest, bias, tri)


def _fox_pairs(nq, kv_major):
    if kv_major:
        pairs = [(i, j) for j in range(nq) for i in range(j, nq)]
    else:
        pairs = [(i, j) for i in range(nq) for j in range(i + 1)]
    qi = jnp.asarray(np.array([p[0] for p in pairs], np.int32))
    kj = jnp.asarray(np.array([p[1] for p in pairs], np.int32))
    return qi, kj


class Fused(NamedTuple):
    ins: tuple
    outs: tuple
    sems: tuple
    hooks: object


def _fused_parts(fused):
    if fused is None:
        return (), (), (), 0, 0
    return tuple(fused.ins), tuple(fused.outs), tuple(fused.sems), len(fused.ins), len(fused.outs)


FOX_FWD_SPLIT = 4
FOX_BWD_SPLIT = 2


def _fox_fwd(main, cum_row, dm, fused=None, tq=512):
    T, H = dm.T, dm.H
    tq = min(tq, T)
    nq = T // tq
    qi_tab, kj_tab = _fox_pairs(nq, kv_major=False)
    npairs = int(qi_tab.shape[0])
    x_ins, x_outs, x_sems, n_in, n_out = _fused_parts(fused)
    ns = FOX_FWD_SPLIT if tq % (8 * FOX_FWD_SPLIT) == 0 else 1
    rq = tq // ns

    def body(qi_ref, kj_ref, q_ref, k_ref, v_ref, ck_ref, *rest):
        x_in, (o_ref, lse_ref) = rest[:n_in], rest[n_in:n_in + 2]
        x_out = rest[n_in + 2:n_in + 2 + n_out]
        m_ref, l_ref, acc_ref = rest[n_in + 2 + n_out:n_in + 5 + n_out]
        hd = pl.program_id(0)
        t = pl.program_id(1)
        i = qi_ref[t]
        j = kj_ref[t]
        if fused is not None:
            start, middle, finish = fused.hooks(x_in, x_out, *rest[n_in + 5 + n_out:])
            pl.when((hd == 0) & (t == 0))(start)
            pl.when((hd == H // 2) & (t == 0))(middle)

        @pl.when(j == 0)
        def _():
            m_ref[...] = jnp.full_like(m_ref, NEG)
            l_ref[...] = jnp.zeros_like(l_ref)
            acc_ref[...] = jnp.zeros_like(acc_ref)

        def step(on_diagonal):
            kb = k_ref[...].astype(BF16)
            vb = v_ref[...].astype(BF16)
            ck = ck_ref[0]
            q_all, m_all, l_all, acc_all = q_ref[...], m_ref[...], l_ref[...], acc_ref[...]
            m_out, l_out, acc_out = [], [], []
            for g in range(ns):
                rows = slice(g * rq, (g + 1) * rq)
                qs = (q_all[rows] * (HEAD ** -0.5)).astype(BF16)
                s = _nt(qs, kb) - ck
                if on_diagonal:
                    row = g * rq + lax.broadcasted_iota(jnp.int32, s.shape, 0)
                    s = jnp.where(row >= lax.broadcasted_iota(jnp.int32, s.shape, 1), s, NEG)
                m_new = jnp.maximum(m_all[rows], jnp.max(s, axis=-1, keepdims=True))
                a = jnp.exp(m_all[rows] - m_new)
                p = jnp.exp(s - m_new)
                m_out.append(m_new)
                l_out.append(a * l_all[rows] + jnp.sum(p, axis=-1, keepdims=True))
                acc_out.append(a * acc_all[rows] + _nn(p.astype(BF16), vb))
            m_ref[...] = jnp.concatenate(m_out, axis=0)
            l_ref[...] = jnp.concatenate(l_out, axis=0)
            acc_ref[...] = jnp.concatenate(acc_out, axis=0)

        @pl.when(j < i)
        def _():
            step(False)

        @pl.when(j == i)
        def _():
            step(True)
            l = l_ref[...]
            o_ref[...] = acc_ref[...] / l
            lse_ref[0] = m_ref[...] + jnp.log(l)

        if fused is not None:
            pl.when((hd == H - 1) & (t == npairs - 1))(finish)

    outs = pl.pallas_call(
        body, name="fox_fwd",
        grid_spec=pltpu.PrefetchScalarGridSpec(
            num_scalar_prefetch=2, grid=(H, npairs),
            in_specs=[pl.BlockSpec((tq, HEAD), lambda h, t, qi, kj: (qi[t], 4 * H + h)),
                      pl.BlockSpec((tq, HEAD), lambda h, t, qi, kj: (kj[t], 5 * H + h)),
                      pl.BlockSpec((tq, HEAD), lambda h, t, qi, kj: (kj[t], 6 * H + h)),
                      pl.BlockSpec((1, 1, tq), lambda h, t, qi, kj: (h, 0, kj[t]))] + [HBM] * n_in,
            out_specs=[pl.BlockSpec((tq, HEAD), lambda h, t, qi, kj: (qi[t], h)),
                       pl.BlockSpec((1, tq, 1), lambda h, t, qi, kj: (h, qi[t], 0))] + [HBM] * n_out,
            scratch_shapes=[pltpu.VMEM((tq, 1), F32), pltpu.VMEM((tq, 1), F32), pltpu.VMEM((tq, HEAD), F32)]
            + list(x_sems)),
        out_shape=[jax.ShapeDtypeStruct((T, dm.AW), F32), jax.ShapeDtypeStruct((H, T, 1), F32)] + list(x_outs),
        compiler_params=_cparams(("arbitrary", "arbitrary")),
    )(qi_tab, kj_tab, main, main, main, cum_row, *x_ins)
    return outs[0], outs[1], tuple(outs[2:])


def _fox_delta(do, o, dm, tr=256):
    T, H = dm.T, dm.H
    tr = min(tr, T)

    def body(do_ref, o_ref, d_ref):
        d_ref[0] = jnp.sum(do_ref[...] * o_ref[...], axis=-1, keepdims=True)

    return pl.pallas_call(
        body, name="fox_delta", grid=(H, T // tr),
        in_specs=[pl.BlockSpec((tr, HEAD), lambda h, i: (i, h)), pl.BlockSpec((tr, HEAD), lambda h, i: (i, h))],
        out_specs=pl.BlockSpec((1, tr, 1), lambda h, i: (h, i, 0)),
        out_shape=jax.ShapeDtypeStruct((H, T, 1), F32),
        compiler_params=_cparams(("parallel", "parallel")),
    )(do, o)


def _fox_bwd(main, cum_row, lse, delta, do, dm, fused=None, tq=512):
    T, H = dm.T, dm.H
    tq = min(tq, T)
    nq = T // tq
    qi_tab, kj_tab = _fox_pairs(nq, kv_major=True)
    npairs = int(qi_tab.shape[0])
    x_ins, x_outs, x_sems, n_in, n_out = _fused_parts(fused)
    ns = FOX_BWD_SPLIT if tq % (16 * FOX_BWD_SPLIT) == 0 else 1
    rq = tq // ns

    def body(qi_ref, kj_ref, q_ref, k_ref, v_ref, ck_ref, lse_ref, dl_ref, do_ref, *rest):
        x_in = rest[:n_in]
        dq_ref, dk_ref, dv_ref, dc_ref, dr_ref = rest[n_in:n_in + 5]
        x_out = rest[n_in + 5:n_in + 5 + n_out]
        dk_acc, dv_acc, dc_acc = rest[n_in + 5 + n_out:n_in + 8 + n_out]
        hd = pl.program_id(0)
        t = pl.program_id(1)
        i = qi_ref[t]
        kj = kj_ref[t]
        if fused is not None:
            start, middle, finish = fused.hooks(x_in, x_out, *rest[n_in + 8 + n_out:])
            pl.when((hd == 0) & (t == 0))(start)
            pl.when((hd == H // 2) & (t == 0))(middle)

        @pl.when(t == 0)
        def _():
            dq_ref[...] = jnp.zeros_like(dq_ref)
            dr_ref[...] = jnp.zeros_like(dr_ref)

        @pl.when(i == kj)
        def _():
            dk_acc[...] = jnp.zeros_like(dk_acc)
            dv_acc[...] = jnp.zeros_like(dv_acc)
            dc_acc[...] = jnp.zeros_like(dc_acc)

        def step(on_diagonal):
            kb = k_ref[...].astype(BF16)
            vb = v_ref[...].astype(BF16)
            ck = ck_ref[0]
            q_all, do_all, lse_all, dl_all = q_ref[...], do_ref[...], lse_ref[0], dl_ref[0]
            dq_g, dr_g, dv_c, dk_c, dc_c = [], [], None, None, None
            for g in range(ns):
                rows = slice(g * rq, (g + 1) * rq)
                qs = (q_all[rows] * (HEAD ** -0.5)).astype(BF16)
                s = _nt(qs, kb) - ck
                if on_diagonal:
                    row = g * rq + lax.broadcasted_iota(jnp.int32, s.shape, 0)
                    s = jnp.where(row >= lax.broadcasted_iota(jnp.int32, s.shape, 1), s, NEG)
                dob = do_all[rows].astype(BF16)
                p = jnp.exp(s - lse_all[rows])
                ds = p * (_nt(dob, vb) - dl_all[rows])
                dsb = ds.astype(BF16)
                dq_g.append(_nn(dsb, kb) * (HEAD ** -0.5))
                dr_g.append(jnp.sum(ds, axis=-1, keepdims=True))
                dv_g, dk_g, dc_g = _tn(p.astype(BF16), dob), _tn(dsb, qs), _colsum(ds)
                dv_c = dv_g if dv_c is None else dv_c + dv_g
                dk_c = dk_g if dk_c is None else dk_c + dk_g
                dc_c = dc_g if dc_c is None else dc_c + dc_g
            dv_acc[...] += dv_c
            dk_acc[...] += dk_c
            dc_acc[...] -= dc_c
            out_rows = pl.ds(pl.multiple_of(i * tq, tq), tq)
            dq_ref[out_rows, :] += jnp.concatenate(dq_g, axis=0)
            dr_ref[0, out_rows, :] += jnp.concatenate(dr_g, axis=0)

        @pl.when(i == kj)
        def _():
            step(True)

        @pl.when(i > kj)
        def _():
            step(False)

        @pl.when(i == nq - 1)
        def _():
            dk_ref[...] = dk_acc[...].astype(BF16)
            dv_ref[...] = dv_acc[...].astype(BF16)
            dc_ref[0] = dc_acc[...]

        if fused is not None:
            pl.when((hd == H - 1) & (t == npairs - 1))(finish)

    qcol = pl.BlockSpec((1, tq, 1), lambda h, t, qi, kj: (h, qi[t], 0))
    outs = pl.pallas_call(
        body, name="fox_bwd",
        grid_spec=pltpu.PrefetchScalarGridSpec(
            num_scalar_prefetch=2, grid=(H, npairs),
            in_specs=[pl.BlockSpec((tq, HEAD), lambda h, t, qi, kj: (qi[t], 4 * H + h)),
                      pl.BlockSpec((tq, HEAD), lambda h, t, qi, kj: (kj[t], 5 * H + h)),
                      pl.BlockSpec((tq, HEAD), lambda h, t, qi, kj: (kj[t], 6 * H + h)),
                      pl.BlockSpec((1, 1, tq), lambda h, t, qi, kj: (h, 0, kj[t])),
                      qcol, qcol,
                      pl.BlockSpec((tq, HEAD), lambda h, t, qi, kj: (qi[t], h))] + [HBM] * n_in,
            out_specs=[pl.BlockSpec((T, HEAD), lambda h, t, qi, kj: (0, h)),
                       pl.BlockSpec((tq, HEAD), lambda h, t, qi, kj: (kj[t], h)),
                       pl.BlockSpec((tq, HEAD), lambda h, t, qi, kj: (kj[t], h)),
                       pl.BlockSpec((1, 1, tq), lambda h, t, qi, kj: (h, 0, kj[t])),
                       pl.BlockSpec((1, T, 1), lambda h, t, qi, kj: (h, 0, 0))] + [HBM] * n_out,
            scratch_shapes=[pltpu.VMEM((tq, HEAD), F32), pltpu.VMEM((tq, HEAD), F32), pltpu.VMEM((1, tq), F32)]
            + list(x_sems)),
        out_shape=[jax.ShapeDtypeStruct((T, dm.AW), F32), jax.ShapeDtypeStruct((T, dm.AW), BF16),
                   jax.ShapeDtypeStruct((T, dm.AW), BF16), jax.ShapeDtypeStruct((H, 1, T), F32),
                   jax.ShapeDtypeStruct((H, T, 1), F32)] + list(x_outs),
        compiler_params=_cparams(("arbitrary", "arbitrary")),
    )(qi_tab, kj_tab, main, main, main, cum_row, lse, delta, do, *x_ins)
    return outs[:5], tuple(outs[5:])


FOX_QG = 256


def _fox_sT(kb, q_rows, ck, g, on_diagonal):
    qs = (q_rows * (HEAD ** -0.5)).astype(BF16)
    reps = q_rows.shape[0] // HEAD
    sT = _nt(kb, qs) - (jnp.concatenate([ck] * reps, axis=1) if reps > 1 else ck)
    if on_diagonal:
        key = lax.broadcasted_iota(jnp.int32, sT.shape, 0)
        qry = g * q_rows.shape[0] + lax.broadcasted_iota(jnp.int32, sT.shape, 1)
        sT = jnp.where(key <= qry, sT, NEG)
    return sT, qs


def _foxt_fwd(main, ckb, dm, fused=None, tq=512):
    T, H = dm.T, dm.H
    tq = min(tq, T)
    nq = T // tq
    qg = min(FOX_QG, tq)
    ns = tq // qg
    qi_tab, kj_tab = _fox_pairs(nq, kv_major=False)
    npairs = int(qi_tab.shape[0])
    x_ins, x_outs, x_sems, n_in, n_out = _fused_parts(fused)

    def body(qi_ref, kj_ref, q_ref, k_ref, v_ref, ck_ref, *rest):
        x_in, (o_ref, lse_ref) = rest[:n_in], rest[n_in:n_in + 2]
        x_out = rest[n_in + 2:n_in + 2 + n_out]
        m_ref, l_ref, acc_ref = rest[n_in + 2 + n_out:n_in + 5 + n_out]
        hd = pl.program_id(0)
        t = pl.program_id(1)
        i = qi_ref[t]
        j = kj_ref[t]
        if fused is not None:
            start, middle, finish = fused.hooks(x_in, x_out, *rest[n_in + 5 + n_out:])
            pl.when((hd == 0) & (t == 0))(start)
            pl.when((hd == H // 2) & (t == 0))(middle)

        @pl.when(j == 0)
        def _():
            m_ref[...] = jnp.full_like(m_ref, NEG)
            l_ref[...] = jnp.zeros_like(l_ref)
            acc_ref[...] = jnp.zeros_like(acc_ref)

        def step(on_diagonal):
            kb = k_ref[...].astype(BF16)
            vb = v_ref[...].astype(BF16)
            ck = ck_ref[0]
            scores = [_fox_sT(kb, q_ref[g * qg:(g + 1) * qg, :], ck, g, on_diagonal)[0] for g in range(ns)]
            for g in range(ns):
                cols = slice(g * qg, (g + 1) * qg)
                sT = scores[g]
                m_old = m_ref[:, cols]
                m_new = jnp.maximum(m_old, jnp.max(sT, axis=0, keepdims=True))
                a = jnp.exp(m_old - m_new)
                pT = jnp.exp(sT - m_new)
                l_ref[:, cols] = a * l_ref[:, cols] + jnp.sum(pT, axis=0, keepdims=True)
                acc_ref[:, cols] = a * acc_ref[:, cols] + _tn(vb, pT.astype(BF16))
                m_ref[:, cols] = m_new

        @pl.when(j < i)
        def _():
            step(False)

        @pl.when(j == i)
        def _():
            step(True)
            l = l_ref[...]
            o_ref[...] = acc_ref[...] / l
            lse_ref[0] = m_ref[...] + jnp.log(l)

        if fused is not None:
            pl.when((hd == H - 1) & (t == npairs - 1))(finish)

    outs = pl.pallas_call(
        body, name="fox_fwd",
        grid_spec=pltpu.PrefetchScalarGridSpec(
            num_scalar_prefetch=2, grid=(H, npairs),
            in_specs=[pl.BlockSpec((tq, HEAD), lambda h, t, qi, kj: (qi[t], 4 * H + h)),
                      pl.BlockSpec((tq, HEAD), lambda h, t, qi, kj: (kj[t], 5 * H + h)),
                      pl.BlockSpec((tq, HEAD), lambda h, t, qi, kj: (kj[t], 6 * H + h)),
                      pl.BlockSpec((1, tq, HEAD), lambda h, t, qi, kj: (h, kj[t], 0))] + [HBM] * n_in,
            out_specs=[pl.BlockSpec((HEAD, tq), lambda h, t, qi, kj: (h, qi[t])),
                       pl.BlockSpec((1, 1, tq), lambda h, t, qi, kj: (h, 0, qi[t]))] + [HBM] * n_out,
            scratch_shapes=[pltpu.VMEM((1, tq), F32), pltpu.VMEM((1, tq), F32), pltpu.VMEM((HEAD, tq), F32)]
            + list(x_sems)),
        out_shape=[jax.ShapeDtypeStruct((dm.AW, T), F32), jax.ShapeDtypeStruct((H, 1, T), F32)] + list(x_outs),
        compiler_params=_cparams(("arbitrary", "arbitrary")),
    )(qi_tab, kj_tab, main, main, main, ckb, *x_ins)
    return outs[0], outs[1], tuple(outs[2:])


def _foxt_delta(doT, oT, dm, tc=1024):
    T, H = dm.T, dm.H
    tc = min(tc, T)

    def body(do_ref, o_ref, d_ref):
        d_ref[0] = jnp.sum(do_ref[...] * o_ref[...], axis=0, keepdims=True)

    blk = pl.BlockSpec((HEAD, tc), lambda h, i: (h, i))
    return pl.pallas_call(
        body, name="fox_delta", grid=(H, T // tc),
        in_specs=[blk, blk], out_specs=pl.BlockSpec((1, 1, tc), lambda h, i: (h, 0, i)),
        out_shape=jax.ShapeDtypeStruct((H, 1, T), F32),
        compiler_params=_cparams(("parallel", "parallel")),
    )(doT, oT)


def _foxt_bwd(main, ckb, lse, delta, doT, dm, fused=None, tq=512):
    T, H = dm.T, dm.H
    tq = min(tq, T)
    nq = T // tq
    qg = min(FOX_QG, tq)
    ns = tq // qg
    qi_tab, kj_tab = _fox_pairs(nq, kv_major=True)
    npairs = int(qi_tab.shape[0])
    x_ins, x_outs, x_sems, n_in, n_out = _fused_parts(fused)

    def body(qi_ref, kj_ref, q_ref, k_ref, v_ref, ck_ref, lse_ref, dl_ref, do_ref, *rest):
        x_in = rest[:n_in]
        dq_ref, dk_ref, dv_ref, dr_ref, dc_ref = rest[n_in:n_in + 5]
        x_out = rest[n_in + 5:n_in + 5 + n_out]
        dq_acc, dk_acc, dv_acc = rest[n_in + 5 + n_out:n_in + 8 + n_out]
        hd = pl.program_id(0)
        t = pl.program_id(1)
        i = qi_ref[t]
        kj = kj_ref[t]
        if fused is not None:
            start, middle, finish = fused.hooks(x_in, x_out, *rest[n_in + 8 + n_out:])
            pl.when((hd == 0) & (t == 0))(start)
            pl.when((hd == H // 2) & (t == 0))(middle)

        @pl.when(t == 0)
        def _():
            dq_acc[...] = jnp.zeros_like(dq_acc)
            dr_ref[...] = jnp.zeros_like(dr_ref)

        @pl.when(i == kj)
        def _():
            dk_acc[...] = jnp.zeros_like(dk_acc)
            dv_acc[...] = jnp.zeros_like(dv_acc)

        def step(on_diagonal):
            kb = k_ref[...].astype(BF16)
            vb = v_ref[...].astype(BF16)
            ck = ck_ref[0]
            ones = jnp.ones((qg, HEAD), BF16)
            dq_g, dr_g = [], []
            ahead = []
            for g in range(ns):
                cols = slice(g * qg, (g + 1) * qg)
                sT, qs = _fox_sT(kb, q_ref[cols, :], ck, g, on_diagonal)
                dob = do_ref[:, cols].astype(BF16)
                ahead.append((sT, qs, dob, _nn(vb, dob)))
            for g in range(ns):
                cols = slice(g * qg, (g + 1) * qg)
                sT, qs, dob, dpT = ahead[g]
                pT = jnp.exp(sT - lse_ref[0, :, cols])
                dsT = pT * (dpT - dl_ref[0, :, cols])
                dsb = dsT.astype(BF16)
                dv_acc[...] += _nt(pT.astype(BF16), dob)
                dk_acc[...] += _nn(dsb, jnp.concatenate([qs, ones], axis=1))
                dq_g.append(_tn(kb, dsb) * (HEAD ** -0.5))
                dr_g.append(_nn(jnp.ones((8, tq), BF16), dsb)[0:1])
            dq_acc[i] += jnp.concatenate(dq_g, axis=1)
            dr_ref[0, pl.ds(i, 1), :] += jnp.concatenate(dr_g, axis=1)

        @pl.when(i == kj)
        def _():
            step(True)

        @pl.when(i > kj)
        def _():
            step(False)

        @pl.when(i == nq - 1)
        def _():
            acc = dk_acc[...]
            dk_ref[...] = acc[:, :HEAD].astype(BF16)
            dc_ref[0] = acc[:, HEAD:HEAD + 1]
            dv_ref[...] = dv_acc[...].astype(BF16)

        @pl.when(t == npairs - 1)
        def _():
            for b in range(nq):
                dq_ref[b * tq:(b + 1) * tq, :] = dq_acc[b].T.astype(BF16)

        if fused is not None:
            pl.when((hd == H - 1) & (t == npairs - 1))(finish)

    qrow = pl.BlockSpec((1, 1, tq), lambda h, t, qi, kj: (h, 0, qi[t]))
    outs = pl.pallas_call(
        body, name="fox_bwd",
        grid_spec=pltpu.PrefetchScalarGridSpec(
            num_scalar_prefetch=2, grid=(H, npairs),
            in_specs=[pl.BlockSpec((tq, HEAD), lambda h, t, qi, kj: (qi[t], 4 * H + h)),
                      pl.BlockSpec((tq, HEAD), lambda h, t, qi, kj: (kj[t], 5 * H + h)),
                      pl.BlockSpec((tq, HEAD), lambda h, t, qi, kj: (kj[t], 6 * H + h)),
                      pl.BlockSpec((1, tq, HEAD), lambda h, t, qi, kj: (h, kj[t], 0)),
                      qrow, qrow,
                      pl.BlockSpec((HEAD, tq), lambda h, t, qi, kj: (h, qi[t]))] + [HBM] * n_in,
            out_specs=[pl.BlockSpec((T, HEAD), lambda h, t, qi, kj: (0, h)),
                       pl.BlockSpec((tq, HEAD), lambda h, t, qi, kj: (kj[t], h)),
                       pl.BlockSpec((tq, HEAD), lambda h, t, qi, kj: (kj[t], h)),
                       pl.BlockSpec((1, nq, tq), lambda h, t, qi, kj: (h, 0, 0)),
                       pl.BlockSpec((1, tq, 1), lambda h, t, qi, kj: (h, kj[t], 0))] + [HBM] * n_out,
            scratch_shapes=[pltpu.VMEM((nq, HEAD, tq), F32), pltpu.VMEM((tq, 2 * HEAD), F32),
                            pltpu.VMEM((tq, HEAD), F32)] + list(x_sems)),
        out_shape=[jax.ShapeDtypeStruct((T, dm.AW), BF16), jax.ShapeDtypeStruct((T, dm.AW), BF16),
                   jax.ShapeDtypeStruct((T, dm.AW), BF16), jax.ShapeDtypeStruct((H, nq, tq), F32),
                   jax.ShapeDtypeStruct((H, T, 1), F32)] + list(x_outs),
        compiler_params=_cparams(("arbitrary", "arbitrary")),
    )(qi_tab, kj_tab, main, main, main, ckb, lse, delta, doT, *x_ins)
    return outs[:5], tuple(outs[5:])


def _local_step(dm, x, tgt, w_main, w_rest, later_weights, bias_p, lb_logits, norm_w, g1, g2, g3, g4,
                reduce_hooks=None):
    T, D, H = dm.T, dm.D, dm.H
    h1, r1 = _rms_fwd(x, g1)
    main = _mm(h1, w_main, "nn", F32, "proj_main")
    rest = _mm(h1, w_rest, "nn", F32, "proj_rest", tn=1408)
    ya, o_a, states = _hgrn_fwd(main, lb_logits, norm_w, dm)
    cum = _cum_fwd(rest, bias_p, dm)
    ckb = jnp.broadcast_to(cum[:, :H].T[:, :, None], (H, T, HEAD))
    if isinstance(later_weights, Fused):
        oT_b, lse, later_weights = _foxt_fwd(main, ckb, dm, fused=later_weights)
    else:
        oT_b, lse, _ = _foxt_fwd(main, ckb, dm)
    w_up_a, w_up_b, w_o, w_ffn_in, w_ffn_down = later_weights
    y_a = _mm(ya, w_up_a, "nn", F32, "up_a")
    y_b = _mm(oT_b, w_up_b, "tn", F32, "up_b")
    merged = _merge_fwd(rest, y_a, y_b)
    u = _mm(merged, w_o, "nn", F32, "w_o")
    x1, r2, h3, r3 = _post_pre(x, u, g2, g3)
    gu = _mm(h3, w_ffn_in, "nn", F32, "ffn_in")
    act = _swiglu_fwd(gu)
    w = _mm(act, w_ffn_down, "nn", F32, "ffn_down")
    loss, dy, dw, dg4 = _loss_bwd(x1, w, g4, tgt)
    dact = _mm(dw, w_ffn_down, "nt", BF16, "d_act", tn=1408)
    d_ffn_down = _mm(act, dw, "tn", F32, "dw_ffn_down", tm=1408)
    dgu = _swiglu_bwd(gu, dact)
    dh3 = _mm(dgu, w_ffn_in, "nt", F32, "d_h3")
    d_ffn_in = _mm(h3, dgu, "tn", F32, "dw_ffn_in")
    dx1, du, dg3, dg2 = _rms_bwd2(dy, dh3, x1, r3, g3, u, r2, g2)
    dmerged = _mm(du, w_o, "nt", F32, "d_merged")
    d_w_o = _mm(merged, du, "tn", F32, "dw_o")
    dy_a, dy_b, dgates = _merge_bwd(dmerged, rest, y_a, y_b)
    dya = _mm(dy_a, w_up_a, "nt", F32, "d_ya")
    d_up_a = _mm(ya, dy_a, "tn", F32, "dw_up_a")
    doT = _mm(w_up_b, dy_b, "nt", F32, "d_ob")
    d_up_b = _mm(oT_b, dy_b, "nn", F32, "dw_up_b")
    later_grads = [d_up_a, d_up_b, d_w_o, d_ffn_in, d_ffn_down]
    d_a, dlb, dnw_h, got = _hgrn_bwd(main, lb_logits, norm_w, o_a, states, dya, dm,
                                     fused=reduce_hooks.swap_later(later_grads) if reduce_hooks else None)
    (dq_b, dk_b, dv_b, d_over_keys, d_over_queries), later_from_chips = _foxt_bwd(
        main, ckb, lse, _foxt_delta(doT, oT_b, dm), doT, dm,
        fused=reduce_hooks.scatter_later(got) if reduce_hooks else None)
    dcum = jnp.pad((d_over_keys.reshape(H, T) - d_over_queries.reshape(H, T)).T, ((0, 0), (0, HEAD - H)))
    dbf, dbias = _cum_bwd(dcum, rest, bias_p, dm)
    dmain = jnp.concatenate([d_a, dq_b, dk_b, dv_b], axis=1)
    drest = jnp.concatenate([dgates, dbf], axis=1)
    d_main = _mm(h1, dmain, "tn", F32, "dw_main")
    d_rest = _mm(h1, drest, "tn", F32, "dw_rest", tn=1408)
    if reduce_hooks:
        dh1, w_in_from_chips = _mm(dmain, w_main, "nt", F32, "d_h1_main",
                                   fused=reduce_hooks.scatter_w_in(d_main, d_rest))
    else:
        dh1, w_in_from_chips = _mm(dmain, w_main, "nt", F32, "d_h1_main"), ()
    dh1 = _mm(drest, w_rest, "nt", F32, "d_h1_rest", add=dh1)
    dx, dg1 = _rms_bwd1(dx1, dh1, x, r1, g1)
    big = dict(main=d_main, rest=d_rest, up_a=d_up_a, up_b=d_up_b, w_o=d_w_o, ffn_in=d_ffn_in, ffn_down=d_ffn_down)
    small = dict(loss=loss, bias=dbias, norm_w=jnp.sum(dnw_h, axis=0), lb=dlb, g1=dg1, g2=dg2, g3=dg3, g4=dg4)
    return dx, big, small, tuple(w_in_from_chips) + tuple(later_from_chips)


HBM = pl.BlockSpec(memory_space=pltpu.HBM)


def _place():
    x, y, c = lax.axis_index("x"), lax.axis_index("y"), lax.axis_index("c")
    chips = [(1 - x, y), (x, 1 - y), (1 - x, 1 - y)]
    return x, y, c, chips


class Sharded(NamedTuple):
    kind: str
    r: int
    c: int

    @property
    def full(self):
        return (self.r, 4 * self.c) if self.kind == "col" else (4 * self.r, self.c)

    @property
    def half(self):
        return (self.r // 2, self.c) if self.kind == "col" else (self.r, self.c // 2)

    @property
    def half_of_full(self):
        return (self.r // 2, 4 * self.c) if self.kind == "col" else (4 * self.r, self.c // 2)

    def shard_window(self, ref, s):
        if self.kind == "col":
            return ref.at[:, pl.ds(pl.multiple_of(s * self.c, 128), self.c)]
        return ref.at[pl.ds(pl.multiple_of(s * self.r, 16), self.r), :]

    def half_window(self, ref, s, h):
        if self.kind == "col":
            return ref.at[pl.ds(pl.multiple_of(h * (self.r // 2), 16), self.r // 2),
                          pl.ds(pl.multiple_of(s * self.c, 128), self.c)]
        return ref.at[pl.ds(pl.multiple_of(s * self.r, 16), self.r),
                      pl.ds(pl.multiple_of(h * (self.c // 2), 128), self.c // 2)]

    def half_of(self, ref, h):
        if self.kind == "col":
            n = ref.shape[0] // 2
            return ref.at[pl.ds(pl.multiple_of(h * n, 16), n), :]
        n = ref.shape[1] // 2
        return ref.at[:, pl.ds(pl.multiple_of(h * n, 128), n)]

    def window_of_half(self, ref, s):
        if self.kind == "col":
            return ref.at[:, pl.ds(pl.multiple_of(s * self.c, 128), self.c)]
        return ref.at[pl.ds(pl.multiple_of(s * self.r, 16), self.r), :]


def _gather_hooks(specs):
    n = len(specs)

    def hooks(w_refs, f_refs, send_sems, recv_sems):
        x, y, c, chips = _place()
        q = 2 * x + y
        sibling = (x, y, 1 - c)

        def copy(t, k, src, dst, to):
            return pltpu.make_async_remote_copy(
                src_ref=src, dst_ref=dst, send_sem=send_sems.at[7 * t + k], recv_sem=recv_sems.at[7 * t + k],
                device_id=to, device_id_type=MESH)

        def over_ici(t, j, chip_from, to):
            src = specs[t].half_of(w_refs[t], c)
            return copy(t, j, src, specs[t].half_window(f_refs[t], 2 * chip_from[0] + chip_from[1], c), to)

        def passed_on(t, j, chip_from, half):
            win = specs[t].half_window(f_refs[t], 2 * chip_from[0] + chip_from[1], half)
            return copy(t, 3 + j, win, win, sibling)

        def own_shard(t):
            return copy(t, 6, w_refs[t], specs[t].shard_window(f_refs[t], q), sibling)

        def start():
            for t in range(n):
                own_shard(t).start()
                for j, chip in enumerate(chips):
                    over_ici(t, j, (x, y), (*chip, c)).start()

        def middle():
            for t in range(n):
                for j, chip in enumerate(chips):
                    over_ici(t, j, chip, sibling).wait_recv()
                    passed_on(t, j, chip, c).start()

        def finish():
            for t in range(n):
                own_shard(t).wait_recv()
                for j, chip in enumerate(chips):
                    passed_on(t, j, chip, 1 - c).wait_recv()
            for t in range(n):
                own_shard(t).wait_send()
                for j, chip in enumerate(chips):
                    over_ici(t, j, (x, y), (*chip, c)).wait_send()
                    passed_on(t, j, chip, c).wait_send()

        return start, middle, finish

    sems = (pltpu.SemaphoreType.DMA((7 * n,)), pltpu.SemaphoreType.DMA((7 * n,)))
    return hooks, sems


def _gather_fused(shards, specs):
    hooks, sems = _gather_hooks(specs)
    outs = tuple(jax.ShapeDtypeStruct(sp.full, w.dtype) for sp, w in zip(specs, shards))
    return Fused(ins=tuple(shards), outs=outs, sems=sems, hooks=hooks)


def _run_fused(fused, name):
    n_in, n_out = len(fused.ins), len(fused.outs)

    def body(*refs):
        start, middle, finish = fused.hooks(refs[:n_in], refs[n_in:n_in + n_out], *refs[n_in + n_out:])
        start()
        middle()
        finish()

    return pl.pallas_call(
        body, name=name, in_specs=[HBM] * n_in, out_specs=[HBM] * n_out,
        out_shape=list(fused.outs), scratch_shapes=list(fused.sems),
    )(*fused.ins)


def _swap_fused(fulls, specs):
    n = len(fulls)

    def hooks(g_refs, o_refs, send_sems, recv_sems):
        x, y, c, _ = _place()

        def copies():
            return [pltpu.make_async_remote_copy(
                src_ref=specs[t].half_of(g_refs[t], 1 - c), dst_ref=o_refs[t],
                send_sem=send_sems.at[t], recv_sem=recv_sems.at[t],
                device_id=(x, y, 1 - c), device_id_type=MESH) for t in range(n)]

        def start():
            for cp in copies():
                cp.start()

        def finish():
            for cp in copies():
                cp.wait_recv()
            for cp in copies():
                cp.wait_send()

        return start, lambda: None, finish

    outs = tuple(jax.ShapeDtypeStruct(sp.half_of_full, g.dtype) for sp, g in zip(specs, fulls))
    sems = (pltpu.SemaphoreType.DMA((n,)), pltpu.SemaphoreType.DMA((n,)))
    return Fused(ins=tuple(fulls), outs=outs, sems=sems, hooks=hooks)


def _rtile(n, target):
    t = min(n, (target // 16) * 16)
    while n % t:
        t -= 16
    return t


def _add_sibling(full, got, sp, cq_idx, name):
    hr, hc = sp.half
    tr = _rtile(hr, 256)
    nrt = hr // tr

    def body(cq_ref, g_ref, r_ref, ob_ref, of_ref):
        s = pl.program_id(1)
        v = g_ref[...] + r_ref[...]
        ob_ref[...] = v.astype(BF16)

        @pl.when(s == cq_ref[1])
        def _():
            of_ref[...] = v

    if sp.kind == "col":
        g_spec = pl.BlockSpec((tr, hc), lambda i, s, cq: (cq[0] * nrt + i, s))
        r_spec = pl.BlockSpec((tr, hc), lambda i, s, cq: (i, s))
    else:
        g_spec = pl.BlockSpec((tr, hc), lambda i, s, cq: (s * nrt + i, cq[0]))
        r_spec = pl.BlockSpec((tr, hc), lambda i, s, cq: (s * nrt + i, 0))
    return pl.pallas_call(
        body, name=name,
        grid_spec=pltpu.PrefetchScalarGridSpec(
            num_scalar_prefetch=1, grid=(nrt, 4),
            in_specs=[g_spec, r_spec],
            out_specs=[r_spec, pl.BlockSpec((tr, hc), lambda i, s, cq: (i, 0))]),
        out_shape=[jax.ShapeDtypeStruct(sp.half_of_full, BF16), jax.ShapeDtypeStruct(sp.half, F32)],
        compiler_params=_cparams(("parallel", "arbitrary")),
    )(cq_idx, full, got)


def _scatter_fused(sums, specs):
    n = len(sums)

    def hooks(a_refs, o_refs, send_sems, recv_sems):
        x, y, c, chips = _place()

        def copies():
            return [pltpu.make_async_remote_copy(
                src_ref=specs[t].window_of_half(a_refs[t], 2 * chip[0] + chip[1]), dst_ref=o_refs[t].at[j],
                send_sem=send_sems.at[3 * t + j], recv_sem=recv_sems.at[3 * t + j],
                device_id=(*chip, c), device_id_type=MESH) for t in range(n) for j, chip in enumerate(chips)]

        def start():
            for cp in copies():
                cp.start()

        def finish():
            for cp in copies():
                cp.wait_recv()
            for cp in copies():
                cp.wait_send()

        return start, lambda: None, finish

    outs = tuple(jax.ShapeDtypeStruct((3,) + sp.half, a.dtype) for sp, a in zip(specs, sums))
    sems = (pltpu.SemaphoreType.DMA((3 * n,)), pltpu.SemaphoreType.DMA((3 * n,)))
    return Fused(ins=tuple(sums), outs=outs, sems=sems, hooks=hooks)


def _add_chips(own, got, sp, cq_idx, name):
    hr, hc = sp.half
    tr = _rtile(hr, 256)
    nrt = hr // tr

    def body(cq_ref, a_ref, r_ref, o_ref):
        o_ref[...] = ((a_ref[...] + r_ref[0].astype(F32)) + r_ref[1].astype(F32)) + r_ref[2].astype(F32)

    if sp.kind == "col":
        o_spec = pl.BlockSpec((tr, hc), lambda i, cq: (cq[0] * nrt + i, 0))
    else:
        o_spec = pl.BlockSpec((tr, hc), lambda i, cq: (i, cq[0]))
    return pl.pallas_call(
        body, name=name,
        grid_spec=pltpu.PrefetchScalarGridSpec(
            num_scalar_prefetch=1, grid=(nrt,),
            in_specs=[pl.BlockSpec((tr, hc), lambda i, cq: (i, 0)), pl.BlockSpec((3, tr, hc), lambda i, cq: (0, i, 0))],
            out_specs=o_spec),
        out_shape=jax.ShapeDtypeStruct((sp.r, sp.c), F32),
        compiler_params=_cparams(("parallel",)),
    )(cq_idx, own, got)


def _share_halves(shards, specs):
    n = len(shards)

    def body(*refs):
        o_refs = refs[n:2 * n]
        send_sems, recv_sems = refs[2 * n:]
        x, y, c, _ = _place()

        def copy(t, half):
            win = specs[t].half_of(o_refs[t], half)
            return pltpu.make_async_remote_copy(
                src_ref=win, dst_ref=win, send_sem=send_sems.at[t], recv_sem=recv_sems.at[t],
                device_id=(x, y, 1 - c), device_id_type=MESH)

        for t in range(n):
            copy(t, c).start()
        for t in range(n):
            copy(t, 1 - c).wait_recv()
        for t in range(n):
            copy(t, c).wait_send()

    return pl.pallas_call(
        body, name="share_halves",
        in_specs=[HBM] * n, out_specs=[HBM] * n,
        out_shape=[jax.ShapeDtypeStruct((sp.r, sp.c), F32) for sp in specs],
        input_output_aliases={t: t for t in range(n)},
        scratch_shapes=[pltpu.SemaphoreType.DMA((n,)), pltpu.SemaphoreType.DMA((n,))],
    )(*shards)


def _sum_small(vec):
    rows, w = vec.shape

    def body(v_ref, o_ref, buf, send_sems, recv_sems):
        x, y, c, _ = _place()
        me = 4 * x + 2 * y + c
        buf[me] = v_ref[...]
        cps = []
        for k in range(1, 8):
            to = (x ^ (k >> 2), y ^ ((k >> 1) & 1), c ^ (k & 1))
            cps.append(pltpu.make_async_remote_copy(
                src_ref=v_ref, dst_ref=buf.at[me], send_sem=send_sems.at[k - 1], recv_sem=recv_sems.at[k - 1],
                device_id=to, device_id_type=MESH))
        for cp in cps:
            cp.start()
        for k in range(1, 8):
            pltpu.make_async_remote_copy(
                src_ref=v_ref, dst_ref=buf.at[me ^ k], send_sem=send_sems.at[k - 1], recv_sem=recv_sems.at[k - 1],
                device_id=(x, y, c), device_id_type=MESH).wait_recv()
        for cp in cps:
            cp.wait_send()
        total = buf[0]
        for d in range(1, 8):
            total = total + buf[d]
        o_ref[...] = total

    return pl.pallas_call(
        body, name="sum_small",
        in_specs=[pl.BlockSpec(memory_space=pltpu.VMEM)], out_specs=pl.BlockSpec(memory_space=pltpu.VMEM),
        out_shape=jax.ShapeDtypeStruct((rows, w), F32),
        scratch_shapes=[pltpu.VMEM((8, rows, w), F32), pltpu.SemaphoreType.DMA((7,)), pltpu.SemaphoreType.DMA((7,))],
    )(vec)


def _adam_math(w, g, m, v):
    m = ADAM_B1 * m + (1.0 - ADAM_B1) * g
    v = ADAM_B2 * v + (1.0 - ADAM_B2) * (g * g)
    m_hat = m / (1.0 - ADAM_B1 ** ADAM_STEP)
    v_hat = v / (1.0 - ADAM_B2 ** ADAM_STEP)
    delta = -ADAM_LR * (m_hat / (jnp.sqrt(v_hat) + ADAM_EPS) + ADAM_WD * w)
    return delta, m, v


def _adamw(w, g, m, v, name, tr=128):
    R, Cn = w.shape
    tr = min(tr, R)
    assert R % tr == 0

    def body(w_ref, g_ref, m_ref, v_ref, go_ref, d_ref, mo_ref, vo_ref):
        gv = g_ref[...]
        d, mn, vn = _adam_math(w_ref[...], gv, m_ref[...], v_ref[...])
        go_ref[...] = gv
        d_ref[...] = d
        mo_ref[...] = mn
        vo_ref[...] = vn

    blk = pl.BlockSpec((tr, Cn), lambda i: (i, 0))
    sds = jax.ShapeDtypeStruct((R, Cn), F32)
    return pl.pallas_call(
        body, name=name, grid=(R // tr,),
        in_specs=[blk] * 4, out_specs=[blk] * 4, out_shape=[sds] * 4,
        compiler_params=_cparams(("parallel",)),
    )(w, g, m, v)


ROW_LOSS, ROW_BIAS, ROW_NORM, ROW_LB0, ROW_G1, ROW_G2, ROW_G3, ROW_G4, ROW_LB1 = range(9)
SMALL_ROWS = 16


def _small_update(gsum, wp, mp, vp):
    _, w = gsum.shape

    def body(g_ref, w_ref, m_ref, v_ref, go_ref, d_ref, mo_ref, vo_ref):
        wv = w_ref[...]
        l0 = wv[ROW_LB0:ROW_LB0 + 1, :]
        l1 = wv[ROW_LB1:ROW_LB1 + 1, :]
        mx = jnp.maximum(l0, l1)
        e0 = jnp.exp(l0 - mx)
        e1 = jnp.exp(l1 - mx)
        p0 = e0 / (e0 + e1)
        gs = g_ref[...]
        dl0 = gs[ROW_LB0:ROW_LB0 + 1, :] * p0 * (1.0 - p0)
        row8 = lax.broadcasted_iota(jnp.int32, gs.shape, 0)
        top = jnp.where(row8 == ROW_LB0, dl0, jnp.where(row8 == ROW_LOSS, 0.0, gs))
        bot = jnp.where(row8 == ROW_LB1 - 8, -dl0, 0.0)
        g16 = jnp.concatenate([top, bot], axis=0)
        d, mn, vn = _adam_math(wv, g16, m_ref[...], v_ref[...])
        go_ref[...] = g16
        d_ref[...] = d
        mo_ref[...] = mn
        vo_ref[...] = vn

    sds = jax.ShapeDtypeStruct((SMALL_ROWS, w), F32)
    full = pl.BlockSpec(memory_space=pltpu.VMEM)
    return pl.pallas_call(
        body, name="small_update", in_specs=[full] * 4, out_specs=[full] * 4, out_shape=[sds] * 4,
    )(gsum, wp, mp, vp)


def _w_in_layout(dm):
    cs = dm.NIN // 4
    place = [((cs * q) // 128, (cs * q) % 128) for q in range(4)]
    cp = -(-(cs + max(sh for _, sh in place)) // 128) * 128
    return cs, cp, place


def _zeros_cols(rows, n, dtype):
    return jnp.zeros((rows, n), dtype)


def _unshuffle_w_in(wp, dm, tr=128):
    D, H = dm.D, dm.H
    cs, cp, place = _w_in_layout(dm)
    nm, nrest = dm.NMAIN, dm.NREST
    ng = nm + nrest
    tr = min(tr, D)

    def body(x_ref, main_ref, rest_ref):
        g = None
        for q, (t0, sh) in enumerate(place):
            xq = x_ref[:, q * cp:(q + 1) * cp]
            yq = pltpu.roll(xq, sh, axis=1) if sh else xq
            width = min(cp, ng - t0 * 128)
            parts = [_zeros_cols(tr, t0 * 128, wp.dtype)] if t0 else []
            parts.append(yq[:, :width])
            if ng - t0 * 128 - width:
                parts.append(_zeros_cols(tr, ng - t0 * 128 - width, wp.dtype))
            placed = jnp.concatenate(parts, axis=1)
            g = placed if g is None else g + placed
        main_ref[...] = g[:, :nm]
        tail = g[:, nm:]
        gates = pltpu.roll(tail, nrest - H, axis=1)[:, :2 * D]
        lane = lax.broadcasted_iota(jnp.int32, (tr, HEAD), 1)
        rest_ref[...] = jnp.concatenate([gates, jnp.where(lane < H, tail[:, :HEAD], 0)], axis=1)

    return pl.pallas_call(
        body, name="unshuffle_w_in", grid=(D // tr,),
        in_specs=[pl.BlockSpec((tr, 4 * cp), lambda i: (i, 0))],
        out_specs=[pl.BlockSpec((tr, nm), lambda i: (i, 0)), pl.BlockSpec((tr, nrest), lambda i: (i, 0))],
        out_shape=[jax.ShapeDtypeStruct((D, nm), wp.dtype), jax.ShapeDtypeStruct((D, nrest), wp.dtype)],
        compiler_params=_cparams(("parallel",)),
    )(wp)


def _shuffle_w_in(d_main, d_rest, dm, tr=64):
    D, H = dm.D, dm.H
    cs, cp, place = _w_in_layout(dm)
    nm, nrest = dm.NMAIN, dm.NREST
    ng = nm + nrest
    tr = min(tr, D)

    def body(m_ref, r_ref, o_ref):
        rv = r_ref[...]
        lane = lax.broadcasted_iota(jnp.int32, (tr, HEAD), 1)
        tail = pltpu.roll(jnp.concatenate([rv[:, :2 * D], _zeros_cols(tr, HEAD, F32)], axis=1), H, axis=1)
        head = jnp.where(lane < H, rv[:, 2 * D:], 0.0) + tail[:, :HEAD]
        g = jnp.concatenate([m_ref[...], head, tail[:, HEAD:]], axis=1)
        lanes = lax.broadcasted_iota(jnp.int32, (tr, cp), 1)
        outs = []
        for q, (t0, sh) in enumerate(place):
            width = min(cp, ng - t0 * 128)
            win = g[:, t0 * 128:t0 * 128 + width]
            if width < cp:
                win = jnp.concatenate([win, _zeros_cols(tr, cp - width, F32)], axis=1)
            xq = pltpu.roll(win, cp - sh, axis=1) if sh else win
            outs.append(jnp.where(lanes < cs, xq, 0.0))
        o_ref[...] = jnp.concatenate(outs, axis=1)

    return pl.pallas_call(
        body, name="shuffle_w_in", grid=(D // tr,),
        in_specs=[pl.BlockSpec((tr, nm), lambda i: (i, 0)), pl.BlockSpec((tr, nrest), lambda i: (i, 0))],
        out_specs=pl.BlockSpec((tr, 4 * cp), lambda i: (i, 0)),
        out_shape=jax.ShapeDtypeStruct((D, 4 * cp), F32),
        compiler_params=_cparams(("parallel",)),
    )(d_main, d_rest)


def _pack_small(dm, bias, norm_w, lb_logits, g1, g2, g3, g4):
    D = dm.D
    row = lambda v: jnp.pad(v.reshape(1, -1), ((0, 0), (0, D - v.size)))
    rows = [jnp.zeros((1, D), F32), row(bias), row(norm_w), row(lb_logits[0]), row(g1), row(g2), row(g3), row(g4),
            row(lb_logits[1]), jnp.zeros((SMALL_ROWS - 9, D), F32)]
    return jnp.concatenate(rows, axis=0)


def _unpack_small(p, dm):
    H, AW = dm.H, dm.AW
    return (p[ROW_BIAS:ROW_BIAS + 1, :H], jnp.concatenate([p[ROW_LB0:ROW_LB0 + 1, :AW], p[ROW_LB1:ROW_LB1 + 1, :AW]], axis=0),
            p[ROW_NORM:ROW_NORM + 1, :HEAD], p[ROW_G1:ROW_G1 + 1], p[ROW_G2:ROW_G2 + 1], p[ROW_G3:ROW_G3 + 1],
            p[ROW_G4:ROW_G4 + 1])


def _step(dm, x, w_in, b_fox_f, hgrn_lb_logits, hgrn_norm_w, w_up_a, w_up_b, w_o, norm_mix_pre, norm_mix_post,
          norm_ffn_pre, norm_ffn_post, w_ffn_in, w_ffn_down, loss_target, moments_m, moments_v):
    xi, yi, ci = lax.axis_index("x"), lax.axis_index("y"), lax.axis_index("c")
    cq_idx = jnp.stack([ci, 2 * xi + yi]).astype(jnp.int32)
    D, AW, FF = dm.D, dm.AW, dm.FF
    cs, cp, _ = _w_in_layout(dm)
    big_names = ["w_in", "w_up_a", "w_up_b", "w_o", "w_ffn_in", "w_ffn_down"]
    specs = [Sharded("col", D, cp), Sharded("col", AW, D // 4), Sharded("col", AW, D // 4),
             Sharded("row", D // 4, D), Sharded("col", D, 2 * FF // 4), Sharded("row", FF // 4, D)]
    shards = [w_in[0], w_up_a[0], w_up_b[0], w_o[0], w_ffn_in[0], w_ffn_down[0]]

    sent = [jnp.pad(shards[0].astype(BF16), ((0, 0), (0, cp - cs)))] + [w.astype(BF16) for w in shards[1:]]
    (f_in,) = _run_fused(_gather_fused(sent[:1], specs[:1]), "gather_w_in")
    w_main, w_rest = _unshuffle_w_in(f_in, dm)

    class ReduceHooks:
        pairs = [None] * 6
        fulls = None

        def pair_sums(self, fulls, from_sibling, first):
            for t, (g, r) in enumerate(zip(fulls, from_sibling), start=first):
                self.pairs[t] = _add_sibling(g, r, specs[t], cq_idx, "add_sibling_" + big_names[t])
            return _scatter_fused([self.pairs[t][0] for t in range(first, first + len(fulls))],
                                  specs[first:first + len(fulls)])

        def swap_later(self, fulls):
            self.fulls = fulls
            return _swap_fused(fulls, specs[1:])

        def scatter_later(self, from_sibling):
            return self.pair_sums(self.fulls, from_sibling, 1)

        def scatter_w_in(self, d_main, d_rest):
            full = [_shuffle_w_in(d_main, d_rest, dm)]
            return self.pair_sums(full, _run_fused(_swap_fused(full, specs[:1]), "swap_halves_w_in"), 0)

    hooks = ReduceHooks()

    bias_p = jnp.pad(b_fox_f, ((0, 0), (0, HEAD - dm.H)))
    dx, _, small, from_chips = _local_step(
        dm, x[0], loss_target[0], w_main, w_rest, _gather_fused(sent[1:], specs[1:]), bias_p,
        hgrn_lb_logits, hgrn_norm_w, norm_mix_pre, norm_mix_post, norm_ffn_pre, norm_ffn_post,
        reduce_hooks=hooks)

    halves = [_add_chips(p[1], r, sp, cq_idx, "add_chips_" + n)
              for p, r, sp, n in zip(hooks.pairs, from_chips, specs, big_names)]
    grads = list(_share_halves(halves, specs))
    grads[0] = grads[0][:, :cs]

    row = lambda v: jnp.pad(v.reshape(1, -1), ((0, 0), (0, D - v.size)))
    vec = jnp.concatenate([row(small["loss"][:, :1]), row(small["bias"][:, :dm.H]), row(small["norm_w"]),
                           row(small["lb"]), small["g1"], small["g2"], small["g3"], small["g4"]], axis=0)
    gsum = _sum_small(vec)
    loss = gsum[ROW_LOSS, 0]

    smalls = lambda t: (t["b_fox_f"], t["hgrn_norm_w"], t["hgrn_lb_logits"], t["norm_mix_pre"], t["norm_mix_post"],
                        t["norm_ffn_pre"], t["norm_ffn_post"])
    params = dict(b_fox_f=b_fox_f, hgrn_norm_w=hgrn_norm_w, hgrn_lb_logits=hgrn_lb_logits, norm_mix_pre=norm_mix_pre,
                  norm_mix_post=norm_mix_post, norm_ffn_pre=norm_ffn_pre, norm_ffn_post=norm_ffn_post)
    sg, sd, sm, sv = _small_update(gsum, _pack_small(dm, *smalls(params)), _pack_small(dm, *smalls(moments_m)),
                                   _pack_small(dm, *smalls(moments_v)))
    big_out = {}
    for name, wsh, g in zip(big_names, shards, grads):
        go, d, mn, vn = _adamw(wsh, g, moments_m[name][0], moments_v[name][0], "adamw_" + name)
        big_out[name] = (go[None], d[None], mn[None], vn[None])

    order = ["w_in", "b_fox_f", "hgrn_lb_logits", "hgrn_norm_w", "w_up_a", "w_up_b", "w_o", "norm_mix_pre",
             "norm_mix_post", "norm_ffn_pre", "norm_ffn_post", "w_ffn_in", "w_ffn_down"]
    outs = []
    for kind, packed in enumerate([sg, sd, sm, sv]):
        b, lbl, nw, p1, p2, p3, p4 = _unpack_small(packed, dm)
        sm_map = dict(b_fox_f=b, hgrn_lb_logits=lbl, hgrn_norm_w=nw, norm_mix_pre=p1, norm_mix_post=p2,
                      norm_ffn_pre=p3, norm_ffn_post=p4)
        outs.append([big_out[n][kind] if n in big_out else sm_map[n] for n in order])
    return (loss, dx[None], *outs[0], *outs[1], *outs[2], *outs[3])


def kernel(x, w_in, b_fox_f, hgrn_lb_logits, hgrn_norm_w, w_up_a, w_up_b, w_o, norm_mix_pre, norm_mix_post, norm_ffn_pre, norm_ffn_post, w_ffn_in, w_ffn_down, loss_target, m_w_in, m_b_fox_f, m_hgrn_lb_logits, m_hgrn_norm_w, m_w_up_a, m_w_up_b, m_w_o, m_norm_mix_pre, m_norm_mix_post, m_norm_ffn_pre, m_norm_ffn_post, m_w_ffn_in, m_w_ffn_down, v_w_in, v_b_fox_f, v_hgrn_lb_logits, v_hgrn_norm_w, v_w_up_a, v_w_up_b, v_w_o, v_norm_mix_pre, v_norm_mix_post, v_norm_ffn_pre, v_norm_ffn_post, v_w_ffn_in, v_w_ffn_down):
    dm = Dims(T=x.shape[1], D=x.shape[2], FF=w_ffn_down.shape[1] * 4)
    moments_m = dict(w_in=m_w_in, b_fox_f=m_b_fox_f, hgrn_lb_logits=m_hgrn_lb_logits, hgrn_norm_w=m_hgrn_norm_w,
                     w_up_a=m_w_up_a, w_up_b=m_w_up_b, w_o=m_w_o, norm_mix_pre=m_norm_mix_pre,
                     norm_mix_post=m_norm_mix_post, norm_ffn_pre=m_norm_ffn_pre, norm_ffn_post=m_norm_ffn_post,
                     w_ffn_in=m_w_ffn_in, w_ffn_down=m_w_ffn_down)
    moments_v = dict(w_in=v_w_in, b_fox_f=v_b_fox_f, hgrn_lb_logits=v_hgrn_lb_logits, hgrn_norm_w=v_hgrn_norm_w,
                     w_up_a=v_w_up_a, w_up_b=v_w_up_b, w_o=v_w_o, norm_mix_pre=v_norm_mix_pre,
                     norm_mix_post=v_norm_mix_post, norm_ffn_pre=v_norm_ffn_pre, norm_ffn_post=v_norm_ffn_post,
                     w_ffn_in=v_w_ffn_in, w_ffn_down=v_w_ffn_down)
    return _step(dm, x, w_in, b_fox_f, hgrn_lb_logits, hgrn_norm_w, w_up_a, w_up_b, w_o, norm_mix_pre, norm_mix_post,
                 norm_ffn_pre, norm_ffn_post, w_ffn_in, w_ffn_down, loss_target, moments_m, moments_v)
```

```python
from typing import NamedTuple

import numpy as np
import jax
import jax.numpy as jnp
from jax import lax
from jax.experimental import pallas as pl
from jax.experimental.pallas import tpu as pltpu

F32 = jnp.float32
BF16 = jnp.bfloat16
MESH = pl.DeviceIdType.MESH

RMS_EPS = 1e-6
HEAD = 128
CHUNK = 64
GROUP = 4
LEVELS = (32, 16, 8, 4, 2, 1)
NEG = -1e30

ADAM_LR = 0.001
ADAM_B1 = 0.9
ADAM_B2 = 0.999
ADAM_EPS = 1e-08
ADAM_WD = 0.01
ADAM_STEP = 10

VMEM_LIMIT = 56 * 1024 * 1024


class Dims(NamedTuple):
    T: int
    D: int
    FF: int

    @property
    def AW(self):
        return self.D // 2

    @property
    def H(self):
        return self.AW // HEAD

    @property
    def NMAIN(self):
        return 7 * self.AW

    @property
    def NREST(self):
        return 2 * self.D + HEAD

    @property
    def NIN(self):
        return 7 * self.AW + self.H + 2 * self.D


def _cparams(sem, vmem=VMEM_LIMIT, **kw):
    return pltpu.CompilerParams(dimension_semantics=sem, vmem_limit_bytes=vmem, **kw)


def _tile(n, target):
    if n <= target:
        return n
    t = (target // 128) * 128
    while t >= 128:
        if n % t == 0:
            return t
        t -= 128
    raise ValueError(f"no tile for {n}")


def _dot(a, b, dims):
    return lax.dot_general(a, b, (dims, ((), ())), preferred_element_type=F32)


def _nn(a, b):
    return _dot(a, b, ((1,), (0,)))


def _nt(a, b):
    return _dot(a, b, ((1,), (1,)))


def _tn(a, b):
    return _dot(a, b, ((0,), (0,)))


def _sigmoid(x):
    return jax.nn.sigmoid(x)


def _split2(x):
    hi = x.astype(BF16)
    lo = (x - hi.astype(F32)).astype(BF16)
    return hi, lo


def _split3(x):
    hi = x.astype(BF16)
    r = x - hi.astype(F32)
    mid = r.astype(BF16)
    lo = (r - mid.astype(F32)).astype(BF16)
    return hi, mid, lo


def _mm(a, b, mode, out_dtype, name, add=None, tm=1024, tn=1024, tk=None, fused=None):
    if mode == "nn":
        (M, K), (K2, N) = a.shape, b.shape
    elif mode == "nt":
        (M, K), (N, K2) = a.shape, b.shape
    else:
        (K, M), (K2, N) = a.shape, b.shape
    assert K == K2, (a.shape, b.shape, mode)
    tm, tn = _tile(M, tm), _tile(N, tn)
    isz = lambda t: jnp.dtype(t.dtype).itemsize

    def vmem_need(tk_):
        need = 2 * (tm * tk_ * isz(a) + tk_ * tn * isz(b) + tm * tn * jnp.dtype(out_dtype).itemsize)
        return need + (tm * tn * 4 if K // tk_ > 1 else 0) + (2 * tm * tn * isz(add) if add is not None else 0)

    budget = VMEM_LIMIT - 16 * 1024 * 1024
    for target in ((tk,) if tk is not None else ((2048, 1024, 512) if mode == "tn" else (4096, 2048, 1024, 512))):
        tk_ = _tile(K, target)
        if vmem_need(tk_) <= budget:
            break
    tk = tk_
    nk = K // tk
    assert vmem_need(tk) <= budget, (name, vmem_need(tk))
    if mode == "nn":
        a_spec = pl.BlockSpec((tm, tk), lambda i, j, k: (i, k))
        b_spec = pl.BlockSpec((tk, tn), lambda i, j, k: (k, j))
        op = _nn
    elif mode == "nt":
        a_spec = pl.BlockSpec((tm, tk), lambda i, j, k: (i, k))
        b_spec = pl.BlockSpec((tn, tk), lambda i, j, k: (j, k))
        op = _nt
    else:
        a_spec = pl.BlockSpec((tk, tm), lambda i, j, k: (k, i))
        b_spec = pl.BlockSpec((tk, tn), lambda i, j, k: (k, j))
        op = _tn
    o_spec = pl.BlockSpec((tm, tn), lambda i, j, k: (i, j))
    has_add = add is not None
    n_own = 3 if has_add else 2
    x_ins, x_outs, x_sems, n_in, n_out = _fused_parts(fused)
    grid = (M // tm, N // tn, nk)

    def body(*refs):
        a_ref, b_ref = refs[:2]
        add_ref = refs[2] if has_add else None
        o_ref = refs[n_own + n_in]
        acc = refs[n_own + n_in + 1 + n_out] if nk > 1 else None
        k = pl.program_id(2)
        if fused is not None:
            step = (pl.program_id(0) * grid[1] + pl.program_id(1)) * nk + k
            start, middle, finish_x = fused.hooks(
                refs[n_own:n_own + n_in], refs[n_own + n_in + 1:n_own + n_in + 1 + n_out],
                *refs[n_own + n_in + 1 + n_out + (1 if nk > 1 else 0):])
            pl.when(step == 0)(start)
            pl.when(step == (grid[0] * grid[1] * nk) // 2)(middle)

        def finish(r):
            if has_add:
                r = r + add_ref[...].astype(F32)
            o_ref[...] = r.astype(out_dtype)

        part = op(a_ref[...].astype(BF16), b_ref[...].astype(BF16))
        if nk == 1:
            finish(part)
        else:
            @pl.when(k == 0)
            def _():
                acc[...] = part

            @pl.when((k > 0) & (k < nk - 1))
            def _():
                acc[...] += part

            @pl.when(k == nk - 1)
            def _():
                finish(acc[...] + part)

        if fused is not None:
            pl.when(step == grid[0] * grid[1] * nk - 1)(finish_x)

    in_specs = [a_spec, b_spec] + ([o_spec] if has_add else []) + [HBM] * n_in
    args = (a, b) + ((add,) if has_add else ()) + x_ins
    outs = pl.pallas_call(
        body, name=name, grid=grid,
        in_specs=in_specs, out_specs=[o_spec] + [HBM] * n_out,
        out_shape=[jax.ShapeDtypeStruct((M, N), out_dtype)] + list(x_outs),
        scratch_shapes=([pltpu.VMEM((tm, tn), F32)] if nk > 1 else []) + list(x_sems),
        compiler_params=_cparams(("parallel", "parallel", "arbitrary") if fused is None else ("arbitrary",) * 3),
    )(*args)
    return outs[0] if fused is None else (outs[0], tuple(outs[1:]))


def _rows(tr, w, col=0):
    return pl.BlockSpec((tr, w), lambda i, *_: (i, col))


def _vec(w):
    return pl.BlockSpec((1, w), lambda i, *_: (0, 0))


def _rstd(v):
    return lax.rsqrt(jnp.mean(v * v, axis=-1, keepdims=True) + RMS_EPS)


def _rms_bwd(dn, n, r):
    return r * (dn - n * jnp.mean(dn * n, axis=-1, keepdims=True))


def _colsum(v):
    return jnp.sum(v, axis=0, keepdims=True)


def _rms_fwd(x, g, tr=256):
    T, D = x.shape

    def body(x_ref, g_ref, h_ref, r_ref):
        xv = x_ref[...]
        r = _rstd(xv)
        h_ref[...] = (xv * r * g_ref[...]).astype(BF16)
        r_ref[...] = r

    return pl.pallas_call(
        body, name="rms_fwd", grid=(T // tr,),
        in_specs=[_rows(tr, D), _vec(D)],
        out_specs=[_rows(tr, D), _rows(tr, 1)],
        out_shape=[jax.ShapeDtypeStruct((T, D), BF16), jax.ShapeDtypeStruct((T, 1), F32)],
        compiler_params=_cparams(("parallel",)),
    )(x, g)


def _merge_fwd(rest, y_a, y_b, tr=256):
    T, D = y_a.shape

    def body(ga_ref, gb_ref, ya_ref, yb_ref, o_ref):
        o_ref[...] = (_sigmoid(ga_ref[...]) * ya_ref[...] + _sigmoid(gb_ref[...]) * yb_ref[...]).astype(BF16)

    return pl.pallas_call(
        body, name="merge_fwd", grid=(T // tr,),
        in_specs=[_rows(tr, D, 0), _rows(tr, D, 1), _rows(tr, D), _rows(tr, D)],
        out_specs=_rows(tr, D),
        out_shape=jax.ShapeDtypeStruct((T, D), BF16),
        compiler_params=_cparams(("parallel",)),
    )(rest, rest, y_a, y_b)


def _post_pre(x, u, g2, g3, tr=256):
    T, D = x.shape

    def body(x_ref, u_ref, g2_ref, g3_ref, x1_ref, r2_ref, h3_ref, r3_ref):
        uv = u_ref[...]
        r2 = _rstd(uv)
        x1 = x_ref[...] + uv * r2 * g2_ref[...]
        r3 = _rstd(x1)
        x1_ref[...] = x1
        r2_ref[...] = r2
        h3_ref[...] = (x1 * r3 * g3_ref[...]).astype(BF16)
        r3_ref[...] = r3

    return pl.pallas_call(
        body, name="post_pre", grid=(T // tr,),
        in_specs=[_rows(tr, D), _rows(tr, D), _vec(D), _vec(D)],
        out_specs=[_rows(tr, D), _rows(tr, 1), _rows(tr, D), _rows(tr, 1)],
        out_shape=[jax.ShapeDtypeStruct((T, D), F32), jax.ShapeDtypeStruct((T, 1), F32),
                   jax.ShapeDtypeStruct((T, D), BF16), jax.ShapeDtypeStruct((T, 1), F32)],
        compiler_params=_cparams(("parallel",)),
    )(x, u, g2, g3)


def _ffn_in_swiglu(h3, w_ffn_in, tm=1024, tn=512):
    T, D = h3.shape
    FF = w_ffn_in.shape[1] // 2
    tm, tn = _tile(T, tm), _tile(FF, tn)
    nj = FF // tn

    def body(a_ref, bg_ref, bu_ref, g_ref, u_ref, act_ref):
        av = a_ref[...]
        gv = _nn(av, bg_ref[...])
        uv = _nn(av, bu_ref[...])
        g_ref[...] = gv
        u_ref[...] = uv
        act_ref[...] = (gv * _sigmoid(gv) * uv).astype(BF16)

    blk = pl.BlockSpec((tm, tn), lambda i, j: (i, j))
    return pl.pallas_call(
        body, name="ffn_in_swiglu", grid=(T // tm, nj),
        in_specs=[pl.BlockSpec((tm, D), lambda i, j: (i, 0)),
                  pl.BlockSpec((D, tn), lambda i, j: (0, j)),
                  pl.BlockSpec((D, tn), lambda i, j: (0, j + nj))],
        out_specs=[blk, blk, blk],
        out_shape=[jax.ShapeDtypeStruct((T, FF), F32), jax.ShapeDtypeStruct((T, FF), F32),
                   jax.ShapeDtypeStruct((T, FF), BF16)],
        compiler_params=_cparams(("parallel", "parallel")),
    )(h3, w_ffn_in, w_ffn_in)


def _d_act_swiglu(dw, w_ffn_down, gate, up, tm=1024, tn=512):
    T, D = dw.shape
    FF = w_ffn_down.shape[0]
    tm, tn = _tile(T, tm), _tile(FF, tn)

    def body(a_ref, b_ref, g_ref, u_ref, dg_ref, du_ref):
        dact = _nt(a_ref[...], b_ref[...])
        gv = g_ref[...]
        s = _sigmoid(gv)
        dg_ref[...] = (dact * u_ref[...] * (s * (1.0 + gv * (1.0 - s)))).astype(BF16)
        du_ref[...] = (dact * (gv * s)).astype(BF16)

    blk = pl.BlockSpec((tm, tn), lambda i, j: (i, j))
    sds = jax.ShapeDtypeStruct((T, FF), BF16)
    return pl.pallas_call(
        body, name="d_act_swiglu", grid=(T // tm, FF // tn),
        in_specs=[pl.BlockSpec((tm, D), lambda i, j: (i, 0)), pl.BlockSpec((tn, D), lambda i, j: (j, 0)), blk, blk],
        out_specs=[blk, blk], out_shape=[sds, sds],
        compiler_params=_cparams(("parallel", "parallel")),
    )(dw, w_ffn_down, gate, up)


def _swiglu_fwd(gu, tr=256):
    T, FF2 = gu.shape
    FF = FF2 // 2
    tc = _tile(FF, 1024)
    nc = FF // tc

    def body(g_ref, u_ref, o_ref):
        gv = g_ref[...]
        o_ref[...] = (gv * _sigmoid(gv) * u_ref[...]).astype(BF16)

    return pl.pallas_call(
        body, name="swiglu_fwd", grid=(T // tr, nc),
        in_specs=[pl.BlockSpec((tr, tc), lambda i, j: (i, j)),
                  pl.BlockSpec((tr, tc), lambda i, j: (i, j + nc))],
        out_specs=pl.BlockSpec((tr, tc), lambda i, j: (i, j)),
        out_shape=jax.ShapeDtypeStruct((T, FF), BF16),
        compiler_params=_cparams(("parallel", "parallel")),
    )(gu, gu)


def _loss_bwd(x1, w, g4, tgt, tr=256):
    T, D = x1.shape

    def body(x1_ref, w_ref, g4_ref, t_ref, loss_ref, dy_ref, dw_ref, dg_ref):
        i = pl.program_id(0)

        @pl.when(i == 0)
        def _():
            loss_ref[...] = jnp.zeros_like(loss_ref)
            dg_ref[...] = jnp.zeros_like(dg_ref)

        wv = w_ref[...]
        g4v = g4_ref[...]
        r4 = _rstd(wv)
        n4 = wv * r4
        e = x1_ref[...] + n4 * g4v - t_ref[...]
        loss_ref[...] += 0.5 * jnp.sum(jnp.mean(e * e, axis=-1, keepdims=True), axis=0, keepdims=True)
        dy = e * (1.0 / D)
        dy_ref[...] = dy
        dg_ref[...] += _colsum(dy * n4)
        dw_ref[...] = _rms_bwd(dy * g4v, n4, r4).astype(BF16)

    return pl.pallas_call(
        body, name="loss_bwd", grid=(T // tr,),
        in_specs=[_rows(tr, D), _rows(tr, D), _vec(D), _rows(tr, D)],
        out_specs=[_vec(HEAD), _rows(tr, D), _rows(tr, D), _vec(D)],
        out_shape=[jax.ShapeDtypeStruct((1, HEAD), F32), jax.ShapeDtypeStruct((T, D), F32),
                   jax.ShapeDtypeStruct((T, D), BF16), jax.ShapeDtypeStruct((1, D), F32)],
        compiler_params=_cparams(("arbitrary",)),
    )(x1, w, g4, tgt)


def _swiglu_bwd(gu, dact, tr=128):
    T, FF2 = gu.shape
    FF = FF2 // 2

    def body(g_ref, u_ref, d_ref, o_ref):
        h = pl.program_id(1)
        gv = g_ref[...]
        s = _sigmoid(gv)
        dv = d_ref[...].astype(F32)

        @pl.when(h == 0)
        def _():
            o_ref[...] = (dv * u_ref[...] * (s * (1.0 + gv * (1.0 - s)))).astype(BF16)

        @pl.when(h == 1)
        def _():
            o_ref[...] = (dv * (gv * s)).astype(BF16)

    return pl.pallas_call(
        body, name="swiglu_bwd", grid=(T // tr, 2),
        in_specs=[pl.BlockSpec((tr, FF), lambda i, h: (i, 0)),
                  pl.BlockSpec((tr, FF), lambda i, h: (i, 1)),
                  pl.BlockSpec((tr, FF), lambda i, h: (i, 0))],
        out_specs=pl.BlockSpec((tr, FF), lambda i, h: (i, h)),
        out_shape=jax.ShapeDtypeStruct((T, FF2), BF16),
        compiler_params=_cparams(("parallel", "arbitrary")),
    )(gu, gu, dact)


def _rms_bwd2(dy, dh3, x1, r3, g3, u, r2, g2, tr=256):
    T, D = dy.shape

    def body(dy_ref, dh_ref, x1_ref, r3_ref, g3_ref, u_ref, r2_ref, g2_ref, dx1_ref, du_ref, dg3_ref, dg2_ref):
        i = pl.program_id(0)

        @pl.when(i == 0)
        def _():
            dg3_ref[...] = jnp.zeros_like(dg3_ref)
            dg2_ref[...] = jnp.zeros_like(dg2_ref)

        r3v, r2v = r3_ref[...], r2_ref[...]
        dh = dh_ref[...]
        n3 = x1_ref[...] * r3v
        dg3_ref[...] += _colsum(dh * n3)
        dx1 = dy_ref[...] + _rms_bwd(dh * g3_ref[...], n3, r3v)
        dx1_ref[...] = dx1
        n2 = u_ref[...] * r2v
        dg2_ref[...] += _colsum(dx1 * n2)
        du_ref[...] = _rms_bwd(dx1 * g2_ref[...], n2, r2v).astype(BF16)

    return pl.pallas_call(
        body, name="rms_bwd2", grid=(T // tr,),
        in_specs=[_rows(tr, D), _rows(tr, D), _rows(tr, D), _rows(tr, 1), _vec(D),
                  _rows(tr, D), _rows(tr, 1), _vec(D)],
        out_specs=[_rows(tr, D), _rows(tr, D), _vec(D), _vec(D)],
        out_shape=[jax.ShapeDtypeStruct((T, D), F32), jax.ShapeDtypeStruct((T, D), BF16),
                   jax.ShapeDtypeStruct((1, D), F32), jax.ShapeDtypeStruct((1, D), F32)],
        compiler_params=_cparams(("arbitrary",)),
    )(dy, dh3, x1, r3, g3, u, r2, g2)


def _merge_bwd(dmerged, rest, y_a, y_b, tr=256):
    T, D = dmerged.shape

    def body(dm_ref, ga_ref, gb_ref, ya_ref, yb_ref, dya_ref, dyb_ref, dg_ref):
        h = pl.program_id(1)
        dm = dm_ref[...]

        @pl.when(h == 0)
        def _():
            s = _sigmoid(ga_ref[...])
            dya_ref[...] = (dm * s).astype(BF16)
            dg_ref[...] = (dm * ya_ref[...] * s * (1.0 - s)).astype(BF16)

        @pl.when(h == 1)
        def _():
            s = _sigmoid(gb_ref[...])
            dyb_ref[...] = (dm * s).astype(BF16)
            dg_ref[...] = (dm * yb_ref[...] * s * (1.0 - s)).astype(BF16)

    blk = lambda col: pl.BlockSpec((tr, D), lambda i, h: (i, col))
    return pl.pallas_call(
        body, name="merge_bwd", grid=(T // tr, 2),
        in_specs=[blk(0), blk(0), blk(1), blk(0), blk(0)],
        out_specs=[blk(0), blk(0), pl.BlockSpec((tr, D), lambda i, h: (i, h))],
        out_shape=[jax.ShapeDtypeStruct((T, D), BF16), jax.ShapeDtypeStruct((T, D), BF16),
                   jax.ShapeDtypeStruct((T, 2 * D), BF16)],
        compiler_params=_cparams(("parallel", "arbitrary")),
    )(dmerged, rest, rest, y_a, y_b)


def _rms_bwd1(dx1, dh1, x, r1, g1, tr=256):
    T, D = x.shape

    def body(dx1_ref, dh_ref, x_ref, r_ref, g_ref, dx_ref, dg_ref):
        i = pl.program_id(0)

        @pl.when(i == 0)
        def _():
            dg_ref[...] = jnp.zeros_like(dg_ref)

        rv = r_ref[...]
        dh = dh_ref[...]
        n = x_ref[...] * rv
        dg_ref[...] += _colsum(dh * n)
        dx_ref[...] = dx1_ref[...] + _rms_bwd(dh * g_ref[...], n, rv)

    return pl.pallas_call(
        body, name="rms_bwd1", grid=(T // tr,),
        in_specs=[_rows(tr, D), _rows(tr, D), _rows(tr, D), _rows(tr, 1), _vec(D)],
        out_specs=[_rows(tr, D), _vec(D)],
        out_shape=[jax.ShapeDtypeStruct((T, D), F32), jax.ShapeDtypeStruct((1, D), F32)],
        compiler_params=_cparams(("arbitrary",)),
    )(dx1, dh1, x, r1, g1)


def _hgrn_consts():
    C = CHUNK
    nl = len(LEVELS) + 1
    w = np.zeros((nl, C, C), np.float32)
    w[0] = np.tril(np.ones((C, C), np.float32))
    for li, m in enumerate(LEVELS, start=1):
        for r in range(C):
            mid = (r // (2 * m)) * 2 * m + m
            if r >= mid:
                w[li, r, mid:r + 1] = 1.0
            else:
                w[li, r, r + 1:mid] = 1.0
    w_all = w.reshape(nl * C, C)
    w2 = np.concatenate([w_all, w_all], axis=1)
    w2t = np.concatenate([w_all.T, w_all.T], axis=1)
    R = GROUP * C
    t = np.arange(R)[:, None]
    s = np.arange(R)[None, :]
    masks = np.zeros((nl, R, R), np.float32)
    masks[0] = (t == s)
    for li, m in enumerate(LEVELS, start=1):
        masks[li] = ((t ^ s) < 2 * m)
    return jnp.asarray(w2, BF16), jnp.asarray(w2t, BF16), jnp.asarray(masks, F32)


def _hgrn_gates(z, lg_ref):
    l0 = lg_ref[0:1, :]
    l1 = lg_ref[1:2, :]
    mx = jnp.maximum(l0, l1)
    e0 = jnp.exp(l0 - mx)
    e1 = jnp.exp(l1 - mx)
    lb = e0 / (e0 + e1)
    om = 1.0 - lb
    sg = _sigmoid(z)
    sgm = _sigmoid(-z)
    f = lb + om * sg
    return lb, om, sg, sgm, f, jnp.log(f), om * sgm


def _hgrn_levels(q, kk, lf, w2_ref):
    C = CHUNK
    nl = len(LEVELS) + 1
    lf_hi, lf_lo = _split2(lf)
    per_chunk = []
    for c in range(GROUP):
        rhs = jnp.concatenate([lf_hi[c * C:(c + 1) * C], lf_lo[c * C:(c + 1) * C]], axis=0)
        per_chunk.append(_nn(w2_ref[...], rhs))
    args = [jnp.concatenate([per_chunk[c][l * C:(l + 1) * C] for c in range(GROUP)], axis=0) for l in range(nl)]
    exps = [jnp.exp(a) for a in args]
    row = lax.broadcasted_iota(jnp.int32, q.shape, 0)
    qf, kf, mts = [q], [kk], [None]
    for li, m in enumerate(LEVELS, start=1):
        mt = jnp.where((row & m) != 0, 1.0, 0.0).astype(F32)
        mts.append(mt)
        qf.append(q * exps[li] * mt)
        kf.append(kk * exps[li] * (1.0 - mt))
    return args, exps, mts, qf, kf


def _hgrn_scores(qf, kf, masks_ref):
    p = None
    for l in range(len(qf)):
        pl_ = _nt(qf[l].astype(BF16), kf[l].astype(BF16)) * masks_ref[l]
        p = pl_ if p is None else p + pl_
    return p


def _hgrn_fwd(main, lb_logits, norm_w, dm):
    T, H, C = dm.T, dm.H, CHUNK
    R = GROUP * C
    nj = T // R
    w2, _, masks = _hgrn_consts()

    def body(x_ref, lg_ref, nw_ref, w2_ref, masks_ref, ya_ref, o_ref, sp_ref, st_ref):
        j = pl.program_id(1)

        @pl.when(j == 0)
        def _():
            st_ref[...] = jnp.zeros_like(st_ref)

        q, z, v, gv = (x_ref[:, s * HEAD:(s + 1) * HEAD] for s in range(4))
        vb = v.astype(BF16)
        _, _, _, _, _, lf, kk = _hgrn_gates(z, lg_ref)
        args, exps, _, qf, kf = _hgrn_levels(q, kk, lf, w2_ref)
        p = _hgrn_scores(qf, kf, masks_ref)
        o_intra = _nn(p.astype(BF16), vb)
        b, eb = args[0], exps[0]
        o_inter = []
        for c in range(GROUP):
            sl = slice(c * C, (c + 1) * C)
            st = st_ref[...]
            sp_ref[0, c] = st
            blast = b[c * C + C - 1:c * C + C, :]
            o_inter.append(_nt((q[sl] * eb[sl]).astype(BF16), st.astype(BF16)))
            kd = (kk[sl] * jnp.exp(blast - b[sl])).astype(BF16)
            st_ref[...] = st * jnp.exp(blast) + _tn(vb[sl], kd)
        o = o_intra + jnp.concatenate(o_inter, axis=0)
        o_ref[...] = o
        ya_ref[...] = (o * _rstd(o) * nw_ref[...] * (gv * _sigmoid(gv))).astype(BF16)

    return pl.pallas_call(
        body, name="hgrn_fwd", grid=(H, nj),
        in_specs=[pl.BlockSpec((R, 4 * HEAD), lambda h, j: (j, h)),
                  pl.BlockSpec((2, HEAD), lambda h, j: (0, h)),
                  pl.BlockSpec((1, HEAD), lambda h, j: (0, 0)),
                  pl.BlockSpec(w2.shape, lambda h, j: (0, 0)),
                  pl.BlockSpec(masks.shape, lambda h, j: (0, 0, 0))],
        out_specs=[pl.BlockSpec((R, HEAD), lambda h, j: (j, h)),
                   pl.BlockSpec((R, HEAD), lambda h, j: (j, h)),
                   pl.BlockSpec((1, GROUP, HEAD, HEAD), lambda h, j: (h, j, 0, 0))],
        out_shape=[jax.ShapeDtypeStruct((T, dm.AW), BF16), jax.ShapeDtypeStruct((T, dm.AW), F32),
                   jax.ShapeDtypeStruct((H, T // C, HEAD, HEAD), F32)],
        scratch_shapes=[pltpu.VMEM((HEAD, HEAD), F32)],
        compiler_params=_cparams(("parallel", "arbitrary")),
    )(main, lb_logits, norm_w, w2, masks)


def _hgrn_bwd(main, lb_logits, norm_w, o_saved, states, dya, dm, fused=None):
    T, H, C = dm.T, dm.H, CHUNK
    R = GROUP * C
    nj = T // R
    nl = len(LEVELS) + 1
    w2, w2t, masks = _hgrn_consts()
    x_ins, x_outs, x_sems, n_in, n_out = _fused_parts(fused)

    def body(x_ref, lg_ref, nw_ref, w2_ref, w2t_ref, masks_ref, o_ref, sp_ref, dya_ref, *rest):
        d_ref, dlb_ref, dnw_ref = rest[n_in:n_in + 3]
        ds_ref = rest[n_in + 3 + n_out]
        j = pl.program_id(1)
        if fused is not None:
            hd = pl.program_id(0)
            start, middle, finish_x = fused.hooks(rest[:n_in], rest[n_in + 3:n_in + 3 + n_out],
                                                  *rest[n_in + 4 + n_out:])
            pl.when((hd == 0) & (j == 0))(start)
            pl.when((hd == H // 2) & (j == 0))(middle)

        @pl.when(j == 0)
        def _():
            ds_ref[...] = jnp.zeros_like(ds_ref)
            dlb_ref[...] = jnp.zeros_like(dlb_ref)
            dnw_ref[...] = jnp.zeros_like(dnw_ref)

        def compute():
            q, z, v, gv = (x_ref[:, s * HEAD:(s + 1) * HEAD] for s in range(4))
            vb = v.astype(BF16)
            lb, om, sg, sgm, f, lf, kk = _hgrn_gates(z, lg_ref)
            args, exps, mts, qf, kf = _hgrn_levels(q, kk, lf, w2_ref)
            qb = [t.astype(BF16) for t in qf]
            kb = [t.astype(BF16) for t in kf]
            p = _hgrn_scores(qf, kf, masks_ref)
            o = o_ref[...]
            nw = nw_ref[...]
            r = _rstd(o)
            n = o * r
            sgg = _sigmoid(gv)
            dya_v = dya_ref[...]
            d_on = dya_v * (gv * sgg)
            dg = dya_v * (n * nw) * (sgg * (1.0 + gv * (1.0 - sgg)))
            dnw_ref[0] += _colsum(d_on * n)
            do = _rms_bwd(d_on * nw, n, r)
            dob = do.astype(BF16)
            dp = _nt(dob, vb)
            dv = _tn(p.astype(BF16), dob)
            dq = jnp.zeros_like(q)
            dkk = jnp.zeros_like(q)
            dargs = [None] * nl
            for l in range(nl):
                dpl = (dp * masks_ref[l]).astype(BF16)
                dql = _nn(dpl, kb[l])
                dkl = _tn(dpl, qb[l])
                if l == 0:
                    dq += dql
                    dkk += dkl
                else:
                    dq += dql * exps[l] * mts[l]
                    dkk += dkl * exps[l] * (1.0 - mts[l])
                    dargs[l] = dql * qf[l] + dkl * kf[l]
            b, eb = args[0], exps[0]
            row = lax.broadcasted_iota(jnp.int32, (C, HEAD), 0)
            dq_i, dkk_i, dv_i, db_i = [None] * GROUP, [None] * GROUP, [None] * GROUP, [None] * GROUP
            for c in reversed(range(GROUP)):
                sl = slice(c * C, (c + 1) * C)
                st = sp_ref[0, c]
                ds = ds_ref[...]
                dsb = ds.astype(BF16)
                blast = b[c * C + C - 1:c * C + C, :]
                ebl = jnp.exp(blast)
                el = jnp.exp(blast - b[sl])
                qe = q[sl] * eb[sl]
                kd = kk[sl] * el
                dqe = _nn(dob[sl], st.astype(BF16))
                dkd = _nn(vb[sl], dsb)
                t = dkd * kd
                dblast = _colsum(t) + _colsum(ds * st) * ebl
                dq_i[c] = dqe * eb[sl]
                dkk_i[c] = dkd * el
                dv_i[c] = _nt(kd.astype(BF16), dsb)
                db_i[c] = dqe * qe - t + jnp.where(row == C - 1, dblast, 0.0)
                ds_ref[...] = ds * ebl + _tn(dob[sl], qe.astype(BF16))
            dq = dq + jnp.concatenate(dq_i, axis=0)
            dkk = dkk + jnp.concatenate(dkk_i, axis=0)
            dv = dv + jnp.concatenate(dv_i, axis=0)
            dlf_c = []
            for c in range(GROUP):
                sl = slice(c * C, (c + 1) * C)
                stack = jnp.concatenate([db_i[c]] + [dargs[l][sl] for l in range(1, nl)], axis=0)
                hi, lo = _split2(stack)
                dlf_c.append(_nn(w2t_ref[...], jnp.concatenate([hi, lo], axis=0)))
            dlf = jnp.concatenate(dlf_c, axis=0)
            dz = dlf * (om * sg * (1.0 - sg) / f) - dkk * (om * sgm * (1.0 - sgm))
            dlb_ref[...] += _colsum(dlf * (1.0 - sg) / f - dkk * sgm)
            d_ref[...] = jnp.concatenate([dq, dz, dv, dg], axis=1).astype(BF16)

        compute()
        if fused is not None:
            pl.when((hd == H - 1) & (j == nj - 1))(finish_x)

    outs = pl.pallas_call(
        body, name="hgrn_bwd", grid=(H, nj),
        in_specs=[pl.BlockSpec((R, 4 * HEAD), lambda h, j: (nj - 1 - j, h)),
                  pl.BlockSpec((2, HEAD), lambda h, j: (0, h)),
                  pl.BlockSpec((1, HEAD), lambda h, j: (0, 0)),
                  pl.BlockSpec(w2.shape, lambda h, j: (0, 0)),
                  pl.BlockSpec(w2t.shape, lambda h, j: (0, 0)),
                  pl.BlockSpec(masks.shape, lambda h, j: (0, 0, 0)),
                  pl.BlockSpec((R, HEAD), lambda h, j: (nj - 1 - j, h)),
                  pl.BlockSpec((1, GROUP, HEAD, HEAD), lambda h, j: (h, nj - 1 - j, 0, 0)),
                  pl.BlockSpec((R, HEAD), lambda h, j: (nj - 1 - j, h))] + [HBM] * n_in,
        out_specs=[pl.BlockSpec((R, 4 * HEAD), lambda h, j: (nj - 1 - j, h)),
                   pl.BlockSpec((1, HEAD), lambda h, j: (0, h)),
                   pl.BlockSpec((1, 1, HEAD), lambda h, j: (h, 0, 0))] + [HBM] * n_out,
        out_shape=[jax.ShapeDtypeStruct((T, 4 * dm.AW), BF16), jax.ShapeDtypeStruct((1, dm.AW), F32),
                   jax.ShapeDtypeStruct((H, 1, HEAD), F32)] + list(x_outs),
        scratch_shapes=[pltpu.VMEM((HEAD, HEAD), F32)] + list(x_sems),
        compiler_params=_cparams(("arbitrary", "arbitrary")),
    )(main, lb_logits, norm_w, w2, w2t, masks, o_saved, states, dya, *x_ins)
    return outs[0], outs[1], outs[2], tuple(outs[3:])


def _log_sigmoid(x):
    return jnp.minimum(x, 0.0) - jnp.log(1.0 + jnp.exp(-jnp.abs(x)))


def _tri(n):
    return jnp.asarray(np.tril(np.ones((n, n), np.float32)), BF16)


def _cum_fwd(rest, bias, dm, tb=256):
    T = dm.T
    tb = min(tb, T)
    cb = 2 * dm.D // HEAD
    tri = _tri(tb)

    def body(x_ref, b_ref, tri_ref, o_ref, carry):
        i = pl.program_id(0)

        @pl.when(i == 0)
        def _():
            carry[...] = jnp.zeros_like(carry)

        lf = _log_sigmoid(x_ref[...] + b_ref[...])
        hi, mid, lo = _split3(lf)
        tr_ = tri_ref[...]
        c = _nn(tr_, hi) + _nn(tr_, mid) + _nn(tr_, lo) + carry[...]
        o_ref[...] = c
        carry[...] = c[tb - 1:tb, :]

    return pl.pallas_call(
        body, name="cum_fwd", grid=(T // tb,),
        in_specs=[_rows(tb, HEAD, cb), _vec(HEAD), pl.BlockSpec((tb, tb), lambda i: (0, 0))],
        out_specs=_rows(tb, HEAD),
        out_shape=jax.ShapeDtypeStruct((T, HEAD), F32),
        scratch_shapes=[pltpu.VMEM((1, HEAD), F32)],
        compiler_params=_cparams(("arbitrary",)),
    )(rest, bias, tri)


def _cum_bwd(dcum, rest, bias, dm, tb=256):
    T = dm.T
    tb = min(tb, T)
    nb = T // tb
    cb = 2 * dm.D // HEAD
    tri = _tri(tb)

    def body(d_ref, x_ref, b_ref, tri_ref, o_ref, db_ref, carry):
        i = pl.program_id(0)

        @pl.when(i == 0)
        def _():
            carry[...] = jnp.zeros_like(carry)
            db_ref[...] = jnp.zeros_like(db_ref)

        hi, mid, lo = _split3(d_ref[...])
        tr_ = tri_ref[...]
        dlf = _tn(tr_, hi) + _tn(tr_, mid) + _tn(tr_, lo) + carry[...]
        carry[...] = dlf[0:1, :]
        dx = dlf * _sigmoid(-(x_ref[...] + b_ref[...]))
        o_ref[...] = dx.astype(BF16)
        db_ref[...] += _colsum(dx)

    return pl.pallas_call(
        body, name="cum_bwd", grid=(nb,),
        in_specs=[pl.BlockSpec((tb, HEAD), lambda i: (nb - 1 - i, 0)),
                  pl.BlockSpec((tb, HEAD), lambda i: (nb - 1 - i, cb)),
                  _vec(HEAD), pl.BlockSpec((tb, tb), lambda i: (0, 0))],
        out_specs=[pl.BlockSpec((tb, HEAD), lambda i: (nb - 1 - i, 0)), _vec(HEAD)],
        out_shape=[jax.ShapeDtypeStruct((T, HEAD), BF16), jax.ShapeDtypeStruct((1, HEAD), F32)],
        scratch_shapes=[pltpu.VMEM((1, HEAD), F32)],
        compiler_params=_cparams(("arbitrary",)),
    )(dcum, rest, bias, tri)


def _fox_pairs(nq, kv_major):
    if kv_major:
        pairs = [(i, j) for j in range(nq) for i in range(j, nq)]
    else:
        pairs = [(i, j) for i in range(nq) for j in range(i + 1)]
    qi = jnp.asarray(np.array([p[0] for p in pairs], np.int32))
    kj = jnp.asarray(np.array([p[1] for p in pairs], np.int32))
    return qi, kj


class Fused(NamedTuple):
    ins: tuple
    outs: tuple
    sems: tuple
    hooks: object


def _fused_parts(fused):
    if fused is None:
        return (), (), (), 0, 0
    return tuple(fused.ins), tuple(fused.outs), tuple(fused.sems), len(fused.ins), len(fused.outs)


FOX_FWD_SPLIT = 4
FOX_BWD_SPLIT = 2


def _fox_fwd(main, cum_row, dm, fused=None, tq=512):
    T, H = dm.T, dm.H
    tq = min(tq, T)
    nq = T // tq
    qi_tab, kj_tab = _fox_pairs(nq, kv_major=False)
    npairs = int(qi_tab.shape[0])
    x_ins, x_outs, x_sems, n_in, n_out = _fused_parts(fused)
    ns = FOX_FWD_SPLIT if tq % (8 * FOX_FWD_SPLIT) == 0 else 1
    rq = tq // ns

    def body(qi_ref, kj_ref, q_ref, k_ref, v_ref, ck_ref, *rest):
        x_in, (o_ref, lse_ref) = rest[:n_in], rest[n_in:n_in + 2]
        x_out = rest[n_in + 2:n_in + 2 + n_out]
        m_ref, l_ref, acc_ref = rest[n_in + 2 + n_out:n_in + 5 + n_out]
        hd = pl.program_id(0)
        t = pl.program_id(1)
        i = qi_ref[t]
        j = kj_ref[t]
        if fused is not None:
            start, middle, finish = fused.hooks(x_in, x_out, *rest[n_in + 5 + n_out:])
            pl.when((hd == 0) & (t == 0))(start)
            pl.when((hd == H // 2) & (t == 0))(middle)

        @pl.when(j == 0)
        def _():
            m_ref[...] = jnp.full_like(m_ref, NEG)
            l_ref[...] = jnp.zeros_like(l_ref)
            acc_ref[...] = jnp.zeros_like(acc_ref)

        def step(on_diagonal):
            kb = k_ref[...].astype(BF16)
            vb = v_ref[...].astype(BF16)
            ck = ck_ref[0]
            q_all, m_all, l_all, acc_all = q_ref[...], m_ref[...], l_ref[...], acc_ref[...]
            m_out, l_out, acc_out = [], [], []
            for g in range(ns):
                rows = slice(g * rq, (g + 1) * rq)
                qs = (q_all[rows] * (HEAD ** -0.5)).astype(BF16)
                s = _nt(qs, kb) - ck
                if on_diagonal:
                    row = g * rq + lax.broadcasted_iota(jnp.int32, s.shape, 0)
                    s = jnp.where(row >= lax.broadcasted_iota(jnp.int32, s.shape, 1), s, NEG)
                m_new = jnp.maximum(m_all[rows], jnp.max(s, axis=-1, keepdims=True))
                a = jnp.exp(m_all[rows] - m_new)
                p = jnp.exp(s - m_new)
                m_out.append(m_new)
                l_out.append(a * l_all[rows] + jnp.sum(p, axis=-1, keepdims=True))
                acc_out.append(a * acc_all[rows] + _nn(p.astype(BF16), vb))
            m_ref[...] = jnp.concatenate(m_out, axis=0)
            l_ref[...] = jnp.concatenate(l_out, axis=0)
            acc_ref[...] = jnp.concatenate(acc_out, axis=0)

        @pl.when(j < i)
        def _():
            step(False)

        @pl.when(j == i)
        def _():
            step(True)
            l = l_ref[...]
            o_ref[...] = acc_ref[...] / l
            lse_ref[0] = m_ref[...] + jnp.log(l)

        if fused is not None:
            pl.when((hd == H - 1) & (t == npairs - 1))(finish)

    outs = pl.pallas_call(
        body, name="fox_fwd",
        grid_spec=pltpu.PrefetchScalarGridSpec(
            num_scalar_prefetch=2, grid=(H, npairs),
            in_specs=[pl.BlockSpec((tq, HEAD), lambda h, t, qi, kj: (qi[t], 4 * H + h)),
                      pl.BlockSpec((tq, HEAD), lambda h, t, qi, kj: (kj[t], 5 * H + h)),
                      pl.BlockSpec((tq, HEAD), lambda h, t, qi, kj: (kj[t], 6 * H + h)),
                      pl.BlockSpec((1, 1, tq), lambda h, t, qi, kj: (h, 0, kj[t]))] + [HBM] * n_in,
            out_specs=[pl.BlockSpec((tq, HEAD), lambda h, t, qi, kj: (qi[t], h)),
                       pl.BlockSpec((1, tq, 1), lambda h, t, qi, kj: (h, qi[t], 0))] + [HBM] * n_out,
            scratch_shapes=[pltpu.VMEM((tq, 1), F32), pltpu.VMEM((tq, 1), F32), pltpu.VMEM((tq, HEAD), F32)]
            + list(x_sems)),
        out_shape=[jax.ShapeDtypeStruct((T, dm.AW), F32), jax.ShapeDtypeStruct((H, T, 1), F32)] + list(x_outs),
        compiler_params=_cparams(("arbitrary", "arbitrary")),
    )(qi_tab, kj_tab, main, main, main, cum_row, *x_ins)
    return outs[0], outs[1], tuple(outs[2:])


def _fox_delta(do, o, dm, tr=256):
    T, H = dm.T, dm.H
    tr = min(tr, T)

    def body(do_ref, o_ref, d_ref):
        d_ref[0] = jnp.sum(do_ref[...] * o_ref[...], axis=-1, keepdims=True)

    return pl.pallas_call(
        body, name="fox_delta", grid=(H, T // tr),
        in_specs=[pl.BlockSpec((tr, HEAD), lambda h, i: (i, h)), pl.BlockSpec((tr, HEAD), lambda h, i: (i, h))],
        out_specs=pl.BlockSpec((1, tr, 1), lambda h, i: (h, i, 0)),
        out_shape=jax.ShapeDtypeStruct((H, T, 1), F32),
        compiler_params=_cparams(("parallel", "parallel")),
    )(do, o)


def _fox_bwd(main, cum_row, lse, delta, do, dm, fused=None, tq=512):
    T, H = dm.T, dm.H
    tq = min(tq, T)
    nq = T // tq
    qi_tab, kj_tab = _fox_pairs(nq, kv_major=True)
    npairs = int(qi_tab.shape[0])
    x_ins, x_outs, x_sems, n_in, n_out = _fused_parts(fused)
    ns = FOX_BWD_SPLIT if tq % (16 * FOX_BWD_SPLIT) == 0 else 1
    rq = tq // ns

    def body(qi_ref, kj_ref, q_ref, k_ref, v_ref, ck_ref, lse_ref, dl_ref, do_ref, *rest):
        x_in = rest[:n_in]
        dq_ref, dk_ref, dv_ref, dc_ref, dr_ref = rest[n_in:n_in + 5]
        x_out = rest[n_in + 5:n_in + 5 + n_out]
        dk_acc, dv_acc, dc_acc = rest[n_in + 5 + n_out:n_in + 8 + n_out]
        hd = pl.program_id(0)
        t = pl.program_id(1)
        i = qi_ref[t]
        kj = kj_ref[t]
        if fused is not None:
            start, middle, finish = fused.hooks(x_in, x_out, *rest[n_in + 8 + n_out:])
            pl.when((hd == 0) & (t == 0))(start)
            pl.when((hd == H // 2) & (t == 0))(middle)

        @pl.when(t == 0)
        def _():
            dq_ref[...] = jnp.zeros_like(dq_ref)
            dr_ref[...] = jnp.zeros_like(dr_ref)

        @pl.when(i == kj)
        def _():
            dk_acc[...] = jnp.zeros_like(dk_acc)
            dv_acc[...] = jnp.zeros_like(dv_acc)
            dc_acc[...] = jnp.zeros_like(dc_acc)

        def step(on_diagonal):
            kb = k_ref[...].astype(BF16)
            vb = v_ref[...].astype(BF16)
            ck = ck_ref[0]
            q_all, do_all, lse_all, dl_all = q_ref[...], do_ref[...], lse_ref[0], dl_ref[0]
            dq_g, dr_g, dv_c, dk_c, dc_c = [], [], None, None, None
            for g in range(ns):
                rows = slice(g * rq, (g + 1) * rq)
                qs = (q_all[rows] * (HEAD ** -0.5)).astype(BF16)
                s = _nt(qs, kb) - ck
                if on_diagonal:
                    row = g * rq + lax.broadcasted_iota(jnp.int32, s.shape, 0)
                    s = jnp.where(row >= lax.broadcasted_iota(jnp.int32, s.shape, 1), s, NEG)
                dob = do_all[rows].astype(BF16)
                p = jnp.exp(s - lse_all[rows])
                ds = p * (_nt(dob, vb) - dl_all[rows])
                dsb = ds.astype(BF16)
                dq_g.append(_nn(dsb, kb) * (HEAD ** -0.5))
                dr_g.append(jnp.sum(ds, axis=-1, keepdims=True))
                dv_g, dk_g, dc_g = _tn(p.astype(BF16), dob), _tn(dsb, qs), _colsum(ds)
                dv_c = dv_g if dv_c is None else dv_c + dv_g
                dk_c = dk_g if dk_c is None else dk_c + dk_g
                dc_c = dc_g if dc_c is None else dc_c + dc_g
            dv_acc[...] += dv_c
            dk_acc[...] += dk_c
            dc_acc[...] -= dc_c
            out_rows = pl.ds(pl.multiple_of(i * tq, tq), tq)
            dq_ref[out_rows, :] += jnp.concatenate(dq_g, axis=0)
            dr_ref[0, out_rows, :] += jnp.concatenate(dr_g, axis=0)

        @pl.when(i == kj)
        def _():
            step(True)

        @pl.when(i > kj)
        def _():
            step(False)

        @pl.when(i == nq - 1)
        def _():
            dk_ref[...] = dk_acc[...].astype(BF16)
            dv_ref[...] = dv_acc[...].astype(BF16)
            dc_ref[0] = dc_acc[...]

        if fused is not None:
            pl.when((hd == H - 1) & (t == npairs - 1))(finish)

    qcol = pl.BlockSpec((1, tq, 1), lambda h, t, qi, kj: (h, qi[t], 0))
    outs = pl.pallas_call(
        body, name="fox_bwd",
        grid_spec=pltpu.PrefetchScalarGridSpec(
            num_scalar_prefetch=2, grid=(H, npairs),
            in_specs=[pl.BlockSpec((tq, HEAD), lambda h, t, qi, kj: (qi[t], 4 * H + h)),
                      pl.BlockSpec((tq, HEAD), lambda h, t, qi, kj: (kj[t], 5 * H + h)),
                      pl.BlockSpec((tq, HEAD), lambda h, t, qi, kj: (kj[t], 6 * H + h)),
                      pl.BlockSpec((1, 1, tq), lambda h, t, qi, kj: (h, 0, kj[t])),
                      qcol, qcol,
                      pl.BlockSpec((tq, HEAD), lambda h, t, qi, kj: (qi[t], h))] + [HBM] * n_in,
            out_specs=[pl.BlockSpec((T, HEAD), lambda h, t, qi, kj: (0, h)),
                       pl.BlockSpec((tq, HEAD), lambda h, t, qi, kj: (kj[t], h)),
                       pl.BlockSpec((tq, HEAD), lambda h, t, qi, kj: (kj[t], h)),
                       pl.BlockSpec((1, 1, tq), lambda h, t, qi, kj: (h, 0, kj[t])),
                       pl.BlockSpec((1, T, 1), lambda h, t, qi, kj: (h, 0, 0))] + [HBM] * n_out,
            scratch_shapes=[pltpu.VMEM((tq, HEAD), F32), pltpu.VMEM((tq, HEAD), F32), pltpu.VMEM((1, tq), F32)]
            + list(x_sems)),
        out_shape=[jax.ShapeDtypeStruct((T, dm.AW), F32), jax.ShapeDtypeStruct((T, dm.AW), BF16),
                   jax.ShapeDtypeStruct((T, dm.AW), BF16), jax.ShapeDtypeStruct((H, 1, T), F32),
                   jax.ShapeDtypeStruct((H, T, 1), F32)] + list(x_outs),
        compiler_params=_cparams(("arbitrary", "arbitrary")),
    )(qi_tab, kj_tab, main, main, main, cum_row, lse, delta, do, *x_ins)
    return outs[:5], tuple(outs[5:])


FOX_QG = 256


def _fox_sT(kb, q_rows, ck, g, on_diagonal):
    qs = (q_rows * (HEAD ** -0.5)).astype(BF16)
    reps = q_rows.shape[0] // HEAD
    sT = _nt(kb, qs) - (jnp.concatenate([ck] * reps, axis=1) if reps > 1 else ck)
    if on_diagonal:
        key = lax.broadcasted_iota(jnp.int32, sT.shape, 0)
        qry = g * q_rows.shape[0] + lax.broadcasted_iota(jnp.int32, sT.shape, 1)
        sT = jnp.where(key <= qry, sT, NEG)
    return sT, qs


def _foxt_fwd(main, ckb, dm, fused=None, tq=512):
    T, H = dm.T, dm.H
    tq = min(tq, T)
    nq = T // tq
    qg = min(FOX_QG, tq)
    ns = tq // qg
    qi_tab, kj_tab = _fox_pairs(nq, kv_major=False)
    npairs = int(qi_tab.shape[0])
    x_ins, x_outs, x_sems, n_in, n_out = _fused_parts(fused)

    def body(qi_ref, kj_ref, q_ref, k_ref, v_ref, ck_ref, *rest):
        x_in, (o_ref, lse_ref) = rest[:n_in], rest[n_in:n_in + 2]
        x_out = rest[n_in + 2:n_in + 2 + n_out]
        m_ref, l_ref, acc_ref = rest[n_in + 2 + n_out:n_in + 5 + n_out]
        hd = pl.program_id(0)
        t = pl.program_id(1)
        i = qi_ref[t]
        j = kj_ref[t]
        if fused is not None:
            start, middle, finish = fused.hooks(x_in, x_out, *rest[n_in + 5 + n_out:])
            pl.when((hd == 0) & (t == 0))(start)
            pl.when((hd == H // 2) & (t == 0))(middle)

        @pl.when(j == 0)
        def _():
            m_ref[...] = jnp.full_like(m_ref, NEG)
            l_ref[...] = jnp.zeros_like(l_ref)
            acc_ref[...] = jnp.zeros_like(acc_ref)

        def step(on_diagonal):
            kb = k_ref[...].astype(BF16)
            vb = v_ref[...].astype(BF16)
            ck = ck_ref[0]
            scores = [_fox_sT(kb, q_ref[g * qg:(g + 1) * qg, :], ck, g, on_diagonal)[0] for g in range(ns)]
            for g in range(ns):
                cols = slice(g * qg, (g + 1) * qg)
                sT = scores[g]
                m_old = m_ref[:, cols]
                m_new = jnp.maximum(m_old, jnp.max(sT, axis=0, keepdims=True))
                a = jnp.exp(m_old - m_new)
                pT = jnp.exp(sT - m_new)
                l_ref[:, cols] = a * l_ref[:, cols] + jnp.sum(pT, axis=0, keepdims=True)
                acc_ref[:, cols] = a * acc_ref[:, cols] + _tn(vb, pT.astype(BF16))
                m_ref[:, cols] = m_new

        @pl.when(j < i)
        def _():
            step(False)

        @pl.when(j == i)
        def _():
            step(True)
            l = l_ref[...]
            o_ref[...] = acc_ref[...] / l
            lse_ref[0] = m_ref[...] + jnp.log(l)

        if fused is not None:
            pl.when((hd == H - 1) & (t == npairs - 1))(finish)

    outs = pl.pallas_call(
        body, name="fox_fwd",
        grid_spec=pltpu.PrefetchScalarGridSpec(
            num_scalar_prefetch=2, grid=(H, npairs),
            in_specs=[pl.BlockSpec((tq, HEAD), lambda h, t, qi, kj: (qi[t], 4 * H + h)),
                      pl.BlockSpec((tq, HEAD), lambda h, t, qi, kj: (kj[t], 5 * H + h)),
                      pl.BlockSpec((tq, HEAD), lambda h, t, qi, kj: (kj[t], 6 * H + h)),
                      pl.BlockSpec((1, tq, HEAD), lambda h, t, qi, kj: (h, kj[t], 0))] + [HBM] * n_in,
            out_specs=[pl.BlockSpec((HEAD, tq), lambda h, t, qi, kj: (h, qi[t])),
                       pl.BlockSpec((1, 1, tq), lambda h, t, qi, kj: (h, 0, qi[t]))] + [HBM] * n_out,
            scratch_shapes=[pltpu.VMEM((1, tq), F32), pltpu.VMEM((1, tq), F32), pltpu.VMEM((HEAD, tq), F32)]
            + list(x_sems)),
        out_shape=[jax.ShapeDtypeStruct((dm.AW, T), F32), jax.ShapeDtypeStruct((H, 1, T), F32)] + list(x_outs),
        compiler_params=_cparams(("arbitrary", "arbitrary")),
    )(qi_tab, kj_tab, main, main, main, ckb, *x_ins)
    return outs[0], outs[1], tuple(outs[2:])


def _foxt_delta(doT, oT, dm, tc=1024):
    T, H = dm.T, dm.H
    tc = min(tc, T)

    def body(do_ref, o_ref, d_ref):
        d_ref[0] = jnp.sum(do_ref[...] * o_ref[...], axis=0, keepdims=True)

    blk = pl.BlockSpec((HEAD, tc), lambda h, i: (h, i))
    return pl.pallas_call(
        body, name="fox_delta", grid=(H, T // tc),
        in_specs=[blk, blk], out_specs=pl.BlockSpec((1, 1, tc), lambda h, i: (h, 0, i)),
        out_shape=jax.ShapeDtypeStruct((H, 1, T), F32),
        compiler_params=_cparams(("parallel", "parallel")),
    )(doT, oT)


def _foxt_bwd(main, ckb, lse, delta, doT, dm, fused=None, tq=512):
    T, H = dm.T, dm.H
    tq = min(tq, T)
    nq = T // tq
    qg = min(FOX_QG, tq)
    ns = tq // qg
    qi_tab, kj_tab = _fox_pairs(nq, kv_major=True)
    npairs = int(qi_tab.shape[0])
    x_ins, x_outs, x_sems, n_in, n_out = _fused_parts(fused)

    def body(qi_ref, kj_ref, q_ref, k_ref, v_ref, ck_ref, lse_ref, dl_ref, do_ref, *rest):
        x_in = rest[:n_in]
        dq_ref, dk_ref, dv_ref, dr_ref, dc_ref = rest[n_in:n_in + 5]
        x_out = rest[n_in + 5:n_in + 5 + n_out]
        dq_acc, dk_acc, dv_acc = rest[n_in + 5 + n_out:n_in + 8 + n_out]
        hd = pl.program_id(0)
        t = pl.program_id(1)
        i = qi_ref[t]
        kj = kj_ref[t]
        if fused is not None:
            start, middle, finish = fused.hooks(x_in, x_out, *rest[n_in + 8 + n_out:])
            pl.when((hd == 0) & (t == 0))(start)
            pl.when((hd == H // 2) & (t == 0))(middle)

        @pl.when(t == 0)
        def _():
            dq_acc[...] = jnp.zeros_like(dq_acc)
            dr_ref[...] = jnp.zeros_like(dr_ref)

        @pl.when(i == kj)
        def _():
            dk_acc[...] = jnp.zeros_like(dk_acc)
            dv_acc[...] = jnp.zeros_like(dv_acc)

        def step(on_diagonal):
            kb = k_ref[...].astype(BF16)
            vb = v_ref[...].astype(BF16)
            ck = ck_ref[0]
            ones = jnp.ones((qg, HEAD), BF16)
            dq_g, dr_g = [], []
            ahead = []
            for g in range(ns):
                cols = slice(g * qg, (g + 1) * qg)
                sT, qs = _fox_sT(kb, q_ref[cols, :], ck, g, on_diagonal)
                dob = do_ref[:, cols].astype(BF16)
                ahead.append((sT, qs, dob, _nn(vb, dob)))
            for g in range(ns):
                cols = slice(g * qg, (g + 1) * qg)
                sT, qs, dob, dpT = ahead[g]
                pT = jnp.exp(sT - lse_ref[0, :, cols])
                dsT = pT * (dpT - dl_ref[0, :, cols])
                dsb = dsT.astype(BF16)
                dv_acc[...] += _nt(pT.astype(BF16), dob)
                dk_acc[...] += _nn(dsb, jnp.concatenate([qs, ones], axis=1))
                dq_g.append(_tn(kb, dsb) * (HEAD ** -0.5))
                dr_g.append(_nn(jnp.ones((8, tq), BF16), dsb)[0:1])
            dq_acc[i] += jnp.concatenate(dq_g, axis=1)
            dr_ref[0, pl.ds(i, 1), :] += jnp.concatenate(dr_g, axis=1)

        @pl.when(i == kj)
        def _():
            step(True)

        @pl.when(i > kj)
        def _():
            step(False)

        @pl.when(i == nq - 1)
        def _():
            acc = dk_acc[...]
            dk_ref[...] = acc[:, :HEAD].astype(BF16)
            dc_ref[0] = acc[:, HEAD:HEAD + 1]
            dv_ref[...] = dv_acc[...].astype(BF16)

        @pl.when(t == npairs - 1)
        def _():
            for b in range(nq):
                dq_ref[b * tq:(b + 1) * tq, :] = dq_acc[b].T.astype(BF16)

        if fused is not None:
            pl.when((hd == H - 1) & (t == npairs - 1))(finish)

    qrow = pl.BlockSpec((1, 1, tq), lambda h, t, qi, kj: (h, 0, qi[t]))
    outs = pl.pallas_call(
        body, name="fox_bwd",
        grid_spec=pltpu.PrefetchScalarGridSpec(
            num_scalar_prefetch=2, grid=(H, npairs),
            in_specs=[pl.BlockSpec((tq, HEAD), lambda h, t, qi, kj: (qi[t], 4 * H + h)),
                      pl.BlockSpec((tq, HEAD), lambda h, t, qi, kj: (kj[t], 5 * H + h)),
                      pl.BlockSpec((tq, HEAD), lambda h, t, qi, kj: (kj[t], 6 * H + h)),
                      pl.BlockSpec((1, tq, HEAD), lambda h, t, qi, kj: (h, kj[t], 0)),
                      qrow, qrow,
                      pl.BlockSpec((HEAD, tq), lambda h, t, qi, kj: (h, qi[t]))] + [HBM] * n_in,
            out_specs=[pl.BlockSpec((T, HEAD), lambda h, t, qi, kj: (0, h)),
                       pl.BlockSpec((tq, HEAD), lambda h, t, qi, kj: (kj[t], h)),
                       pl.BlockSpec((tq, HEAD), lambda h, t, qi, kj: (kj[t], h)),
                       pl.BlockSpec((1, nq, tq), lambda h, t, qi, kj: (h, 0, 0)),
                       pl.BlockSpec((1, tq, 1), lambda h, t, qi, kj: (h, kj[t], 0))] + [HBM] * n_out,
            scratch_shapes=[pltpu.VMEM((nq, HEAD, tq), F32), pltpu.VMEM((tq, 2 * HEAD), F32),
                            pltpu.VMEM((tq, HEAD), F32)] + list(x_sems)),
        out_shape=[jax.ShapeDtypeStruct((T, dm.AW), BF16), jax.ShapeDtypeStruct((T, dm.AW), BF16),
                   jax.ShapeDtypeStruct((T, dm.AW), BF16), jax.ShapeDtypeStruct((H, nq, tq), F32),
                   jax.ShapeDtypeStruct((H, T, 1), F32)] + list(x_outs),
        compiler_params=_cparams(("arbitrary", "arbitrary")),
    )(qi_tab, kj_tab, main, main, main, ckb, lse, delta, doT, *x_ins)
    return outs[:5], tuple(outs[5:])


def _local_step(dm, x, tgt, w_main, w_rest, later_weights, bias_p, lb_logits, norm_w, g1, g2, g3, g4,
                reduce_hooks=None):
    T, D, H = dm.T, dm.D, dm.H
    h1, r1 = _rms_fwd(x, g1)
    main = _mm(h1, w_main, "nn", F32, "proj_main")
    rest = _mm(h1, w_rest, "nn", F32, "proj_rest", tn=1408)
    ya, o_a, states = _hgrn_fwd(main, lb_logits, norm_w, dm)
    cum = _cum_fwd(rest, bias_p, dm)
    ckb = jnp.broadcast_to(cum[:, :H].T[:, :, None], (H, T, HEAD))
    if isinstance(later_weights, Fused):
        oT_b, lse, later_weights = _foxt_fwd(main, ckb, dm, fused=later_weights)
    else:
        oT_b, lse, _ = _foxt_fwd(main, ckb, dm)
    w_up_a, w_up_b, w_o, w_ffn_in, w_ffn_down = later_weights
    y_a = _mm(ya, w_up_a, "nn", F32, "up_a")
    y_b = _mm(oT_b, w_up_b, "tn", F32, "up_b")
    merged = _merge_fwd(rest, y_a, y_b)
    u = _mm(merged, w_o, "nn", F32, "w_o")
    x1, r2, h3, r3 = _post_pre(x, u, g2, g3)
    gate, up, act = _ffn_in_swiglu(h3, w_ffn_in)
    w = _mm(act, w_ffn_down, "nn", F32, "ffn_down")
    loss, dy, dw, dg4 = _loss_bwd(x1, w, g4, tgt)
    dgate, dup = _d_act_swiglu(dw, w_ffn_down, gate, up)
    d_ffn_down = _mm(act, dw, "tn", F32, "dw_ffn_down", tm=1408)
    dh3 = _mm(dgate, w_ffn_in[:, :dm.FF], "nt", F32, "d_h3_gate")
    dh3 = _mm(dup, w_ffn_in[:, dm.FF:], "nt", F32, "d_h3_up", add=dh3)
    d_ffn_in = jnp.concatenate([_mm(h3, dgate, "tn", F32, "dw_ffn_gate", tn=1408),
                                _mm(h3, dup, "tn", F32, "dw_ffn_up", tn=1408)], axis=1)
    dx1, du, dg3, dg2 = _rms_bwd2(dy, dh3, x1, r3, g3, u, r2, g2)
    dmerged = _mm(du, w_o, "nt", F32, "d_merged")
    d_w_o = _mm(merged, du, "tn", F32, "dw_o")
    dy_a, dy_b, dgates = _merge_bwd(dmerged, rest, y_a, y_b)
    dya = _mm(dy_a, w_up_a, "nt", F32, "d_ya")
    d_up_a = _mm(ya, dy_a, "tn", F32, "dw_up_a")
    doT = _mm(w_up_b, dy_b, "nt", F32, "d_ob")
    d_up_b = _mm(oT_b, dy_b, "nn", F32, "dw_up_b")
    later_grads = [d_up_a, d_up_b, d_w_o, d_ffn_in, d_ffn_down]
    d_a, dlb, dnw_h, got = _hgrn_bwd(main, lb_logits, norm_w, o_a, states, dya, dm,
                                     fused=reduce_hooks.swap_later(later_grads) if reduce_hooks else None)
    (dq_b, dk_b, dv_b, d_over_keys, d_over_queries), later_from_chips = _foxt_bwd(
        main, ckb, lse, _foxt_delta(doT, oT_b, dm), doT, dm,
        fused=reduce_hooks.scatter_later(got) if reduce_hooks else None)
    dcum = jnp.pad((d_over_keys.reshape(H, T) - d_over_queries.reshape(H, T)).T, ((0, 0), (0, HEAD - H)))
    dbf, dbias = _cum_bwd(dcum, rest, bias_p, dm)
    dmain = jnp.concatenate([d_a, dq_b, dk_b, dv_b], axis=1)
    drest = jnp.concatenate([dgates, dbf], axis=1)
    d_main = _mm(h1, dmain, "tn", F32, "dw_main")
    d_rest = _mm(h1, drest, "tn", F32, "dw_rest", tn=1408)
    if reduce_hooks:
        dh1, w_in_from_chips = _mm(dmain, w_main, "nt", F32, "d_h1_main",
                                   fused=reduce_hooks.scatter_w_in(d_main, d_rest))
    else:
        dh1, w_in_from_chips = _mm(dmain, w_main, "nt", F32, "d_h1_main"), ()
    dh1 = _mm(drest, w_rest, "nt", F32, "d_h1_rest", add=dh1)
    dx, dg1 = _rms_bwd1(dx1, dh1, x, r1, g1)
    big = dict(main=d_main, rest=d_rest, up_a=d_up_a, up_b=d_up_b, w_o=d_w_o, ffn_in=d_ffn_in, ffn_down=d_ffn_down)
    small = dict(loss=loss, bias=dbias, norm_w=jnp.sum(dnw_h, axis=0), lb=dlb, g1=dg1, g2=dg2, g3=dg3, g4=dg4)
    return dx, big, small, tuple(w_in_from_chips) + tuple(later_from_chips)


HBM = pl.BlockSpec(memory_space=pltpu.HBM)


def _place():
    x, y, c = lax.axis_index("x"), lax.axis_index("y"), lax.axis_index("c")
    chips = [(1 - x, y), (x, 1 - y), (1 - x, 1 - y)]
    return x, y, c, chips


class Sharded(NamedTuple):
    kind: str
    r: int
    c: int

    @property
    def full(self):
        return (self.r, 4 * self.c) if self.kind == "col" else (4 * self.r, self.c)

    @property
    def half(self):
        return (self.r // 2, self.c) if self.kind == "col" else (self.r, self.c // 2)

    @property
    def half_of_full(self):
        return (self.r // 2, 4 * self.c) if self.kind == "col" else (4 * self.r, self.c // 2)

    def shard_window(self, ref, s):
        if self.kind == "col":
            return ref.at[:, pl.ds(pl.multiple_of(s * self.c, 128), self.c)]
        return ref.at[pl.ds(pl.multiple_of(s * self.r, 16), self.r), :]

    def half_window(self, ref, s, h):
        if self.kind == "col":
            return ref.at[pl.ds(pl.multiple_of(h * (self.r // 2), 16), self.r // 2),
                          pl.ds(pl.multiple_of(s * self.c, 128), self.c)]
        return ref.at[pl.ds(pl.multiple_of(s * self.r, 16), self.r),
                      pl.ds(pl.multiple_of(h * (self.c // 2), 128), self.c // 2)]

    def half_of(self, ref, h):
        if self.kind == "col":
            n = ref.shape[0] // 2
            return ref.at[pl.ds(pl.multiple_of(h * n, 16), n), :]
        n = ref.shape[1] // 2
        return ref.at[:, pl.ds(pl.multiple_of(h * n, 128), n)]

    def window_of_half(self, ref, s):
        if self.kind == "col":
            return ref.at[:, pl.ds(pl.multiple_of(s * self.c, 128), self.c)]
        return ref.at[pl.ds(pl.multiple_of(s * self.r, 16), self.r), :]


def _gather_hooks(specs):
    n = len(specs)

    def hooks(w_refs, f_refs, send_sems, recv_sems):
        x, y, c, chips = _place()
        q = 2 * x + y
        sibling = (x, y, 1 - c)

        def copy(t, k, src, dst, to):
            return pltpu.make_async_remote_copy(
                src_ref=src, dst_ref=dst, send_sem=send_sems.at[7 * t + k], recv_sem=recv_sems.at[7 * t + k],
                device_id=to, device_id_type=MESH)

        def over_ici(t, j, chip_from, to):
            src = specs[t].half_of(w_refs[t], c)
            return copy(t, j, src, specs[t].half_window(f_refs[t], 2 * chip_from[0] + chip_from[1], c), to)

        def passed_on(t, j, chip_from, half):
            win = specs[t].half_window(f_refs[t], 2 * chip_from[0] + chip_from[1], half)
            return copy(t, 3 + j, win, win, sibling)

        def own_shard(t):
            return copy(t, 6, w_refs[t], specs[t].shard_window(f_refs[t], q), sibling)

        def start():
            for t in range(n):
                own_shard(t).start()
                for j, chip in enumerate(chips):
                    over_ici(t, j, (x, y), (*chip, c)).start()

        def middle():
            for t in range(n):
                for j, chip in enumerate(chips):
                    over_ici(t, j, chip, sibling).wait_recv()
                    passed_on(t, j, chip, c).start()

        def finish():
            for t in range(n):
                own_shard(t).wait_recv()
                for j, chip in enumerate(chips):
                    passed_on(t, j, chip, 1 - c).wait_recv()
            for t in range(n):
                own_shard(t).wait_send()
                for j, chip in enumerate(chips):
                    over_ici(t, j, (x, y), (*chip, c)).wait_send()
                    passed_on(t, j, chip, c).wait_send()

        return start, middle, finish

    sems = (pltpu.SemaphoreType.DMA((7 * n,)), pltpu.SemaphoreType.DMA((7 * n,)))
    return hooks, sems


def _gather_fused(shards, specs):
    hooks, sems = _gather_hooks(specs)
    outs = tuple(jax.ShapeDtypeStruct(sp.full, w.dtype) for sp, w in zip(specs, shards))
    return Fused(ins=tuple(shards), outs=outs, sems=sems, hooks=hooks)


def _run_fused(fused, name):
    n_in, n_out = len(fused.ins), len(fused.outs)

    def body(*refs):
        start, middle, finish = fused.hooks(refs[:n_in], refs[n_in:n_in + n_out], *refs[n_in + n_out:])
        start()
        middle()
        finish()

    return pl.pallas_call(
        body, name=name, in_specs=[HBM] * n_in, out_specs=[HBM] * n_out,
        out_shape=list(fused.outs), scratch_shapes=list(fused.sems),
    )(*fused.ins)


def _swap_fused(fulls, specs):
    n = len(fulls)

    def hooks(g_refs, o_refs, send_sems, recv_sems):
        x, y, c, _ = _place()

        def copies():
            return [pltpu.make_async_remote_copy(
                src_ref=specs[t].half_of(g_refs[t], 1 - c), dst_ref=o_refs[t],
                send_sem=send_sems.at[t], recv_sem=recv_sems.at[t],
                device_id=(x, y, 1 - c), device_id_type=MESH) for t in range(n)]

        def start():
            for cp in copies():
                cp.start()

        def finish():
            for cp in copies():
                cp.wait_recv()
            for cp in copies():
                cp.wait_send()

        return start, lambda: None, finish

    outs = tuple(jax.ShapeDtypeStruct(sp.half_of_full, g.dtype) for sp, g in zip(specs, fulls))
    sems = (pltpu.SemaphoreType.DMA((n,)), pltpu.SemaphoreType.DMA((n,)))
    return Fused(ins=tuple(fulls), outs=outs, sems=sems, hooks=hooks)


def _rtile(n, target):
    t = min(n, (target // 16) * 16)
    while n % t:
        t -= 16
    return t


def _add_sibling(full, got, sp, cq_idx, name):
    hr, hc = sp.half
    tr = _rtile(hr, 256)
    nrt = hr // tr

    def body(cq_ref, g_ref, r_ref, ob_ref, of_ref):
        s = pl.program_id(1)
        v = g_ref[...] + r_ref[...]
        ob_ref[...] = v.astype(BF16)

        @pl.when(s == cq_ref[1])
        def _():
            of_ref[...] = v

    if sp.kind == "col":
        g_spec = pl.BlockSpec((tr, hc), lambda i, s, cq: (cq[0] * nrt + i, s))
        r_spec = pl.BlockSpec((tr, hc), lambda i, s, cq: (i, s))
    else:
        g_spec = pl.BlockSpec((tr, hc), lambda i, s, cq: (s * nrt + i, cq[0]))
        r_spec = pl.BlockSpec((tr, hc), lambda i, s, cq: (s * nrt + i, 0))
    return pl.pallas_call(
        body, name=name,
        grid_spec=pltpu.PrefetchScalarGridSpec(
            num_scalar_prefetch=1, grid=(nrt, 4),
            in_specs=[g_spec, r_spec],
            out_specs=[r_spec, pl.BlockSpec((tr, hc), lambda i, s, cq: (i, 0))]),
        out_shape=[jax.ShapeDtypeStruct(sp.half_of_full, BF16), jax.ShapeDtypeStruct(sp.half, F32)],
        compiler_params=_cparams(("parallel", "arbitrary")),
    )(cq_idx, full, got)


def _scatter_fused(sums, specs):
    n = len(sums)

    def hooks(a_refs, o_refs, send_sems, recv_sems):
        x, y, c, chips = _place()

        def copies():
            return [pltpu.make_async_remote_copy(
                src_ref=specs[t].window_of_half(a_refs[t], 2 * chip[0] + chip[1]), dst_ref=o_refs[t].at[j],
                send_sem=send_sems.at[3 * t + j], recv_sem=recv_sems.at[3 * t + j],
                device_id=(*chip, c), device_id_type=MESH) for t in range(n) for j, chip in enumerate(chips)]

        def start():
            for cp in copies():
                cp.start()

        def finish():
            for cp in copies():
                cp.wait_recv()
            for cp in copies():
                cp.wait_send()

        return start, lambda: None, finish

    outs = tuple(jax.ShapeDtypeStruct((3,) + sp.half, a.dtype) for sp, a in zip(specs, sums))
    sems = (pltpu.SemaphoreType.DMA((3 * n,)), pltpu.SemaphoreType.DMA((3 * n,)))
    return Fused(ins=tuple(sums), outs=outs, sems=sems, hooks=hooks)


def _add_chips(own, got, sp, cq_idx, name):
    hr, hc = sp.half
    tr = _rtile(hr, 256)
    nrt = hr // tr

    def body(cq_ref, a_ref, r_ref, o_ref):
        o_ref[...] = ((a_ref[...] + r_ref[0].astype(F32)) + r_ref[1].astype(F32)) + r_ref[2].astype(F32)

    if sp.kind == "col":
        o_spec = pl.BlockSpec((tr, hc), lambda i, cq: (cq[0] * nrt + i, 0))
    else:
        o_spec = pl.BlockSpec((tr, hc), lambda i, cq: (i, cq[0]))
    return pl.pallas_call(
        body, name=name,
        grid_spec=pltpu.PrefetchScalarGridSpec(
            num_scalar_prefetch=1, grid=(nrt,),
            in_specs=[pl.BlockSpec((tr, hc), lambda i, cq: (i, 0)), pl.BlockSpec((3, tr, hc), lambda i, cq: (0, i, 0))],
            out_specs=o_spec),
        out_shape=jax.ShapeDtypeStruct((sp.r, sp.c), F32),
        compiler_params=_cparams(("parallel",)),
    )(cq_idx, own, got)


def _share_halves(shards, specs):
    n = len(shards)

    def body(*refs):
        o_refs = refs[n:2 * n]
        send_sems, recv_sems = refs[2 * n:]
        x, y, c, _ = _place()

        def copy(t, half):
            win = specs[t].half_of(o_refs[t], half)
            return pltpu.make_async_remote_copy(
                src_ref=win, dst_ref=win, send_sem=send_sems.at[t], recv_sem=recv_sems.at[t],
                device_id=(x, y, 1 - c), device_id_type=MESH)

        for t in range(n):
            copy(t, c).start()
        for t in range(n):
            copy(t, 1 - c).wait_recv()
        for t in range(n):
            copy(t, c).wait_send()

    return pl.pallas_call(
        body, name="share_halves",
        in_specs=[HBM] * n, out_specs=[HBM] * n,
        out_shape=[jax.ShapeDtypeStruct((sp.r, sp.c), F32) for sp in specs],
        input_output_aliases={t: t for t in range(n)},
        scratch_shapes=[pltpu.SemaphoreType.DMA((n,)), pltpu.SemaphoreType.DMA((n,))],
    )(*shards)


def _sum_small(vec):
    rows, w = vec.shape

    def body(v_ref, o_ref, buf, send_sems, recv_sems):
        x, y, c, _ = _place()
        me = 4 * x + 2 * y + c
        buf[me] = v_ref[...]
        cps = []
        for k in range(1, 8):
            to = (x ^ (k >> 2), y ^ ((k >> 1) & 1), c ^ (k & 1))
            cps.append(pltpu.make_async_remote_copy(
                src_ref=v_ref, dst_ref=buf.at[me], send_sem=send_sems.at[k - 1], recv_sem=recv_sems.at[k - 1],
                device_id=to, device_id_type=MESH))
        for cp in cps:
            cp.start()
        for k in range(1, 8):
            pltpu.make_async_remote_copy(
                src_ref=v_ref, dst_ref=buf.at[me ^ k], send_sem=send_sems.at[k - 1], recv_sem=recv_sems.at[k - 1],
                device_id=(x, y, c), device_id_type=MESH).wait_recv()
        for cp in cps:
            cp.wait_send()
        total = buf[0]
        for d in range(1, 8):
            total = total + buf[d]
        o_ref[...] = total

    return pl.pallas_call(
        body, name="sum_small",
        in_specs=[pl.BlockSpec(memory_space=pltpu.VMEM)], out_specs=pl.BlockSpec(memory_space=pltpu.VMEM),
        out_shape=jax.ShapeDtypeStruct((rows, w), F32),
        scratch_shapes=[pltpu.VMEM((8, rows, w), F32), pltpu.SemaphoreType.DMA((7,)), pltpu.SemaphoreType.DMA((7,))],
    )(vec)


def _adam_math(w, g, m, v):
    m = ADAM_B1 * m + (1.0 - ADAM_B1) * g
    v = ADAM_B2 * v + (1.0 - ADAM_B2) * (g * g)
    m_hat = m / (1.0 - ADAM_B1 ** ADAM_STEP)
    v_hat = v / (1.0 - ADAM_B2 ** ADAM_STEP)
    delta = -ADAM_LR * (m_hat / (jnp.sqrt(v_hat) + ADAM_EPS) + ADAM_WD * w)
    return delta, m, v


def _adamw(w, g, m, v, name, tr=128):
    R, Cn = w.shape
    tr = min(tr, R)
    assert R % tr == 0

    def body(w_ref, g_ref, m_ref, v_ref, go_ref, d_ref, mo_ref, vo_ref):
        gv = g_ref[...]
        d, mn, vn = _adam_math(w_ref[...], gv, m_ref[...], v_ref[...])
        go_ref[...] = gv
        d_ref[...] = d
        mo_ref[...] = mn
        vo_ref[...] = vn

    blk = pl.BlockSpec((tr, Cn), lambda i: (i, 0))
    sds = jax.ShapeDtypeStruct((R, Cn), F32)
    return pl.pallas_call(
        body, name=name, grid=(R // tr,),
        in_specs=[blk] * 4, out_specs=[blk] * 4, out_shape=[sds] * 4,
        compiler_params=_cparams(("parallel",)),
    )(w, g, m, v)


ROW_LOSS, ROW_BIAS, ROW_NORM, ROW_LB0, ROW_G1, ROW_G2, ROW_G3, ROW_G4, ROW_LB1 = range(9)
SMALL_ROWS = 16


def _small_update(gsum, wp, mp, vp):
    _, w = gsum.shape

    def body(g_ref, w_ref, m_ref, v_ref, go_ref, d_ref, mo_ref, vo_ref):
        wv = w_ref[...]
        l0 = wv[ROW_LB0:ROW_LB0 + 1, :]
        l1 = wv[ROW_LB1:ROW_LB1 + 1, :]
        mx = jnp.maximum(l0, l1)
        e0 = jnp.exp(l0 - mx)
        e1 = jnp.exp(l1 - mx)
        p0 = e0 / (e0 + e1)
        gs = g_ref[...]
        dl0 = gs[ROW_LB0:ROW_LB0 + 1, :] * p0 * (1.0 - p0)
        row8 = lax.broadcasted_iota(jnp.int32, gs.shape, 0)
        top = jnp.where(row8 == ROW_LB0, dl0, jnp.where(row8 == ROW_LOSS, 0.0, gs))
        bot = jnp.where(row8 == ROW_LB1 - 8, -dl0, 0.0)
        g16 = jnp.concatenate([top, bot], axis=0)
        d, mn, vn = _adam_math(wv, g16, m_ref[...], v_ref[...])
        go_ref[...] = g16
        d_ref[...] = d
        mo_ref[...] = mn
        vo_ref[...] = vn

    sds = jax.ShapeDtypeStruct((SMALL_ROWS, w), F32)
    full = pl.BlockSpec(memory_space=pltpu.VMEM)
    return pl.pallas_call(
        body, name="small_update", in_specs=[full] * 4, out_specs=[full] * 4, out_shape=[sds] * 4,
    )(gsum, wp, mp, vp)


def _w_in_layout(dm):
    cs = dm.NIN // 4
    place = [((cs * q) // 128, (cs * q) % 128) for q in range(4)]
    cp = -(-(cs + max(sh for _, sh in place)) // 128) * 128
    return cs, cp, place


def _zeros_cols(rows, n, dtype):
    return jnp.zeros((rows, n), dtype)


def _a_part_tiles(v, H, to_head_major):
    tile = lambda k: v[:, k * HEAD:(k + 1) * HEAD]
    if to_head_major:
        tiles = [tile(s * H + h) for h in range(H) for s in range(4)]
    else:
        tiles = [tile(h * 4 + s) for s in range(4) for h in range(H)]
    return jnp.concatenate(tiles + [v[:, 4 * H * HEAD:]], axis=1)


def _unshuffle_w_in(wp, dm, tr=128):
    D, H = dm.D, dm.H
    cs, cp, place = _w_in_layout(dm)
    nm, nrest = dm.NMAIN, dm.NREST
    ng = nm + nrest
    tr = min(tr, D)

    def body(x_ref, main_ref, rest_ref):
        g = None
        for q, (t0, sh) in enumerate(place):
            xq = x_ref[:, q * cp:(q + 1) * cp]
            yq = pltpu.roll(xq, sh, axis=1) if sh else xq
            width = min(cp, ng - t0 * 128)
            parts = [_zeros_cols(tr, t0 * 128, wp.dtype)] if t0 else []
            parts.append(yq[:, :width])
            if ng - t0 * 128 - width:
                parts.append(_zeros_cols(tr, ng - t0 * 128 - width, wp.dtype))
            placed = jnp.concatenate(parts, axis=1)
            g = placed if g is None else g + placed
        main_ref[...] = _a_part_tiles(g[:, :nm], H, to_head_major=True)
        tail = g[:, nm:]
        gates = pltpu.roll(tail, nrest - H, axis=1)[:, :2 * D]
        lane = lax.broadcasted_iota(jnp.int32, (tr, HEAD), 1)
        rest_ref[...] = jnp.concatenate([gates, jnp.where(lane < H, tail[:, :HEAD], 0)], axis=1)

    return pl.pallas_call(
        body, name="unshuffle_w_in", grid=(D // tr,),
        in_specs=[pl.BlockSpec((tr, 4 * cp), lambda i: (i, 0))],
        out_specs=[pl.BlockSpec((tr, nm), lambda i: (i, 0)), pl.BlockSpec((tr, nrest), lambda i: (i, 0))],
        out_shape=[jax.ShapeDtypeStruct((D, nm), wp.dtype), jax.ShapeDtypeStruct((D, nrest), wp.dtype)],
        compiler_params=_cparams(("parallel",)),
    )(wp)


def _shuffle_w_in(d_main, d_rest, dm, tr=64):
    D, H = dm.D, dm.H
    cs, cp, place = _w_in_layout(dm)
    nm, nrest = dm.NMAIN, dm.NREST
    ng = nm + nrest
    tr = min(tr, D)

    def body(m_ref, r_ref, o_ref):
        rv = r_ref[...]
        lane = lax.broadcasted_iota(jnp.int32, (tr, HEAD), 1)
        tail = pltpu.roll(jnp.concatenate([rv[:, :2 * D], _zeros_cols(tr, HEAD, F32)], axis=1), H, axis=1)
        head = jnp.where(lane < H, rv[:, 2 * D:], 0.0) + tail[:, :HEAD]
        main_std = _a_part_tiles(m_ref[...], H, to_head_major=False)
        g = jnp.concatenate([main_std, head, tail[:, HEAD:]], axis=1)
        lanes = lax.broadcasted_iota(jnp.int32, (tr, cp), 1)
        outs = []
        for q, (t0, sh) in enumerate(place):
            width = min(cp, ng - t0 * 128)
            win = g[:, t0 * 128:t0 * 128 + width]
            if width < cp:
                win = jnp.concatenate([win, _zeros_cols(tr, cp - width, F32)], axis=1)
            xq = pltpu.roll(win, cp - sh, axis=1) if sh else win
            outs.append(jnp.where(lanes < cs, xq, 0.0))
        o_ref[...] = jnp.concatenate(outs, axis=1)

    return pl.pallas_call(
        body, name="shuffle_w_in", grid=(D // tr,),
        in_specs=[pl.BlockSpec((tr, nm), lambda i: (i, 0)), pl.BlockSpec((tr, nrest), lambda i: (i, 0))],
        out_specs=pl.BlockSpec((tr, 4 * cp), lambda i: (i, 0)),
        out_shape=jax.ShapeDtypeStruct((D, 4 * cp), F32),
        compiler_params=_cparams(("parallel",)),
    )(d_main, d_rest)


def _pack_small(dm, bias, norm_w, lb_logits, g1, g2, g3, g4):
    D = dm.D
    row = lambda v: jnp.pad(v.reshape(1, -1), ((0, 0), (0, D - v.size)))
    rows = [jnp.zeros((1, D), F32), row(bias), row(norm_w), row(lb_logits[0]), row(g1), row(g2), row(g3), row(g4),
            row(lb_logits[1]), jnp.zeros((SMALL_ROWS - 9, D), F32)]
    return jnp.concatenate(rows, axis=0)


def _unpack_small(p, dm):
    H, AW = dm.H, dm.AW
    return (p[ROW_BIAS:ROW_BIAS + 1, :H], jnp.concatenate([p[ROW_LB0:ROW_LB0 + 1, :AW], p[ROW_LB1:ROW_LB1 + 1, :AW]], axis=0),
            p[ROW_NORM:ROW_NORM + 1, :HEAD], p[ROW_G1:ROW_G1 + 1], p[ROW_G2:ROW_G2 + 1], p[ROW_G3:ROW_G3 + 1],
            p[ROW_G4:ROW_G4 + 1])


def _step(dm, x, w_in, b_fox_f, hgrn_lb_logits, hgrn_norm_w, w_up_a, w_up_b, w_o, norm_mix_pre, norm_mix_post,
          norm_ffn_pre, norm_ffn_post, w_ffn_in, w_ffn_down, loss_target, moments_m, moments_v):
    xi, yi, ci = lax.axis_index("x"), lax.axis_index("y"), lax.axis_index("c")
    cq_idx = jnp.stack([ci, 2 * xi + yi]).astype(jnp.int32)
    D, AW, FF = dm.D, dm.AW, dm.FF
    cs, cp, _ = _w_in_layout(dm)
    big_names = ["w_in", "w_up_a", "w_up_b", "w_o", "w_ffn_in", "w_ffn_down"]
    specs = [Sharded("col", D, cp), Sharded("col", AW, D // 4), Sharded("col", AW, D // 4),
             Sharded("row", D // 4, D), Sharded("col", D, 2 * FF // 4), Sharded("row", FF // 4, D)]
    shards = [w_in[0], w_up_a[0], w_up_b[0], w_o[0], w_ffn_in[0], w_ffn_down[0]]

    sent = [jnp.pad(shards[0].astype(BF16), ((0, 0), (0, cp - cs)))] + [w.astype(BF16) for w in shards[1:]]
    (f_in,) = _run_fused(_gather_fused(sent[:1], specs[:1]), "gather_w_in")
    w_main, w_rest = _unshuffle_w_in(f_in, dm)

    class ReduceHooks:
        pairs = [None] * 6
        fulls = None

        def pair_sums(self, fulls, from_sibling, first):
            for t, (g, r) in enumerate(zip(fulls, from_sibling), start=first):
                self.pairs[t] = _add_sibling(g, r, specs[t], cq_idx, "add_sibling_" + big_names[t])
            return _scatter_fused([self.pairs[t][0] for t in range(first, first + len(fulls))],
                                  specs[first:first + len(fulls)])

        def swap_later(self, fulls):
            self.fulls = fulls
            return _swap_fused(fulls, specs[1:])

        def scatter_later(self, from_sibling):
            return self.pair_sums(self.fulls, from_sibling, 1)

        def scatter_w_in(self, d_main, d_rest):
            full = [_shuffle_w_in(d_main, d_rest, dm)]
            return self.pair_sums(full, _run_fused(_swap_fused(full, specs[:1]), "swap_halves_w_in"), 0)

    hooks = ReduceHooks()

    bias_p = jnp.pad(b_fox_f, ((0, 0), (0, HEAD - dm.H)))
    dx, _, small, from_chips = _local_step(
        dm, x[0], loss_target[0], w_main, w_rest, _gather_fused(sent[1:], specs[1:]), bias_p,
        hgrn_lb_logits, hgrn_norm_w, norm_mix_pre, norm_mix_post, norm_ffn_pre, norm_ffn_post,
        reduce_hooks=hooks)

    halves = [_add_chips(p[1], r, sp, cq_idx, "add_chips_" + n)
              for p, r, sp, n in zip(hooks.pairs, from_chips, specs, big_names)]
    grads = list(_share_halves(halves, specs))
    grads[0] = grads[0][:, :cs]

    row = lambda v: jnp.pad(v.reshape(1, -1), ((0, 0), (0, D - v.size)))
    vec = jnp.concatenate([row(small["loss"][:, :1]), row(small["bias"][:, :dm.H]), row(small["norm_w"]),
                           row(small["lb"]), small["g1"], small["g2"], small["g3"], small["g4"]], axis=0)
    gsum = _sum_small(vec)
    loss = gsum[ROW_LOSS, 0]

    smalls = lambda t: (t["b_fox_f"], t["hgrn_norm_w"], t["hgrn_lb_logits"], t["norm_mix_pre"], t["norm_mix_post"],
                        t["norm_ffn_pre"], t["norm_ffn_post"])
    params = dict(b_fox_f=b_fox_f, hgrn_norm_w=hgrn_norm_w, hgrn_lb_logits=hgrn_lb_logits, norm_mix_pre=norm_mix_pre,
                  norm_mix_post=norm_mix_post, norm_ffn_pre=norm_ffn_pre, norm_ffn_post=norm_ffn_post)
    sg, sd, sm, sv = _small_update(gsum, _pack_small(dm, *smalls(params)), _pack_small(dm, *smalls(moments_m)),
                                   _pack_small(dm, *smalls(moments_v)))
    big_out = {}
    for name, wsh, g in zip(big_names, shards, grads):
        go, d, mn, vn = _adamw(wsh, g, moments_m[name][0], moments_v[name][0], "adamw_" + name)
        big_out[name] = (go[None], d[None], mn[None], vn[None])

    order = ["w_in", "b_fox_f", "hgrn_lb_logits", "hgrn_norm_w", "w_up_a", "w_up_b", "w_o", "norm_mix_pre",
             "norm_mix_post", "norm_ffn_pre", "norm_ffn_post", "w_ffn_in", "w_ffn_down"]
    outs = []
    for kind, packed in enumerate([sg, sd, sm, sv]):
        b, lbl, nw, p1, p2, p3, p4 = _unpack_small(packed, dm)
        sm_map = dict(b_fox_f=b, hgrn_lb_logits=lbl, hgrn_norm_w=nw, norm_mix_pre=p1, norm_mix_post=p2,
                      norm_ffn_pre=p3, norm_ffn_post=p4)
        outs.append([big_out[n][kind] if n in big_out else sm_map[n] for n in order])
    return (loss, dx[None], *outs[0], *outs[1], *outs[2], *outs[3])


def kernel(x, w_in, b_fox_f, hgrn_lb_logits, hgrn_norm_w, w_up_a, w_up_b, w_o, norm_mix_pre, norm_mix_post, norm_ffn_pre, norm_ffn_post, w_ffn_in, w_ffn_down, loss_target, m_w_in, m_b_fox_f, m_hgrn_lb_logits, m_hgrn_norm_w, m_w_up_a, m_w_up_b, m_w_o, m_norm_mix_pre, m_norm_mix_post, m_norm_ffn_pre, m_norm_ffn_post, m_w_ffn_in, m_w_ffn_down, v_w_in, v_b_fox_f, v_hgrn_lb_logits, v_hgrn_norm_w, v_w_up_a, v_w_up_b, v_w_o, v_norm_mix_pre, v_norm_mix_post, v_norm_ffn_pre, v_norm_ffn_post, v_w_ffn_in, v_w_ffn_down):
    dm = Dims(T=x.shape[1], D=x.shape[2], FF=w_ffn_down.shape[1] * 4)
    moments_m = dict(w_in=m_w_in, b_fox_f=m_b_fox_f, hgrn_lb_logits=m_hgrn_lb_logits, hgrn_norm_w=m_hgrn_norm_w,
                     w_up_a=m_w_up_a, w_up_b=m_w_up_b, w_o=m_w_o, norm_mix_pre=m_norm_mix_pre,
                     norm_mix_post=m_norm_mix_post, norm_ffn_pre=m_norm_ffn_pre, norm_ffn_post=m_norm_ffn_post,
                     w_ffn_in=m_w_ffn_in, w_ffn_down=m_w_ffn_down)
    moments_v = dict(w_in=v_w_in, b_fox_f=v_b_fox_f, hgrn_lb_logits=v_hgrn_lb_logits, hgrn_norm_w=v_hgrn_norm_w,
                     w_up_a=v_w_up_a, w_up_b=v_w_up_b, w_o=v_w_o, norm_mix_pre=v_norm_mix_pre,
                     norm_mix_post=v_norm_mix_post, norm_ffn_pre=v_norm_ffn_pre, norm_ffn_post=v_norm_ffn_post,
                     w_ffn_in=v_w_ffn_in, w_ffn_down=v_w_ffn_down)
    return _step(dm, x, w_in, b_fox_f, hgrn_lb_logits, hgrn_norm_w, w_up_a, w_up_b, w_o, norm_mix_pre, norm_mix_post,
                 norm_ffn_pre, norm_ffn_post, w_ffn_in, w_ffn_down, loss_target, moments_m, moments_v)
```

```python
from typing import NamedTuple

import numpy as np
import jax
import jax.numpy as jnp
from jax import lax
from jax.experimental import pallas as pl
from jax.experimental.pallas import tpu as pltpu

F32 = jnp.float32
BF16 = jnp.bfloat16
MESH = pl.DeviceIdType.MESH

RMS_EPS = 1e-6
HEAD = 128
CHUNK = 64
GROUP = 4
LEVELS = (32, 16, 8, 4, 2, 1)
NEG = -1e30

ADAM_LR = 0.001
ADAM_B1 = 0.9
ADAM_B2 = 0.999
ADAM_EPS = 1e-08
ADAM_WD = 0.01
ADAM_STEP = 10

VMEM_LIMIT = 56 * 1024 * 1024


class Dims(NamedTuple):
    T: int
    D: int
    FF: int

    @property
    def AW(self):
        return self.D // 2

    @property
    def H(self):
        return self.AW // HEAD

    @property
    def NMAIN(self):
        return 7 * self.AW

    @property
    def NREST(self):
        return 2 * self.D + HEAD

    @property
    def NIN(self):
        return 7 * self.AW + self.H + 2 * self.D


def _cparams(sem, vmem=VMEM_LIMIT, **kw):
    return pltpu.CompilerParams(dimension_semantics=sem, vmem_limit_bytes=vmem, **kw)


def _tile(n, target):
    if n <= target:
        return n
    t = (target // 128) * 128
    while t >= 128:
        if n % t == 0:
            return t
        t -= 128
    raise ValueError(f"no tile for {n}")


def _dot(a, b, dims):
    return lax.dot_general(a, b, (dims, ((), ())), preferred_element_type=F32)


def _nn(a, b):
    return _dot(a, b, ((1,), (0,)))


def _nt(a, b):
    return _dot(a, b, ((1,), (1,)))


def _tn(a, b):
    return _dot(a, b, ((0,), (0,)))


def _sigmoid(x):
    return jax.nn.sigmoid(x)


def _split2(x):
    hi = x.astype(BF16)
    lo = (x - hi.astype(F32)).astype(BF16)
    return hi, lo


def _split3(x):
    hi = x.astype(BF16)
    r = x - hi.astype(F32)
    mid = r.astype(BF16)
    lo = (r - mid.astype(F32)).astype(BF16)
    return hi, mid, lo


def _mm(a, b, mode, out_dtype, name, add=None, tm=1024, tn=1024, tk=None, fused=None):
    if mode == "nn":
        (M, K), (K2, N) = a.shape, b.shape
    elif mode == "nt":
        (M, K), (N, K2) = a.shape, b.shape
    else:
        (K, M), (K2, N) = a.shape, b.shape
    assert K == K2, (a.shape, b.shape, mode)
    tm, tn = _tile(M, tm), _tile(N, tn)
    isz = lambda t: jnp.dtype(t.dtype).itemsize

    def vmem_need(tk_):
        need = 2 * (tm * tk_ * isz(a) + tk_ * tn * isz(b) + tm * tn * jnp.dtype(out_dtype).itemsize)
        return need + (tm * tn * 4 if K // tk_ > 1 else 0) + (2 * tm * tn * isz(add) if add is not None else 0)

    budget = VMEM_LIMIT - 16 * 1024 * 1024
    for target in ((tk,) if tk is not None else ((2048, 1024, 512) if mode == "tn" else (4096, 2048, 1024, 512))):
        tk_ = _tile(K, target)
        if vmem_need(tk_) <= budget:
            break
    tk = tk_
    nk = K // tk
    assert vmem_need(tk) <= budget, (name, vmem_need(tk))
    if mode == "nn":
        a_spec = pl.BlockSpec((tm, tk), lambda i, j, k: (i, k))
        b_spec = pl.BlockSpec((tk, tn), lambda i, j, k: (k, j))
        op = _nn
    elif mode == "nt":
        a_spec = pl.BlockSpec((tm, tk), lambda i, j, k: (i, k))
        b_spec = pl.BlockSpec((tn, tk), lambda i, j, k: (j, k))
        op = _nt
    else:
        a_spec = pl.BlockSpec((tk, tm), lambda i, j, k: (k, i))
        b_spec = pl.BlockSpec((tk, tn), lambda i, j, k: (k, j))
        op = _tn
    o_spec = pl.BlockSpec((tm, tn), lambda i, j, k: (i, j))
    has_add = add is not None
    n_own = 3 if has_add else 2
    x_ins, x_outs, x_sems, n_in, n_out = _fused_parts(fused)
    grid = (M // tm, N // tn, nk)

    def body(*refs):
        a_ref, b_ref = refs[:2]
        add_ref = refs[2] if has_add else None
        o_ref = refs[n_own + n_in]
        acc = refs[n_own + n_in + 1 + n_out] if nk > 1 else None
        k = pl.program_id(2)
        if fused is not None:
            step = (pl.program_id(0) * grid[1] + pl.program_id(1)) * nk + k
            start, middle, finish_x = fused.hooks(
                refs[n_own:n_own + n_in], refs[n_own + n_in + 1:n_own + n_in + 1 + n_out],
                *refs[n_own + n_in + 1 + n_out + (1 if nk > 1 else 0):])
            pl.when(step == 0)(start)
            pl.when(step == (grid[0] * grid[1] * nk) // 2)(middle)

        def finish(r):
            if has_add:
                r = r + add_ref[...].astype(F32)
            o_ref[...] = r.astype(out_dtype)

        part = op(a_ref[...].astype(BF16), b_ref[...].astype(BF16))
        if nk == 1:
            finish(part)
        else:
            @pl.when(k == 0)
            def _():
                acc[...] = part

            @pl.when((k > 0) & (k < nk - 1))
            def _():
                acc[...] += part

            @pl.when(k == nk - 1)
            def _():
                finish(acc[...] + part)

        if fused is not None:
            pl.when(step == grid[0] * grid[1] * nk - 1)(finish_x)

    in_specs = [a_spec, b_spec] + ([o_spec] if has_add else []) + [HBM] * n_in
    args = (a, b) + ((add,) if has_add else ()) + x_ins
    outs = pl.pallas_call(
        body, name=name, grid=grid,
        in_specs=in_specs, out_specs=[o_spec] + [HBM] * n_out,
        out_shape=[jax.ShapeDtypeStruct((M, N), out_dtype)] + list(x_outs),
        scratch_shapes=([pltpu.VMEM((tm, tn), F32)] if nk > 1 else []) + list(x_sems),
        compiler_params=_cparams(("parallel", "parallel", "arbitrary") if fused is None else ("arbitrary",) * 3),
    )(*args)
    return outs[0] if fused is None else (outs[0], tuple(outs[1:]))


def _rows(tr, w, col=0):
    return pl.BlockSpec((tr, w), lambda i, *_: (i, col))


def _vec(w):
    return pl.BlockSpec((1, w), lambda i, *_: (0, 0))


def _rstd(v):
    return lax.rsqrt(jnp.mean(v * v, axis=-1, keepdims=True) + RMS_EPS)


def _rms_bwd(dn, n, r):
    return r * (dn - n * jnp.mean(dn * n, axis=-1, keepdims=True))


def _colsum(v):
    return jnp.sum(v, axis=0, keepdims=True)


def _rms_fwd(x, g, tr=256):
    T, D = x.shape

    def body(x_ref, g_ref, h_ref, r_ref):
        xv = x_ref[...]
        r = _rstd(xv)
        h_ref[...] = (xv * r * g_ref[...]).astype(BF16)
        r_ref[...] = r

    return pl.pallas_call(
        body, name="rms_fwd", grid=(T // tr,),
        in_specs=[_rows(tr, D), _vec(D)],
        out_specs=[_rows(tr, D), _rows(tr, 1)],
        out_shape=[jax.ShapeDtypeStruct((T, D), BF16), jax.ShapeDtypeStruct((T, 1), F32)],
        compiler_params=_cparams(("parallel",)),
    )(x, g)


def _merge_fwd(rest, y_a, y_b, tr=256):
    T, D = y_a.shape

    def body(ga_ref, gb_ref, ya_ref, yb_ref, o_ref):
        o_ref[...] = (_sigmoid(ga_ref[...]) * ya_ref[...] + _sigmoid(gb_ref[...]) * yb_ref[...]).astype(BF16)

    return pl.pallas_call(
        body, name="merge_fwd", grid=(T // tr,),
        in_specs=[_rows(tr, D, 0), _rows(tr, D, 1), _rows(tr, D), _rows(tr, D)],
        out_specs=_rows(tr, D),
        out_shape=jax.ShapeDtypeStruct((T, D), BF16),
        compiler_params=_cparams(("parallel",)),
    )(rest, rest, y_a, y_b)


def _post_pre(x, u, g2, g3, tr=256):
    T, D = x.shape

    def body(x_ref, u_ref, g2_ref, g3_ref, x1_ref, r2_ref, h3_ref, r3_ref):
        uv = u_ref[...]
        r2 = _rstd(uv)
        x1 = x_ref[...] + uv * r2 * g2_ref[...]
        r3 = _rstd(x1)
        x1_ref[...] = x1
        r2_ref[...] = r2
        h3_ref[...] = (x1 * r3 * g3_ref[...]).astype(BF16)
        r3_ref[...] = r3

    return pl.pallas_call(
        body, name="post_pre", grid=(T // tr,),
        in_specs=[_rows(tr, D), _rows(tr, D), _vec(D), _vec(D)],
        out_specs=[_rows(tr, D), _rows(tr, 1), _rows(tr, D), _rows(tr, 1)],
        out_shape=[jax.ShapeDtypeStruct((T, D), F32), jax.ShapeDtypeStruct((T, 1), F32),
                   jax.ShapeDtypeStruct((T, D), BF16), jax.ShapeDtypeStruct((T, 1), F32)],
        compiler_params=_cparams(("parallel",)),
    )(x, u, g2, g3)


def _ffn_in_swiglu(h3, w_ffn_in, tm=1024, tn=512):
    T, D = h3.shape
    FF = w_ffn_in.shape[1] // 2
    tm, tn = _tile(T, tm), _tile(FF, tn)
    nj = FF // tn

    def body(a_ref, bg_ref, bu_ref, g_ref, u_ref, act_ref):
        av = a_ref[...]
        gv = _nn(av, bg_ref[...])
        uv = _nn(av, bu_ref[...])
        g_ref[...] = gv
        u_ref[...] = uv
        act_ref[...] = (gv * _sigmoid(gv) * uv).astype(BF16)

    blk = pl.BlockSpec((tm, tn), lambda i, j: (i, j))
    return pl.pallas_call(
        body, name="ffn_in_swiglu", grid=(T // tm, nj),
        in_specs=[pl.BlockSpec((tm, D), lambda i, j: (i, 0)),
                  pl.BlockSpec((D, tn), lambda i, j: (0, j)),
                  pl.BlockSpec((D, tn), lambda i, j: (0, j + nj))],
        out_specs=[blk, blk, blk],
        out_shape=[jax.ShapeDtypeStruct((T, FF), F32), jax.ShapeDtypeStruct((T, FF), F32),
                   jax.ShapeDtypeStruct((T, FF), BF16)],
        compiler_params=_cparams(("parallel", "parallel")),
    )(h3, w_ffn_in, w_ffn_in)


def _d_act_swiglu(dw, w_ffn_down, gate, up, tm=1024, tn=512):
    T, D = dw.shape
    FF = w_ffn_down.shape[0]
    tm, tn = _tile(T, tm), _tile(FF, tn)

    def body(a_ref, b_ref, g_ref, u_ref, dg_ref, du_ref):
        dact = _nt(a_ref[...], b_ref[...])
        gv = g_ref[...]
        s = _sigmoid(gv)
        dg_ref[...] = (dact * u_ref[...] * (s * (1.0 + gv * (1.0 - s)))).astype(BF16)
        du_ref[...] = (dact * (gv * s)).astype(BF16)

    blk = pl.BlockSpec((tm, tn), lambda i, j: (i, j))
    sds = jax.ShapeDtypeStruct((T, FF), BF16)
    return pl.pallas_call(
        body, name="d_act_swiglu", grid=(T // tm, FF // tn),
        in_specs=[pl.BlockSpec((tm, D), lambda i, j: (i, 0)), pl.BlockSpec((tn, D), lambda i, j: (j, 0)), blk, blk],
        out_specs=[blk, blk], out_shape=[sds, sds],
        compiler_params=_cparams(("parallel", "parallel")),
    )(dw, w_ffn_down, gate, up)


def _swiglu_fwd(gu, tr=256):
    T, FF2 = gu.shape
    FF = FF2 // 2
    tc = _tile(FF, 1024)
    nc = FF // tc

    def body(g_ref, u_ref, o_ref):
        gv = g_ref[...]
        o_ref[...] = (gv * _sigmoid(gv) * u_ref[...]).astype(BF16)

    return pl.pallas_call(
        body, name="swiglu_fwd", grid=(T // tr, nc),
        in_specs=[pl.BlockSpec((tr, tc), lambda i, j: (i, j)),
                  pl.BlockSpec((tr, tc), lambda i, j: (i, j + nc))],
        out_specs=pl.BlockSpec((tr, tc), lambda i, j: (i, j)),
        out_shape=jax.ShapeDtypeStruct((T, FF), BF16),
        compiler_params=_cparams(("parallel", "parallel")),
    )(gu, gu)


def _loss_bwd(x1, w, g4, tgt, tr=256):
    T, D = x1.shape

    def body(x1_ref, w_ref, g4_ref, t_ref, loss_ref, dy_ref, dw_ref, dg_ref):
        i = pl.program_id(0)

        @pl.when(i == 0)
        def _():
            loss_ref[...] = jnp.zeros_like(loss_ref)
            dg_ref[...] = jnp.zeros_like(dg_ref)

        wv = w_ref[...]
        g4v = g4_ref[...]
        r4 = _rstd(wv)
        n4 = wv * r4
        e = x1_ref[...] + n4 * g4v - t_ref[...]
        loss_ref[...] += 0.5 * jnp.sum(jnp.mean(e * e, axis=-1, keepdims=True), axis=0, keepdims=True)
        dy = e * (1.0 / D)
        dy_ref[...] = dy
        dg_ref[...] += _colsum(dy * n4)
        dw_ref[...] = _rms_bwd(dy * g4v, n4, r4).astype(BF16)

    return pl.pallas_call(
        body, name="loss_bwd", grid=(T // tr,),
        in_specs=[_rows(tr, D), _rows(tr, D), _vec(D), _rows(tr, D)],
        out_specs=[_vec(HEAD), _rows(tr, D), _rows(tr, D), _vec(D)],
        out_shape=[jax.ShapeDtypeStruct((1, HEAD), F32), jax.ShapeDtypeStruct((T, D), F32),
                   jax.ShapeDtypeStruct((T, D), BF16), jax.ShapeDtypeStruct((1, D), F32)],
        compiler_params=_cparams(("arbitrary",)),
    )(x1, w, g4, tgt)


def _swiglu_bwd(gu, dact, tr=128):
    T, FF2 = gu.shape
    FF = FF2 // 2

    def body(g_ref, u_ref, d_ref, o_ref):
        h = pl.program_id(1)
        gv = g_ref[...]
        s = _sigmoid(gv)
        dv = d_ref[...].astype(F32)

        @pl.when(h == 0)
        def _():
            o_ref[...] = (dv * u_ref[...] * (s * (1.0 + gv * (1.0 - s)))).astype(BF16)

        @pl.when(h == 1)
        def _():
            o_ref[...] = (dv * (gv * s)).astype(BF16)

    return pl.pallas_call(
        body, name="swiglu_bwd", grid=(T // tr, 2),
        in_specs=[pl.BlockSpec((tr, FF), lambda i, h: (i, 0)),
                  pl.BlockSpec((tr, FF), lambda i, h: (i, 1)),
                  pl.BlockSpec((tr, FF), lambda i, h: (i, 0))],
        out_specs=pl.BlockSpec((tr, FF), lambda i, h: (i, h)),
        out_shape=jax.ShapeDtypeStruct((T, FF2), BF16),
        compiler_params=_cparams(("parallel", "arbitrary")),
    )(gu, gu, dact)


def _rms_bwd2(dy, dh3, x1, r3, g3, u, r2, g2, tr=256):
    T, D = dy.shape

    def body(dy_ref, dh_ref, x1_ref, r3_ref, g3_ref, u_ref, r2_ref, g2_ref, dx1_ref, du_ref, dg3_ref, dg2_ref):
        i = pl.program_id(0)

        @pl.when(i == 0)
        def _():
            dg3_ref[...] = jnp.zeros_like(dg3_ref)
            dg2_ref[...] = jnp.zeros_like(dg2_ref)

        r3v, r2v = r3_ref[...], r2_ref[...]
        dh = dh_ref[...]
        n3 = x1_ref[...] * r3v
        dg3_ref[...] += _colsum(dh * n3)
        dx1 = dy_ref[...] + _rms_bwd(dh * g3_ref[...], n3, r3v)
        dx1_ref[...] = dx1
        n2 = u_ref[...] * r2v
        dg2_ref[...] += _colsum(dx1 * n2)
        du_ref[...] = _rms_bwd(dx1 * g2_ref[...], n2, r2v).astype(BF16)

    return pl.pallas_call(
        body, name="rms_bwd2", grid=(T // tr,),
        in_specs=[_rows(tr, D), _rows(tr, D), _rows(tr, D), _rows(tr, 1), _vec(D),
                  _rows(tr, D), _rows(tr, 1), _vec(D)],
        out_specs=[_rows(tr, D), _rows(tr, D), _vec(D), _vec(D)],
        out_shape=[jax.ShapeDtypeStruct((T, D), F32), jax.ShapeDtypeStruct((T, D), BF16),
                   jax.ShapeDtypeStruct((1, D), F32), jax.ShapeDtypeStruct((1, D), F32)],
        compiler_params=_cparams(("arbitrary",)),
    )(dy, dh3, x1, r3, g3, u, r2, g2)


def _merge_bwd(dmerged, rest, y_a, y_b, tr=256):
    T, D = dmerged.shape

    def body(dm_ref, ga_ref, gb_ref, ya_ref, yb_ref, dya_ref, dyb_ref, dg_ref):
        h = pl.program_id(1)
        dm = dm_ref[...]

        @pl.when(h == 0)
        def _():
            s = _sigmoid(ga_ref[...])
            dya_ref[...] = (dm * s).astype(BF16)
            dg_ref[...] = (dm * ya_ref[...] * s * (1.0 - s)).astype(BF16)

        @pl.when(h == 1)
        def _():
            s = _sigmoid(gb_ref[...])
            dyb_ref[...] = (dm * s).astype(BF16)
            dg_ref[...] = (dm * yb_ref[...] * s * (1.0 - s)).astype(BF16)

    blk = lambda col: pl.BlockSpec((tr, D), lambda i, h: (i, col))
    return pl.pallas_call(
        body, name="merge_bwd", grid=(T // tr, 2),
        in_specs=[blk(0), blk(0), blk(1), blk(0), blk(0)],
        out_specs=[blk(0), blk(0), pl.BlockSpec((tr, D), lambda i, h: (i, h))],
        out_shape=[jax.ShapeDtypeStruct((T, D), BF16), jax.ShapeDtypeStruct((T, D), BF16),
                   jax.ShapeDtypeStruct((T, 2 * D), BF16)],
        compiler_params=_cparams(("parallel", "arbitrary")),
    )(dmerged, rest, rest, y_a, y_b)


def _rms_bwd1(dx1, dh1, x, r1, g1, tr=256):
    T, D = x.shape

    def body(dx1_ref, dh_ref, x_ref, r_ref, g_ref, dx_ref, dg_ref):
        i = pl.program_id(0)

        @pl.when(i == 0)
        def _():
            dg_ref[...] = jnp.zeros_like(dg_ref)

        rv = r_ref[...]
        dh = dh_ref[...]
        n = x_ref[...] * rv
        dg_ref[...] += _colsum(dh * n)
        dx_ref[...] = dx1_ref[...] + _rms_bwd(dh * g_ref[...], n, rv)

    return pl.pallas_call(
        body, name="rms_bwd1", grid=(T // tr,),
        in_specs=[_rows(tr, D), _rows(tr, D), _rows(tr, D), _rows(tr, 1), _vec(D)],
        out_specs=[_rows(tr, D), _vec(D)],
        out_shape=[jax.ShapeDtypeStruct((T, D), F32), jax.ShapeDtypeStruct((1, D), F32)],
        compiler_params=_cparams(("arbitrary",)),
    )(dx1, dh1, x, r1, g1)


def _hgrn_consts():
    C = CHUNK
    nl = len(LEVELS) + 1
    w = np.zeros((nl, C, C), np.float32)
    w[0] = np.tril(np.ones((C, C), np.float32))
    for li, m in enumerate(LEVELS, start=1):
        for r in range(C):
            mid = (r // (2 * m)) * 2 * m + m
            if r >= mid:
                w[li, r, mid:r + 1] = 1.0
            else:
                w[li, r, r + 1:mid] = 1.0
    w_all = w.reshape(nl * C, C)
    w2 = np.concatenate([w_all, w_all], axis=1)
    w2t = np.concatenate([w_all.T, w_all.T], axis=1)
    R = GROUP * C
    t = np.arange(R)[:, None]
    s = np.arange(R)[None, :]
    masks = np.zeros((nl, R, R), np.float32)
    masks[0] = (t == s)
    for li, m in enumerate(LEVELS, start=1):
        masks[li] = ((t ^ s) < 2 * m)
    return jnp.asarray(w2, BF16), jnp.asarray(w2t, BF16), jnp.asarray(masks, F32)


def _hgrn_gates(z, lg_ref):
    l0 = lg_ref[0:1, :]
    l1 = lg_ref[1:2, :]
    mx = jnp.maximum(l0, l1)
    e0 = jnp.exp(l0 - mx)
    e1 = jnp.exp(l1 - mx)
    lb = e0 / (e0 + e1)
    om = 1.0 - lb
    sg = _sigmoid(z)
    sgm = _sigmoid(-z)
    f = lb + om * sg
    return lb, om, sg, sgm, f, jnp.log(f), om * sgm


def _hgrn_levels(q, kk, lf, w2_ref):
    C = CHUNK
    nl = len(LEVELS) + 1
    lf_hi, lf_lo = _split2(lf)
    per_chunk = []
    for c in range(GROUP):
        rhs = jnp.concatenate([lf_hi[c * C:(c + 1) * C], lf_lo[c * C:(c + 1) * C]], axis=0)
        per_chunk.append(_nn(w2_ref[...], rhs))
    args = [jnp.concatenate([per_chunk[c][l * C:(l + 1) * C] for c in range(GROUP)], axis=0) for l in range(nl)]
    exps = [jnp.exp(a) for a in args]
    row = lax.broadcasted_iota(jnp.int32, q.shape, 0)
    qf, kf, mts = [q], [kk], [None]
    for li, m in enumerate(LEVELS, start=1):
        mt = jnp.where((row & m) != 0, 1.0, 0.0).astype(F32)
        mts.append(mt)
        qf.append(q * exps[li] * mt)
        kf.append(kk * exps[li] * (1.0 - mt))
    return args, exps, mts, qf, kf


def _hgrn_scores(qf, kf, masks_ref):
    p = None
    for l in range(len(qf)):
        pl_ = _nt(qf[l].astype(BF16), kf[l].astype(BF16)) * masks_ref[l]
        p = pl_ if p is None else p + pl_
    return p


def _hgrn_fwd(main, lb_logits, norm_w, dm):
    T, H, C = dm.T, dm.H, CHUNK
    R = GROUP * C
    nj = T // R
    w2, _, masks = _hgrn_consts()

    def body(x_ref, lg_ref, nw_ref, w2_ref, masks_ref, ya_ref, o_ref, sp_ref, st_ref):
        j = pl.program_id(1)

        @pl.when(j == 0)
        def _():
            st_ref[...] = jnp.zeros_like(st_ref)

        q, z, v, gv = (x_ref[:, s * HEAD:(s + 1) * HEAD] for s in range(4))
        vb = v.astype(BF16)
        _, _, _, _, _, lf, kk = _hgrn_gates(z, lg_ref)
        args, exps, _, qf, kf = _hgrn_levels(q, kk, lf, w2_ref)
        p = _hgrn_scores(qf, kf, masks_ref)
        o_intra = _nn(p.astype(BF16), vb)
        b, eb = args[0], exps[0]
        o_inter = []
        for c in range(GROUP):
            sl = slice(c * C, (c + 1) * C)
            st = st_ref[...]
            sp_ref[0, c] = st
            blast = b[c * C + C - 1:c * C + C, :]
            o_inter.append(_nt((q[sl] * eb[sl]).astype(BF16), st.astype(BF16)))
            kd = (kk[sl] * jnp.exp(blast - b[sl])).astype(BF16)
            st_ref[...] = st * jnp.exp(blast) + _tn(vb[sl], kd)
        o = o_intra + jnp.concatenate(o_inter, axis=0)
        o_ref[...] = o
        ya_ref[...] = (o * _rstd(o) * nw_ref[...] * (gv * _sigmoid(gv))).astype(BF16)

    return pl.pallas_call(
        body, name="hgrn_fwd", grid=(H, nj),
        in_specs=[pl.BlockSpec((R, 4 * HEAD), lambda h, j: (j, h)),
                  pl.BlockSpec((2, HEAD), lambda h, j: (0, h)),
                  pl.BlockSpec((1, HEAD), lambda h, j: (0, 0)),
                  pl.BlockSpec(w2.shape, lambda h, j: (0, 0)),
                  pl.BlockSpec(masks.shape, lambda h, j: (0, 0, 0))],
        out_specs=[pl.BlockSpec((R, HEAD), lambda h, j: (j, h)),
                   pl.BlockSpec((R, HEAD), lambda h, j: (j, h)),
                   pl.BlockSpec((1, GROUP, HEAD, HEAD), lambda h, j: (h, j, 0, 0))],
        out_shape=[jax.ShapeDtypeStruct((T, dm.AW), BF16), jax.ShapeDtypeStruct((T, dm.AW), F32),
                   jax.ShapeDtypeStruct((H, T // C, HEAD, HEAD), F32)],
        scratch_shapes=[pltpu.VMEM((HEAD, HEAD), F32)],
        compiler_params=_cparams(("parallel", "arbitrary")),
    )(main, lb_logits, norm_w, w2, masks)


def _hgrn_bwd(main, lb_logits, norm_w, o_saved, states, dya, dm, fused=None):
    T, H, C = dm.T, dm.H, CHUNK
    R = GROUP * C
    nj = T // R
    nl = len(LEVELS) + 1
    w2, w2t, masks = _hgrn_consts()
    x_ins, x_outs, x_sems, n_in, n_out = _fused_parts(fused)

    def body(x_ref, lg_ref, nw_ref, w2_ref, w2t_ref, masks_ref, o_ref, sp_ref, dya_ref, *rest):
        d_ref, dlb_ref, dnw_ref = rest[n_in:n_in + 3]
        ds_ref = rest[n_in + 3 + n_out]
        j = pl.program_id(1)
        if fused is not None:
            hd = pl.program_id(0)
            start, middle, finish_x = fused.hooks(rest[:n_in], rest[n_in + 3:n_in + 3 + n_out],
                                                  *rest[n_in + 4 + n_out:])
            pl.when((hd == 0) & (j == 0))(start)
            pl.when((hd == H // 2) & (j == 0))(middle)

        @pl.when(j == 0)
        def _():
            ds_ref[...] = jnp.zeros_like(ds_ref)
            dlb_ref[...] = jnp.zeros_like(dlb_ref)
            dnw_ref[...] = jnp.zeros_like(dnw_ref)

        def compute():
            q, z, v, gv = (x_ref[:, s * HEAD:(s + 1) * HEAD] for s in range(4))
            vb = v.astype(BF16)
            lb, om, sg, sgm, f, lf, kk = _hgrn_gates(z, lg_ref)
            args, exps, mts, qf, kf = _hgrn_levels(q, kk, lf, w2_ref)
            qb = [t.astype(BF16) for t in qf]
            kb = [t.astype(BF16) for t in kf]
            p = _hgrn_scores(qf, kf, masks_ref)
            o = o_ref[...]
            nw = nw_ref[...]
            r = _rstd(o)
            n = o * r
            sgg = _sigmoid(gv)
            dya_v = dya_ref[...]
            d_on = dya_v * (gv * sgg)
            dg = dya_v * (n * nw) * (sgg * (1.0 + gv * (1.0 - sgg)))
            dnw_ref[0] += _colsum(d_on * n)
            do = _rms_bwd(d_on * nw, n, r)
            dob = do.astype(BF16)
            dp = _nt(dob, vb)
            dv = _tn(p.astype(BF16), dob)
            dq = jnp.zeros_like(q)
            dkk = jnp.zeros_like(q)
            dargs = [None] * nl
            for l in range(nl):
                dpl = (dp * masks_ref[l]).astype(BF16)
                dql = _nn(dpl, kb[l])
                dkl = _tn(dpl, qb[l])
                if l == 0:
                    dq += dql
                    dkk += dkl
                else:
                    dq += dql * exps[l] * mts[l]
                    dkk += dkl * exps[l] * (1.0 - mts[l])
                    dargs[l] = dql * qf[l] + dkl * kf[l]
            b, eb = args[0], exps[0]
            row = lax.broadcasted_iota(jnp.int32, (C, HEAD), 0)
            dq_i, dkk_i, dv_i, db_i = [None] * GROUP, [None] * GROUP, [None] * GROUP, [None] * GROUP
            for c in reversed(range(GROUP)):
                sl = slice(c * C, (c + 1) * C)
                st = sp_ref[0, c]
                ds = ds_ref[...]
                dsb = ds.astype(BF16)
                blast = b[c * C + C - 1:c * C + C, :]
                ebl = jnp.exp(blast)
                el = jnp.exp(blast - b[sl])
                qe = q[sl] * eb[sl]
                kd = kk[sl] * el
                dqe = _nn(dob[sl], st.astype(BF16))
                dkd = _nn(vb[sl], dsb)
                t = dkd * kd
                dblast = _colsum(t) + _colsum(ds * st) * ebl
                dq_i[c] = dqe * eb[sl]
                dkk_i[c] = dkd * el
                dv_i[c] = _nt(kd.astype(BF16), dsb)
                db_i[c] = dqe * qe - t + jnp.where(row == C - 1, dblast, 0.0)
                ds_ref[...] = ds * ebl + _tn(dob[sl], qe.astype(BF16))
            dq = dq + jnp.concatenate(dq_i, axis=0)
            dkk = dkk + jnp.concatenate(dkk_i, axis=0)
            dv = dv + jnp.concatenate(dv_i, axis=0)
            dlf_c = []
            for c in range(GROUP):
                sl = slice(c * C, (c + 1) * C)
                stack = jnp.concatenate([db_i[c]] + [dargs[l][sl] for l in range(1, nl)], axis=0)
                hi, lo = _split2(stack)
                dlf_c.append(_nn(w2t_ref[...], jnp.concatenate([hi, lo], axis=0)))
            dlf = jnp.concatenate(dlf_c, axis=0)
            dz = dlf * (om * sg * (1.0 - sg) / f) - dkk * (om * sgm * (1.0 - sgm))
            dlb_ref[...] += _colsum(dlf * (1.0 - sg) / f - dkk * sgm)
            d_ref[...] = jnp.concatenate([dq, dz, dv, dg], axis=1).astype(BF16)

        compute()
        if fused is not None:
            pl.when((hd == H - 1) & (j == nj - 1))(finish_x)

    outs = pl.pallas_call(
        body, name="hgrn_bwd", grid=(H, nj),
        in_specs=[pl.BlockSpec((R, 4 * HEAD), lambda h, j: (nj - 1 - j, h)),
                  pl.BlockSpec((2, HEAD), lambda h, j: (0, h)),
                  pl.BlockSpec((1, HEAD), lambda h, j: (0, 0)),
                  pl.BlockSpec(w2.shape, lambda h, j: (0, 0)),
                  pl.BlockSpec(w2t.shape, lambda h, j: (0, 0)),
                  pl.BlockSpec(masks.shape, lambda h, j: (0, 0, 0)),
                  pl.BlockSpec((R, HEAD), lambda h, j: (nj - 1 - j, h)),
                  pl.BlockSpec((1, GROUP, HEAD, HEAD), lambda h, j: (h, nj - 1 - j, 0, 0)),
                  pl.BlockSpec((R, HEAD), lambda h, j: (nj - 1 - j, h))] + [HBM] * n_in,
        out_specs=[pl.BlockSpec((R, 4 * HEAD), lambda h, j: (nj - 1 - j, h)),
                   pl.BlockSpec((1, HEAD), lambda h, j: (0, h)),
                   pl.BlockSpec((1, 1, HEAD), lambda h, j: (h, 0, 0))] + [HBM] * n_out,
        out_shape=[jax.ShapeDtypeStruct((T, 4 * dm.AW), BF16), jax.ShapeDtypeStruct((1, dm.AW), F32),
                   jax.ShapeDtypeStruct((H, 1, HEAD), F32)] + list(x_outs),
        scratch_shapes=[pltpu.VMEM((HEAD, HEAD), F32)] + list(x_sems),
        compiler_params=_cparams(("arbitrary", "arbitrary")),
    )(main, lb_logits, norm_w, w2, w2t, masks, o_saved, states, dya, *x_ins)
    return outs[0], outs[1], outs[2], tuple(outs[3:])


def _log_sigmoid(x):
    return jnp.minimum(x, 0.0) - jnp.log(1.0 + jnp.exp(-jnp.abs(x)))


def _tri(n):
    return jnp.asarray(np.tril(np.ones((n, n), np.float32)), BF16)


def _cum_fwd(rest, bias, dm, tb=256):
    T = dm.T
    tb = min(tb, T)
    cb = 2 * dm.D // HEAD
    tri = _tri(tb)

    def body(x_ref, b_ref, tri_ref, o_ref, carry):
        i = pl.program_id(0)

        @pl.when(i == 0)
        def _():
            carry[...] = jnp.zeros_like(carry)

        lf = _log_sigmoid(x_ref[...] + b_ref[...])
        hi, mid, lo = _split3(lf)
        tr_ = tri_ref[...]
        c = _nn(tr_, hi) + _nn(tr_, mid) + _nn(tr_, lo) + carry[...]
        o_ref[...] = c
        carry[...] = c[tb - 1:tb, :]

    return pl.pallas_call(
        body, name="cum_fwd", grid=(T // tb,),
        in_specs=[_rows(tb, HEAD, cb), _vec(HEAD), pl.BlockSpec((tb, tb), lambda i: (0, 0))],
        out_specs=_rows(tb, HEAD),
        out_shape=jax.ShapeDtypeStruct((T, HEAD), F32),
        scratch_shapes=[pltpu.VMEM((1, HEAD), F32)],
        compiler_params=_cparams(("arbitrary",)),
    )(rest, bias, tri)


def _cum_bwd(dcum, rest, bias, dm, tb=256):
    T = dm.T
    tb = min(tb, T)
    nb = T // tb
    cb = 2 * dm.D // HEAD
    tri = _tri(tb)

    def body(d_ref, x_ref, b_ref, tri_ref, o_ref, db_ref, carry):
        i = pl.program_id(0)

        @pl.when(i == 0)
        def _():
            carry[...] = jnp.zeros_like(carry)
            db_ref[...] = jnp.zeros_like(db_ref)

        hi, mid, lo = _split3(d_ref[...])
        tr_ = tri_ref[...]
        dlf = _tn(tr_, hi) + _tn(tr_, mid) + _tn(tr_, lo) + carry[...]
        carry[...] = dlf[0:1, :]
        dx = dlf * _sigmoid(-(x_ref[...] + b_ref[...]))
        o_ref[...] = dx.astype(BF16)
        db_ref[...] += _colsum(dx)

    return pl.pallas_call(
        body, name="cum_bwd", grid=(nb,),
        in_specs=[pl.BlockSpec((tb, HEAD), lambda i: (nb - 1 - i, 0)),
                  pl.BlockSpec((tb, HEAD), lambda i: (nb - 1 - i, cb)),
                  _vec(HEAD), pl.BlockSpec((tb, tb), lambda i: (0, 0))],
        out_specs=[pl.BlockSpec((tb, HEAD), lambda i: (nb - 1 - i, 0)), _vec(HEAD)],
        out_shape=[jax.ShapeDtypeStruct((T, HEAD), BF16), jax.ShapeDtypeStruct((1, HEAD), F32)],
        scratch_shapes=[pltpu.VMEM((1, HEAD), F32)],
        compiler_params=_cparams(("arbitrary",)),
    )(dcum, rest, bias, tri)


def _fox_pairs(nq, kv_major):
    if kv_major:
        pairs = [(i, j) for j in range(nq) for i in range(j, nq)]
    else:
        pairs = [(i, j) for i in range(nq) for j in range(i + 1)]
    qi = jnp.asarray(np.array([p[0] for p in pairs], np.int32))
    kj = jnp.asarray(np.array([p[1] for p in pairs], np.int32))
    return qi, kj


class Fused(NamedTuple):
    ins: tuple
    outs: tuple
    sems: tuple
    hooks: object


def _fused_parts(fused):
    if fused is None:
        return (), (), (), 0, 0
    return tuple(fused.ins), tuple(fused.outs), tuple(fused.sems), len(fused.ins), len(fused.outs)


FOX_FWD_SPLIT = 4
FOX_BWD_SPLIT = 2


def _fox_fwd(main, cum_row, dm, fused=None, tq=512):
    T, H = dm.T, dm.H
    tq = min(tq, T)
    nq = T // tq
    qi_tab, kj_tab = _fox_pairs(nq, kv_major=False)
    npairs = int(qi_tab.shape[0])
    x_ins, x_outs, x_sems, n_in, n_out = _fused_parts(fused)
    ns = FOX_FWD_SPLIT if tq % (8 * FOX_FWD_SPLIT) == 0 else 1
    rq = tq // ns

    def body(qi_ref, kj_ref, q_ref, k_ref, v_ref, ck_ref, *rest):
        x_in, (o_ref, lse_ref) = rest[:n_in], rest[n_in:n_in + 2]
        x_out = rest[n_in + 2:n_in + 2 + n_out]
        m_ref, l_ref, acc_ref = rest[n_in + 2 + n_out:n_in + 5 + n_out]
        hd = pl.program_id(0)
        t = pl.program_id(1)
        i = qi_ref[t]
        j = kj_ref[t]
        if fused is not None:
            start, middle, finish = fused.hooks(x_in, x_out, *rest[n_in + 5 + n_out:])
            pl.when((hd == 0) & (t == 0))(start)
            pl.when((hd == H // 2) & (t == 0))(middle)

        @pl.when(j == 0)
        def _():
            m_ref[...] = jnp.full_like(m_ref, NEG)
            l_ref[...] = jnp.zeros_like(l_ref)
            acc_ref[...] = jnp.zeros_like(acc_ref)

        def step(on_diagonal):
            kb = k_ref[...].astype(BF16)
            vb = v_ref[...].astype(BF16)
            ck = ck_ref[0]
            q_all, m_all, l_all, acc_all = q_ref[...], m_ref[...], l_ref[...], acc_ref[...]
            m_out, l_out, acc_out = [], [], []
            for g in range(ns):
                rows = slice(g * rq, (g + 1) * rq)
                qs = (q_all[rows] * (HEAD ** -0.5)).astype(BF16)
                s = _nt(qs, kb) - ck
                if on_diagonal:
                    row = g * rq + lax.broadcasted_iota(jnp.int32, s.shape, 0)
                    s = jnp.where(row >= lax.broadcasted_iota(jnp.int32, s.shape, 1), s, NEG)
                m_new = jnp.maximum(m_all[rows], jnp.max(s, axis=-1, keepdims=True))
                a = jnp.exp(m_all[rows] - m_new)
                p = jnp.exp(s - m_new)
                m_out.append(m_new)
                l_out.append(a * l_all[rows] + jnp.sum(p, axis=-1, keepdims=True))
                acc_out.append(a * acc_all[rows] + _nn(p.astype(BF16), vb))
            m_ref[...] = jnp.concatenate(m_out, axis=0)
            l_ref[...] = jnp.concatenate(l_out, axis=0)
            acc_ref[...] = jnp.concatenate(acc_out, axis=0)

        @pl.when(j < i)
        def _():
            step(False)

        @pl.when(j == i)
        def _():
            step(True)
            l = l_ref[...]
            o_ref[...] = acc_ref[...] / l
            lse_ref[0] = m_ref[...] + jnp.log(l)

        if fused is not None:
            pl.when((hd == H - 1) & (t == npairs - 1))(finish)

    outs = pl.pallas_call(
        body, name="fox_fwd",
        grid_spec=pltpu.PrefetchScalarGridSpec(
            num_scalar_prefetch=2, grid=(H, npairs),
            in_specs=[pl.BlockSpec((tq, HEAD), lambda h, t, qi, kj: (qi[t], 4 * H + h)),
                      pl.BlockSpec((tq, HEAD), lambda h, t, qi, kj: (kj[t], 5 * H + h)),
                      pl.BlockSpec((tq, HEAD), lambda h, t, qi, kj: (kj[t], 6 * H + h)),
                      pl.BlockSpec((1, 1, tq), lambda h, t, qi, kj: (h, 0, kj[t]))] + [HBM] * n_in,
            out_specs=[pl.BlockSpec((tq, HEAD), lambda h, t, qi, kj: (qi[t], h)),
                       pl.BlockSpec((1, tq, 1), lambda h, t, qi, kj: (h, qi[t], 0))] + [HBM] * n_out,
            scratch_shapes=[pltpu.VMEM((tq, 1), F32), pltpu.VMEM((tq, 1), F32), pltpu.VMEM((tq, HEAD), F32)]
            + list(x_sems)),
        out_shape=[jax.ShapeDtypeStruct((T, dm.AW), F32), jax.ShapeDtypeStruct((H, T, 1), F32)] + list(x_outs),
        compiler_params=_cparams(("arbitrary", "arbitrary")),
    )(qi_tab, kj_tab, main, main, main, cum_row, *x_ins)
    return outs[0], outs[1], tuple(outs[2:])


def _fox_delta(do, o, dm, tr=256):
    T, H = dm.T, dm.H
    tr = min(tr, T)

    def body(do_ref, o_ref, d_ref):
        d_ref[0] = jnp.sum(do_ref[...] * o_ref[...], axis=-1, keepdims=True)

    return pl.pallas_call(
        body, name="fox_delta", grid=(H, T // tr),
        in_specs=[pl.BlockSpec((tr, HEAD), lambda h, i: (i, h)), pl.BlockSpec((tr, HEAD), lambda h, i: (i, h))],
        out_specs=pl.BlockSpec((1, tr, 1), lambda h, i: (h, i, 0)),
        out_shape=jax.ShapeDtypeStruct((H, T, 1), F32),
        compiler_params=_cparams(("parallel", "parallel")),
    )(do, o)


def _fox_bwd(main, cum_row, lse, delta, do, dm, fused=None, tq=512):
    T, H = dm.T, dm.H
    tq = min(tq, T)
    nq = T // tq
    qi_tab, kj_tab = _fox_pairs(nq, kv_major=True)
    npairs = int(qi_tab.shape[0])
    x_ins, x_outs, x_sems, n_in, n_out = _fused_parts(fused)
    ns = FOX_BWD_SPLIT if tq % (16 * FOX_BWD_SPLIT) == 0 else 1
    rq = tq // ns

    def body(qi_ref, kj_ref, q_ref, k_ref, v_ref, ck_ref, lse_ref, dl_ref, do_ref, *rest):
        x_in = rest[:n_in]
        dq_ref, dk_ref, dv_ref, dc_ref, dr_ref = rest[n_in:n_in + 5]
        x_out = rest[n_in + 5:n_in + 5 + n_out]
        dk_acc, dv_acc, dc_acc = rest[n_in + 5 + n_out:n_in + 8 + n_out]
        hd = pl.program_id(0)
        t = pl.program_id(1)
        i = qi_ref[t]
        kj = kj_ref[t]
        if fused is not None:
            start, middle, finish = fused.hooks(x_in, x_out, *rest[n_in + 8 + n_out:])
            pl.when((hd == 0) & (t == 0))(start)
            pl.when((hd == H // 2) & (t == 0))(middle)

        @pl.when(t == 0)
        def _():
            dq_ref[...] = jnp.zeros_like(dq_ref)
            dr_ref[...] = jnp.zeros_like(dr_ref)

        @pl.when(i == kj)
        def _():
            dk_acc[...] = jnp.zeros_like(dk_acc)
            dv_acc[...] = jnp.zeros_like(dv_acc)
            dc_acc[...] = jnp.zeros_like(dc_acc)

        def step(on_diagonal):
            kb = k_ref[...].astype(BF16)
            vb = v_ref[...].astype(BF16)
            ck = ck_ref[0]
            q_all, do_all, lse_all, dl_all = q_ref[...], do_ref[...], lse_ref[0], dl_ref[0]
            dq_g, dr_g, dv_c, dk_c, dc_c = [], [], None, None, None
            for g in range(ns):
                rows = slice(g * rq, (g + 1) * rq)
                qs = (q_all[rows] * (HEAD ** -0.5)).astype(BF16)
                s = _nt(qs, kb) - ck
                if on_diagonal:
                    row = g * rq + lax.broadcasted_iota(jnp.int32, s.shape, 0)
                    s = jnp.where(row >= lax.broadcasted_iota(jnp.int32, s.shape, 1), s, NEG)
                dob = do_all[rows].astype(BF16)
                p = jnp.exp(s - lse_all[rows])
                ds = p * (_nt(dob, vb) - dl_all[rows])
                dsb = ds.astype(BF16)
                dq_g.append(_nn(dsb, kb) * (HEAD ** -0.5))
                dr_g.append(jnp.sum(ds, axis=-1, keepdims=True))
                dv_g, dk_g, dc_g = _tn(p.astype(BF16), dob), _tn(dsb, qs), _colsum(ds)
                dv_c = dv_g if dv_c is None else dv_c + dv_g
                dk_c = dk_g if dk_c is None else dk_c + dk_g
                dc_c = dc_g if dc_c is None else dc_c + dc_g
            dv_acc[...] += dv_c
            dk_acc[...] += dk_c
            dc_acc[...] -= dc_c
            out_rows = pl.ds(pl.multiple_of(i * tq, tq), tq)
            dq_ref[out_rows, :] += jnp.concatenate(dq_g, axis=0)
            dr_ref[0, out_rows, :] += jnp.concatenate(dr_g, axis=0)

        @pl.when(i == kj)
        def _():
            step(True)

        @pl.when(i > kj)
        def _():
            step(False)

        @pl.when(i == nq - 1)
        def _():
            dk_ref[...] = dk_acc[...].astype(BF16)
            dv_ref[...] = dv_acc[...].astype(BF16)
            dc_ref[0] = dc_acc[...]

        if fused is not None:
            pl.when((hd == H - 1) & (t == npairs - 1))(finish)

    qcol = pl.BlockSpec((1, tq, 1), lambda h, t, qi, kj: (h, qi[t], 0))
    outs = pl.pallas_call(
        body, name="fox_bwd",
        grid_spec=pltpu.PrefetchScalarGridSpec(
            num_scalar_prefetch=2, grid=(H, npairs),
            in_specs=[pl.BlockSpec((tq, HEAD), lambda h, t, qi, kj: (qi[t], 4 * H + h)),
                      pl.BlockSpec((tq, HEAD), lambda h, t, qi, kj: (kj[t], 5 * H + h)),
                      pl.BlockSpec((tq, HEAD), lambda h, t, qi, kj: (kj[t], 6 * H + h)),
                      pl.BlockSpec((1, 1, tq), lambda h, t, qi, kj: (h, 0, kj[t])),
                      qcol, qcol,
                      pl.BlockSpec((tq, HEAD), lambda h, t, qi, kj: (qi[t], h))] + [HBM] * n_in,
            out_specs=[pl.BlockSpec((T, HEAD), lambda h, t, qi, kj: (0, h)),
                       pl.BlockSpec((tq, HEAD), lambda h, t, qi, kj: (kj[t], h)),
                       pl.BlockSpec((tq, HEAD), lambda h, t, qi, kj: (kj[t], h)),
                       pl.BlockSpec((1, 1, tq), lambda h, t, qi, kj: (h, 0, kj[t])),
                       pl.BlockSpec((1, T, 1), lambda h, t, qi, kj: (h, 0, 0))] + [HBM] * n_out,
            scratch_shapes=[pltpu.VMEM((tq, HEAD), F32), pltpu.VMEM((tq, HEAD), F32), pltpu.VMEM((1, tq), F32)]
            + list(x_sems)),
        out_shape=[jax.ShapeDtypeStruct((T, dm.AW), F32), jax.ShapeDtypeStruct((T, dm.AW), BF16),
                   jax.ShapeDtypeStruct((T, dm.AW), BF16), jax.ShapeDtypeStruct((H, 1, T), F32),
                   jax.ShapeDtypeStruct((H, T, 1), F32)] + list(x_outs),
        compiler_params=_cparams(("arbitrary", "arbitrary")),
    )(qi_tab, kj_tab, main, main, main, cum_row, lse, delta, do, *x_ins)
    return outs[:5], tuple(outs[5:])


FOX_QG = 256


def _fox_sT(kb, q_rows, ck, g, on_diagonal):
    qs = (q_rows * (HEAD ** -0.5)).astype(BF16)
    reps = q_rows.shape[0] // HEAD
    sT = _nt(kb, qs) - (jnp.concatenate([ck] * reps, axis=1) if reps > 1 else ck)
    if on_diagonal:
        key = lax.broadcasted_iota(jnp.int32, sT.shape, 0)
        qry = g * q_rows.shape[0] + lax.broadcasted_iota(jnp.int32, sT.shape, 1)
        sT = jnp.where(key <= qry, sT, NEG)
    return sT, qs


def _foxt_fwd(main, ckb, dm, fused=None, tq=1024):
    T, H = dm.T, dm.H
    tq = min(tq, T)
    nq = T // tq
    qg = min(FOX_QG, tq)
    ns = tq // qg
    qi_tab, kj_tab = _fox_pairs(nq, kv_major=False)
    npairs = int(qi_tab.shape[0])
    x_ins, x_outs, x_sems, n_in, n_out = _fused_parts(fused)

    def body(qi_ref, kj_ref, q_ref, k_ref, v_ref, ck_ref, *rest):
        x_in, (o_ref, lse_ref) = rest[:n_in], rest[n_in:n_in + 2]
        x_out = rest[n_in + 2:n_in + 2 + n_out]
        m_ref, l_ref, acc_ref = rest[n_in + 2 + n_out:n_in + 5 + n_out]
        hd = pl.program_id(0)
        t = pl.program_id(1)
        i = qi_ref[t]
        j = kj_ref[t]
        if fused is not None:
            start, middle, finish = fused.hooks(x_in, x_out, *rest[n_in + 5 + n_out:])
            pl.when((hd == 0) & (t == 0))(start)
            pl.when((hd == H // 2) & (t == 0))(middle)

        @pl.when(j == 0)
        def _():
            m_ref[...] = jnp.full_like(m_ref, NEG)
            l_ref[...] = jnp.zeros_like(l_ref)
            acc_ref[...] = jnp.zeros_like(acc_ref)

        def step(on_diagonal):
            kb = k_ref[...].astype(BF16)
            vb = v_ref[...].astype(BF16)
            ck = ck_ref[0]
            scores = [_fox_sT(kb, q_ref[g * qg:(g + 1) * qg, :], ck, g, on_diagonal)[0] for g in range(ns)]
            for g in range(ns):
                cols = slice(g * qg, (g + 1) * qg)
                sT = scores[g]
                m_old = m_ref[:, cols]
                m_new = jnp.maximum(m_old, jnp.max(sT, axis=0, keepdims=True))
                a = jnp.exp(m_old - m_new)
                pT = jnp.exp(sT - m_new)
                l_ref[:, cols] = a * l_ref[:, cols] + jnp.sum(pT, axis=0, keepdims=True)
                acc_ref[:, cols] = a * acc_ref[:, cols] + _tn(vb, pT.astype(BF16))
                m_ref[:, cols] = m_new

        @pl.when(j < i)
        def _():
            step(False)

        @pl.when(j == i)
        def _():
            step(True)
            l = l_ref[...]
            o_ref[...] = acc_ref[...] / l
            lse_ref[0] = m_ref[...] + jnp.log(l)

        if fused is not None:
            pl.when((hd == H - 1) & (t == npairs - 1))(finish)

    outs = pl.pallas_call(
        body, name="fox_fwd",
        grid_spec=pltpu.PrefetchScalarGridSpec(
            num_scalar_prefetch=2, grid=(H, npairs),
            in_specs=[pl.BlockSpec((tq, HEAD), lambda h, t, qi, kj: (qi[t], 4 * H + h)),
                      pl.BlockSpec((tq, HEAD), lambda h, t, qi, kj: (kj[t], 5 * H + h)),
                      pl.BlockSpec((tq, HEAD), lambda h, t, qi, kj: (kj[t], 6 * H + h)),
                      pl.BlockSpec((1, tq, HEAD), lambda h, t, qi, kj: (h, kj[t], 0))] + [HBM] * n_in,
            out_specs=[pl.BlockSpec((HEAD, tq), lambda h, t, qi, kj: (h, qi[t])),
                       pl.BlockSpec((1, 1, tq), lambda h, t, qi, kj: (h, 0, qi[t]))] + [HBM] * n_out,
            scratch_shapes=[pltpu.VMEM((1, tq), F32), pltpu.VMEM((1, tq), F32), pltpu.VMEM((HEAD, tq), F32)]
            + list(x_sems)),
        out_shape=[jax.ShapeDtypeStruct((dm.AW, T), F32), jax.ShapeDtypeStruct((H, 1, T), F32)] + list(x_outs),
        compiler_params=_cparams(("arbitrary", "arbitrary")),
    )(qi_tab, kj_tab, main, main, main, ckb, *x_ins)
    return outs[0], outs[1], tuple(outs[2:])


def _foxt_delta(doT, oT, dm, tc=1024):
    T, H = dm.T, dm.H
    tc = min(tc, T)

    def body(do_ref, o_ref, d_ref):
        d_ref[0] = jnp.sum(do_ref[...] * o_ref[...], axis=0, keepdims=True)

    blk = pl.BlockSpec((HEAD, tc), lambda h, i: (h, i))
    return pl.pallas_call(
        body, name="fox_delta", grid=(H, T // tc),
        in_specs=[blk, blk], out_specs=pl.BlockSpec((1, 1, tc), lambda h, i: (h, 0, i)),
        out_shape=jax.ShapeDtypeStruct((H, 1, T), F32),
        compiler_params=_cparams(("parallel", "parallel")),
    )(doT, oT)


def _foxt_bwd(main, ckb, lse, delta, doT, dm, fused=None, tq=1024):
    T, H = dm.T, dm.H
    tq = min(tq, T)
    nq = T // tq
    qg = min(FOX_QG, tq)
    ns = tq // qg
    qi_tab, kj_tab = _fox_pairs(nq, kv_major=True)
    npairs = int(qi_tab.shape[0])
    x_ins, x_outs, x_sems, n_in, n_out = _fused_parts(fused)

    def body(qi_ref, kj_ref, q_ref, k_ref, v_ref, ck_ref, lse_ref, dl_ref, do_ref, *rest):
        x_in = rest[:n_in]
        dq_ref, dk_ref, dv_ref, dr_ref, dc_ref = rest[n_in:n_in + 5]
        x_out = rest[n_in + 5:n_in + 5 + n_out]
        dq_acc, dk_acc, dv_acc = rest[n_in + 5 + n_out:n_in + 8 + n_out]
        hd = pl.program_id(0)
        t = pl.program_id(1)
        i = qi_ref[t]
        kj = kj_ref[t]
        if fused is not None:
            start, middle, finish = fused.hooks(x_in, x_out, *rest[n_in + 8 + n_out:])
            pl.when((hd == 0) & (t == 0))(start)
            pl.when((hd == H // 2) & (t == 0))(middle)

        @pl.when(t == 0)
        def _():
            dq_acc[...] = jnp.zeros_like(dq_acc)
            dr_ref[...] = jnp.zeros_like(dr_ref)

        @pl.when(i == kj)
        def _():
            dk_acc[...] = jnp.zeros_like(dk_acc)
            dv_acc[...] = jnp.zeros_like(dv_acc)

        def step(on_diagonal):
            kb = k_ref[...].astype(BF16)
            vb = v_ref[...].astype(BF16)
            ck = ck_ref[0]
            ones = jnp.ones((qg, HEAD), BF16)
            dq_g, dr_g = [], []
            ahead = []
            for g in range(ns):
                cols = slice(g * qg, (g + 1) * qg)
                sT, qs = _fox_sT(kb, q_ref[cols, :], ck, g, on_diagonal)
                dob = do_ref[:, cols].astype(BF16)
                ahead.append((sT, qs, dob, _nn(vb, dob)))
            for g in range(ns):
                cols = slice(g * qg, (g + 1) * qg)
                sT, qs, dob, dpT = ahead[g]
                pT = jnp.exp(sT - lse_ref[0, :, cols])
                dsT = pT * (dpT - dl_ref[0, :, cols])
                dsb = dsT.astype(BF16)
                dv_acc[...] += _nt(pT.astype(BF16), dob)
                dk_acc[...] += _nn(dsb, jnp.concatenate([qs, ones], axis=1))
                dq_g.append(_tn(kb, dsb) * (HEAD ** -0.5))
                dr_g.append(_nn(jnp.ones((8, tq), BF16), dsb)[0:1])
            dq_acc[i] += jnp.concatenate(dq_g, axis=1)
            dr_ref[0, pl.ds(i, 1), :] += jnp.concatenate(dr_g, axis=1)

        @pl.when(i == kj)
        def _():
            step(True)

        @pl.when(i > kj)
        def _():
            step(False)

        @pl.when(i == nq - 1)
        def _():
            acc = dk_acc[...]
            dk_ref[...] = acc[:, :HEAD].astype(BF16)
            dc_ref[0] = acc[:, HEAD:HEAD + 1]
            dv_ref[...] = dv_acc[...].astype(BF16)

        @pl.when(t == npairs - 1)
        def _():
            for b in range(nq):
                dq_ref[b * tq:(b + 1) * tq, :] = dq_acc[b].T.astype(BF16)

        if fused is not None:
            pl.when((hd == H - 1) & (t == npairs - 1))(finish)

    qrow = pl.BlockSpec((1, 1, tq), lambda h, t, qi, kj: (h, 0, qi[t]))
    outs = pl.pallas_call(
        body, name="fox_bwd",
        grid_spec=pltpu.PrefetchScalarGridSpec(
            num_scalar_prefetch=2, grid=(H, npairs),
            in_specs=[pl.BlockSpec((tq, HEAD), lambda h, t, qi, kj: (qi[t], 4 * H + h)),
                      pl.BlockSpec((tq, HEAD), lambda h, t, qi, kj: (kj[t], 5 * H + h)),
                      pl.BlockSpec((tq, HEAD), lambda h, t, qi, kj: (kj[t], 6 * H + h)),
                      pl.BlockSpec((1, tq, HEAD), lambda h, t, qi, kj: (h, kj[t], 0)),
                      qrow, qrow,
                      pl.BlockSpec((HEAD, tq), lambda h, t, qi, kj: (h, qi[t]))] + [HBM] * n_in,
            out_specs=[pl.BlockSpec((T, HEAD), lambda h, t, qi, kj: (0, h)),
                       pl.BlockSpec((tq, HEAD), lambda h, t, qi, kj: (kj[t], h)),
                       pl.BlockSpec((tq, HEAD), lambda h, t, qi, kj: (kj[t], h)),
                       pl.BlockSpec((1, nq, tq), lambda h, t, qi, kj: (h, 0, 0)),
                       pl.BlockSpec((1, tq, 1), lambda h, t, qi, kj: (h, kj[t], 0))] + [HBM] * n_out,
            scratch_shapes=[pltpu.VMEM((nq, HEAD, tq), F32), pltpu.VMEM((tq, 2 * HEAD), F32),
                            pltpu.VMEM((tq, HEAD), F32)] + list(x_sems)),
        out_shape=[jax.ShapeDtypeStruct((T, dm.AW), BF16), jax.ShapeDtypeStruct((T, dm.AW), BF16),
                   jax.ShapeDtypeStruct((T, dm.AW), BF16), jax.ShapeDtypeStruct((H, nq, tq), F32),
                   jax.ShapeDtypeStruct((H, T, 1), F32)] + list(x_outs),
        compiler_params=_cparams(("arbitrary", "arbitrary")),
    )(qi_tab, kj_tab, main, main, main, ckb, lse, delta, doT, *x_ins)
    return outs[:5], tuple(outs[5:])


def _local_step(dm, x, tgt, w_main, w_rest, later_weights, bias_p, lb_logits, norm_w, g1, g2, g3, g4,
                reduce_hooks=None):
    T, D, H = dm.T, dm.D, dm.H
    h1, r1 = _rms_fwd(x, g1)
    main = _mm(h1, w_main, "nn", F32, "proj_main")
    rest = _mm(h1, w_rest, "nn", F32, "proj_rest", tn=1408)
    ya, o_a, states = _hgrn_fwd(main, lb_logits, norm_w, dm)
    cum = _cum_fwd(rest, bias_p, dm)
    ckb = jnp.broadcast_to(cum[:, :H].T[:, :, None], (H, T, HEAD))
    if isinstance(later_weights, Fused):
        oT_b, lse, later_weights = _foxt_fwd(main, ckb, dm, fused=later_weights)
    else:
        oT_b, lse, _ = _foxt_fwd(main, ckb, dm)
    w_up_a, w_up_b, w_o, w_ffn_in, w_ffn_down = later_weights
    y_a = _mm(ya, w_up_a, "nn", F32, "up_a")
    y_b = _mm(oT_b, w_up_b, "tn", F32, "up_b")
    merged = _merge_fwd(rest, y_a, y_b)
    u = _mm(merged, w_o, "nn", F32, "w_o")
    x1, r2, h3, r3 = _post_pre(x, u, g2, g3)
    gate, up, act = _ffn_in_swiglu(h3, w_ffn_in)
    w = _mm(act, w_ffn_down, "nn", F32, "ffn_down")
    loss, dy, dw, dg4 = _loss_bwd(x1, w, g4, tgt)
    dgate, dup = _d_act_swiglu(dw, w_ffn_down, gate, up)
    d_ffn_down = _mm(act, dw, "tn", F32, "dw_ffn_down", tm=1408)
    dh3 = _mm(dgate, w_ffn_in[:, :dm.FF], "nt", F32, "d_h3_gate")
    dh3 = _mm(dup, w_ffn_in[:, dm.FF:], "nt", F32, "d_h3_up", add=dh3)
    d_ffn_in = jnp.concatenate([_mm(h3, dgate, "tn", F32, "dw_ffn_gate", tn=1408),
                                _mm(h3, dup, "tn", F32, "dw_ffn_up", tn=1408)], axis=1)
    dx1, du, dg3, dg2 = _rms_bwd2(dy, dh3, x1, r3, g3, u, r2, g2)
    dmerged = _mm(du, w_o, "nt", F32, "d_merged")
    d_w_o = _mm(merged, du, "tn", F32, "dw_o")
    dy_a, dy_b, dgates = _merge_bwd(dmerged, rest, y_a, y_b)
    dya = _mm(dy_a, w_up_a, "nt", F32, "d_ya")
    d_up_a = _mm(ya, dy_a, "tn", F32, "dw_up_a")
    doT = _mm(w_up_b, dy_b, "nt", F32, "d_ob")
    d_up_b = _mm(oT_b, dy_b, "nn", F32, "dw_up_b")
    later_grads = [d_up_a, d_up_b, d_w_o, d_ffn_in, d_ffn_down]
    d_a, dlb, dnw_h, got = _hgrn_bwd(main, lb_logits, norm_w, o_a, states, dya, dm,
                                     fused=reduce_hooks.swap_later(later_grads) if reduce_hooks else None)
    (dq_b, dk_b, dv_b, d_over_keys, d_over_queries), later_from_chips = _foxt_bwd(
        main, ckb, lse, _foxt_delta(doT, oT_b, dm), doT, dm,
        fused=reduce_hooks.scatter_later(got) if reduce_hooks else None)
    dcum = jnp.pad((d_over_keys.reshape(H, T) - d_over_queries.reshape(H, T)).T, ((0, 0), (0, HEAD - H)))
    dbf, dbias = _cum_bwd(dcum, rest, bias_p, dm)
    dmain = jnp.concatenate([d_a, dq_b, dk_b, dv_b], axis=1)
    drest = jnp.concatenate([dgates, dbf], axis=1)
    d_main = _mm(h1, dmain, "tn", F32, "dw_main")
    d_rest = _mm(h1, drest, "tn", F32, "dw_rest", tn=1408)
    if reduce_hooks:
        dh1, w_in_from_chips = _mm(dmain, w_main, "nt", F32, "d_h1_main",
                                   fused=reduce_hooks.scatter_w_in(d_main, d_rest))
    else:
        dh1, w_in_from_chips = _mm(dmain, w_main, "nt", F32, "d_h1_main"), ()
    dh1 = _mm(drest, w_rest, "nt", F32, "d_h1_rest", add=dh1)
    dx, dg1 = _rms_bwd1(dx1, dh1, x, r1, g1)
    big = dict(main=d_main, rest=d_rest, up_a=d_up_a, up_b=d_up_b, w_o=d_w_o, ffn_in=d_ffn_in, ffn_down=d_ffn_down)
    small = dict(loss=loss, bias=dbias, norm_w=jnp.sum(dnw_h, axis=0), lb=dlb, g1=dg1, g2=dg2, g3=dg3, g4=dg4)
    return dx, big, small, tuple(w_in_from_chips) + tuple(later_from_chips)


HBM = pl.BlockSpec(memory_space=pltpu.HBM)


def _place():
    x, y, c = lax.axis_index("x"), lax.axis_index("y"), lax.axis_index("c")
    chips = [(1 - x, y), (x, 1 - y), (1 - x, 1 - y)]
    return x, y, c, chips


class Sharded(NamedTuple):
    kind: str
    r: int
    c: int

    @property
    def full(self):
        return (self.r, 4 * self.c) if self.kind == "col" else (4 * self.r, self.c)

    @property
    def half(self):
        return (self.r // 2, self.c) if self.kind == "col" else (self.r, self.c // 2)

    @property
    def half_of_full(self):
        return (self.r // 2, 4 * self.c) if self.kind == "col" else (4 * self.r, self.c // 2)

    def shard_window(self, ref, s):
        if self.kind == "col":
            return ref.at[:, pl.ds(pl.multiple_of(s * self.c, 128), self.c)]
        return ref.at[pl.ds(pl.multiple_of(s * self.r, 16), self.r), :]

    def half_window(self, ref, s, h):
        if self.kind == "col":
            return ref.at[pl.ds(pl.multiple_of(h * (self.r // 2), 16), self.r // 2),
                          pl.ds(pl.multiple_of(s * self.c, 128), self.c)]
        return ref.at[pl.ds(pl.multiple_of(s * self.r, 16), self.r),
                      pl.ds(pl.multiple_of(h * (self.c // 2), 128), self.c // 2)]

    def half_of(self, ref, h):
        if self.kind == "col":
            n = ref.shape[0] // 2
            return ref.at[pl.ds(pl.multiple_of(h * n, 16), n), :]
        n = ref.shape[1] // 2
        return ref.at[:, pl.ds(pl.multiple_of(h * n, 128), n)]

    def window_of_half(self, ref, s):
        if self.kind == "col":
            return ref.at[:, pl.ds(pl.multiple_of(s * self.c, 128), self.c)]
        return ref.at[pl.ds(pl.multiple_of(s * self.r, 16), self.r), :]


def _gather_hooks(specs):
    n = len(specs)

    def hooks(w_refs, f_refs, send_sems, recv_sems):
        x, y, c, chips = _place()
        q = 2 * x + y
        sibling = (x, y, 1 - c)

        def copy(t, k, src, dst, to):
            return pltpu.make_async_remote_copy(
                src_ref=src, dst_ref=dst, send_sem=send_sems.at[7 * t + k], recv_sem=recv_sems.at[7 * t + k],
                device_id=to, device_id_type=MESH)

        def over_ici(t, j, chip_from, to):
            src = specs[t].half_of(w_refs[t], c)
            return copy(t, j, src, specs[t].half_window(f_refs[t], 2 * chip_from[0] + chip_from[1], c), to)

        def passed_on(t, j, chip_from, half):
            win = specs[t].half_window(f_refs[t], 2 * chip_from[0] + chip_from[1], half)
            return copy(t, 3 + j, win, win, sibling)

        def own_shard(t):
            return copy(t, 6, w_refs[t], specs[t].shard_window(f_refs[t], q), sibling)

        def start():
            for t in range(n):
                own_shard(t).start()
                for j, chip in enumerate(chips):
                    over_ici(t, j, (x, y), (*chip, c)).start()

        def middle():
            for t in range(n):
                for j, chip in enumerate(chips):
                    over_ici(t, j, chip, sibling).wait_recv()
                    passed_on(t, j, chip, c).start()

        def finish():
            for t in range(n):
                own_shard(t).wait_recv()
                for j, chip in enumerate(chips):
                    passed_on(t, j, chip, 1 - c).wait_recv()
            for t in range(n):
                own_shard(t).wait_send()
                for j, chip in enumerate(chips):
                    over_ici(t, j, (x, y), (*chip, c)).wait_send()
                    passed_on(t, j, chip, c).wait_send()

        return start, middle, finish

    sems = (pltpu.SemaphoreType.DMA((7 * n,)), pltpu.SemaphoreType.DMA((7 * n,)))
    return hooks, sems


def _gather_fused(shards, specs):
    hooks, sems = _gather_hooks(specs)
    outs = tuple(jax.ShapeDtypeStruct(sp.full, w.dtype) for sp, w in zip(specs, shards))
    return Fused(ins=tuple(shards), outs=outs, sems=sems, hooks=hooks)


def _run_fused(fused, name):
    n_in, n_out = len(fused.ins), len(fused.outs)

    def body(*refs):
        start, middle, finish = fused.hooks(refs[:n_in], refs[n_in:n_in + n_out], *refs[n_in + n_out:])
        start()
        middle()
        finish()

    return pl.pallas_call(
        body, name=name, in_specs=[HBM] * n_in, out_specs=[HBM] * n_out,
        out_shape=list(fused.outs), scratch_shapes=list(fused.sems),
    )(*fused.ins)


def _swap_fused(fulls, specs):
    n = len(fulls)

    def hooks(g_refs, o_refs, send_sems, recv_sems):
        x, y, c, _ = _place()

        def copies():
            return [pltpu.make_async_remote_copy(
                src_ref=specs[t].half_of(g_refs[t], 1 - c), dst_ref=o_refs[t],
                send_sem=send_sems.at[t], recv_sem=recv_sems.at[t],
                device_id=(x, y, 1 - c), device_id_type=MESH) for t in range(n)]

        def start():
            for cp in copies():
                cp.start()

        def finish():
            for cp in copies():
                cp.wait_recv()
            for cp in copies():
                cp.wait_send()

        return start, lambda: None, finish

    outs = tuple(jax.ShapeDtypeStruct(sp.half_of_full, g.dtype) for sp, g in zip(specs, fulls))
    sems = (pltpu.SemaphoreType.DMA((n,)), pltpu.SemaphoreType.DMA((n,)))
    return Fused(ins=tuple(fulls), outs=outs, sems=sems, hooks=hooks)


def _rtile(n, target):
    t = min(n, (target // 16) * 16)
    while n % t:
        t -= 16
    return t


def _add_sibling(full, got, sp, cq_idx, name):
    hr, hc = sp.half
    tr = _rtile(hr, 256)
    nrt = hr // tr

    def body(cq_ref, g_ref, r_ref, ob_ref, of_ref):
        s = pl.program_id(1)
        v = g_ref[...] + r_ref[...]
        ob_ref[...] = v.astype(BF16)

        @pl.when(s == cq_ref[1])
        def _():
            of_ref[...] = v

    if sp.kind == "col":
        g_spec = pl.BlockSpec((tr, hc), lambda i, s, cq: (cq[0] * nrt + i, s))
        r_spec = pl.BlockSpec((tr, hc), lambda i, s, cq: (i, s))
    else:
        g_spec = pl.BlockSpec((tr, hc), lambda i, s, cq: (s * nrt + i, cq[0]))
        r_spec = pl.BlockSpec((tr, hc), lambda i, s, cq: (s * nrt + i, 0))
    return pl.pallas_call(
        body, name=name,
        grid_spec=pltpu.PrefetchScalarGridSpec(
            num_scalar_prefetch=1, grid=(nrt, 4),
            in_specs=[g_spec, r_spec],
            out_specs=[r_spec, pl.BlockSpec((tr, hc), lambda i, s, cq: (i, 0))]),
        out_shape=[jax.ShapeDtypeStruct(sp.half_of_full, BF16), jax.ShapeDtypeStruct(sp.half, F32)],
        compiler_params=_cparams(("parallel", "arbitrary")),
    )(cq_idx, full, got)


def _scatter_fused(sums, specs):
    n = len(sums)

    def hooks(a_refs, o_refs, send_sems, recv_sems):
        x, y, c, chips = _place()

        def copies():
            return [pltpu.make_async_remote_copy(
                src_ref=specs[t].window_of_half(a_refs[t], 2 * chip[0] + chip[1]), dst_ref=o_refs[t].at[j],
                send_sem=send_sems.at[3 * t + j], recv_sem=recv_sems.at[3 * t + j],
                device_id=(*chip, c), device_id_type=MESH) for t in range(n) for j, chip in enumerate(chips)]

        def start():
            for cp in copies():
                cp.start()

        def finish():
            for cp in copies():
                cp.wait_recv()
            for cp in copies():
                cp.wait_send()

        return start, lambda: None, finish

    outs = tuple(jax.ShapeDtypeStruct((3,) + sp.half, a.dtype) for sp, a in zip(specs, sums))
    sems = (pltpu.SemaphoreType.DMA((3 * n,)), pltpu.SemaphoreType.DMA((3 * n,)))
    return Fused(ins=tuple(sums), outs=outs, sems=sems, hooks=hooks)


def _add_chips(own, got, sp, cq_idx, name):
    hr, hc = sp.half
    tr = _rtile(hr, 256)
    nrt = hr // tr

    def body(cq_ref, a_ref, r_ref, o_ref):
        o_ref[...] = ((a_ref[...] + r_ref[0].astype(F32)) + r_ref[1].astype(F32)) + r_ref[2].astype(F32)

    if sp.kind == "col":
        o_spec = pl.BlockSpec((tr, hc), lambda i, cq: (cq[0] * nrt + i, 0))
    else:
        o_spec = pl.BlockSpec((tr, hc), lambda i, cq: (i, cq[0]))
    return pl.pallas_call(
        body, name=name,
        grid_spec=pltpu.PrefetchScalarGridSpec(
            num_scalar_prefetch=1, grid=(nrt,),
            in_specs=[pl.BlockSpec((tr, hc), lambda i, cq: (i, 0)), pl.BlockSpec((3, tr, hc), lambda i, cq: (0, i, 0))],
            out_specs=o_spec),
        out_shape=jax.ShapeDtypeStruct((sp.r, sp.c), F32),
        compiler_params=_cparams(("parallel",)),
    )(cq_idx, own, got)


def _share_halves(shards, specs):
    n = len(shards)

    def body(*refs):
        o_refs = refs[n:2 * n]
        send_sems, recv_sems = refs[2 * n:]
        x, y, c, _ = _place()

        def copy(t, half):
            win = specs[t].half_of(o_refs[t], half)
            return pltpu.make_async_remote_copy(
                src_ref=win, dst_ref=win, send_sem=send_sems.at[t], recv_sem=recv_sems.at[t],
                device_id=(x, y, 1 - c), device_id_type=MESH)

        for t in range(n):
            copy(t, c).start()
        for t in range(n):
            copy(t, 1 - c).wait_recv()
        for t in range(n):
            copy(t, c).wait_send()

    return pl.pallas_call(
        body, name="share_halves",
        in_specs=[HBM] * n, out_specs=[HBM] * n,
        out_shape=[jax.ShapeDtypeStruct((sp.r, sp.c), F32) for sp in specs],
        input_output_aliases={t: t for t in range(n)},
        scratch_shapes=[pltpu.SemaphoreType.DMA((n,)), pltpu.SemaphoreType.DMA((n,))],
    )(*shards)


def _sum_small(vec):
    rows, w = vec.shape

    def body(v_ref, o_ref, buf, send_sems, recv_sems):
        x, y, c, _ = _place()
        me = 4 * x + 2 * y + c
        buf[me] = v_ref[...]
        cps = []
        for k in range(1, 8):
            to = (x ^ (k >> 2), y ^ ((k >> 1) & 1), c ^ (k & 1))
            cps.append(pltpu.make_async_remote_copy(
                src_ref=v_ref, dst_ref=buf.at[me], send_sem=send_sems.at[k - 1], recv_sem=recv_sems.at[k - 1],
                device_id=to, device_id_type=MESH))
        for cp in cps:
            cp.start()
        for k in range(1, 8):
            pltpu.make_async_remote_copy(
                src_ref=v_ref, dst_ref=buf.at[me ^ k], send_sem=send_sems.at[k - 1], recv_sem=recv_sems.at[k - 1],
                device_id=(x, y, c), device_id_type=MESH).wait_recv()
        for cp in cps:
            cp.wait_send()
        total = buf[0]
        for d in range(1, 8):
            total = total + buf[d]
        o_ref[...] = total

    return pl.pallas_call(
        body, name="sum_small",
        in_specs=[pl.BlockSpec(memory_space=pltpu.VMEM)], out_specs=pl.BlockSpec(memory_space=pltpu.VMEM),
        out_shape=jax.ShapeDtypeStruct((rows, w), F32),
        scratch_shapes=[pltpu.VMEM((8, rows, w), F32), pltpu.SemaphoreType.DMA((7,)), pltpu.SemaphoreType.DMA((7,))],
    )(vec)


def _adam_math(w, g, m, v):
    m = ADAM_B1 * m + (1.0 - ADAM_B1) * g
    v = ADAM_B2 * v + (1.0 - ADAM_B2) * (g * g)
    m_hat = m / (1.0 - ADAM_B1 ** ADAM_STEP)
    v_hat = v / (1.0 - ADAM_B2 ** ADAM_STEP)
    delta = -ADAM_LR * (m_hat / (jnp.sqrt(v_hat) + ADAM_EPS) + ADAM_WD * w)
    return delta, m, v


def _adamw(w, g, m, v, name, tr=128):
    R, Cn = w.shape
    tr = min(tr, R)
    assert R % tr == 0

    def body(w_ref, g_ref, m_ref, v_ref, go_ref, d_ref, mo_ref, vo_ref):
        gv = g_ref[...]
        d, mn, vn = _adam_math(w_ref[...], gv, m_ref[...], v_ref[...])
        go_ref[...] = gv
        d_ref[...] = d
        mo_ref[...] = mn
        vo_ref[...] = vn

    blk = pl.BlockSpec((tr, Cn), lambda i: (i, 0))
    sds = jax.ShapeDtypeStruct((R, Cn), F32)
    return pl.pallas_call(
        body, name=name, grid=(R // tr,),
        in_specs=[blk] * 4, out_specs=[blk] * 4, out_shape=[sds] * 4,
        compiler_params=_cparams(("parallel",)),
    )(w, g, m, v)


ROW_LOSS, ROW_BIAS, ROW_NORM, ROW_LB0, ROW_G1, ROW_G2, ROW_G3, ROW_G4, ROW_LB1 = range(9)
SMALL_ROWS = 16


def _small_update(gsum, wp, mp, vp):
    _, w = gsum.shape

    def body(g_ref, w_ref, m_ref, v_ref, go_ref, d_ref, mo_ref, vo_ref):
        wv = w_ref[...]
        l0 = wv[ROW_LB0:ROW_LB0 + 1, :]
        l1 = wv[ROW_LB1:ROW_LB1 + 1, :]
        mx = jnp.maximum(l0, l1)
        e0 = jnp.exp(l0 - mx)
        e1 = jnp.exp(l1 - mx)
        p0 = e0 / (e0 + e1)
        gs = g_ref[...]
        dl0 = gs[ROW_LB0:ROW_LB0 + 1, :] * p0 * (1.0 - p0)
        row8 = lax.broadcasted_iota(jnp.int32, gs.shape, 0)
        top = jnp.where(row8 == ROW_LB0, dl0, jnp.where(row8 == ROW_LOSS, 0.0, gs))
        bot = jnp.where(row8 == ROW_LB1 - 8, -dl0, 0.0)
        g16 = jnp.concatenate([top, bot], axis=0)
        d, mn, vn = _adam_math(wv, g16, m_ref[...], v_ref[...])
        go_ref[...] = g16
        d_ref[...] = d
        mo_ref[...] = mn
        vo_ref[...] = vn

    sds = jax.ShapeDtypeStruct((SMALL_ROWS, w), F32)
    full = pl.BlockSpec(memory_space=pltpu.VMEM)
    return pl.pallas_call(
        body, name="small_update", in_specs=[full] * 4, out_specs=[full] * 4, out_shape=[sds] * 4,
    )(gsum, wp, mp, vp)


def _w_in_layout(dm):
    cs = dm.NIN // 4
    place = [((cs * q) // 128, (cs * q) % 128) for q in range(4)]
    cp = -(-(cs + max(sh for _, sh in place)) // 128) * 128
    return cs, cp, place


def _zeros_cols(rows, n, dtype):
    return jnp.zeros((rows, n), dtype)


def _a_part_tiles(v, H, to_head_major):
    tile = lambda k: v[:, k * HEAD:(k + 1) * HEAD]
    if to_head_major:
        tiles = [tile(s * H + h) for h in range(H) for s in range(4)]
    else:
        tiles = [tile(h * 4 + s) for s in range(4) for h in range(H)]
    return jnp.concatenate(tiles + [v[:, 4 * H * HEAD:]], axis=1)


def _unshuffle_w_in(wp, dm, tr=128):
    D, H = dm.D, dm.H
    cs, cp, place = _w_in_layout(dm)
    nm, nrest = dm.NMAIN, dm.NREST
    ng = nm + nrest
    tr = min(tr, D)

    def body(x_ref, main_ref, rest_ref):
        g = None
        for q, (t0, sh) in enumerate(place):
            xq = x_ref[:, q * cp:(q + 1) * cp]
            yq = pltpu.roll(xq, sh, axis=1) if sh else xq
            width = min(cp, ng - t0 * 128)
            parts = [_zeros_cols(tr, t0 * 128, wp.dtype)] if t0 else []
            parts.append(yq[:, :width])
            if ng - t0 * 128 - width:
                parts.append(_zeros_cols(tr, ng - t0 * 128 - width, wp.dtype))
            placed = jnp.concatenate(parts, axis=1)
            g = placed if g is None else g + placed
        main_ref[...] = _a_part_tiles(g[:, :nm], H, to_head_major=True)
        tail = g[:, nm:]
        gates = pltpu.roll(tail, nrest - H, axis=1)[:, :2 * D]
        lane = lax.broadcasted_iota(jnp.int32, (tr, HEAD), 1)
        rest_ref[...] = jnp.concatenate([gates, jnp.where(lane < H, tail[:, :HEAD], 0)], axis=1)

    return pl.pallas_call(
        body, name="unshuffle_w_in", grid=(D // tr,),
        in_specs=[pl.BlockSpec((tr, 4 * cp), lambda i: (i, 0))],
        out_specs=[pl.BlockSpec((tr, nm), lambda i: (i, 0)), pl.BlockSpec((tr, nrest), lambda i: (i, 0))],
        out_shape=[jax.ShapeDtypeStruct((D, nm), wp.dtype), jax.ShapeDtypeStruct((D, nrest), wp.dtype)],
        compiler_params=_cparams(("parallel",)),
    )(wp)


def _shuffle_w_in(d_main, d_rest, dm, tr=64):
    D, H = dm.D, dm.H
    cs, cp, place = _w_in_layout(dm)
    nm, nrest = dm.NMAIN, dm.NREST
    ng = nm + nrest
    tr = min(tr, D)

    def body(m_ref, r_ref, o_ref):
        rv = r_ref[...]
        lane = lax.broadcasted_iota(jnp.int32, (tr, HEAD), 1)
        tail = pltpu.roll(jnp.concatenate([rv[:, :2 * D], _zeros_cols(tr, HEAD, F32)], axis=1), H, axis=1)
        head = jnp.where(lane < H, rv[:, 2 * D:], 0.0) + tail[:, :HEAD]
        main_std = _a_part_tiles(m_ref[...], H, to_head_major=False)
        g = jnp.concatenate([main_std, head, tail[:, HEAD:]], axis=1)
        lanes = lax.broadcasted_iota(jnp.int32, (tr, cp), 1)
        outs = []
        for q, (t0, sh) in enumerate(place):
            width = min(cp, ng - t0 * 128)
            win = g[:, t0 * 128:t0 * 128 + width]
            if width < cp:
                win = jnp.concatenate([win, _zeros_cols(tr, cp - width, F32)], axis=1)
            xq = pltpu.roll(win, cp - sh, axis=1) if sh else win
            outs.append(jnp.where(lanes < cs, xq, 0.0))
        o_ref[...] = jnp.concatenate(outs, axis=1)

    return pl.pallas_call(
        body, name="shuffle_w_in", grid=(D // tr,),
        in_specs=[pl.BlockSpec((tr, nm), lambda i: (i, 0)), pl.BlockSpec((tr, nrest), lambda i: (i, 0))],
        out_specs=pl.BlockSpec((tr, 4 * cp), lambda i: (i, 0)),
        out_shape=jax.ShapeDtypeStruct((D, 4 * cp), F32),
        compiler_params=_cparams(("parallel",)),
    )(d_main, d_rest)


def _pack_small(dm, bias, norm_w, lb_logits, g1, g2, g3, g4):
    D = dm.D
    row = lambda v: jnp.pad(v.reshape(1, -1), ((0, 0), (0, D - v.size)))
    rows = [jnp.zeros((1, D), F32), row(bias), row(norm_w), row(lb_logits[0]), row(g1), row(g2), row(g3), row(g4),
            row(lb_logits[1]), jnp.zeros((SMALL_ROWS - 9, D), F32)]
    return jnp.concatenate(rows, axis=0)


def _unpack_small(p, dm):
    H, AW = dm.H, dm.AW
    return (p[ROW_BIAS:ROW_BIAS + 1, :H], jnp.concatenate([p[ROW_LB0:ROW_LB0 + 1, :AW], p[ROW_LB1:ROW_LB1 + 1, :AW]], axis=0),
            p[ROW_NORM:ROW_NORM + 1, :HEAD], p[ROW_G1:ROW_G1 + 1], p[ROW_G2:ROW_G2 + 1], p[ROW_G3:ROW_G3 + 1],
            p[ROW_G4:ROW_G4 + 1])


def _step(dm, x, w_in, b_fox_f, hgrn_lb_logits, hgrn_norm_w, w_up_a, w_up_b, w_o, norm_mix_pre, norm_mix_post,
          norm_ffn_pre, norm_ffn_post, w_ffn_in, w_ffn_down, loss_target, moments_m, moments_v):
    xi, yi, ci = lax.axis_index("x"), lax.axis_index("y"), lax.axis_index("c")
    cq_idx = jnp.stack([ci, 2 * xi + yi]).astype(jnp.int32)
    D, AW, FF = dm.D, dm.AW, dm.FF
    cs, cp, _ = _w_in_layout(dm)
    big_names = ["w_in", "w_up_a", "w_up_b", "w_o", "w_ffn_in", "w_ffn_down"]
    specs = [Sharded("col", D, cp), Sharded("col", AW, D // 4), Sharded("col", AW, D // 4),
             Sharded("row", D // 4, D), Sharded("col", D, 2 * FF // 4), Sharded("row", FF // 4, D)]
    shards = [w_in[0], w_up_a[0], w_up_b[0], w_o[0], w_ffn_in[0], w_ffn_down[0]]

    sent = [jnp.pad(shards[0].astype(BF16), ((0, 0), (0, cp - cs)))] + [w.astype(BF16) for w in shards[1:]]
    (f_in,) = _run_fused(_gather_fused(sent[:1], specs[:1]), "gather_w_in")
    w_main, w_rest = _unshuffle_w_in(f_in, dm)

    class ReduceHooks:
        pairs = [None] * 6
        fulls = None

        def pair_sums(self, fulls, from_sibling, first):
            for t, (g, r) in enumerate(zip(fulls, from_sibling), start=first):
                self.pairs[t] = _add_sibling(g, r, specs[t], cq_idx, "add_sibling_" + big_names[t])
            return _scatter_fused([self.pairs[t][0] for t in range(first, first + len(fulls))],
                                  specs[first:first + len(fulls)])

        def swap_later(self, fulls):
            self.fulls = fulls
            return _swap_fused(fulls, specs[1:])

        def scatter_later(self, from_sibling):
            return self.pair_sums(self.fulls, from_sibling, 1)

        def scatter_w_in(self, d_main, d_rest):
            full = [_shuffle_w_in(d_main, d_rest, dm)]
            return self.pair_sums(full, _run_fused(_swap_fused(full, specs[:1]), "swap_halves_w_in"), 0)

    hooks = ReduceHooks()

    bias_p = jnp.pad(b_fox_f, ((0, 0), (0, HEAD - dm.H)))
    dx, _, small, from_chips = _local_step(
        dm, x[0], loss_target[0], w_main, w_rest, _gather_fused(sent[1:], specs[1:]), bias_p,
        hgrn_lb_logits, hgrn_norm_w, norm_mix_pre, norm_mix_post, norm_ffn_pre, norm_ffn_post,
        reduce_hooks=hooks)

    halves = [_add_chips(p[1], r, sp, cq_idx, "add_chips_" + n)
              for p, r, sp, n in zip(hooks.pairs, from_chips, specs, big_names)]
    grads = list(_share_halves(halves, specs))
    grads[0] = grads[0][:, :cs]

    row = lambda v: jnp.pad(v.reshape(1, -1), ((0, 0), (0, D - v.size)))
    vec = jnp.concatenate([row(small["loss"][:, :1]), row(small["bias"][:, :dm.H]), row(small["norm_w"]),
                           row(small["lb"]), small["g1"], small["g2"], small["g3"], small["g4"]], axis=0)
    gsum = _sum_small(vec)
    loss = gsum[ROW_LOSS, 0]

    smalls = lambda t: (t["b_fox_f"], t["hgrn_norm_w"], t["hgrn_lb_logits"], t["norm_mix_pre"], t["norm_mix_post"],
                        t["norm_ffn_pre"], t["norm_ffn_post"])
    params = dict(b_fox_f=b_fox_f, hgrn_norm_w=hgrn_norm_w, hgrn_lb_logits=hgrn_lb_logits, norm_mix_pre=norm_mix_pre,
                  norm_mix_post=norm_mix_post, norm_ffn_pre=norm_ffn_pre, norm_ffn_post=norm_ffn_post)
    sg, sd, sm, sv = _small_update(gsum, _pack_small(dm, *smalls(params)), _pack_small(dm, *smalls(moments_m)),
                                   _pack_small(dm, *smalls(moments_v)))
    big_out = {}
    for name, wsh, g in zip(big_names, shards, grads):
        go, d, mn, vn = _adamw(wsh, g, moments_m[name][0], moments_v[name][0], "adamw_" + name)
        big_out[name] = (go[None], d[None], mn[None], vn[None])

    order = ["w_in", "b_fox_f", "hgrn_lb_logits", "hgrn_norm_w", "w_up_a", "w_up_b", "w_o", "norm_mix_pre",
             "norm_mix_post", "norm_ffn_pre", "norm_ffn_post", "w_ffn_in", "w_ffn_down"]
    outs = []
    for kind, packed in enumerate([sg, sd, sm, sv]):
        b, lbl, nw, p1, p2, p3, p4 = _unpack_small(packed, dm)
        sm_map = dict(b_fox_f=b, hgrn_lb_logits=lbl, hgrn_norm_w=nw, norm_mix_pre=p1, norm_mix_post=p2,
                      norm_ffn_pre=p3, norm_ffn_post=p4)
        outs.append([big_out[n][kind] if n in big_out else sm_map[n] for n in order])
    return (loss, dx[None], *outs[0], *outs[1], *outs[2], *outs[3])


def kernel(x, w_in, b_fox_f, hgrn_lb_logits, hgrn_norm_w, w_up_a, w_up_b, w_o, norm_mix_pre, norm_mix_post, norm_ffn_pre, norm_ffn_post, w_ffn_in, w_ffn_down, loss_target, m_w_in, m_b_fox_f, m_hgrn_lb_logits, m_hgrn_norm_w, m_w_up_a, m_w_up_b, m_w_o, m_norm_mix_pre, m_norm_mix_post, m_norm_ffn_pre, m_norm_ffn_post, m_w_ffn_in, m_w_ffn_down, v_w_in, v_b_fox_f, v_hgrn_lb_logits, v_hgrn_norm_w, v_w_up_a, v_w_up_b, v_w_o, v_norm_mix_pre, v_norm_mix_post, v_norm_ffn_pre, v_norm_ffn_post, v_w_ffn_in, v_w_ffn_down):
    dm = Dims(T=x.shape[1], D=x.shape[2], FF=w_ffn_down.shape[1] * 4)
    moments_m = dict(w_in=m_w_in, b_fox_f=m_b_fox_f, hgrn_lb_logits=m_hgrn_lb_logits, hgrn_norm_w=m_hgrn_norm_w,
                     w_up_a=m_w_up_a, w_up_b=m_w_up_b, w_o=m_w_o, norm_mix_pre=m_norm_mix_pre,
                     norm_mix_post=m_norm_mix_post, norm_ffn_pre=m_norm_ffn_pre, norm_ffn_post=m_norm_ffn_post,
                     w_ffn_in=m_w_ffn_in, w_ffn_down=m_w_ffn_down)
    moments_v = dict(w_in=v_w_in, b_fox_f=v_b_fox_f, hgrn_lb_logits=v_hgrn_lb_logits, hgrn_norm_w=v_hgrn_norm_w,
                     w_up_a=v_w_up_a, w_up_b=v_w_up_b, w_o=v_w_o, norm_mix_pre=v_norm_mix_pre,
                     norm_mix_post=v_norm_mix_post, norm_ffn_pre=v_norm_ffn_pre, norm_ffn_post=v_norm_ffn_post,
                     w_ffn_in=v_w_ffn_in, w_ffn_down=v_w_ffn_down)
    return _step(dm, x, w_in, b_fox_f, hgrn_lb_logits, hgrn_norm_w, w_up_a, w_up_b, w_o, norm_mix_pre, norm_mix_post,
                 norm_ffn_pre, norm_ffn_post, w_ffn_in, w_ffn_down, loss_target, moments_m, moments_v)
```

```python
from typing import NamedTuple

import numpy as np
import jax
import jax.numpy as jnp
from jax import lax
from jax.experimental import pallas as pl
from jax.experimental.pallas import tpu as pltpu

F32 = jnp.float32
BF16 = jnp.bfloat16
MESH = pl.DeviceIdType.MESH

RMS_EPS = 1e-6
HEAD = 128
CHUNK = 64
GROUP = 4
LEVELS = (32, 16, 8, 4, 2, 1)
NEG = -1e30

ADAM_LR = 0.001
ADAM_B1 = 0.9
ADAM_B2 = 0.999
ADAM_EPS = 1e-08
ADAM_WD = 0.01
ADAM_STEP = 10

VMEM_LIMIT = 56 * 1024 * 1024


class Dims(NamedTuple):
    T: int
    D: int
    FF: int

    @property
    def AW(self):
        return self.D // 2

    @property
    def H(self):
        return self.AW // HEAD

    @property
    def NMAIN(self):
        return 7 * self.AW

    @property
    def NREST(self):
        return 2 * self.D + HEAD

    @property
    def NIN(self):
        return 7 * self.AW + self.H + 2 * self.D


def _cparams(sem, vmem=VMEM_LIMIT, **kw):
    return pltpu.CompilerParams(dimension_semantics=sem, vmem_limit_bytes=vmem, **kw)


def _tile(n, target):
    if n <= target:
        return n
    t = (target // 128) * 128
    while t >= 128:
        if n % t == 0:
            return t
        t -= 128
    raise ValueError(f"no tile for {n}")


def _dot(a, b, dims):
    return lax.dot_general(a, b, (dims, ((), ())), preferred_element_type=F32)


def _nn(a, b):
    return _dot(a, b, ((1,), (0,)))


def _nt(a, b):
    return _dot(a, b, ((1,), (1,)))


def _tn(a, b):
    return _dot(a, b, ((0,), (0,)))


def _sigmoid(x):
    return jax.nn.sigmoid(x)


def _split2(x):
    hi = x.astype(BF16)
    lo = (x - hi.astype(F32)).astype(BF16)
    return hi, lo


def _split3(x):
    hi = x.astype(BF16)
    r = x - hi.astype(F32)
    mid = r.astype(BF16)
    lo = (r - mid.astype(F32)).astype(BF16)
    return hi, mid, lo


def _mm(a, b, mode, out_dtype, name, add=None, tm=1024, tn=1024, tk=None, fused=None):
    if mode == "nn":
        (M, K), (K2, N) = a.shape, b.shape
    elif mode == "nt":
        (M, K), (N, K2) = a.shape, b.shape
    else:
        (K, M), (K2, N) = a.shape, b.shape
    assert K == K2, (a.shape, b.shape, mode)
    tm, tn = _tile(M, tm), _tile(N, tn)
    isz = lambda t: jnp.dtype(t.dtype).itemsize

    def vmem_need(tk_):
        need = 2 * (tm * tk_ * isz(a) + tk_ * tn * isz(b) + tm * tn * jnp.dtype(out_dtype).itemsize)
        return need + (tm * tn * 4 if K // tk_ > 1 else 0) + (2 * tm * tn * isz(add) if add is not None else 0)

    budget = VMEM_LIMIT - 16 * 1024 * 1024
    for target in ((tk,) if tk is not None else ((2048, 1024, 512) if mode == "tn" else (4096, 2048, 1024, 512))):
        tk_ = _tile(K, target)
        if vmem_need(tk_) <= budget:
            break
    tk = tk_
    nk = K // tk
    assert vmem_need(tk) <= budget, (name, vmem_need(tk))
    if mode == "nn":
        a_spec = pl.BlockSpec((tm, tk), lambda i, j, k: (i, k))
        b_spec = pl.BlockSpec((tk, tn), lambda i, j, k: (k, j))
        op = _nn
    elif mode == "nt":
        a_spec = pl.BlockSpec((tm, tk), lambda i, j, k: (i, k))
        b_spec = pl.BlockSpec((tn, tk), lambda i, j, k: (j, k))
        op = _nt
    else:
        a_spec = pl.BlockSpec((tk, tm), lambda i, j, k: (k, i))
        b_spec = pl.BlockSpec((tk, tn), lambda i, j, k: (k, j))
        op = _tn
    o_spec = pl.BlockSpec((tm, tn), lambda i, j, k: (i, j))
    has_add = add is not None
    n_own = 3 if has_add else 2
    x_ins, x_outs, x_sems, n_in, n_out = _fused_parts(fused)
    grid = (M // tm, N // tn, nk)

    def body(*refs):
        a_ref, b_ref = refs[:2]
        add_ref = refs[2] if has_add else None
        o_ref = refs[n_own + n_in]
        acc = refs[n_own + n_in + 1 + n_out] if nk > 1 else None
        k = pl.program_id(2)
        if fused is not None:
            step = (pl.program_id(0) * grid[1] + pl.program_id(1)) * nk + k
            start, middle, finish_x = fused.hooks(
                refs[n_own:n_own + n_in], refs[n_own + n_in + 1:n_own + n_in + 1 + n_out],
                *refs[n_own + n_in + 1 + n_out + (1 if nk > 1 else 0):])
            pl.when(step == 0)(start)
            pl.when(step == (grid[0] * grid[1] * nk) // 2)(middle)

        def finish(r):
            if has_add:
                r = r + add_ref[...].astype(F32)
            o_ref[...] = r.astype(out_dtype)

        part = op(a_ref[...].astype(BF16), b_ref[...].astype(BF16))
        if nk == 1:
            finish(part)
        else:
            @pl.when(k == 0)
            def _():
                acc[...] = part

            @pl.when((k > 0) & (k < nk - 1))
            def _():
                acc[...] += part

            @pl.when(k == nk - 1)
            def _():
                finish(acc[...] + part)

        if fused is not None:
            pl.when(step == grid[0] * grid[1] * nk - 1)(finish_x)

    in_specs = [a_spec, b_spec] + ([o_spec] if has_add else []) + [HBM] * n_in
    args = (a, b) + ((add,) if has_add else ()) + x_ins
    outs = pl.pallas_call(
        body, name=name, grid=grid,
        in_specs=in_specs, out_specs=[o_spec] + [HBM] * n_out,
        out_shape=[jax.ShapeDtypeStruct((M, N), out_dtype)] + list(x_outs),
        scratch_shapes=([pltpu.VMEM((tm, tn), F32)] if nk > 1 else []) + list(x_sems),
        compiler_params=_cparams(("parallel", "parallel", "arbitrary") if fused is None else ("arbitrary",) * 3),
    )(*args)
    return outs[0] if fused is None else (outs[0], tuple(outs[1:]))


def _rows(tr, w, col=0):
    return pl.BlockSpec((tr, w), lambda i, *_: (i, col))


def _vec(w):
    return pl.BlockSpec((1, w), lambda i, *_: (0, 0))


def _rstd(v):
    return lax.rsqrt(jnp.mean(v * v, axis=-1, keepdims=True) + RMS_EPS)


def _rms_bwd(dn, n, r):
    return r * (dn - n * jnp.mean(dn * n, axis=-1, keepdims=True))


def _colsum(v):
    return jnp.sum(v, axis=0, keepdims=True)


def _rms_fwd(x, g, tr=256):
    T, D = x.shape

    def body(x_ref, g_ref, h_ref, r_ref):
        xv = x_ref[...]
        r = _rstd(xv)
        h_ref[...] = (xv * r * g_ref[...]).astype(BF16)
        r_ref[...] = r

    return pl.pallas_call(
        body, name="rms_fwd", grid=(T // tr,),
        in_specs=[_rows(tr, D), _vec(D)],
        out_specs=[_rows(tr, D), _rows(tr, 1)],
        out_shape=[jax.ShapeDtypeStruct((T, D), BF16), jax.ShapeDtypeStruct((T, 1), F32)],
        compiler_params=_cparams(("parallel",)),
    )(x, g)


def _merge_fwd(rest, y_a, y_b, tr=256):
    T, D = y_a.shape

    def body(ga_ref, gb_ref, ya_ref, yb_ref, o_ref):
        o_ref[...] = (_sigmoid(ga_ref[...]) * ya_ref[...] + _sigmoid(gb_ref[...]) * yb_ref[...]).astype(BF16)

    return pl.pallas_call(
        body, name="merge_fwd", grid=(T // tr,),
        in_specs=[_rows(tr, D, 0), _rows(tr, D, 1), _rows(tr, D), _rows(tr, D)],
        out_specs=_rows(tr, D),
        out_shape=jax.ShapeDtypeStruct((T, D), BF16),
        compiler_params=_cparams(("parallel",)),
    )(rest, rest, y_a, y_b)


def _post_pre(x, u, g2, g3, tr=256):
    T, D = x.shape

    def body(x_ref, u_ref, g2_ref, g3_ref, x1_ref, r2_ref, h3_ref, r3_ref):
        uv = u_ref[...]
        r2 = _rstd(uv)
        x1 = x_ref[...] + uv * r2 * g2_ref[...]
        r3 = _rstd(x1)
        x1_ref[...] = x1
        r2_ref[...] = r2
        h3_ref[...] = (x1 * r3 * g3_ref[...]).astype(BF16)
        r3_ref[...] = r3

    return pl.pallas_call(
        body, name="post_pre", grid=(T // tr,),
        in_specs=[_rows(tr, D), _rows(tr, D), _vec(D), _vec(D)],
        out_specs=[_rows(tr, D), _rows(tr, 1), _rows(tr, D), _rows(tr, 1)],
        out_shape=[jax.ShapeDtypeStruct((T, D), F32), jax.ShapeDtypeStruct((T, 1), F32),
                   jax.ShapeDtypeStruct((T, D), BF16), jax.ShapeDtypeStruct((T, 1), F32)],
        compiler_params=_cparams(("parallel",)),
    )(x, u, g2, g3)


def _ffn_in_swiglu(h3, w_ffn_in, tm=1024, tn=512):
    T, D = h3.shape
    FF = w_ffn_in.shape[1] // 2
    tm, tn = _tile(T, tm), _tile(FF, tn)
    nj = FF // tn

    def body(a_ref, bg_ref, bu_ref, g_ref, u_ref, act_ref):
        av = a_ref[...]
        gv = _nn(av, bg_ref[...])
        uv = _nn(av, bu_ref[...])
        g_ref[...] = gv
        u_ref[...] = uv
        act_ref[...] = (gv * _sigmoid(gv) * uv).astype(BF16)

    blk = pl.BlockSpec((tm, tn), lambda i, j: (i, j))
    return pl.pallas_call(
        body, name="ffn_in_swiglu", grid=(T // tm, nj),
        in_specs=[pl.BlockSpec((tm, D), lambda i, j: (i, 0)),
                  pl.BlockSpec((D, tn), lambda i, j: (0, j)),
                  pl.BlockSpec((D, tn), lambda i, j: (0, j + nj))],
        out_specs=[blk, blk, blk],
        out_shape=[jax.ShapeDtypeStruct((T, FF), F32), jax.ShapeDtypeStruct((T, FF), F32),
                   jax.ShapeDtypeStruct((T, FF), BF16)],
        compiler_params=_cparams(("parallel", "parallel")),
    )(h3, w_ffn_in, w_ffn_in)


def _d_act_swiglu(dw, w_ffn_down, gate, up, tm=1024, tn=512):
    T, D = dw.shape
    FF = w_ffn_down.shape[0]
    tm, tn = _tile(T, tm), _tile(FF, tn)

    def body(a_ref, b_ref, g_ref, u_ref, dg_ref, du_ref):
        dact = _nt(a_ref[...], b_ref[...])
        gv = g_ref[...]
        s = _sigmoid(gv)
        dg_ref[...] = (dact * u_ref[...] * (s * (1.0 + gv * (1.0 - s)))).astype(BF16)
        du_ref[...] = (dact * (gv * s)).astype(BF16)

    blk = pl.BlockSpec((tm, tn), lambda i, j: (i, j))
    sds = jax.ShapeDtypeStruct((T, FF), BF16)
    return pl.pallas_call(
        body, name="d_act_swiglu", grid=(T // tm, FF // tn),
        in_specs=[pl.BlockSpec((tm, D), lambda i, j: (i, 0)), pl.BlockSpec((tn, D), lambda i, j: (j, 0)), blk, blk],
        out_specs=[blk, blk], out_shape=[sds, sds],
        compiler_params=_cparams(("parallel", "parallel")),
    )(dw, w_ffn_down, gate, up)


def _swiglu_fwd(gu, tr=256):
    T, FF2 = gu.shape
    FF = FF2 // 2
    tc = _tile(FF, 1024)
    nc = FF // tc

    def body(g_ref, u_ref, o_ref):
        gv = g_ref[...]
        o_ref[...] = (gv * _sigmoid(gv) * u_ref[...]).astype(BF16)

    return pl.pallas_call(
        body, name="swiglu_fwd", grid=(T // tr, nc),
        in_specs=[pl.BlockSpec((tr, tc), lambda i, j: (i, j)),
                  pl.BlockSpec((tr, tc), lambda i, j: (i, j + nc))],
        out_specs=pl.BlockSpec((tr, tc), lambda i, j: (i, j)),
        out_shape=jax.ShapeDtypeStruct((T, FF), BF16),
        compiler_params=_cparams(("parallel", "parallel")),
    )(gu, gu)


def _loss_bwd(x1, w, g4, tgt, tr=256):
    T, D = x1.shape

    def body(x1_ref, w_ref, g4_ref, t_ref, loss_ref, dy_ref, dw_ref, dg_ref):
        i = pl.program_id(0)

        @pl.when(i == 0)
        def _():
            loss_ref[...] = jnp.zeros_like(loss_ref)
            dg_ref[...] = jnp.zeros_like(dg_ref)

        wv = w_ref[...]
        g4v = g4_ref[...]
        r4 = _rstd(wv)
        n4 = wv * r4
        e = x1_ref[...] + n4 * g4v - t_ref[...]
        loss_ref[...] += 0.5 * jnp.sum(jnp.mean(e * e, axis=-1, keepdims=True), axis=0, keepdims=True)
        dy = e * (1.0 / D)
        dy_ref[...] = dy
        dg_ref[...] += _colsum(dy * n4)
        dw_ref[...] = _rms_bwd(dy * g4v, n4, r4).astype(BF16)

    return pl.pallas_call(
        body, name="loss_bwd", grid=(T // tr,),
        in_specs=[_rows(tr, D), _rows(tr, D), _vec(D), _rows(tr, D)],
        out_specs=[_vec(HEAD), _rows(tr, D), _rows(tr, D), _vec(D)],
        out_shape=[jax.ShapeDtypeStruct((1, HEAD), F32), jax.ShapeDtypeStruct((T, D), F32),
                   jax.ShapeDtypeStruct((T, D), BF16), jax.ShapeDtypeStruct((1, D), F32)],
        compiler_params=_cparams(("arbitrary",)),
    )(x1, w, g4, tgt)


def _swiglu_bwd(gu, dact, tr=128):
    T, FF2 = gu.shape
    FF = FF2 // 2

    def body(g_ref, u_ref, d_ref, o_ref):
        h = pl.program_id(1)
        gv = g_ref[...]
        s = _sigmoid(gv)
        dv = d_ref[...].astype(F32)

        @pl.when(h == 0)
        def _():
            o_ref[...] = (dv * u_ref[...] * (s * (1.0 + gv * (1.0 - s)))).astype(BF16)

        @pl.when(h == 1)
        def _():
            o_ref[...] = (dv * (gv * s)).astype(BF16)

    return pl.pallas_call(
        body, name="swiglu_bwd", grid=(T // tr, 2),
        in_specs=[pl.BlockSpec((tr, FF), lambda i, h: (i, 0)),
                  pl.BlockSpec((tr, FF), lambda i, h: (i, 1)),
                  pl.BlockSpec((tr, FF), lambda i, h: (i, 0))],
        out_specs=pl.BlockSpec((tr, FF), lambda i, h: (i, h)),
        out_shape=jax.ShapeDtypeStruct((T, FF2), BF16),
        compiler_params=_cparams(("parallel", "arbitrary")),
    )(gu, gu, dact)


def _rms_bwd2(dy, dh3, x1, r3, g3, u, r2, g2, tr=256):
    T, D = dy.shape

    def body(dy_ref, dh_ref, x1_ref, r3_ref, g3_ref, u_ref, r2_ref, g2_ref, dx1_ref, du_ref, dg3_ref, dg2_ref):
        i = pl.program_id(0)

        @pl.when(i == 0)
        def _():
            dg3_ref[...] = jnp.zeros_like(dg3_ref)
            dg2_ref[...] = jnp.zeros_like(dg2_ref)

        r3v, r2v = r3_ref[...], r2_ref[...]
        dh = dh_ref[...]
        n3 = x1_ref[...] * r3v
        dg3_ref[...] += _colsum(dh * n3)
        dx1 = dy_ref[...] + _rms_bwd(dh * g3_ref[...], n3, r3v)
        dx1_ref[...] = dx1
        n2 = u_ref[...] * r2v
        dg2_ref[...] += _colsum(dx1 * n2)
        du_ref[...] = _rms_bwd(dx1 * g2_ref[...], n2, r2v).astype(BF16)

    return pl.pallas_call(
        body, name="rms_bwd2", grid=(T // tr,),
        in_specs=[_rows(tr, D), _rows(tr, D), _rows(tr, D), _rows(tr, 1), _vec(D),
                  _rows(tr, D), _rows(tr, 1), _vec(D)],
        out_specs=[_rows(tr, D), _rows(tr, D), _vec(D), _vec(D)],
        out_shape=[jax.ShapeDtypeStruct((T, D), F32), jax.ShapeDtypeStruct((T, D), BF16),
                   jax.ShapeDtypeStruct((1, D), F32), jax.ShapeDtypeStruct((1, D), F32)],
        compiler_params=_cparams(("arbitrary",)),
    )(dy, dh3, x1, r3, g3, u, r2, g2)


def _merge_bwd(dmerged, rest, y_a, y_b, tr=256):
    T, D = dmerged.shape

    def body(dm_ref, ga_ref, gb_ref, ya_ref, yb_ref, dya_ref, dyb_ref, dg_ref):
        h = pl.program_id(1)
        dm = dm_ref[...]

        @pl.when(h == 0)
        def _():
            s = _sigmoid(ga_ref[...])
            dya_ref[...] = (dm * s).astype(BF16)
            dg_ref[...] = (dm * ya_ref[...] * s * (1.0 - s)).astype(BF16)

        @pl.when(h == 1)
        def _():
            s = _sigmoid(gb_ref[...])
            dyb_ref[...] = (dm * s).astype(BF16)
            dg_ref[...] = (dm * yb_ref[...] * s * (1.0 - s)).astype(BF16)

    blk = lambda col: pl.BlockSpec((tr, D), lambda i, h: (i, col))
    return pl.pallas_call(
        body, name="merge_bwd", grid=(T // tr, 2),
        in_specs=[blk(0), blk(0), blk(1), blk(0), blk(0)],
        out_specs=[blk(0), blk(0), pl.BlockSpec((tr, D), lambda i, h: (i, h))],
        out_shape=[jax.ShapeDtypeStruct((T, D), BF16), jax.ShapeDtypeStruct((T, D), BF16),
                   jax.ShapeDtypeStruct((T, 2 * D), BF16)],
        compiler_params=_cparams(("parallel", "arbitrary")),
    )(dmerged, rest, rest, y_a, y_b)


def _rms_bwd1(dx1, dh1, x, r1, g1, tr=256):
    T, D = x.shape

    def body(dx1_ref, dh_ref, x_ref, r_ref, g_ref, dx_ref, dg_ref):
        i = pl.program_id(0)

        @pl.when(i == 0)
        def _():
            dg_ref[...] = jnp.zeros_like(dg_ref)

        rv = r_ref[...]
        dh = dh_ref[...]
        n = x_ref[...] * rv
        dg_ref[...] += _colsum(dh * n)
        dx_ref[...] = dx1_ref[...] + _rms_bwd(dh * g_ref[...], n, rv)

    return pl.pallas_call(
        body, name="rms_bwd1", grid=(T // tr,),
        in_specs=[_rows(tr, D), _rows(tr, D), _rows(tr, D), _rows(tr, 1), _vec(D)],
        out_specs=[_rows(tr, D), _vec(D)],
        out_shape=[jax.ShapeDtypeStruct((T, D), F32), jax.ShapeDtypeStruct((1, D), F32)],
        compiler_params=_cparams(("arbitrary",)),
    )(dx1, dh1, x, r1, g1)


def _hgrn_consts():
    C = CHUNK
    nl = len(LEVELS) + 1
    w = np.zeros((nl, C, C), np.float32)
    w[0] = np.tril(np.ones((C, C), np.float32))
    for li, m in enumerate(LEVELS, start=1):
        for r in range(C):
            mid = (r // (2 * m)) * 2 * m + m
            if r >= mid:
                w[li, r, mid:r + 1] = 1.0
            else:
                w[li, r, r + 1:mid] = 1.0
    w_all = w.reshape(nl * C, C)
    w2 = np.concatenate([w_all, w_all], axis=1)
    w2t = np.concatenate([w_all.T, w_all.T], axis=1)
    R = GROUP * C
    t = np.arange(R)[:, None]
    s = np.arange(R)[None, :]
    masks = np.zeros((nl, R, R), np.float32)
    masks[0] = (t == s)
    for li, m in enumerate(LEVELS, start=1):
        masks[li] = ((t ^ s) < 2 * m)
    return jnp.asarray(w2, BF16), jnp.asarray(w2t, BF16), jnp.asarray(masks, F32)


def _hgrn_gates(z, lg_ref):
    l0 = lg_ref[0:1, :]
    l1 = lg_ref[1:2, :]
    mx = jnp.maximum(l0, l1)
    e0 = jnp.exp(l0 - mx)
    e1 = jnp.exp(l1 - mx)
    lb = e0 / (e0 + e1)
    om = 1.0 - lb
    sg = _sigmoid(z)
    sgm = _sigmoid(-z)
    f = lb + om * sg
    return lb, om, sg, sgm, f, jnp.log(f), om * sgm


def _hgrn_levels(q, kk, lf, w2_ref):
    C = CHUNK
    nl = len(LEVELS) + 1
    lf_hi, lf_lo = _split2(lf)
    per_chunk = []
    for c in range(GROUP):
        rhs = jnp.concatenate([lf_hi[c * C:(c + 1) * C], lf_lo[c * C:(c + 1) * C]], axis=0)
        per_chunk.append(_nn(w2_ref[...], rhs))
    args = [jnp.concatenate([per_chunk[c][l * C:(l + 1) * C] for c in range(GROUP)], axis=0) for l in range(nl)]
    exps = [jnp.exp(a) for a in args]
    row = lax.broadcasted_iota(jnp.int32, q.shape, 0)
    qf, kf, mts = [q], [kk], [None]
    for li, m in enumerate(LEVELS, start=1):
        mt = jnp.where((row & m) != 0, 1.0, 0.0).astype(F32)
        mts.append(mt)
        qf.append(q * exps[li] * mt)
        kf.append(kk * exps[li] * (1.0 - mt))
    return args, exps, mts, qf, kf


def _hgrn_scores(qf, kf, masks_ref):
    p = None
    for l in range(len(qf)):
        pl_ = _nt(qf[l].astype(BF16), kf[l].astype(BF16)) * masks_ref[l]
        p = pl_ if p is None else p + pl_
    return p


def _hgrn_fwd(main, lb_logits, norm_w, dm):
    T, H, C = dm.T, dm.H, CHUNK
    R = GROUP * C
    nj = T // R
    w2, _, masks = _hgrn_consts()

    def body(x_ref, lg_ref, nw_ref, w2_ref, masks_ref, ya_ref, o_ref, sp_ref, st_ref):
        j = pl.program_id(1)

        @pl.when(j == 0)
        def _():
            st_ref[...] = jnp.zeros_like(st_ref)

        q, z, v, gv = (x_ref[:, s * HEAD:(s + 1) * HEAD] for s in range(4))
        vb = v.astype(BF16)
        _, _, _, _, _, lf, kk = _hgrn_gates(z, lg_ref)
        args, exps, _, qf, kf = _hgrn_levels(q, kk, lf, w2_ref)
        p = _hgrn_scores(qf, kf, masks_ref)
        o_intra = _nn(p.astype(BF16), vb)
        b, eb = args[0], exps[0]
        o_inter = []
        for c in range(GROUP):
            sl = slice(c * C, (c + 1) * C)
            st = st_ref[...]
            sp_ref[0, c] = st
            blast = b[c * C + C - 1:c * C + C, :]
            o_inter.append(_nt((q[sl] * eb[sl]).astype(BF16), st.astype(BF16)))
            kd = (kk[sl] * jnp.exp(blast - b[sl])).astype(BF16)
            st_ref[...] = st * jnp.exp(blast) + _tn(vb[sl], kd)
        o = o_intra + jnp.concatenate(o_inter, axis=0)
        o_ref[...] = o
        ya_ref[...] = (o * _rstd(o) * nw_ref[...] * (gv * _sigmoid(gv))).astype(BF16)

    return pl.pallas_call(
        body, name="hgrn_fwd", grid=(H, nj),
        in_specs=[pl.BlockSpec((R, 4 * HEAD), lambda h, j: (j, h)),
                  pl.BlockSpec((2, HEAD), lambda h, j: (0, h)),
                  pl.BlockSpec((1, HEAD), lambda h, j: (0, 0)),
                  pl.BlockSpec(w2.shape, lambda h, j: (0, 0)),
                  pl.BlockSpec(masks.shape, lambda h, j: (0, 0, 0))],
        out_specs=[pl.BlockSpec((R, HEAD), lambda h, j: (j, h)),
                   pl.BlockSpec((R, HEAD), lambda h, j: (j, h)),
                   pl.BlockSpec((1, GROUP, HEAD, HEAD), lambda h, j: (h, j, 0, 0))],
        out_shape=[jax.ShapeDtypeStruct((T, dm.AW), BF16), jax.ShapeDtypeStruct((T, dm.AW), F32),
                   jax.ShapeDtypeStruct((H, T // C, HEAD, HEAD), F32)],
        scratch_shapes=[pltpu.VMEM((HEAD, HEAD), F32)],
        compiler_params=_cparams(("parallel", "arbitrary")),
    )(main, lb_logits, norm_w, w2, masks)


def _hgrn_bwd(main, lb_logits, norm_w, o_saved, states, dya, dm, fused=None):
    T, H, C = dm.T, dm.H, CHUNK
    R = GROUP * C
    nj = T // R
    nl = len(LEVELS) + 1
    w2, w2t, masks = _hgrn_consts()
    x_ins, x_outs, x_sems, n_in, n_out = _fused_parts(fused)

    def body(x_ref, lg_ref, nw_ref, w2_ref, w2t_ref, masks_ref, o_ref, sp_ref, dya_ref, *rest):
        d_ref, dlb_ref, dnw_ref = rest[n_in:n_in + 3]
        ds_ref = rest[n_in + 3 + n_out]
        j = pl.program_id(1)
        if fused is not None:
            hd = pl.program_id(0)
            start, middle, finish_x = fused.hooks(rest[:n_in], rest[n_in + 3:n_in + 3 + n_out],
                                                  *rest[n_in + 4 + n_out:])
            pl.when((hd == 0) & (j == 0))(start)
            pl.when((hd == H // 2) & (j == 0))(middle)

        @pl.when(j == 0)
        def _():
            ds_ref[...] = jnp.zeros_like(ds_ref)
            dlb_ref[...] = jnp.zeros_like(dlb_ref)
            dnw_ref[...] = jnp.zeros_like(dnw_ref)

        def compute():
            q, z, v, gv = (x_ref[:, s * HEAD:(s + 1) * HEAD] for s in range(4))
            vb = v.astype(BF16)
            lb, om, sg, sgm, f, lf, kk = _hgrn_gates(z, lg_ref)
            args, exps, mts, qf, kf = _hgrn_levels(q, kk, lf, w2_ref)
            qb = [t.astype(BF16) for t in qf]
            kb = [t.astype(BF16) for t in kf]
            p = _hgrn_scores(qf, kf, masks_ref)
            o = o_ref[...]
            nw = nw_ref[...]
            r = _rstd(o)
            n = o * r
            sgg = _sigmoid(gv)
            dya_v = dya_ref[...]
            d_on = dya_v * (gv * sgg)
            dg = dya_v * (n * nw) * (sgg * (1.0 + gv * (1.0 - sgg)))
            dnw_ref[0] += _colsum(d_on * n)
            do = _rms_bwd(d_on * nw, n, r)
            dob = do.astype(BF16)
            dp = _nt(dob, vb)
            dv = _tn(p.astype(BF16), dob)
            dq = jnp.zeros_like(q)
            dkk = jnp.zeros_like(q)
            dargs = [None] * nl
            for l in range(nl):
                dpl = (dp * masks_ref[l]).astype(BF16)
                dql = _nn(dpl, kb[l])
                dkl = _tn(dpl, qb[l])
                if l == 0:
                    dq += dql
                    dkk += dkl
                else:
                    dq += dql * exps[l] * mts[l]
                    dkk += dkl * exps[l] * (1.0 - mts[l])
                    dargs[l] = dql * qf[l] + dkl * kf[l]
            b, eb = args[0], exps[0]
            row = lax.broadcasted_iota(jnp.int32, (C, HEAD), 0)
            dq_i, dkk_i, dv_i, db_i = [None] * GROUP, [None] * GROUP, [None] * GROUP, [None] * GROUP
            for c in reversed(range(GROUP)):
                sl = slice(c * C, (c + 1) * C)
                st = sp_ref[0, c]
                ds = ds_ref[...]
                dsb = ds.astype(BF16)
                blast = b[c * C + C - 1:c * C + C, :]
                ebl = jnp.exp(blast)
                el = jnp.exp(blast - b[sl])
                qe = q[sl] * eb[sl]
                kd = kk[sl] * el
                dqe = _nn(dob[sl], st.astype(BF16))
                dkd = _nn(vb[sl], dsb)
                t = dkd * kd
                dblast = _colsum(t) + _colsum(ds * st) * ebl
                dq_i[c] = dqe * eb[sl]
                dkk_i[c] = dkd * el
                dv_i[c] = _nt(kd.astype(BF16), dsb)
                db_i[c] = dqe * qe - t + jnp.where(row == C - 1, dblast, 0.0)
                ds_ref[...] = ds * ebl + _tn(dob[sl], qe.astype(BF16))
            dq = dq + jnp.concatenate(dq_i, axis=0)
            dkk = dkk + jnp.concatenate(dkk_i, axis=0)
            dv = dv + jnp.concatenate(dv_i, axis=0)
            dlf_c = []
            for c in range(GROUP):
                sl = slice(c * C, (c + 1) * C)
                stack = jnp.concatenate([db_i[c]] + [dargs[l][sl] for l in range(1, nl)], axis=0)
                hi, lo = _split2(stack)
                dlf_c.append(_nn(w2t_ref[...], jnp.concatenate([hi, lo], axis=0)))
            dlf = jnp.concatenate(dlf_c, axis=0)
            dz = dlf * (om * sg * (1.0 - sg) / f) - dkk * (om * sgm * (1.0 - sgm))
            dlb_ref[...] += _colsum(dlf * (1.0 - sg) / f - dkk * sgm)
            d_ref[...] = jnp.concatenate([dq, dz, dv, dg], axis=1).astype(BF16)

        compute()
        if fused is not None:
            pl.when((hd == H - 1) & (j == nj - 1))(finish_x)

    outs = pl.pallas_call(
        body, name="hgrn_bwd", grid=(H, nj),
        in_specs=[pl.BlockSpec((R, 4 * HEAD), lambda h, j: (nj - 1 - j, h)),
                  pl.BlockSpec((2, HEAD), lambda h, j: (0, h)),
                  pl.BlockSpec((1, HEAD), lambda h, j: (0, 0)),
                  pl.BlockSpec(w2.shape, lambda h, j: (0, 0)),
                  pl.BlockSpec(w2t.shape, lambda h, j: (0, 0)),
                  pl.BlockSpec(masks.shape, lambda h, j: (0, 0, 0)),
                  pl.BlockSpec((R, HEAD), lambda h, j: (nj - 1 - j, h)),
                  pl.BlockSpec((1, GROUP, HEAD, HEAD), lambda h, j: (h, nj - 1 - j, 0, 0)),
                  pl.BlockSpec((R, HEAD), lambda h, j: (nj - 1 - j, h))] + [HBM] * n_in,
        out_specs=[pl.BlockSpec((R, 4 * HEAD), lambda h, j: (nj - 1 - j, h)),
                   pl.BlockSpec((1, HEAD), lambda h, j: (0, h)),
                   pl.BlockSpec((1, 1, HEAD), lambda h, j: (h, 0, 0))] + [HBM] * n_out,
        out_shape=[jax.ShapeDtypeStruct((T, 4 * dm.AW), BF16), jax.ShapeDtypeStruct((1, dm.AW), F32),
                   jax.ShapeDtypeStruct((H, 1, HEAD), F32)] + list(x_outs),
        scratch_shapes=[pltpu.VMEM((HEAD, HEAD), F32)] + list(x_sems),
        compiler_params=_cparams(("arbitrary", "arbitrary")),
    )(main, lb_logits, norm_w, w2, w2t, masks, o_saved, states, dya, *x_ins)
    return outs[0], outs[1], outs[2], tuple(outs[3:])


def _log_sigmoid(x):
    return jnp.minimum(x, 0.0) - jnp.log(1.0 + jnp.exp(-jnp.abs(x)))


def _tri(n):
    return jnp.asarray(np.tril(np.ones((n, n), np.float32)), BF16)


def _cum_fwd(rest, bias, dm, tb=256):
    T = dm.T
    tb = min(tb, T)
    cb = 2 * dm.D // HEAD
    tri = _tri(tb)

    def body(x_ref, b_ref, tri_ref, o_ref, carry):
        i = pl.program_id(0)

        @pl.when(i == 0)
        def _():
            carry[...] = jnp.zeros_like(carry)

        lf = _log_sigmoid(x_ref[...] + b_ref[...])
        hi, mid, lo = _split3(lf)
        tr_ = tri_ref[...]
        c = _nn(tr_, hi) + _nn(tr_, mid) + _nn(tr_, lo) + carry[...]
        o_ref[...] = c
        carry[...] = c[tb - 1:tb, :]

    return pl.pallas_call(
        body, name="cum_fwd", grid=(T // tb,),
        in_specs=[_rows(tb, HEAD, cb), _vec(HEAD), pl.BlockSpec((tb, tb), lambda i: (0, 0))],
        out_specs=_rows(tb, HEAD),
        out_shape=jax.ShapeDtypeStruct((T, HEAD), F32),
        scratch_shapes=[pltpu.VMEM((1, HEAD), F32)],
        compiler_params=_cparams(("arbitrary",)),
    )(rest, bias, tri)


def _cum_bwd(dcum, rest, bias, dm, tb=256):
    T = dm.T
    tb = min(tb, T)
    nb = T // tb
    cb = 2 * dm.D // HEAD
    tri = _tri(tb)

    def body(d_ref, x_ref, b_ref, tri_ref, o_ref, db_ref, carry):
        i = pl.program_id(0)

        @pl.when(i == 0)
        def _():
            carry[...] = jnp.zeros_like(carry)
            db_ref[...] = jnp.zeros_like(db_ref)

        hi, mid, lo = _split3(d_ref[...])
        tr_ = tri_ref[...]
        dlf = _tn(tr_, hi) + _tn(tr_, mid) + _tn(tr_, lo) + carry[...]
        carry[...] = dlf[0:1, :]
        dx = dlf * _sigmoid(-(x_ref[...] + b_ref[...]))
        o_ref[...] = dx.astype(BF16)
        db_ref[...] += _colsum(dx)

    return pl.pallas_call(
        body, name="cum_bwd", grid=(nb,),
        in_specs=[pl.BlockSpec((tb, HEAD), lambda i: (nb - 1 - i, 0)),
                  pl.BlockSpec((tb, HEAD), lambda i: (nb - 1 - i, cb)),
                  _vec(HEAD), pl.BlockSpec((tb, tb), lambda i: (0, 0))],
        out_specs=[pl.BlockSpec((tb, HEAD), lambda i: (nb - 1 - i, 0)), _vec(HEAD)],
        out_shape=[jax.ShapeDtypeStruct((T, HEAD), BF16), jax.ShapeDtypeStruct((1, HEAD), F32)],
        scratch_shapes=[pltpu.VMEM((1, HEAD), F32)],
        compiler_params=_cparams(("arbitrary",)),
    )(dcum, rest, bias, tri)


def _fox_pairs(nq, kv_major):
    if kv_major:
        pairs = [(i, j) for j in range(nq) for i in range(j, nq)]
    else:
        pairs = [(i, j) for i in range(nq) for j in range(i + 1)]
    qi = jnp.asarray(np.array([p[0] for p in pairs], np.int32))
    kj = jnp.asarray(np.array([p[1] for p in pairs], np.int32))
    return qi, kj


class Fused(NamedTuple):
    ins: tuple
    outs: tuple
    sems: tuple
    hooks: object


def _fused_parts(fused):
    if fused is None:
        return (), (), (), 0, 0
    return tuple(fused.ins), tuple(fused.outs), tuple(fused.sems), len(fused.ins), len(fused.outs)


FOX_FWD_SPLIT = 4
FOX_BWD_SPLIT = 2


def _fox_fwd(main, cum_row, dm, fused=None, tq=512):
    T, H = dm.T, dm.H
    tq = min(tq, T)
    nq = T // tq
    qi_tab, kj_tab = _fox_pairs(nq, kv_major=False)
    npairs = int(qi_tab.shape[0])
    x_ins, x_outs, x_sems, n_in, n_out = _fused_parts(fused)
    ns = FOX_FWD_SPLIT if tq % (8 * FOX_FWD_SPLIT) == 0 else 1
    rq = tq // ns

    def body(qi_ref, kj_ref, q_ref, k_ref, v_ref, ck_ref, *rest):
        x_in, (o_ref, lse_ref) = rest[:n_in], rest[n_in:n_in + 2]
        x_out = rest[n_in + 2:n_in + 2 + n_out]
        m_ref, l_ref, acc_ref = rest[n_in + 2 + n_out:n_in + 5 + n_out]
        hd = pl.program_id(0)
        t = pl.program_id(1)
        i = qi_ref[t]
        j = kj_ref[t]
        if fused is not None:
            start, middle, finish = fused.hooks(x_in, x_out, *rest[n_in + 5 + n_out:])
            pl.when((hd == 0) & (t == 0))(start)
            pl.when((hd == H // 2) & (t == 0))(middle)

        @pl.when(j == 0)
        def _():
            m_ref[...] = jnp.full_like(m_ref, NEG)
            l_ref[...] = jnp.zeros_like(l_ref)
            acc_ref[...] = jnp.zeros_like(acc_ref)

        def step(on_diagonal):
            kb = k_ref[...].astype(BF16)
            vb = v_ref[...].astype(BF16)
            ck = ck_ref[0]
            q_all, m_all, l_all, acc_all = q_ref[...], m_ref[...], l_ref[...], acc_ref[...]
            m_out, l_out, acc_out = [], [], []
            for g in range(ns):
                rows = slice(g * rq, (g + 1) * rq)
                qs = (q_all[rows] * (HEAD ** -0.5)).astype(BF16)
                s = _nt(qs, kb) - ck
                if on_diagonal:
                    row = g * rq + lax.broadcasted_iota(jnp.int32, s.shape, 0)
                    s = jnp.where(row >= lax.broadcasted_iota(jnp.int32, s.shape, 1), s, NEG)
                m_new = jnp.maximum(m_all[rows], jnp.max(s, axis=-1, keepdims=True))
                a = jnp.exp(m_all[rows] - m_new)
                p = jnp.exp(s - m_new)
                m_out.append(m_new)
                l_out.append(a * l_all[rows] + jnp.sum(p, axis=-1, keepdims=True))
                acc_out.append(a * acc_all[rows] + _nn(p.astype(BF16), vb))
            m_ref[...] = jnp.concatenate(m_out, axis=0)
            l_ref[...] = jnp.concatenate(l_out, axis=0)
            acc_ref[...] = jnp.concatenate(acc_out, axis=0)

        @pl.when(j < i)
        def _():
            step(False)

        @pl.when(j == i)
        def _():
            step(True)
            l = l_ref[...]
            o_ref[...] = acc_ref[...] / l
            lse_ref[0] = m_ref[...] + jnp.log(l)

        if fused is not None:
            pl.when((hd == H - 1) & (t == npairs - 1))(finish)

    outs = pl.pallas_call(
        body, name="fox_fwd",
        grid_spec=pltpu.PrefetchScalarGridSpec(
            num_scalar_prefetch=2, grid=(H, npairs),
            in_specs=[pl.BlockSpec((tq, HEAD), lambda h, t, qi, kj: (qi[t], 4 * H + h)),
                      pl.BlockSpec((tq, HEAD), lambda h, t, qi, kj: (kj[t], 5 * H + h)),
                      pl.BlockSpec((tq, HEAD), lambda h, t, qi, kj: (kj[t], 6 * H + h)),
                      pl.BlockSpec((1, 1, tq), lambda h, t, qi, kj: (h, 0, kj[t]))] + [HBM] * n_in,
            out_specs=[pl.BlockSpec((tq, HEAD), lambda h, t, qi, kj: (qi[t], h)),
                       pl.BlockSpec((1, tq, 1), lambda h, t, qi, kj: (h, qi[t], 0))] + [HBM] * n_out,
            scratch_shapes=[pltpu.VMEM((tq, 1), F32), pltpu.VMEM((tq, 1), F32), pltpu.VMEM((tq, HEAD), F32)]
            + list(x_sems)),
        out_shape=[jax.ShapeDtypeStruct((T, dm.AW), F32), jax.ShapeDtypeStruct((H, T, 1), F32)] + list(x_outs),
        compiler_params=_cparams(("arbitrary", "arbitrary")),
    )(qi_tab, kj_tab, main, main, main, cum_row, *x_ins)
    return outs[0], outs[1], tuple(outs[2:])


def _fox_delta(do, o, dm, tr=256):
    T, H = dm.T, dm.H
    tr = min(tr, T)

    def body(do_ref, o_ref, d_ref):
        d_ref[0] = jnp.sum(do_ref[...] * o_ref[...], axis=-1, keepdims=True)

    return pl.pallas_call(
        body, name="fox_delta", grid=(H, T // tr),
        in_specs=[pl.BlockSpec((tr, HEAD), lambda h, i: (i, h)), pl.BlockSpec((tr, HEAD), lambda h, i: (i, h))],
        out_specs=pl.BlockSpec((1, tr, 1), lambda h, i: (h, i, 0)),
        out_shape=jax.ShapeDtypeStruct((H, T, 1), F32),
        compiler_params=_cparams(("parallel", "parallel")),
    )(do, o)


def _fox_bwd(main, cum_row, lse, delta, do, dm, fused=None, tq=512):
    T, H = dm.T, dm.H
    tq = min(tq, T)
    nq = T // tq
    qi_tab, kj_tab = _fox_pairs(nq, kv_major=True)
    npairs = int(qi_tab.shape[0])
    x_ins, x_outs, x_sems, n_in, n_out = _fused_parts(fused)
    ns = FOX_BWD_SPLIT if tq % (16 * FOX_BWD_SPLIT) == 0 else 1
    rq = tq // ns

    def body(qi_ref, kj_ref, q_ref, k_ref, v_ref, ck_ref, lse_ref, dl_ref, do_ref, *rest):
        x_in = rest[:n_in]
        dq_ref, dk_ref, dv_ref, dc_ref, dr_ref = rest[n_in:n_in + 5]
        x_out = rest[n_in + 5:n_in + 5 + n_out]
        dk_acc, dv_acc, dc_acc = rest[n_in + 5 + n_out:n_in + 8 + n_out]
        hd = pl.program_id(0)
        t = pl.program_id(1)
        i = qi_ref[t]
        kj = kj_ref[t]
        if fused is not None:
            start, middle, finish = fused.hooks(x_in, x_out, *rest[n_in + 8 + n_out:])
            pl.when((hd == 0) & (t == 0))(start)
            pl.when((hd == H // 2) & (t == 0))(middle)

        @pl.when(t == 0)
        def _():
            dq_ref[...] = jnp.zeros_like(dq_ref)
            dr_ref[...] = jnp.zeros_like(dr_ref)

        @pl.when(i == kj)
        def _():
            dk_acc[...] = jnp.zeros_like(dk_acc)
            dv_acc[...] = jnp.zeros_like(dv_acc)
            dc_acc[...] = jnp.zeros_like(dc_acc)

        def step(on_diagonal):
            kb = k_ref[...].astype(BF16)
            vb = v_ref[...].astype(BF16)
            ck = ck_ref[0]
            q_all, do_all, lse_all, dl_all = q_ref[...], do_ref[...], lse_ref[0], dl_ref[0]
            dq_g, dr_g, dv_c, dk_c, dc_c = [], [], None, None, None
            for g in range(ns):
                rows = slice(g * rq, (g + 1) * rq)
                qs = (q_all[rows] * (HEAD ** -0.5)).astype(BF16)
                s = _nt(qs, kb) - ck
                if on_diagonal:
                    row = g * rq + lax.broadcasted_iota(jnp.int32, s.shape, 0)
                    s = jnp.where(row >= lax.broadcasted_iota(jnp.int32, s.shape, 1), s, NEG)
                dob = do_all[rows].astype(BF16)
                p = jnp.exp(s - lse_all[rows])
                ds = p * (_nt(dob, vb) - dl_all[rows])
                dsb = ds.astype(BF16)
                dq_g.append(_nn(dsb, kb) * (HEAD ** -0.5))
                dr_g.append(jnp.sum(ds, axis=-1, keepdims=True))
                dv_g, dk_g, dc_g = _tn(p.astype(BF16), dob), _tn(dsb, qs), _colsum(ds)
                dv_c = dv_g if dv_c is None else dv_c + dv_g
                dk_c = dk_g if dk_c is None else dk_c + dk_g
                dc_c = dc_g if dc_c is None else dc_c + dc_g
            dv_acc[...] += dv_c
            dk_acc[...] += dk_c
            dc_acc[...] -= dc_c
            out_rows = pl.ds(pl.multiple_of(i * tq, tq), tq)
            dq_ref[out_rows, :] += jnp.concatenate(dq_g, axis=0)
            dr_ref[0, out_rows, :] += jnp.concatenate(dr_g, axis=0)

        @pl.when(i == kj)
        def _():
            step(True)

        @pl.when(i > kj)
        def _():
            step(False)

        @pl.when(i == nq - 1)
        def _():
            dk_ref[...] = dk_acc[...].astype(BF16)
            dv_ref[...] = dv_acc[...].astype(BF16)
            dc_ref[0] = dc_acc[...]

        if fused is not None:
            pl.when((hd == H - 1) & (t == npairs - 1))(finish)

    qcol = pl.BlockSpec((1, tq, 1), lambda h, t, qi, kj: (h, qi[t], 0))
    outs = pl.pallas_call(
        body, name="fox_bwd",
        grid_spec=pltpu.PrefetchScalarGridSpec(
            num_scalar_prefetch=2, grid=(H, npairs),
            in_specs=[pl.BlockSpec((tq, HEAD), lambda h, t, qi, kj: (qi[t], 4 * H + h)),
                      pl.BlockSpec((tq, HEAD), lambda h, t, qi, kj: (kj[t], 5 * H + h)),
                      pl.BlockSpec((tq, HEAD), lambda h, t, qi, kj: (kj[t], 6 * H + h)),
                      pl.BlockSpec((1, 1, tq), lambda h, t, qi, kj: (h, 0, kj[t])),
                      qcol, qcol,
                      pl.BlockSpec((tq, HEAD), lambda h, t, qi, kj: (qi[t], h))] + [HBM] * n_in,
            out_specs=[pl.BlockSpec((T, HEAD), lambda h, t, qi, kj: (0, h)),
                       pl.BlockSpec((tq, HEAD), lambda h, t, qi, kj: (kj[t], h)),
                       pl.BlockSpec((tq, HEAD), lambda h, t, qi, kj: (kj[t], h)),
                       pl.BlockSpec((1, 1, tq), lambda h, t, qi, kj: (h, 0, kj[t])),
                       pl.BlockSpec((1, T, 1), lambda h, t, qi, kj: (h, 0, 0))] + [HBM] * n_out,
            scratch_shapes=[pltpu.VMEM((tq, HEAD), F32), pltpu.VMEM((tq, HEAD), F32), pltpu.VMEM((1, tq), F32)]
            + list(x_sems)),
        out_shape=[jax.ShapeDtypeStruct((T, dm.AW), F32), jax.ShapeDtypeStruct((T, dm.AW), BF16),
                   jax.ShapeDtypeStruct((T, dm.AW), BF16), jax.ShapeDtypeStruct((H, 1, T), F32),
                   jax.ShapeDtypeStruct((H, T, 1), F32)] + list(x_outs),
        compiler_params=_cparams(("arbitrary", "arbitrary")),
    )(qi_tab, kj_tab, main, main, main, cum_row, lse, delta, do, *x_ins)
    return outs[:5], tuple(outs[5:])


FOX_QG = 256


def _fox_sT(kb, q_rows, ck, g, on_diagonal):
    qs = (q_rows * (HEAD ** -0.5)).astype(BF16)
    reps = q_rows.shape[0] // HEAD
    sT = _nt(kb, qs) - (jnp.concatenate([ck] * reps, axis=1) if reps > 1 else ck)
    if on_diagonal:
        key = lax.broadcasted_iota(jnp.int32, sT.shape, 0)
        qry = g * q_rows.shape[0] + lax.broadcasted_iota(jnp.int32, sT.shape, 1)
        sT = jnp.where(key <= qry, sT, NEG)
    return sT, qs


def _foxt_fwd(main, ckb, dm, fused=None, tq=1024):
    T, H = dm.T, dm.H
    tq = min(tq, T)
    nq = T // tq
    qg = min(FOX_QG, tq)
    ns = tq // qg
    qi_tab, kj_tab = _fox_pairs(nq, kv_major=False)
    npairs = int(qi_tab.shape[0])
    x_ins, x_outs, x_sems, n_in, n_out = _fused_parts(fused)

    def body(qi_ref, kj_ref, q_ref, k_ref, v_ref, ck_ref, *rest):
        x_in, (o_ref, lse_ref) = rest[:n_in], rest[n_in:n_in + 2]
        x_out = rest[n_in + 2:n_in + 2 + n_out]
        m_ref, l_ref, acc_ref = rest[n_in + 2 + n_out:n_in + 5 + n_out]
        hd = pl.program_id(0)
        t = pl.program_id(1)
        i = qi_ref[t]
        j = kj_ref[t]
        if fused is not None:
            start, middle, finish = fused.hooks(x_in, x_out, *rest[n_in + 5 + n_out:])
            pl.when((hd == 0) & (t == 0))(start)
            pl.when((hd == H // 2) & (t == 0))(middle)

        @pl.when(j == 0)
        def _():
            m_ref[...] = jnp.full_like(m_ref, NEG)
            l_ref[...] = jnp.zeros_like(l_ref)
            acc_ref[...] = jnp.zeros_like(acc_ref)

        def step(on_diagonal):
            kb = k_ref[...].astype(BF16)
            vb = v_ref[...].astype(BF16)
            ck = ck_ref[0]
            scores = [_fox_sT(kb, q_ref[g * qg:(g + 1) * qg, :], ck, g, on_diagonal)[0] for g in range(ns)]
            for g in range(ns):
                cols = slice(g * qg, (g + 1) * qg)
                sT = scores[g]
                m_old = m_ref[:, cols]
                m_new = jnp.maximum(m_old, jnp.max(sT, axis=0, keepdims=True))
                a = jnp.exp(m_old - m_new)
                pT = jnp.exp(sT - m_new)
                l_ref[:, cols] = a * l_ref[:, cols] + jnp.sum(pT, axis=0, keepdims=True)
                acc_ref[:, cols] = a * acc_ref[:, cols] + _tn(vb, pT.astype(BF16))
                m_ref[:, cols] = m_new

        @pl.when(j < i)
        def _():
            step(False)

        @pl.when(j == i)
        def _():
            step(True)
            l = l_ref[...]
            o_ref[...] = acc_ref[...] / l
            lse_ref[0] = m_ref[...] + jnp.log(l)

        if fused is not None:
            pl.when((hd == H - 1) & (t == npairs - 1))(finish)

    outs = pl.pallas_call(
        body, name="fox_fwd",
        grid_spec=pltpu.PrefetchScalarGridSpec(
            num_scalar_prefetch=2, grid=(H, npairs),
            in_specs=[pl.BlockSpec((tq, HEAD), lambda h, t, qi, kj: (qi[t], 4 * H + h)),
                      pl.BlockSpec((tq, HEAD), lambda h, t, qi, kj: (kj[t], 5 * H + h)),
                      pl.BlockSpec((tq, HEAD), lambda h, t, qi, kj: (kj[t], 6 * H + h)),
                      pl.BlockSpec((1, tq, HEAD), lambda h, t, qi, kj: (h, kj[t], 0))] + [HBM] * n_in,
            out_specs=[pl.BlockSpec((HEAD, tq), lambda h, t, qi, kj: (h, qi[t])),
                       pl.BlockSpec((1, 1, tq), lambda h, t, qi, kj: (h, 0, qi[t]))] + [HBM] * n_out,
            scratch_shapes=[pltpu.VMEM((1, tq), F32), pltpu.VMEM((1, tq), F32), pltpu.VMEM((HEAD, tq), F32)]
            + list(x_sems)),
        out_shape=[jax.ShapeDtypeStruct((dm.AW, T), F32), jax.ShapeDtypeStruct((H, 1, T), F32)] + list(x_outs),
        compiler_params=_cparams(("arbitrary", "arbitrary")),
    )(qi_tab, kj_tab, main, main, main, ckb, *x_ins)
    return outs[0], outs[1], tuple(outs[2:])


def _foxt_delta(doT, oT, dm, tc=1024):
    T, H = dm.T, dm.H
    tc = min(tc, T)

    def body(do_ref, o_ref, d_ref):
        d_ref[0] = jnp.sum(do_ref[...] * o_ref[...], axis=0, keepdims=True)

    blk = pl.BlockSpec((HEAD, tc), lambda h, i: (h, i))
    return pl.pallas_call(
        body, name="fox_delta", grid=(H, T // tc),
        in_specs=[blk, blk], out_specs=pl.BlockSpec((1, 1, tc), lambda h, i: (h, 0, i)),
        out_shape=jax.ShapeDtypeStruct((H, 1, T), F32),
        compiler_params=_cparams(("parallel", "parallel")),
    )(doT, oT)


def _foxt_bwd(main, ckb, lse, delta, doT, dm, fused=None, tq=1024):
    T, H = dm.T, dm.H
    tq = min(tq, T)
    nq = T // tq
    qg = min(FOX_QG, tq)
    ns = tq // qg
    qi_tab, kj_tab = _fox_pairs(nq, kv_major=True)
    npairs = int(qi_tab.shape[0])
    x_ins, x_outs, x_sems, n_in, n_out = _fused_parts(fused)

    def body(qi_ref, kj_ref, q_ref, k_ref, v_ref, ck_ref, lse_ref, dl_ref, do_ref, *rest):
        x_in = rest[:n_in]
        dq_ref, dk_ref, dv_ref, dr_ref, dc_ref = rest[n_in:n_in + 5]
        x_out = rest[n_in + 5:n_in + 5 + n_out]
        dq_acc, dk_acc, dv_acc = rest[n_in + 5 + n_out:n_in + 8 + n_out]
        hd = pl.program_id(0)
        t = pl.program_id(1)
        i = qi_ref[t]
        kj = kj_ref[t]
        if fused is not None:
            start, middle, finish = fused.hooks(x_in, x_out, *rest[n_in + 8 + n_out:])
            pl.when((hd == 0) & (t == 0))(start)
            pl.when((hd == H // 2) & (t == 0))(middle)

        @pl.when(t == 0)
        def _():
            dq_acc[...] = jnp.zeros_like(dq_acc)
            dr_ref[...] = jnp.zeros_like(dr_ref)

        @pl.when(i == kj)
        def _():
            dk_acc[...] = jnp.zeros_like(dk_acc)
            dv_acc[...] = jnp.zeros_like(dv_acc)

        def step(on_diagonal):
            kb = k_ref[...].astype(BF16)
            vb = v_ref[...].astype(BF16)
            ck = ck_ref[0]
            ones = jnp.ones((qg, HEAD), BF16)
            dq_g, dr_g = [], []
            ahead = []
            for g in range(ns):
                cols = slice(g * qg, (g + 1) * qg)
                sT, qs = _fox_sT(kb, q_ref[cols, :], ck, g, on_diagonal)
                dob = do_ref[:, cols].astype(BF16)
                ahead.append((sT, qs, dob, _nn(vb, dob)))
            for g in range(ns):
                cols = slice(g * qg, (g + 1) * qg)
                sT, qs, dob, dpT = ahead[g]
                pT = jnp.exp(sT - lse_ref[0, :, cols])
                dsT = pT * (dpT - dl_ref[0, :, cols])
                dsb = dsT.astype(BF16)
                dv_acc[...] += _nt(pT.astype(BF16), dob)
                dk_acc[...] += _nn(dsb, jnp.concatenate([qs, ones], axis=1))
                dq_g.append(_tn(kb, dsb) * (HEAD ** -0.5))
                dr_g.append(_nn(jnp.ones((8, tq), BF16), dsb)[0:1])
            dq_acc[i] += jnp.concatenate(dq_g, axis=1)
            dr_ref[0, pl.ds(i, 1), :] += jnp.concatenate(dr_g, axis=1)

        @pl.when(i == kj)
        def _():
            step(True)

        @pl.when(i > kj)
        def _():
            step(False)

        @pl.when(i == nq - 1)
        def _():
            acc = dk_acc[...]
            dk_ref[...] = acc[:, :HEAD].astype(BF16)
            dc_ref[0] = acc[:, HEAD:HEAD + 1]
            dv_ref[...] = dv_acc[...].astype(BF16)

        @pl.when(t == npairs - 1)
        def _():
            for b in range(nq):
                dq_ref[b * tq:(b + 1) * tq, :] = dq_acc[b].T.astype(BF16)

        if fused is not None:
            pl.when((hd == H - 1) & (t == npairs - 1))(finish)

    qrow = pl.BlockSpec((1, 1, tq), lambda h, t, qi, kj: (h, 0, qi[t]))
    outs = pl.pallas_call(
        body, name="fox_bwd",
        grid_spec=pltpu.PrefetchScalarGridSpec(
            num_scalar_prefetch=2, grid=(H, npairs),
            in_specs=[pl.BlockSpec((tq, HEAD), lambda h, t, qi, kj: (qi[t], 4 * H + h)),
                      pl.BlockSpec((tq, HEAD), lambda h, t, qi, kj: (kj[t], 5 * H + h)),
                      pl.BlockSpec((tq, HEAD), lambda h, t, qi, kj: (kj[t], 6 * H + h)),
                      pl.BlockSpec((1, tq, HEAD), lambda h, t, qi, kj: (h, kj[t], 0)),
                      qrow, qrow,
                      pl.BlockSpec((HEAD, tq), lambda h, t, qi, kj: (h, qi[t]))] + [HBM] * n_in,
            out_specs=[pl.BlockSpec((T, HEAD), lambda h, t, qi, kj: (0, h)),
                       pl.BlockSpec((tq, HEAD), lambda h, t, qi, kj: (kj[t], h)),
                       pl.BlockSpec((tq, HEAD), lambda h, t, qi, kj: (kj[t], h)),
                       pl.BlockSpec((1, nq, tq), lambda h, t, qi, kj: (h, 0, 0)),
                       pl.BlockSpec((1, tq, 1), lambda h, t, qi, kj: (h, kj[t], 0))] + [HBM] * n_out,
            scratch_shapes=[pltpu.VMEM((nq, HEAD, tq), F32), pltpu.VMEM((tq, 2 * HEAD), F32),
                            pltpu.VMEM((tq, HEAD), F32)] + list(x_sems)),
        out_shape=[jax.ShapeDtypeStruct((T, dm.AW), BF16), jax.ShapeDtypeStruct((T, dm.AW), BF16),
                   jax.ShapeDtypeStruct((T, dm.AW), BF16), jax.ShapeDtypeStruct((H, nq, tq), F32),
                   jax.ShapeDtypeStruct((H, T, 1), F32)] + list(x_outs),
        compiler_params=_cparams(("arbitrary", "arbitrary")),
    )(qi_tab, kj_tab, main, main, main, ckb, lse, delta, doT, *x_ins)
    return outs[:5], tuple(outs[5:])


def _local_step(dm, x, tgt, w_main, w_rest, later_weights, bias_p, lb_logits, norm_w, g1, g2, g3, g4,
                reduce_hooks=None):
    T, D, H = dm.T, dm.D, dm.H
    h1, r1 = _rms_fwd(x, g1)
    main = _mm(h1, w_main, "nn", F32, "proj_main")
    rest = _mm(h1, w_rest, "nn", F32, "proj_rest", tn=1408)
    ya, o_a, states = _hgrn_fwd(main, lb_logits, norm_w, dm)
    cum = _cum_fwd(rest, bias_p, dm)
    ckb = jnp.broadcast_to(cum[:, :H].T[:, :, None], (H, T, HEAD))
    if isinstance(later_weights, Fused):
        oT_b, lse, later_weights = _foxt_fwd(main, ckb, dm, fused=later_weights)
    else:
        oT_b, lse, _ = _foxt_fwd(main, ckb, dm)
    w_up_a, w_up_b, w_o, w_ffn_in, w_ffn_down = later_weights
    y_a = _mm(ya, w_up_a, "nn", F32, "up_a")
    y_b = _mm(oT_b, w_up_b, "tn", F32, "up_b")
    merged = _merge_fwd(rest, y_a, y_b)
    u = _mm(merged, w_o, "nn", F32, "w_o")
    x1, r2, h3, r3 = _post_pre(x, u, g2, g3)
    gate, up, act = _ffn_in_swiglu(h3, w_ffn_in)
    w = _mm(act, w_ffn_down, "nn", F32, "ffn_down")
    loss, dy, dw, dg4 = _loss_bwd(x1, w, g4, tgt)
    dgate, dup = _d_act_swiglu(dw, w_ffn_down, gate, up)
    d_ffn_down = _mm(act, dw, "tn", F32, "dw_ffn_down", tm=1408)
    dh3 = _mm(dgate, w_ffn_in[:, :dm.FF], "nt", F32, "d_h3_gate")
    dh3 = _mm(dup, w_ffn_in[:, dm.FF:], "nt", F32, "d_h3_up", add=dh3)
    d_ffn_in = jnp.concatenate([_mm(h3, dgate, "tn", F32, "dw_ffn_gate", tn=1408),
                                _mm(h3, dup, "tn", F32, "dw_ffn_up", tn=1408)], axis=1)
    dx1, du, dg3, dg2 = _rms_bwd2(dy, dh3, x1, r3, g3, u, r2, g2)
    dmerged = _mm(du, w_o, "nt", F32, "d_merged")
    d_w_o = _mm(merged, du, "tn", F32, "dw_o")
    dy_a, dy_b, dgates = _merge_bwd(dmerged, rest, y_a, y_b)
    dya = _mm(dy_a, w_up_a, "nt", F32, "d_ya")
    d_up_a = _mm(ya, dy_a, "tn", F32, "dw_up_a")
    doT = _mm(w_up_b, dy_b, "nt", F32, "d_ob")
    d_up_b = _mm(oT_b, dy_b, "nn", F32, "dw_up_b")
    later_grads = [d_up_a, d_up_b, d_w_o, d_ffn_in, d_ffn_down]
    d_a, dlb, dnw_h, got = _hgrn_bwd(main, lb_logits, norm_w, o_a, states, dya, dm,
                                     fused=reduce_hooks.swap_later(later_grads) if reduce_hooks else None)
    (dq_b, dk_b, dv_b, d_over_keys, d_over_queries), later_from_chips = _foxt_bwd(
        main, ckb, lse, _foxt_delta(doT, oT_b, dm), doT, dm,
        fused=reduce_hooks.scatter_later(got) if reduce_hooks else None)
    dcum = jnp.pad((d_over_keys.reshape(H, T) - d_over_queries.reshape(H, T)).T, ((0, 0), (0, HEAD - H)))
    dbf, dbias = _cum_bwd(dcum, rest, bias_p, dm)
    dmain = jnp.concatenate([d_a, dq_b, dk_b, dv_b], axis=1)
    drest = jnp.concatenate([dgates, dbf], axis=1)
    d_main = _mm(h1, dmain, "tn", F32, "dw_main")
    d_rest = _mm(h1, drest, "tn", F32, "dw_rest", tn=1408)
    if reduce_hooks:
        dh1, w_in_from_chips = _mm(dmain, w_main, "nt", F32, "d_h1_main",
                                   fused=reduce_hooks.scatter_w_in(d_main, d_rest))
    else:
        dh1, w_in_from_chips = _mm(dmain, w_main, "nt", F32, "d_h1_main"), ()
    dh1 = _mm(drest, w_rest, "nt", F32, "d_h1_rest", add=dh1)
    dx, dg1 = _rms_bwd1(dx1, dh1, x, r1, g1)
    big = dict(main=d_main, rest=d_rest, up_a=d_up_a, up_b=d_up_b, w_o=d_w_o, ffn_in=d_ffn_in, ffn_down=d_ffn_down)
    small = dict(loss=loss, bias=dbias, norm_w=jnp.sum(dnw_h, axis=0), lb=dlb, g1=dg1, g2=dg2, g3=dg3, g4=dg4)
    return dx, big, small, tuple(w_in_from_chips) + tuple(later_from_chips)


HBM = pl.BlockSpec(memory_space=pltpu.HBM)


def _place():
    x, y, c = lax.axis_index("x"), lax.axis_index("y"), lax.axis_index("c")
    chips = [(1 - x, y), (x, 1 - y), (1 - x, 1 - y)]
    return x, y, c, chips


class Sharded(NamedTuple):
    kind: str
    r: int
    c: int

    @property
    def full(self):
        return (self.r, 4 * self.c) if self.kind == "col" else (4 * self.r, self.c)

    @property
    def half(self):
        return (self.r // 2, self.c) if self.kind == "col" else (self.r, self.c // 2)

    @property
    def half_of_full(self):
        return (self.r // 2, 4 * self.c) if self.kind == "col" else (4 * self.r, self.c // 2)

    def shard_window(self, ref, s):
        if self.kind == "col":
            return ref.at[:, pl.ds(pl.multiple_of(s * self.c, 128), self.c)]
        return ref.at[pl.ds(pl.multiple_of(s * self.r, 16), self.r), :]

    def half_window(self, ref, s, h):
        if self.kind == "col":
            return ref.at[pl.ds(pl.multiple_of(h * (self.r // 2), 16), self.r // 2),
                          pl.ds(pl.multiple_of(s * self.c, 128), self.c)]
        return ref.at[pl.ds(pl.multiple_of(s * self.r, 16), self.r),
                      pl.ds(pl.multiple_of(h * (self.c // 2), 128), self.c // 2)]

    def half_of(self, ref, h):
        if self.kind == "col":
            n = ref.shape[0] // 2
            return ref.at[pl.ds(pl.multiple_of(h * n, 16), n), :]
        n = ref.shape[1] // 2
        return ref.at[:, pl.ds(pl.multiple_of(h * n, 128), n)]

    def window_of_half(self, ref, s):
        if self.kind == "col":
            return ref.at[:, pl.ds(pl.multiple_of(s * self.c, 128), self.c)]
        return ref.at[pl.ds(pl.multiple_of(s * self.r, 16), self.r), :]


def _gather_hooks(specs):
    n = len(specs)

    def hooks(w_refs, f_refs, send_sems, recv_sems):
        x, y, c, chips = _place()
        q = 2 * x + y
        sibling = (x, y, 1 - c)

        def copy(t, k, src, dst, to):
            return pltpu.make_async_remote_copy(
                src_ref=src, dst_ref=dst, send_sem=send_sems.at[7 * t + k], recv_sem=recv_sems.at[7 * t + k],
                device_id=to, device_id_type=MESH)

        def over_ici(t, j, chip_from, to):
            src = specs[t].half_of(w_refs[t], c)
            return copy(t, j, src, specs[t].half_window(f_refs[t], 2 * chip_from[0] + chip_from[1], c), to)

        def passed_on(t, j, chip_from, half):
            win = specs[t].half_window(f_refs[t], 2 * chip_from[0] + chip_from[1], half)
            return copy(t, 3 + j, win, win, sibling)

        def own_shard(t):
            return copy(t, 6, w_refs[t], specs[t].shard_window(f_refs[t], q), sibling)

        def start():
            for t in range(n):
                own_shard(t).start()
                for j, chip in enumerate(chips):
                    over_ici(t, j, (x, y), (*chip, c)).start()

        def middle():
            for t in range(n):
                for j, chip in enumerate(chips):
                    over_ici(t, j, chip, sibling).wait_recv()
                    passed_on(t, j, chip, c).start()

        def finish():
            for t in range(n):
                own_shard(t).wait_recv()
                for j, chip in enumerate(chips):
                    passed_on(t, j, chip, 1 - c).wait_recv()
            for t in range(n):
                own_shard(t).wait_send()
                for j, chip in enumerate(chips):
                    over_ici(t, j, (x, y), (*chip, c)).wait_send()
                    passed_on(t, j, chip, c).wait_send()

        return start, middle, finish

    sems = (pltpu.SemaphoreType.DMA((7 * n,)), pltpu.SemaphoreType.DMA((7 * n,)))
    return hooks, sems


def _gather_fused(shards, specs):
    hooks, sems = _gather_hooks(specs)
    outs = tuple(jax.ShapeDtypeStruct(sp.full, w.dtype) for sp, w in zip(specs, shards))
    return Fused(ins=tuple(shards), outs=outs, sems=sems, hooks=hooks)


def _run_fused(fused, name):
    n_in, n_out = len(fused.ins), len(fused.outs)

    def body(*refs):
        start, middle, finish = fused.hooks(refs[:n_in], refs[n_in:n_in + n_out], *refs[n_in + n_out:])
        start()
        middle()
        finish()

    return pl.pallas_call(
        body, name=name, in_specs=[HBM] * n_in, out_specs=[HBM] * n_out,
        out_shape=list(fused.outs), scratch_shapes=list(fused.sems),
    )(*fused.ins)


def _swap_fused(fulls, specs):
    n = len(fulls)

    def hooks(g_refs, o_refs, send_sems, recv_sems):
        x, y, c, _ = _place()

        def copies():
            return [pltpu.make_async_remote_copy(
                src_ref=specs[t].half_of(g_refs[t], 1 - c), dst_ref=o_refs[t],
                send_sem=send_sems.at[t], recv_sem=recv_sems.at[t],
                device_id=(x, y, 1 - c), device_id_type=MESH) for t in range(n)]

        def start():
            for cp in copies():
                cp.start()

        def finish():
            for cp in copies():
                cp.wait_recv()
            for cp in copies():
                cp.wait_send()

        return start, lambda: None, finish

    outs = tuple(jax.ShapeDtypeStruct(sp.half_of_full, g.dtype) for sp, g in zip(specs, fulls))
    sems = (pltpu.SemaphoreType.DMA((n,)), pltpu.SemaphoreType.DMA((n,)))
    return Fused(ins=tuple(fulls), outs=outs, sems=sems, hooks=hooks)


def _rtile(n, target):
    t = min(n, (target // 16) * 16)
    while n % t:
        t -= 16
    return t


def _add_sibling(full, got, sp, cq_idx, name):
    hr, hc = sp.half
    tr = _rtile(hr, 256)
    nrt = hr // tr

    def body(cq_ref, g_ref, r_ref, ob_ref, of_ref):
        s = pl.program_id(1)
        v = g_ref[...] + r_ref[...]
        ob_ref[...] = v.astype(BF16)

        @pl.when(s == cq_ref[1])
        def _():
            of_ref[...] = v

    if sp.kind == "col":
        g_spec = pl.BlockSpec((tr, hc), lambda i, s, cq: (cq[0] * nrt + i, s))
        r_spec = pl.BlockSpec((tr, hc), lambda i, s, cq: (i, s))
    else:
        g_spec = pl.BlockSpec((tr, hc), lambda i, s, cq: (s * nrt + i, cq[0]))
        r_spec = pl.BlockSpec((tr, hc), lambda i, s, cq: (s * nrt + i, 0))
    return pl.pallas_call(
        body, name=name,
        grid_spec=pltpu.PrefetchScalarGridSpec(
            num_scalar_prefetch=1, grid=(nrt, 4),
            in_specs=[g_spec, r_spec],
            out_specs=[r_spec, pl.BlockSpec((tr, hc), lambda i, s, cq: (i, 0))]),
        out_shape=[jax.ShapeDtypeStruct(sp.half_of_full, BF16), jax.ShapeDtypeStruct(sp.half, F32)],
        compiler_params=_cparams(("parallel", "arbitrary")),
    )(cq_idx, full, got)


def _scatter_fused(sums, specs):
    n = len(sums)

    def hooks(a_refs, o_refs, send_sems, recv_sems):
        x, y, c, chips = _place()

        def copies():
            return [pltpu.make_async_remote_copy(
                src_ref=specs[t].window_of_half(a_refs[t], 2 * chip[0] + chip[1]), dst_ref=o_refs[t].at[j],
                send_sem=send_sems.at[3 * t + j], recv_sem=recv_sems.at[3 * t + j],
                device_id=(*chip, c), device_id_type=MESH) for t in range(n) for j, chip in enumerate(chips)]

        def start():
            for cp in copies():
                cp.start()

        def finish():
            for cp in copies():
                cp.wait_recv()
            for cp in copies():
                cp.wait_send()

        return start, lambda: None, finish

    outs = tuple(jax.ShapeDtypeStruct((3,) + sp.half, a.dtype) for sp, a in zip(specs, sums))
    sems = (pltpu.SemaphoreType.DMA((3 * n,)), pltpu.SemaphoreType.DMA((3 * n,)))
    return Fused(ins=tuple(sums), outs=outs, sems=sems, hooks=hooks)


def _add_chips(own, got, sp, cq_idx, name):
    hr, hc = sp.half
    tr = _rtile(hr, 256)
    nrt = hr // tr

    def body(cq_ref, a_ref, r_ref, o_ref):
        o_ref[...] = ((a_ref[...] + r_ref[0].astype(F32)) + r_ref[1].astype(F32)) + r_ref[2].astype(F32)

    if sp.kind == "col":
        o_spec = pl.BlockSpec((tr, hc), lambda i, cq: (cq[0] * nrt + i, 0))
    else:
        o_spec = pl.BlockSpec((tr, hc), lambda i, cq: (i, cq[0]))
    return pl.pallas_call(
        body, name=name,
        grid_spec=pltpu.PrefetchScalarGridSpec(
            num_scalar_prefetch=1, grid=(nrt,),
            in_specs=[pl.BlockSpec((tr, hc), lambda i, cq: (i, 0)), pl.BlockSpec((3, tr, hc), lambda i, cq: (0, i, 0))],
            out_specs=o_spec),
        out_shape=jax.ShapeDtypeStruct((sp.r, sp.c), F32),
        compiler_params=_cparams(("parallel",)),
    )(cq_idx, own, got)


def _share_halves(shards, specs):
    n = len(shards)

    def body(*refs):
        o_refs = refs[n:2 * n]
        send_sems, recv_sems = refs[2 * n:]
        x, y, c, _ = _place()

        def copy(t, half):
            win = specs[t].half_of(o_refs[t], half)
            return pltpu.make_async_remote_copy(
                src_ref=win, dst_ref=win, send_sem=send_sems.at[t], recv_sem=recv_sems.at[t],
                device_id=(x, y, 1 - c), device_id_type=MESH)

        for t in range(n):
            copy(t, c).start()
        for t in range(n):
            copy(t, 1 - c).wait_recv()
        for t in range(n):
            copy(t, c).wait_send()

    return pl.pallas_call(
        body, name="share_halves",
        in_specs=[HBM] * n, out_specs=[HBM] * n,
        out_shape=[jax.ShapeDtypeStruct((sp.r, sp.c), F32) for sp in specs],
        input_output_aliases={t: t for t in range(n)},
        scratch_shapes=[pltpu.SemaphoreType.DMA((n,)), pltpu.SemaphoreType.DMA((n,))],
    )(*shards)


def _sum_small(vec):
    rows, w = vec.shape

    def body(v_ref, o_ref, buf, send_sems, recv_sems):
        x, y, c, _ = _place()
        me = 4 * x + 2 * y + c
        buf[me] = v_ref[...]
        cps = []
        for k in range(1, 8):
            to = (x ^ (k >> 2), y ^ ((k >> 1) & 1), c ^ (k & 1))
            cps.append(pltpu.make_async_remote_copy(
                src_ref=v_ref, dst_ref=buf.at[me], send_sem=send_sems.at[k - 1], recv_sem=recv_sems.at[k - 1],
                device_id=to, device_id_type=MESH))
        for cp in cps:
            cp.start()
        for k in range(1, 8):
            pltpu.make_async_remote_copy(
                src_ref=v_ref, dst_ref=buf.at[me ^ k], send_sem=send_sems.at[k - 1], recv_sem=recv_sems.at[k - 1],
                device_id=(x, y, c), device_id_type=MESH).wait_recv()
        for cp in cps:
            cp.wait_send()
        total = buf[0]
        for d in range(1, 8):
            total = total + buf[d]
        o_ref[...] = total

    return pl.pallas_call(
        body, name="sum_small",
        in_specs=[pl.BlockSpec(memory_space=pltpu.VMEM)], out_specs=pl.BlockSpec(memory_space=pltpu.VMEM),
        out_shape=jax.ShapeDtypeStruct((rows, w), F32),
        scratch_shapes=[pltpu.VMEM((8, rows, w), F32), pltpu.SemaphoreType.DMA((7,)), pltpu.SemaphoreType.DMA((7,))],
    )(vec)


def _adam_math(w, g, m, v):
    m = ADAM_B1 * m + (1.0 - ADAM_B1) * g
    v = ADAM_B2 * v + (1.0 - ADAM_B2) * (g * g)
    m_hat = m / (1.0 - ADAM_B1 ** ADAM_STEP)
    v_hat = v / (1.0 - ADAM_B2 ** ADAM_STEP)
    delta = -ADAM_LR * (m_hat / (jnp.sqrt(v_hat) + ADAM_EPS) + ADAM_WD * w)
    return delta, m, v


def _adamw(w, g, m, v, name, tr=128):
    _, R, Cn = w.shape
    tr = min(tr, R)
    assert R % tr == 0

    def body(w_ref, g_ref, m_ref, v_ref, go_ref, d_ref, mo_ref, vo_ref):
        gv = g_ref[...]
        d, mn, vn = _adam_math(w_ref[0], gv, m_ref[0], v_ref[0])
        go_ref[0] = gv
        d_ref[0] = d
        mo_ref[0] = mn
        vo_ref[0] = vn

    blk = pl.BlockSpec((1, tr, Cn), lambda i: (0, i, 0))
    sds = jax.ShapeDtypeStruct((1, R, Cn), F32)
    return pl.pallas_call(
        body, name=name, grid=(R // tr,),
        in_specs=[blk, pl.BlockSpec((tr, Cn), lambda i: (i, 0)), blk, blk], out_specs=[blk] * 4, out_shape=[sds] * 4,
        compiler_params=_cparams(("parallel",)),
    )(w, g, m, v)


ROW_LOSS, ROW_BIAS, ROW_NORM, ROW_LB0, ROW_G1, ROW_G2, ROW_G3, ROW_G4, ROW_LB1 = range(9)
SMALL_ROWS = 16


def _small_update(gsum, wp, mp, vp):
    _, w = gsum.shape

    def body(g_ref, w_ref, m_ref, v_ref, go_ref, d_ref, mo_ref, vo_ref):
        wv = w_ref[...]
        l0 = wv[ROW_LB0:ROW_LB0 + 1, :]
        l1 = wv[ROW_LB1:ROW_LB1 + 1, :]
        mx = jnp.maximum(l0, l1)
        e0 = jnp.exp(l0 - mx)
        e1 = jnp.exp(l1 - mx)
        p0 = e0 / (e0 + e1)
        gs = g_ref[...]
        dl0 = gs[ROW_LB0:ROW_LB0 + 1, :] * p0 * (1.0 - p0)
        row8 = lax.broadcasted_iota(jnp.int32, gs.shape, 0)
        top = jnp.where(row8 == ROW_LB0, dl0, jnp.where(row8 == ROW_LOSS, 0.0, gs))
        bot = jnp.where(row8 == ROW_LB1 - 8, -dl0, 0.0)
        g16 = jnp.concatenate([top, bot], axis=0)
        d, mn, vn = _adam_math(wv, g16, m_ref[...], v_ref[...])
        go_ref[...] = g16
        d_ref[...] = d
        mo_ref[...] = mn
        vo_ref[...] = vn

    sds = jax.ShapeDtypeStruct((SMALL_ROWS, w), F32)
    full = pl.BlockSpec(memory_space=pltpu.VMEM)
    return pl.pallas_call(
        body, name="small_update", in_specs=[full] * 4, out_specs=[full] * 4, out_shape=[sds] * 4,
    )(gsum, wp, mp, vp)


def _w_in_layout(dm):
    cs = dm.NIN // 4
    place = [((cs * q) // 128, (cs * q) % 128) for q in range(4)]
    cp = -(-(cs + max(sh for _, sh in place)) // 128) * 128
    return cs, cp, place


def _zeros_cols(rows, n, dtype):
    return jnp.zeros((rows, n), dtype)


def _a_part_tiles(v, H, to_head_major):
    tile = lambda k: v[:, k * HEAD:(k + 1) * HEAD]
    if to_head_major:
        tiles = [tile(s * H + h) for h in range(H) for s in range(4)]
    else:
        tiles = [tile(h * 4 + s) for s in range(4) for h in range(H)]
    return jnp.concatenate(tiles + [v[:, 4 * H * HEAD:]], axis=1)


def _unshuffle_w_in(wp, dm, tr=128):
    D, H = dm.D, dm.H
    cs, cp, place = _w_in_layout(dm)
    nm, nrest = dm.NMAIN, dm.NREST
    ng = nm + nrest
    tr = min(tr, D)

    def body(x_ref, main_ref, rest_ref):
        g = None
        for q, (t0, sh) in enumerate(place):
            xq = x_ref[:, q * cp:(q + 1) * cp]
            yq = pltpu.roll(xq, sh, axis=1) if sh else xq
            width = min(cp, ng - t0 * 128)
            parts = [_zeros_cols(tr, t0 * 128, wp.dtype)] if t0 else []
            parts.append(yq[:, :width])
            if ng - t0 * 128 - width:
                parts.append(_zeros_cols(tr, ng - t0 * 128 - width, wp.dtype))
            placed = jnp.concatenate(parts, axis=1)
            g = placed if g is None else g + placed
        main_ref[...] = _a_part_tiles(g[:, :nm], H, to_head_major=True)
        tail = g[:, nm:]
        gates = pltpu.roll(tail, nrest - H, axis=1)[:, :2 * D]
        lane = lax.broadcasted_iota(jnp.int32, (tr, HEAD), 1)
        rest_ref[...] = jnp.concatenate([gates, jnp.where(lane < H, tail[:, :HEAD], 0)], axis=1)

    return pl.pallas_call(
        body, name="unshuffle_w_in", grid=(D // tr,),
        in_specs=[pl.BlockSpec((tr, 4 * cp), lambda i: (i, 0))],
        out_specs=[pl.BlockSpec((tr, nm), lambda i: (i, 0)), pl.BlockSpec((tr, nrest), lambda i: (i, 0))],
        out_shape=[jax.ShapeDtypeStruct((D, nm), wp.dtype), jax.ShapeDtypeStruct((D, nrest), wp.dtype)],
        compiler_params=_cparams(("parallel",)),
    )(wp)


def _shuffle_w_in(d_main, d_rest, dm, tr=64):
    D, H = dm.D, dm.H
    cs, cp, place = _w_in_layout(dm)
    nm, nrest = dm.NMAIN, dm.NREST
    ng = nm + nrest
    tr = min(tr, D)

    def body(m_ref, r_ref, o_ref):
        rv = r_ref[...]
        lane = lax.broadcasted_iota(jnp.int32, (tr, HEAD), 1)
        tail = pltpu.roll(jnp.concatenate([rv[:, :2 * D], _zeros_cols(tr, HEAD, F32)], axis=1), H, axis=1)
        head = jnp.where(lane < H, rv[:, 2 * D:], 0.0) + tail[:, :HEAD]
        main_std = _a_part_tiles(m_ref[...], H, to_head_major=False)
        g = jnp.concatenate([main_std, head, tail[:, HEAD:]], axis=1)
        lanes = lax.broadcasted_iota(jnp.int32, (tr, cp), 1)
        outs = []
        for q, (t0, sh) in enumerate(place):
            width = min(cp, ng - t0 * 128)
            win = g[:, t0 * 128:t0 * 128 + width]
            if width < cp:
                win = jnp.concatenate([win, _zeros_cols(tr, cp - width, F32)], axis=1)
            xq = pltpu.roll(win, cp - sh, axis=1) if sh else win
            outs.append(jnp.where(lanes < cs, xq, 0.0))
        o_ref[...] = jnp.concatenate(outs, axis=1)

    return pl.pallas_call(
        body, name="shuffle_w_in", grid=(D // tr,),
        in_specs=[pl.BlockSpec((tr, nm), lambda i: (i, 0)), pl.BlockSpec((tr, nrest), lambda i: (i, 0))],
        out_specs=pl.BlockSpec((tr, 4 * cp), lambda i: (i, 0)),
        out_shape=jax.ShapeDtypeStruct((D, 4 * cp), F32),
        compiler_params=_cparams(("parallel",)),
    )(d_main, d_rest)


def _pack_small(dm, bias, norm_w, lb_logits, g1, g2, g3, g4):
    D = dm.D
    row = lambda v: jnp.pad(v.reshape(1, -1), ((0, 0), (0, D - v.size)))
    rows = [jnp.zeros((1, D), F32), row(bias), row(norm_w), row(lb_logits[0]), row(g1), row(g2), row(g3), row(g4),
            row(lb_logits[1]), jnp.zeros((SMALL_ROWS - 9, D), F32)]
    return jnp.concatenate(rows, axis=0)


def _unpack_small(p, dm):
    H, AW = dm.H, dm.AW
    return (p[ROW_BIAS:ROW_BIAS + 1, :H], jnp.concatenate([p[ROW_LB0:ROW_LB0 + 1, :AW], p[ROW_LB1:ROW_LB1 + 1, :AW]], axis=0),
            p[ROW_NORM:ROW_NORM + 1, :HEAD], p[ROW_G1:ROW_G1 + 1], p[ROW_G2:ROW_G2 + 1], p[ROW_G3:ROW_G3 + 1],
            p[ROW_G4:ROW_G4 + 1])


def _step(dm, x, w_in, b_fox_f, hgrn_lb_logits, hgrn_norm_w, w_up_a, w_up_b, w_o, norm_mix_pre, norm_mix_post,
          norm_ffn_pre, norm_ffn_post, w_ffn_in, w_ffn_down, loss_target, moments_m, moments_v):
    xi, yi, ci = lax.axis_index("x"), lax.axis_index("y"), lax.axis_index("c")
    cq_idx = jnp.stack([ci, 2 * xi + yi]).astype(jnp.int32)
    D, AW, FF = dm.D, dm.AW, dm.FF
    cs, cp, _ = _w_in_layout(dm)
    big_names = ["w_in", "w_up_a", "w_up_b", "w_o", "w_ffn_in", "w_ffn_down"]
    specs = [Sharded("col", D, cp), Sharded("col", AW, D // 4), Sharded("col", AW, D // 4),
             Sharded("row", D // 4, D), Sharded("col", D, 2 * FF // 4), Sharded("row", FF // 4, D)]
    shards = [w_in[0], w_up_a[0], w_up_b[0], w_o[0], w_ffn_in[0], w_ffn_down[0]]

    sent = [jnp.pad(shards[0].astype(BF16), ((0, 0), (0, cp - cs)))] + [w.astype(BF16) for w in shards[1:]]
    (f_in,) = _run_fused(_gather_fused(sent[:1], specs[:1]), "gather_w_in")
    w_main, w_rest = _unshuffle_w_in(f_in, dm)

    class ReduceHooks:
        pairs = [None] * 6
        fulls = None

        def pair_sums(self, fulls, from_sibling, first):
            for t, (g, r) in enumerate(zip(fulls, from_sibling), start=first):
                self.pairs[t] = _add_sibling(g, r, specs[t], cq_idx, "add_sibling_" + big_names[t])
            return _scatter_fused([self.pairs[t][0] for t in range(first, first + len(fulls))],
                                  specs[first:first + len(fulls)])

        def swap_later(self, fulls):
            self.fulls = fulls
            return _swap_fused(fulls, specs[1:])

        def scatter_later(self, from_sibling):
            return self.pair_sums(self.fulls, from_sibling, 1)

        def scatter_w_in(self, d_main, d_rest):
            full = [_shuffle_w_in(d_main, d_rest, dm)]
            return self.pair_sums(full, _run_fused(_swap_fused(full, specs[:1]), "swap_halves_w_in"), 0)

    hooks = ReduceHooks()

    bias_p = jnp.pad(b_fox_f, ((0, 0), (0, HEAD - dm.H)))
    dx, _, small, from_chips = _local_step(
        dm, x[0], loss_target[0], w_main, w_rest, _gather_fused(sent[1:], specs[1:]), bias_p,
        hgrn_lb_logits, hgrn_norm_w, norm_mix_pre, norm_mix_post, norm_ffn_pre, norm_ffn_post,
        reduce_hooks=hooks)

    halves = [_add_chips(p[1], r, sp, cq_idx, "add_chips_" + n)
              for p, r, sp, n in zip(hooks.pairs, from_chips, specs, big_names)]
    grads = list(_share_halves(halves, specs))
    grads[0] = grads[0][:, :cs]

    row = lambda v: jnp.pad(v.reshape(1, -1), ((0, 0), (0, D - v.size)))
    vec = jnp.concatenate([row(small["loss"][:, :1]), row(small["bias"][:, :dm.H]), row(small["norm_w"]),
                           row(small["lb"]), small["g1"], small["g2"], small["g3"], small["g4"]], axis=0)
    gsum = _sum_small(vec)
    loss = gsum[ROW_LOSS, 0]

    smalls = lambda t: (t["b_fox_f"], t["hgrn_norm_w"], t["hgrn_lb_logits"], t["norm_mix_pre"], t["norm_mix_post"],
                        t["norm_ffn_pre"], t["norm_ffn_post"])
    params = dict(b_fox_f=b_fox_f, hgrn_norm_w=hgrn_norm_w, hgrn_lb_logits=hgrn_lb_logits, norm_mix_pre=norm_mix_pre,
                  norm_mix_post=norm_mix_post, norm_ffn_pre=norm_ffn_pre, norm_ffn_post=norm_ffn_post)
    sg, sd, sm, sv = _small_update(gsum, _pack_small(dm, *smalls(params)), _pack_small(dm, *smalls(moments_m)),
                                   _pack_small(dm, *smalls(moments_v)))
    big_out = {}
    whole = dict(w_in=w_in, w_up_a=w_up_a, w_up_b=w_up_b, w_o=w_o, w_ffn_in=w_ffn_in, w_ffn_down=w_ffn_down)
    for name, g in zip(big_names, grads):
        big_out[name] = _adamw(whole[name], g, moments_m[name], moments_v[name], "adamw_" + name)

    order = ["w_in", "b_fox_f", "hgrn_lb_logits", "hgrn_norm_w", "w_up_a", "w_up_b", "w_o", "norm_mix_pre",
             "norm_mix_post", "norm_ffn_pre", "norm_ffn_post", "w_ffn_in", "w_ffn_down"]
    outs = []
    for kind, packed in enumerate([sg, sd, sm, sv]):
        b, lbl, nw, p1, p2, p3, p4 = _unpack_small(packed, dm)
        sm_map = dict(b_fox_f=b, hgrn_lb_logits=lbl, hgrn_norm_w=nw, norm_mix_pre=p1, norm_mix_post=p2,
                      norm_ffn_pre=p3, norm_ffn_post=p4)
        outs.append([big_out[n][kind] if n in big_out else sm_map[n] for n in order])
    return (loss, dx[None], *outs[0], *outs[1], *outs[2], *outs[3])


def kernel(x, w_in, b_fox_f, hgrn_lb_logits, hgrn_norm_w, w_up_a, w_up_b, w_o, norm_mix_pre, norm_mix_post, norm_ffn_pre, norm_ffn_post, w_ffn_in, w_ffn_down, loss_target, m_w_in, m_b_fox_f, m_hgrn_lb_logits, m_hgrn_norm_w, m_w_up_a, m_w_up_b, m_w_o, m_norm_mix_pre, m_norm_mix_post, m_norm_ffn_pre, m_norm_ffn_post, m_w_ffn_in, m_w_ffn_down, v_w_in, v_b_fox_f, v_hgrn_lb_logits, v_hgrn_norm_w, v_w_up_a, v_w_up_b, v_w_o, v_norm_mix_pre, v_norm_mix_post, v_norm_ffn_pre, v_norm_ffn_post, v_w_ffn_in, v_w_ffn_down):
    dm = Dims(T=x.shape[1], D=x.shape[2], FF=w_ffn_down.shape[1] * 4)
    moments_m = dict(w_in=m_w_in, b_fox_f=m_b_fox_f, hgrn_lb_logits=m_hgrn_lb_logits, hgrn_norm_w=m_hgrn_norm_w,
                     w_up_a=m_w_up_a, w_up_b=m_w_up_b, w_o=m_w_o, norm_mix_pre=m_norm_mix_pre,
                     norm_mix_post=m_norm_mix_post, norm_ffn_pre=m_norm_ffn_pre, norm_ffn_post=m_norm_ffn_post,
                     w_ffn_in=m_w_ffn_in, w_ffn_down=m_w_ffn_down)
    moments_v = dict(w_in=v_w_in, b_fox_f=v_b_fox_f, hgrn_lb_logits=v_hgrn_lb_logits, hgrn_norm_w=v_hgrn_norm_w,
                     w_up_a=v_w_up_a, w_up_b=v_w_up_b, w_o=v_w_o, norm_mix_pre=v_norm_mix_pre,
                     norm_mix_post=v_norm_mix_post, norm_ffn_pre=v_norm_ffn_pre, norm_ffn_post=v_norm_ffn_post,
                     w_ffn_in=v_w_ffn_in, w_ffn_down=v_w_ffn_down)
    return _step(dm, x, w_in, b_fox_f, hgrn_lb_logits, hgrn_norm_w, w_up_a, w_up_b, w_o, norm_mix_pre, norm_mix_post,
                 norm_ffn_pre, norm_ffn_post, w_ffn_in, w_ffn_down, loss_target, moments_m, moments_v)
```

```python
from typing import NamedTuple

import numpy as np
import jax
import jax.numpy as jnp
from jax import lax
from jax.experimental import pallas as pl
from jax.experimental.pallas import tpu as pltpu

F32 = jnp.float32
BF16 = jnp.bfloat16
MESH = pl.DeviceIdType.MESH

RMS_EPS = 1e-6
HEAD = 128
CHUNK = 64
GROUP = 4
LEVELS = (32, 16, 8, 4, 2, 1)
NEG = -1e30

ADAM_LR = 0.001
ADAM_B1 = 0.9
ADAM_B2 = 0.999
ADAM_EPS = 1e-08
ADAM_WD = 0.01
ADAM_STEP = 10

VMEM_LIMIT = 56 * 1024 * 1024


class Dims(NamedTuple):
    T: int
    D: int
    FF: int

    @property
    def AW(self):
        return self.D // 2

    @property
    def H(self):
        return self.AW // HEAD

    @property
    def NMAIN(self):
        return 7 * self.AW

    @property
    def NREST(self):
        return 2 * self.D + HEAD

    @property
    def NIN(self):
        return 7 * self.AW + self.H + 2 * self.D


def _cparams(sem, vmem=VMEM_LIMIT, **kw):
    return pltpu.CompilerParams(dimension_semantics=sem, vmem_limit_bytes=vmem, **kw)


def _tile(n, target):
    if n <= target:
        return n
    t = (target // 128) * 128
    while t >= 128:
        if n % t == 0:
            return t
        t -= 128
    raise ValueError(f"no tile for {n}")


def _dot(a, b, dims):
    return lax.dot_general(a, b, (dims, ((), ())), preferred_element_type=F32)


def _nn(a, b):
    return _dot(a, b, ((1,), (0,)))


def _nt(a, b):
    return _dot(a, b, ((1,), (1,)))


def _tn(a, b):
    return _dot(a, b, ((0,), (0,)))


def _sigmoid(x):
    return jax.nn.sigmoid(x)


def _split2(x):
    hi = x.astype(BF16)
    lo = (x - hi.astype(F32)).astype(BF16)
    return hi, lo


def _split3(x):
    hi = x.astype(BF16)
    r = x - hi.astype(F32)
    mid = r.astype(BF16)
    lo = (r - mid.astype(F32)).astype(BF16)
    return hi, mid, lo


def _mm(a, b, mode, out_dtype, name, add=None, tm=1024, tn=1024, tk=None, fused=None):
    if mode == "nn":
        (M, K), (K2, N) = a.shape, b.shape
    elif mode == "nt":
        (M, K), (N, K2) = a.shape, b.shape
    else:
        (K, M), (K2, N) = a.shape, b.shape
    assert K == K2, (a.shape, b.shape, mode)
    tm, tn = _tile(M, tm), _tile(N, tn)
    isz = lambda t: jnp.dtype(t.dtype).itemsize

    def vmem_need(tk_):
        need = 2 * (tm * tk_ * isz(a) + tk_ * tn * isz(b) + tm * tn * jnp.dtype(out_dtype).itemsize)
        return need + (tm * tn * 4 if K // tk_ > 1 else 0) + (2 * tm * tn * isz(add) if add is not None else 0)

    budget = VMEM_LIMIT - 16 * 1024 * 1024
    for target in ((tk,) if tk is not None else ((2048, 1024, 512) if mode == "tn" else (4096, 2048, 1024, 512))):
        tk_ = _tile(K, target)
        if vmem_need(tk_) <= budget:
            break
    tk = tk_
    nk = K // tk
    assert vmem_need(tk) <= budget, (name, vmem_need(tk))
    if mode == "nn":
        a_spec = pl.BlockSpec((tm, tk), lambda i, j, k: (i, k))
        b_spec = pl.BlockSpec((tk, tn), lambda i, j, k: (k, j))
        op = _nn
    elif mode == "nt":
        a_spec = pl.BlockSpec((tm, tk), lambda i, j, k: (i, k))
        b_spec = pl.BlockSpec((tn, tk), lambda i, j, k: (j, k))
        op = _nt
    else:
        a_spec = pl.BlockSpec((tk, tm), lambda i, j, k: (k, i))
        b_spec = pl.BlockSpec((tk, tn), lambda i, j, k: (k, j))
        op = _tn
    o_spec = pl.BlockSpec((tm, tn), lambda i, j, k: (i, j))
    has_add = add is not None
    n_own = 3 if has_add else 2
    x_ins, x_outs, x_sems, n_in, n_out = _fused_parts(fused)
    grid = (M // tm, N // tn, nk)

    def body(*refs):
        a_ref, b_ref = refs[:2]
        add_ref = refs[2] if has_add else None
        o_ref = refs[n_own + n_in]
        acc = refs[n_own + n_in + 1 + n_out] if nk > 1 else None
        k = pl.program_id(2)
        if fused is not None:
            step = (pl.program_id(0) * grid[1] + pl.program_id(1)) * nk + k
            start, middle, finish_x = fused.hooks(
                refs[n_own:n_own + n_in], refs[n_own + n_in + 1:n_own + n_in + 1 + n_out],
                *refs[n_own + n_in + 1 + n_out + (1 if nk > 1 else 0):])
            pl.when(step == 0)(start)
            pl.when(step == (grid[0] * grid[1] * nk) // 2)(middle)

        def finish(r):
            if has_add:
                r = r + add_ref[...].astype(F32)
            o_ref[...] = r.astype(out_dtype)

        part = op(a_ref[...].astype(BF16), b_ref[...].astype(BF16))
        if nk == 1:
            finish(part)
        else:
            @pl.when(k == 0)
            def _():
                acc[...] = part

            @pl.when((k > 0) & (k < nk - 1))
            def _():
                acc[...] += part

            @pl.when(k == nk - 1)
            def _():
                finish(acc[...] + part)

        if fused is not None:
            pl.when(step == grid[0] * grid[1] * nk - 1)(finish_x)

    in_specs = [a_spec, b_spec] + ([o_spec] if has_add else []) + [HBM] * n_in
    args = (a, b) + ((add,) if has_add else ()) + x_ins
    outs = pl.pallas_call(
        body, name=name, grid=grid,
        in_specs=in_specs, out_specs=[o_spec] + [HBM] * n_out,
        out_shape=[jax.ShapeDtypeStruct((M, N), out_dtype)] + list(x_outs),
        scratch_shapes=([pltpu.VMEM((tm, tn), F32)] if nk > 1 else []) + list(x_sems),
        compiler_params=_cparams(("parallel", "parallel", "arbitrary") if fused is None else ("arbitrary",) * 3),
    )(*args)
    return outs[0] if fused is None else (outs[0], tuple(outs[1:]))


def _rows(tr, w, col=0):
    return pl.BlockSpec((tr, w), lambda i, *_: (i, col))


def _vec(w):
    return pl.BlockSpec((1, w), lambda i, *_: (0, 0))


def _rstd(v):
    return lax.rsqrt(jnp.mean(v * v, axis=-1, keepdims=True) + RMS_EPS)


def _rms_bwd(dn, n, r):
    return r * (dn - n * jnp.mean(dn * n, axis=-1, keepdims=True))


def _colsum(v):
    return jnp.sum(v, axis=0, keepdims=True)


def _rms_fwd(x, g, tr=256):
    T, D = x.shape

    def body(x_ref, g_ref, h_ref, r_ref):
        xv = x_ref[...]
        r = _rstd(xv)
        h_ref[...] = (xv * r * g_ref[...]).astype(BF16)
        r_ref[...] = r

    return pl.pallas_call(
        body, name="rms_fwd", grid=(T // tr,),
        in_specs=[_rows(tr, D), _vec(D)],
        out_specs=[_rows(tr, D), _rows(tr, 1)],
        out_shape=[jax.ShapeDtypeStruct((T, D), BF16), jax.ShapeDtypeStruct((T, 1), F32)],
        compiler_params=_cparams(("parallel",)),
    )(x, g)


def _merge_fwd(rest, y_a, y_b, tr=256):
    T, D = y_a.shape

    def body(ga_ref, gb_ref, ya_ref, yb_ref, o_ref):
        o_ref[...] = (_sigmoid(ga_ref[...]) * ya_ref[...] + _sigmoid(gb_ref[...]) * yb_ref[...]).astype(BF16)

    return pl.pallas_call(
        body, name="merge_fwd", grid=(T // tr,),
        in_specs=[_rows(tr, D, 0), _rows(tr, D, 1), _rows(tr, D), _rows(tr, D)],
        out_specs=_rows(tr, D),
        out_shape=jax.ShapeDtypeStruct((T, D), BF16),
        compiler_params=_cparams(("parallel",)),
    )(rest, rest, y_a, y_b)


def _post_pre(x, u, g2, g3, tr=256):
    T, D = x.shape

    def body(x_ref, u_ref, g2_ref, g3_ref, x1_ref, r2_ref, h3_ref, r3_ref):
        uv = u_ref[...]
        r2 = _rstd(uv)
        x1 = x_ref[...] + uv * r2 * g2_ref[...]
        r3 = _rstd(x1)
        x1_ref[...] = x1
        r2_ref[...] = r2
        h3_ref[...] = (x1 * r3 * g3_ref[...]).astype(BF16)
        r3_ref[...] = r3

    return pl.pallas_call(
        body, name="post_pre", grid=(T // tr,),
        in_specs=[_rows(tr, D), _rows(tr, D), _vec(D), _vec(D)],
        out_specs=[_rows(tr, D), _rows(tr, 1), _rows(tr, D), _rows(tr, 1)],
        out_shape=[jax.ShapeDtypeStruct((T, D), F32), jax.ShapeDtypeStruct((T, 1), F32),
                   jax.ShapeDtypeStruct((T, D), BF16), jax.ShapeDtypeStruct((T, 1), F32)],
        compiler_params=_cparams(("parallel",)),
    )(x, u, g2, g3)


def _ffn_in_swiglu(h3, w_ffn_in, tm=1024, tn=512):
    T, D = h3.shape
    FF = w_ffn_in.shape[1] // 2
    tm, tn = _tile(T, tm), _tile(FF, tn)
    nj = FF // tn

    def body(a_ref, bg_ref, bu_ref, g_ref, u_ref, act_ref):
        av = a_ref[...]
        gv = _nn(av, bg_ref[...])
        uv = _nn(av, bu_ref[...])
        g_ref[...] = gv
        u_ref[...] = uv
        act_ref[...] = (gv * _sigmoid(gv) * uv).astype(BF16)

    blk = pl.BlockSpec((tm, tn), lambda i, j: (i, j))
    return pl.pallas_call(
        body, name="ffn_in_swiglu", grid=(T // tm, nj),
        in_specs=[pl.BlockSpec((tm, D), lambda i, j: (i, 0)),
                  pl.BlockSpec((D, tn), lambda i, j: (0, j)),
                  pl.BlockSpec((D, tn), lambda i, j: (0, j + nj))],
        out_specs=[blk, blk, blk],
        out_shape=[jax.ShapeDtypeStruct((T, FF), F32), jax.ShapeDtypeStruct((T, FF), F32),
                   jax.ShapeDtypeStruct((T, FF), BF16)],
        compiler_params=_cparams(("parallel", "parallel")),
    )(h3, w_ffn_in, w_ffn_in)


def _d_act_swiglu(dw, w_ffn_down, gate, up, tm=1024, tn=512):
    T, D = dw.shape
    FF = w_ffn_down.shape[0]
    tm, tn = _tile(T, tm), _tile(FF, tn)

    def body(a_ref, b_ref, g_ref, u_ref, dg_ref, du_ref):
        dact = _nt(a_ref[...], b_ref[...])
        gv = g_ref[...]
        s = _sigmoid(gv)
        dg_ref[...] = (dact * u_ref[...] * (s * (1.0 + gv * (1.0 - s)))).astype(BF16)
        du_ref[...] = (dact * (gv * s)).astype(BF16)

    blk = pl.BlockSpec((tm, tn), lambda i, j: (i, j))
    sds = jax.ShapeDtypeStruct((T, FF), BF16)
    return pl.pallas_call(
        body, name="d_act_swiglu", grid=(T // tm, FF // tn),
        in_specs=[pl.BlockSpec((tm, D), lambda i, j: (i, 0)), pl.BlockSpec((tn, D), lambda i, j: (j, 0)), blk, blk],
        out_specs=[blk, blk], out_shape=[sds, sds],
        compiler_params=_cparams(("parallel", "parallel")),
    )(dw, w_ffn_down, gate, up)


def _swiglu_fwd(gu, tr=256):
    T, FF2 = gu.shape
    FF = FF2 // 2
    tc = _tile(FF, 1024)
    nc = FF // tc

    def body(g_ref, u_ref, o_ref):
        gv = g_ref[...]
        o_ref[...] = (gv * _sigmoid(gv) * u_ref[...]).astype(BF16)

    return pl.pallas_call(
        body, name="swiglu_fwd", grid=(T // tr, nc),
        in_specs=[pl.BlockSpec((tr, tc), lambda i, j: (i, j)),
                  pl.BlockSpec((tr, tc), lambda i, j: (i, j + nc))],
        out_specs=pl.BlockSpec((tr, tc), lambda i, j: (i, j)),
        out_shape=jax.ShapeDtypeStruct((T, FF), BF16),
        compiler_params=_cparams(("parallel", "parallel")),
    )(gu, gu)


def _loss_bwd(x1, w, g4, tgt, tr=256):
    T, D = x1.shape

    def body(x1_ref, w_ref, g4_ref, t_ref, loss_ref, dy_ref, dw_ref, dg_ref):
        i = pl.program_id(0)

        @pl.when(i == 0)
        def _():
            loss_ref[...] = jnp.zeros_like(loss_ref)
            dg_ref[...] = jnp.zeros_like(dg_ref)

        wv = w_ref[...]
        g4v = g4_ref[...]
        r4 = _rstd(wv)
        n4 = wv * r4
        e = x1_ref[...] + n4 * g4v - t_ref[...]
        loss_ref[...] += 0.5 * jnp.sum(jnp.mean(e * e, axis=-1, keepdims=True), axis=0, keepdims=True)
        dy = e * (1.0 / D)
        dy_ref[...] = dy
        dg_ref[...] += _colsum(dy * n4)
        dw_ref[...] = _rms_bwd(dy * g4v, n4, r4).astype(BF16)

    return pl.pallas_call(
        body, name="loss_bwd", grid=(T // tr,),
        in_specs=[_rows(tr, D), _rows(tr, D), _vec(D), _rows(tr, D)],
        out_specs=[_vec(HEAD), _rows(tr, D), _rows(tr, D), _vec(D)],
        out_shape=[jax.ShapeDtypeStruct((1, HEAD), F32), jax.ShapeDtypeStruct((T, D), F32),
                   jax.ShapeDtypeStruct((T, D), BF16), jax.ShapeDtypeStruct((1, D), F32)],
        compiler_params=_cparams(("arbitrary",)),
    )(x1, w, g4, tgt)


def _swiglu_bwd(gu, dact, tr=128):
    T, FF2 = gu.shape
    FF = FF2 // 2

    def body(g_ref, u_ref, d_ref, o_ref):
        h = pl.program_id(1)
        gv = g_ref[...]
        s = _sigmoid(gv)
        dv = d_ref[...].astype(F32)

        @pl.when(h == 0)
        def _():
            o_ref[...] = (dv * u_ref[...] * (s * (1.0 + gv * (1.0 - s)))).astype(BF16)

        @pl.when(h == 1)
        def _():
            o_ref[...] = (dv * (gv * s)).astype(BF16)

    return pl.pallas_call(
        body, name="swiglu_bwd", grid=(T // tr, 2),
        in_specs=[pl.BlockSpec((tr, FF), lambda i, h: (i, 0)),
                  pl.BlockSpec((tr, FF), lambda i, h: (i, 1)),
                  pl.BlockSpec((tr, FF), lambda i, h: (i, 0))],
        out_specs=pl.BlockSpec((tr, FF), lambda i, h: (i, h)),
        out_shape=jax.ShapeDtypeStruct((T, FF2), BF16),
        compiler_params=_cparams(("parallel", "arbitrary")),
    )(gu, gu, dact)


def _rms_bwd2(dy, dh3, x1, r3, g3, u, r2, g2, tr=256):
    T, D = dy.shape

    def body(dy_ref, dh_ref, x1_ref, r3_ref, g3_ref, u_ref, r2_ref, g2_ref, dx1_ref, du_ref, dg3_ref, dg2_ref):
        i = pl.program_id(0)

        @pl.when(i == 0)
        def _():
            dg3_ref[...] = jnp.zeros_like(dg3_ref)
            dg2_ref[...] = jnp.zeros_like(dg2_ref)

        r3v, r2v = r3_ref[...], r2_ref[...]
        dh = dh_ref[...]
        n3 = x1_ref[...] * r3v
        dg3_ref[...] += _colsum(dh * n3)
        dx1 = dy_ref[...] + _rms_bwd(dh * g3_ref[...], n3, r3v)
        dx1_ref[...] = dx1
        n2 = u_ref[...] * r2v
        dg2_ref[...] += _colsum(dx1 * n2)
        du_ref[...] = _rms_bwd(dx1 * g2_ref[...], n2, r2v).astype(BF16)

    return pl.pallas_call(
        body, name="rms_bwd2", grid=(T // tr,),
        in_specs=[_rows(tr, D), _rows(tr, D), _rows(tr, D), _rows(tr, 1), _vec(D),
                  _rows(tr, D), _rows(tr, 1), _vec(D)],
        out_specs=[_rows(tr, D), _rows(tr, D), _vec(D), _vec(D)],
        out_shape=[jax.ShapeDtypeStruct((T, D), F32), jax.ShapeDtypeStruct((T, D), BF16),
                   jax.ShapeDtypeStruct((1, D), F32), jax.ShapeDtypeStruct((1, D), F32)],
        compiler_params=_cparams(("arbitrary",)),
    )(dy, dh3, x1, r3, g3, u, r2, g2)


def _merge_bwd(dmerged, rest, y_a, y_b, tr=256):
    T, D = dmerged.shape

    def body(dm_ref, ga_ref, gb_ref, ya_ref, yb_ref, dya_ref, dyb_ref, dg_ref):
        h = pl.program_id(1)
        dm = dm_ref[...]

        @pl.when(h == 0)
        def _():
            s = _sigmoid(ga_ref[...])
            dya_ref[...] = (dm * s).astype(BF16)
            dg_ref[...] = (dm * ya_ref[...] * s * (1.0 - s)).astype(BF16)

        @pl.when(h == 1)
        def _():
            s = _sigmoid(gb_ref[...])
            dyb_ref[...] = (dm * s).astype(BF16)
            dg_ref[...] = (dm * yb_ref[...] * s * (1.0 - s)).astype(BF16)

    blk = lambda col: pl.BlockSpec((tr, D), lambda i, h: (i, col))
    return pl.pallas_call(
        body, name="merge_bwd", grid=(T // tr, 2),
        in_specs=[blk(0), blk(0), blk(1), blk(0), blk(0)],
        out_specs=[blk(0), blk(0), pl.BlockSpec((tr, D), lambda i, h: (i, h))],
        out_shape=[jax.ShapeDtypeStruct((T, D), BF16), jax.ShapeDtypeStruct((T, D), BF16),
                   jax.ShapeDtypeStruct((T, 2 * D), BF16)],
        compiler_params=_cparams(("parallel", "arbitrary")),
    )(dmerged, rest, rest, y_a, y_b)


def _d_merged_gates(du, w_o, rest, y_a, y_b, tm=1024, tn=512):
    T, D = du.shape
    tm, tn = _tile(T, tm), _tile(D, tn)
    nj = D // tn

    def body(a_ref, b_ref, ga_ref, gb_ref, ya_ref, yb_ref, dya_ref, dyb_ref, dga_ref, dgb_ref):
        dm = _nt(a_ref[...], b_ref[...])
        sa = _sigmoid(ga_ref[...])
        sb = _sigmoid(gb_ref[...])
        dya_ref[...] = (dm * sa).astype(BF16)
        dyb_ref[...] = (dm * sb).astype(BF16)
        dga_ref[...] = (dm * ya_ref[...] * sa * (1.0 - sa)).astype(BF16)
        dgb_ref[...] = (dm * yb_ref[...] * sb * (1.0 - sb)).astype(BF16)

    blk = pl.BlockSpec((tm, tn), lambda i, j: (i, j))
    sds = jax.ShapeDtypeStruct((T, D), BF16)
    return pl.pallas_call(
        body, name="d_merged_gates", grid=(T // tm, nj),
        in_specs=[pl.BlockSpec((tm, D), lambda i, j: (i, 0)), pl.BlockSpec((tn, D), lambda i, j: (j, 0)),
                  blk, pl.BlockSpec((tm, tn), lambda i, j: (i, j + nj)), blk, blk],
        out_specs=[blk] * 4, out_shape=[sds] * 4,
        compiler_params=_cparams(("parallel", "parallel")),
    )(du, w_o, rest, rest, y_a, y_b)


def _rms_bwd1(dx1, dh1, x, r1, g1, tr=256):
    T, D = x.shape

    def body(dx1_ref, dh_ref, x_ref, r_ref, g_ref, dx_ref, dg_ref):
        i = pl.program_id(0)

        @pl.when(i == 0)
        def _():
            dg_ref[...] = jnp.zeros_like(dg_ref)

        rv = r_ref[...]
        dh = dh_ref[...]
        n = x_ref[...] * rv
        dg_ref[...] += _colsum(dh * n)
        dx_ref[...] = dx1_ref[...] + _rms_bwd(dh * g_ref[...], n, rv)

    return pl.pallas_call(
        body, name="rms_bwd1", grid=(T // tr,),
        in_specs=[_rows(tr, D), _rows(tr, D), _rows(tr, D), _rows(tr, 1), _vec(D)],
        out_specs=[_rows(tr, D), _vec(D)],
        out_shape=[jax.ShapeDtypeStruct((T, D), F32), jax.ShapeDtypeStruct((1, D), F32)],
        compiler_params=_cparams(("arbitrary",)),
    )(dx1, dh1, x, r1, g1)


def _hgrn_consts():
    C = CHUNK
    nl = len(LEVELS) + 1
    w = np.zeros((nl, C, C), np.float32)
    w[0] = np.tril(np.ones((C, C), np.float32))
    for li, m in enumerate(LEVELS, start=1):
        for r in range(C):
            mid = (r // (2 * m)) * 2 * m + m
            if r >= mid:
                w[li, r, mid:r + 1] = 1.0
            else:
                w[li, r, r + 1:mid] = 1.0
    w_all = w.reshape(nl * C, C)
    w2 = np.concatenate([w_all, w_all], axis=1)
    w2t = np.concatenate([w_all.T, w_all.T], axis=1)
    R = GROUP * C
    t = np.arange(R)[:, None]
    s = np.arange(R)[None, :]
    masks = np.zeros((nl, R, R), np.float32)
    masks[0] = (t == s)
    for li, m in enumerate(LEVELS, start=1):
        masks[li] = ((t ^ s) < 2 * m)
    return jnp.asarray(w2, BF16), jnp.asarray(w2t, BF16), jnp.asarray(masks, F32)


def _hgrn_gates(z, lg_ref):
    l0 = lg_ref[0:1, :]
    l1 = lg_ref[1:2, :]
    mx = jnp.maximum(l0, l1)
    e0 = jnp.exp(l0 - mx)
    e1 = jnp.exp(l1 - mx)
    lb = e0 / (e0 + e1)
    om = 1.0 - lb
    sg = _sigmoid(z)
    sgm = _sigmoid(-z)
    f = lb + om * sg
    return lb, om, sg, sgm, f, jnp.log(f), om * sgm


def _hgrn_levels(q, kk, lf, w2_ref):
    C = CHUNK
    nl = len(LEVELS) + 1
    lf_hi, lf_lo = _split2(lf)
    per_chunk = []
    for c in range(GROUP):
        rhs = jnp.concatenate([lf_hi[c * C:(c + 1) * C], lf_lo[c * C:(c + 1) * C]], axis=0)
        per_chunk.append(_nn(w2_ref[...], rhs))
    args = [jnp.concatenate([per_chunk[c][l * C:(l + 1) * C] for c in range(GROUP)], axis=0) for l in range(nl)]
    exps = [jnp.exp(a) for a in args]
    row = lax.broadcasted_iota(jnp.int32, q.shape, 0)
    qf, kf, mts = [q], [kk], [None]
    for li, m in enumerate(LEVELS, start=1):
        mt = jnp.where((row & m) != 0, 1.0, 0.0).astype(F32)
        mts.append(mt)
        qf.append(q * exps[li] * mt)
        kf.append(kk * exps[li] * (1.0 - mt))
    return args, exps, mts, qf, kf


def _hgrn_scores(qf, kf, masks_ref):
    p = None
    for l in range(len(qf)):
        pl_ = _nt(qf[l].astype(BF16), kf[l].astype(BF16)) * masks_ref[l]
        p = pl_ if p is None else p + pl_
    return p


def _hgrn_fwd(main, lb_logits, norm_w, dm):
    T, H, C = dm.T, dm.H, CHUNK
    R = GROUP * C
    nj = T // R
    w2, _, masks = _hgrn_consts()

    def body(x_ref, lg_ref, nw_ref, w2_ref, masks_ref, ya_ref, o_ref, sp_ref, st_ref):
        j = pl.program_id(1)

        @pl.when(j == 0)
        def _():
            st_ref[...] = jnp.zeros_like(st_ref)

        q, z, v, gv = (x_ref[:, s * HEAD:(s + 1) * HEAD] for s in range(4))
        vb = v.astype(BF16)
        _, _, _, _, _, lf, kk = _hgrn_gates(z, lg_ref)
        args, exps, _, qf, kf = _hgrn_levels(q, kk, lf, w2_ref)
        p = _hgrn_scores(qf, kf, masks_ref)
        o_intra = _nn(p.astype(BF16), vb)
        b, eb = args[0], exps[0]
        o_inter = []
        for c in range(GROUP):
            sl = slice(c * C, (c + 1) * C)
            st = st_ref[...]
            sp_ref[0, c] = st
            blast = b[c * C + C - 1:c * C + C, :]
            o_inter.append(_nt((q[sl] * eb[sl]).astype(BF16), st.astype(BF16)))
            kd = (kk[sl] * jnp.exp(blast - b[sl])).astype(BF16)
            st_ref[...] = st * jnp.exp(blast) + _tn(vb[sl], kd)
        o = o_intra + jnp.concatenate(o_inter, axis=0)
        o_ref[...] = o
        ya_ref[...] = (o * _rstd(o) * nw_ref[...] * (gv * _sigmoid(gv))).astype(BF16)

    return pl.pallas_call(
        body, name="hgrn_fwd", grid=(H, nj),
        in_specs=[pl.BlockSpec((R, 4 * HEAD), lambda h, j: (j, h)),
                  pl.BlockSpec((2, HEAD), lambda h, j: (0, h)),
                  pl.BlockSpec((1, HEAD), lambda h, j: (0, 0)),
                  pl.BlockSpec(w2.shape, lambda h, j: (0, 0)),
                  pl.BlockSpec(masks.shape, lambda h, j: (0, 0, 0))],
        out_specs=[pl.BlockSpec((R, HEAD), lambda h, j: (j, h)),
                   pl.BlockSpec((R, HEAD), lambda h, j: (j, h)),
                   pl.BlockSpec((1, GROUP, HEAD, HEAD), lambda h, j: (h, j, 0, 0))],
        out_shape=[jax.ShapeDtypeStruct((T, dm.AW), BF16), jax.ShapeDtypeStruct((T, dm.AW), F32),
                   jax.ShapeDtypeStruct((H, T // C, HEAD, HEAD), F32)],
        scratch_shapes=[pltpu.VMEM((HEAD, HEAD), F32)],
        compiler_params=_cparams(("parallel", "arbitrary")),
    )(main, lb_logits, norm_w, w2, masks)


def _hgrn_bwd(main, lb_logits, norm_w, o_saved, states, dya, dm, fused=None):
    T, H, C = dm.T, dm.H, CHUNK
    R = GROUP * C
    nj = T // R
    nl = len(LEVELS) + 1
    w2, w2t, masks = _hgrn_consts()
    x_ins, x_outs, x_sems, n_in, n_out = _fused_parts(fused)

    def body(x_ref, lg_ref, nw_ref, w2_ref, w2t_ref, masks_ref, o_ref, sp_ref, dya_ref, *rest):
        d_ref, dlb_ref, dnw_ref = rest[n_in:n_in + 3]
        ds_ref = rest[n_in + 3 + n_out]
        j = pl.program_id(1)
        if fused is not None:
            hd = pl.program_id(0)
            start, middle, finish_x = fused.hooks(rest[:n_in], rest[n_in + 3:n_in + 3 + n_out],
                                                  *rest[n_in + 4 + n_out:])
            pl.when((hd == 0) & (j == 0))(start)
            pl.when((hd == H // 2) & (j == 0))(middle)

        @pl.when(j == 0)
        def _():
            ds_ref[...] = jnp.zeros_like(ds_ref)
            dlb_ref[...] = jnp.zeros_like(dlb_ref)
            dnw_ref[...] = jnp.zeros_like(dnw_ref)

        def compute():
            q, z, v, gv = (x_ref[:, s * HEAD:(s + 1) * HEAD] for s in range(4))
            vb = v.astype(BF16)
            lb, om, sg, sgm, f, lf, kk = _hgrn_gates(z, lg_ref)
            args, exps, mts, qf, kf = _hgrn_levels(q, kk, lf, w2_ref)
            qb = [t.astype(BF16) for t in qf]
            kb = [t.astype(BF16) for t in kf]
            p = _hgrn_scores(qf, kf, masks_ref)
            o = o_ref[...]
            nw = nw_ref[...]
            r = _rstd(o)
            n = o * r
            sgg = _sigmoid(gv)
            dya_v = dya_ref[...]
            d_on = dya_v * (gv * sgg)
            dg = dya_v * (n * nw) * (sgg * (1.0 + gv * (1.0 - sgg)))
            dnw_ref[0] += _colsum(d_on * n)
            do = _rms_bwd(d_on * nw, n, r)
            dob = do.astype(BF16)
            dp = _nt(dob, vb)
            dv = _tn(p.astype(BF16), dob)
            dq = jnp.zeros_like(q)
            dkk = jnp.zeros_like(q)
            dargs = [None] * nl
            for l in range(nl):
                dpl = (dp * masks_ref[l]).astype(BF16)
                dql = _nn(dpl, kb[l])
                dkl = _tn(dpl, qb[l])
                if l == 0:
                    dq += dql
                    dkk += dkl
                else:
                    dq += dql * exps[l] * mts[l]
                    dkk += dkl * exps[l] * (1.0 - mts[l])
                    dargs[l] = dql * qf[l] + dkl * kf[l]
            b, eb = args[0], exps[0]
            row = lax.broadcasted_iota(jnp.int32, (C, HEAD), 0)
            dq_i, dkk_i, dv_i, db_i = [None] * GROUP, [None] * GROUP, [None] * GROUP, [None] * GROUP
            for c in reversed(range(GROUP)):
                sl = slice(c * C, (c + 1) * C)
                st = sp_ref[0, c]
                ds = ds_ref[...]
                dsb = ds.astype(BF16)
                blast = b[c * C + C - 1:c * C + C, :]
                ebl = jnp.exp(blast)
                el = jnp.exp(blast - b[sl])
                qe = q[sl] * eb[sl]
                kd = kk[sl] * el
                dqe = _nn(dob[sl], st.astype(BF16))
                dkd = _nn(vb[sl], dsb)
                t = dkd * kd
                dblast = _colsum(t) + _colsum(ds * st) * ebl
                dq_i[c] = dqe * eb[sl]
                dkk_i[c] = dkd * el
                dv_i[c] = _nt(kd.astype(BF16), dsb)
                db_i[c] = dqe * qe - t + jnp.where(row == C - 1, dblast, 0.0)
                ds_ref[...] = ds * ebl + _tn(dob[sl], qe.astype(BF16))
            dq = dq + jnp.concatenate(dq_i, axis=0)
            dkk = dkk + jnp.concatenate(dkk_i, axis=0)
            dv = dv + jnp.concatenate(dv_i, axis=0)
            dlf_c = []
            for c in range(GROUP):
                sl = slice(c * C, (c + 1) * C)
                stack = jnp.concatenate([db_i[c]] + [dargs[l][sl] for l in range(1, nl)], axis=0)
                hi, lo = _split2(stack)
                dlf_c.append(_nn(w2t_ref[...], jnp.concatenate([hi, lo], axis=0)))
            dlf = jnp.concatenate(dlf_c, axis=0)
            dz = dlf * (om * sg * (1.0 - sg) / f) - dkk * (om * sgm * (1.0 - sgm))
            dlb_ref[...] += _colsum(dlf * (1.0 - sg) / f - dkk * sgm)
            d_ref[...] = jnp.concatenate([dq, dz, dv, dg], axis=1).astype(BF16)

        compute()
        if fused is not None:
            pl.when((hd == H - 1) & (j == nj - 1))(finish_x)

    outs = pl.pallas_call(
        body, name="hgrn_bwd", grid=(H, nj),
        in_specs=[pl.BlockSpec((R, 4 * HEAD), lambda h, j: (nj - 1 - j, h)),
                  pl.BlockSpec((2, HEAD), lambda h, j: (0, h)),
                  pl.BlockSpec((1, HEAD), lambda h, j: (0, 0)),
                  pl.BlockSpec(w2.shape, lambda h, j: (0, 0)),
                  pl.BlockSpec(w2t.shape, lambda h, j: (0, 0)),
                  pl.BlockSpec(masks.shape, lambda h, j: (0, 0, 0)),
                  pl.BlockSpec((R, HEAD), lambda h, j: (nj - 1 - j, h)),
                  pl.BlockSpec((1, GROUP, HEAD, HEAD), lambda h, j: (h, nj - 1 - j, 0, 0)),
                  pl.BlockSpec((R, HEAD), lambda h, j: (nj - 1 - j, h))] + [HBM] * n_in,
        out_specs=[pl.BlockSpec((R, 4 * HEAD), lambda h, j: (nj - 1 - j, h)),
                   pl.BlockSpec((1, HEAD), lambda h, j: (0, h)),
                   pl.BlockSpec((1, 1, HEAD), lambda h, j: (h, 0, 0))] + [HBM] * n_out,
        out_shape=[jax.ShapeDtypeStruct((T, 4 * dm.AW), BF16), jax.ShapeDtypeStruct((1, dm.AW), F32),
                   jax.ShapeDtypeStruct((H, 1, HEAD), F32)] + list(x_outs),
        scratch_shapes=[pltpu.VMEM((HEAD, HEAD), F32)] + list(x_sems),
        compiler_params=_cparams(("arbitrary", "arbitrary")),
    )(main, lb_logits, norm_w, w2, w2t, masks, o_saved, states, dya, *x_ins)
    return outs[0], outs[1], outs[2], tuple(outs[3:])


def _log_sigmoid(x):
    return jnp.minimum(x, 0.0) - jnp.log(1.0 + jnp.exp(-jnp.abs(x)))


def _tri(n):
    return jnp.asarray(np.tril(np.ones((n, n), np.float32)), BF16)


def _cum_fwd(rest, bias, dm, tb=256):
    T = dm.T
    tb = min(tb, T)
    cb = 2 * dm.D // HEAD
    tri = _tri(tb)

    def body(x_ref, b_ref, tri_ref, o_ref, carry):
        i = pl.program_id(0)

        @pl.when(i == 0)
        def _():
            carry[...] = jnp.zeros_like(carry)

        lf = _log_sigmoid(x_ref[...] + b_ref[...])
        hi, mid, lo = _split3(lf)
        tr_ = tri_ref[...]
        c = _nn(tr_, hi) + _nn(tr_, mid) + _nn(tr_, lo) + carry[...]
        o_ref[...] = c
        carry[...] = c[tb - 1:tb, :]

    return pl.pallas_call(
        body, name="cum_fwd", grid=(T // tb,),
        in_specs=[_rows(tb, HEAD, cb), _vec(HEAD), pl.BlockSpec((tb, tb), lambda i: (0, 0))],
        out_specs=_rows(tb, HEAD),
        out_shape=jax.ShapeDtypeStruct((T, HEAD), F32),
        scratch_shapes=[pltpu.VMEM((1, HEAD), F32)],
        compiler_params=_cparams(("arbitrary",)),
    )(rest, bias, tri)


def _cum_bwd(dcum, rest, bias, dm, tb=256):
    T = dm.T
    tb = min(tb, T)
    nb = T // tb
    cb = 2 * dm.D // HEAD
    tri = _tri(tb)

    def body(d_ref, x_ref, b_ref, tri_ref, o_ref, db_ref, carry):
        i = pl.program_id(0)

        @pl.when(i == 0)
        def _():
            carry[...] = jnp.zeros_like(carry)
            db_ref[...] = jnp.zeros_like(db_ref)

        hi, mid, lo = _split3(d_ref[...])
        tr_ = tri_ref[...]
        dlf = _tn(tr_, hi) + _tn(tr_, mid) + _tn(tr_, lo) + carry[...]
        carry[...] = dlf[0:1, :]
        dx = dlf * _sigmoid(-(x_ref[...] + b_ref[...]))
        o_ref[...] = dx.astype(BF16)
        db_ref[...] += _colsum(dx)

    return pl.pallas_call(
        body, name="cum_bwd", grid=(nb,),
        in_specs=[pl.BlockSpec((tb, HEAD), lambda i: (nb - 1 - i, 0)),
                  pl.BlockSpec((tb, HEAD), lambda i: (nb - 1 - i, cb)),
                  _vec(HEAD), pl.BlockSpec((tb, tb), lambda i: (0, 0))],
        out_specs=[pl.BlockSpec((tb, HEAD), lambda i: (nb - 1 - i, 0)), _vec(HEAD)],
        out_shape=[jax.ShapeDtypeStruct((T, HEAD), BF16), jax.ShapeDtypeStruct((1, HEAD), F32)],
        scratch_shapes=[pltpu.VMEM((1, HEAD), F32)],
        compiler_params=_cparams(("arbitrary",)),
    )(dcum, rest, bias, tri)


def _fox_pairs(nq, kv_major):
    if kv_major:
        pairs = [(i, j) for j in range(nq) for i in range(j, nq)]
    else:
        pairs = [(i, j) for i in range(nq) for j in range(i + 1)]
    qi = jnp.asarray(np.array([p[0] for p in pairs], np.int32))
    kj = jnp.asarray(np.array([p[1] for p in pairs], np.int32))
    return qi, kj


class Fused(NamedTuple):
    ins: tuple
    outs: tuple
    sems: tuple
    hooks: object


def _fused_parts(fused):
    if fused is None:
        return (), (), (), 0, 0
    return tuple(fused.ins), tuple(fused.outs), tuple(fused.sems), len(fused.ins), len(fused.outs)


FOX_FWD_SPLIT = 4
FOX_BWD_SPLIT = 2


def _fox_fwd(main, cum_row, dm, fused=None, tq=512):
    T, H = dm.T, dm.H
    tq = min(tq, T)
    nq = T // tq
    qi_tab, kj_tab = _fox_pairs(nq, kv_major=False)
    npairs = int(qi_tab.shape[0])
    x_ins, x_outs, x_sems, n_in, n_out = _fused_parts(fused)
    ns = FOX_FWD_SPLIT if tq % (8 * FOX_FWD_SPLIT) == 0 else 1
    rq = tq // ns

    def body(qi_ref, kj_ref, q_ref, k_ref, v_ref, ck_ref, *rest):
        x_in, (o_ref, lse_ref) = rest[:n_in], rest[n_in:n_in + 2]
        x_out = rest[n_in + 2:n_in + 2 + n_out]
        m_ref, l_ref, acc_ref = rest[n_in + 2 + n_out:n_in + 5 + n_out]
        hd = pl.program_id(0)
        t = pl.program_id(1)
        i = qi_ref[t]
        j = kj_ref[t]
        if fused is not None:
            start, middle, finish = fused.hooks(x_in, x_out, *rest[n_in + 5 + n_out:])
            pl.when((hd == 0) & (t == 0))(start)
            pl.when((hd == H // 2) & (t == 0))(middle)

        @pl.when(j == 0)
        def _():
            m_ref[...] = jnp.full_like(m_ref, NEG)
            l_ref[...] = jnp.zeros_like(l_ref)
            acc_ref[...] = jnp.zeros_like(acc_ref)

        def step(on_diagonal):
            kb = k_ref[...].astype(BF16)
            vb = v_ref[...].astype(BF16)
            ck = ck_ref[0]
            q_all, m_all, l_all, acc_all = q_ref[...], m_ref[...], l_ref[...], acc_ref[...]
            m_out, l_out, acc_out = [], [], []
            for g in range(ns):
                rows = slice(g * rq, (g + 1) * rq)
                qs = (q_all[rows] * (HEAD ** -0.5)).astype(BF16)
                s = _nt(qs, kb) - ck
                if on_diagonal:
                    row = g * rq + lax.broadcasted_iota(jnp.int32, s.shape, 0)
                    s = jnp.where(row >= lax.broadcasted_iota(jnp.int32, s.shape, 1), s, NEG)
                m_new = jnp.maximum(m_all[rows], jnp.max(s, axis=-1, keepdims=True))
                a = jnp.exp(m_all[rows] - m_new)
                p = jnp.exp(s - m_new)
                m_out.append(m_new)
                l_out.append(a * l_all[rows] + jnp.sum(p, axis=-1, keepdims=True))
                acc_out.append(a * acc_all[rows] + _nn(p.astype(BF16), vb))
            m_ref[...] = jnp.concatenate(m_out, axis=0)
            l_ref[...] = jnp.concatenate(l_out, axis=0)
            acc_ref[...] = jnp.concatenate(acc_out, axis=0)

        @pl.when(j < i)
        def _():
            step(False)

        @pl.when(j == i)
        def _():
            step(True)
            l = l_ref[...]
            o_ref[...] = acc_ref[...] / l
            lse_ref[0] = m_ref[...] + jnp.log(l)

        if fused is not None:
            pl.when((hd == H - 1) & (t == npairs - 1))(finish)

    outs = pl.pallas_call(
        body, name="fox_fwd",
        grid_spec=pltpu.PrefetchScalarGridSpec(
            num_scalar_prefetch=2, grid=(H, npairs),
            in_specs=[pl.BlockSpec((tq, HEAD), lambda h, t, qi, kj: (qi[t], 4 * H + h)),
                      pl.BlockSpec((tq, HEAD), lambda h, t, qi, kj: (kj[t], 5 * H + h)),
                      pl.BlockSpec((tq, HEAD), lambda h, t, qi, kj: (kj[t], 6 * H + h)),
                      pl.BlockSpec((1, 1, tq), lambda h, t, qi, kj: (h, 0, kj[t]))] + [HBM] * n_in,
            out_specs=[pl.BlockSpec((tq, HEAD), lambda h, t, qi, kj: (qi[t], h)),
                       pl.BlockSpec((1, tq, 1), lambda h, t, qi, kj: (h, qi[t], 0))] + [HBM] * n_out,
            scratch_shapes=[pltpu.VMEM((tq, 1), F32), pltpu.VMEM((tq, 1), F32), pltpu.VMEM((tq, HEAD), F32)]
            + list(x_sems)),
        out_shape=[jax.ShapeDtypeStruct((T, dm.AW), F32), jax.ShapeDtypeStruct((H, T, 1), F32)] + list(x_outs),
        compiler_params=_cparams(("arbitrary", "arbitrary")),
    )(qi_tab, kj_tab, main, main, main, cum_row, *x_ins)
    return outs[0], outs[1], tuple(outs[2:])


def _fox_delta(do, o, dm, tr=256):
    T, H = dm.T, dm.H
    tr = min(tr, T)

    def body(do_ref, o_ref, d_ref):
        d_ref[0] = jnp.sum(do_ref[...] * o_ref[...], axis=-1, keepdims=True)

    return pl.pallas_call(
        body, name="fox_delta", grid=(H, T // tr),
        in_specs=[pl.BlockSpec((tr, HEAD), lambda h, i: (i, h)), pl.BlockSpec((tr, HEAD), lambda h, i: (i, h))],
        out_specs=pl.BlockSpec((1, tr, 1), lambda h, i: (h, i, 0)),
        out_shape=jax.ShapeDtypeStruct((H, T, 1), F32),
        compiler_params=_cparams(("parallel", "parallel")),
    )(do, o)


def _fox_bwd(main, cum_row, lse, delta, do, dm, fused=None, tq=512):
    T, H = dm.T, dm.H
    tq = min(tq, T)
    nq = T // tq
    qi_tab, kj_tab = _fox_pairs(nq, kv_major=True)
    npairs = int(qi_tab.shape[0])
    x_ins, x_outs, x_sems, n_in, n_out = _fused_parts(fused)
    ns = FOX_BWD_SPLIT if tq % (16 * FOX_BWD_SPLIT) == 0 else 1
    rq = tq // ns

    def body(qi_ref, kj_ref, q_ref, k_ref, v_ref, ck_ref, lse_ref, dl_ref, do_ref, *rest):
        x_in = rest[:n_in]
        dq_ref, dk_ref, dv_ref, dc_ref, dr_ref = rest[n_in:n_in + 5]
        x_out = rest[n_in + 5:n_in + 5 + n_out]
        dk_acc, dv_acc, dc_acc = rest[n_in + 5 + n_out:n_in + 8 + n_out]
        hd = pl.program_id(0)
        t = pl.program_id(1)
        i = qi_ref[t]
        kj = kj_ref[t]
        if fused is not None:
            start, middle, finish = fused.hooks(x_in, x_out, *rest[n_in + 8 + n_out:])
            pl.when((hd == 0) & (t == 0))(start)
            pl.when((hd == H // 2) & (t == 0))(middle)

        @pl.when(t == 0)
        def _():
            dq_ref[...] = jnp.zeros_like(dq_ref)
            dr_ref[...] = jnp.zeros_like(dr_ref)

        @pl.when(i == kj)
        def _():
            dk_acc[...] = jnp.zeros_like(dk_acc)
            dv_acc[...] = jnp.zeros_like(dv_acc)
            dc_acc[...] = jnp.zeros_like(dc_acc)

        def step(on_diagonal):
            kb = k_ref[...].astype(BF16)
            vb = v_ref[...].astype(BF16)
            ck = ck_ref[0]
            q_all, do_all, lse_all, dl_all = q_ref[...], do_ref[...], lse_ref[0], dl_ref[0]
            dq_g, dr_g, dv_c, dk_c, dc_c = [], [], None, None, None
            for g in range(ns):
                rows = slice(g * rq, (g + 1) * rq)
                qs = (q_all[rows] * (HEAD ** -0.5)).astype(BF16)
                s = _nt(qs, kb) - ck
                if on_diagonal:
                    row = g * rq + lax.broadcasted_iota(jnp.int32, s.shape, 0)
                    s = jnp.where(row >= lax.broadcasted_iota(jnp.int32, s.shape, 1), s, NEG)
                dob = do_all[rows].astype(BF16)
                p = jnp.exp(s - lse_all[rows])
                ds = p * (_nt(dob, vb) - dl_all[rows])
                dsb = ds.astype(BF16)
                dq_g.append(_nn(dsb, kb) * (HEAD ** -0.5))
                dr_g.append(jnp.sum(ds, axis=-1, keepdims=True))
                dv_g, dk_g, dc_g = _tn(p.astype(BF16), dob), _tn(dsb, qs), _colsum(ds)
                dv_c = dv_g if dv_c is None else dv_c + dv_g
                dk_c = dk_g if dk_c is None else dk_c + dk_g
                dc_c = dc_g if dc_c is None else dc_c + dc_g
            dv_acc[...] += dv_c
            dk_acc[...] += dk_c
            dc_acc[...] -= dc_c
            out_rows = pl.ds(pl.multiple_of(i * tq, tq), tq)
            dq_ref[out_rows, :] += jnp.concatenate(dq_g, axis=0)
            dr_ref[0, out_rows, :] += jnp.concatenate(dr_g, axis=0)

        @pl.when(i == kj)
        def _():
            step(True)

        @pl.when(i > kj)
        def _():
            step(False)

        @pl.when(i == nq - 1)
        def _():
            dk_ref[...] = dk_acc[...].astype(BF16)
            dv_ref[...] = dv_acc[...].astype(BF16)
            dc_ref[0] = dc_acc[...]

        if fused is not None:
            pl.when((hd == H - 1) & (t == npairs - 1))(finish)

    qcol = pl.BlockSpec((1, tq, 1), lambda h, t, qi, kj: (h, qi[t], 0))
    outs = pl.pallas_call(
        body, name="fox_bwd",
        grid_spec=pltpu.PrefetchScalarGridSpec(
            num_scalar_prefetch=2, grid=(H, npairs),
            in_specs=[pl.BlockSpec((tq, HEAD), lambda h, t, qi, kj: (qi[t], 4 * H + h)),
                      pl.BlockSpec((tq, HEAD), lambda h, t, qi, kj: (kj[t], 5 * H + h)),
                      pl.BlockSpec((tq, HEAD), lambda h, t, qi, kj: (kj[t], 6 * H + h)),
                      pl.BlockSpec((1, 1, tq), lambda h, t, qi, kj: (h, 0, kj[t])),
                      qcol, qcol,
                      pl.BlockSpec((tq, HEAD), lambda h, t, qi, kj: (qi[t], h))] + [HBM] * n_in,
            out_specs=[pl.BlockSpec((T, HEAD), lambda h, t, qi, kj: (0, h)),
                       pl.BlockSpec((tq, HEAD), lambda h, t, qi, kj: (kj[t], h)),
                       pl.BlockSpec((tq, HEAD), lambda h, t, qi, kj: (kj[t], h)),
                       pl.BlockSpec((1, 1, tq), lambda h, t, qi, kj: (h, 0, kj[t])),
                       pl.BlockSpec((1, T, 1), lambda h, t, qi, kj: (h, 0, 0))] + [HBM] * n_out,
            scratch_shapes=[pltpu.VMEM((tq, HEAD), F32), pltpu.VMEM((tq, HEAD), F32), pltpu.VMEM((1, tq), F32)]
            + list(x_sems)),
        out_shape=[jax.ShapeDtypeStruct((T, dm.AW), F32), jax.ShapeDtypeStruct((T, dm.AW), BF16),
                   jax.ShapeDtypeStruct((T, dm.AW), BF16), jax.ShapeDtypeStruct((H, 1, T), F32),
                   jax.ShapeDtypeStruct((H, T, 1), F32)] + list(x_outs),
        compiler_params=_cparams(("arbitrary", "arbitrary")),
    )(qi_tab, kj_tab, main, main, main, cum_row, lse, delta, do, *x_ins)
    return outs[:5], tuple(outs[5:])


FOX_QG = 256


def _fox_sT(kb, q_rows, ck, g, on_diagonal):
    qs = (q_rows * (HEAD ** -0.5)).astype(BF16)
    reps = q_rows.shape[0] // HEAD
    sT = _nt(kb, qs) - (jnp.concatenate([ck] * reps, axis=1) if reps > 1 else ck)
    if on_diagonal:
        key = lax.broadcasted_iota(jnp.int32, sT.shape, 0)
        qry = g * q_rows.shape[0] + lax.broadcasted_iota(jnp.int32, sT.shape, 1)
        sT = jnp.where(key <= qry, sT, NEG)
    return sT, qs


def _foxt_fwd(main, ckb, dm, fused=None, tq=1024):
    T, H = dm.T, dm.H
    tq = min(tq, T)
    nq = T // tq
    qg = min(FOX_QG, tq)
    ns = tq // qg
    qi_tab, kj_tab = _fox_pairs(nq, kv_major=False)
    npairs = int(qi_tab.shape[0])
    x_ins, x_outs, x_sems, n_in, n_out = _fused_parts(fused)

    def body(qi_ref, kj_ref, q_ref, k_ref, v_ref, ck_ref, *rest):
        x_in, (o_ref, lse_ref) = rest[:n_in], rest[n_in:n_in + 2]
        x_out = rest[n_in + 2:n_in + 2 + n_out]
        m_ref, l_ref, acc_ref = rest[n_in + 2 + n_out:n_in + 5 + n_out]
        hd = pl.program_id(0)
        t = pl.program_id(1)
        i = qi_ref[t]
        j = kj_ref[t]
        if fused is not None:
            start, middle, finish = fused.hooks(x_in, x_out, *rest[n_in + 5 + n_out:])
            pl.when((hd == 0) & (t == 0))(start)
            pl.when((hd == H // 2) & (t == 0))(middle)

        @pl.when(j == 0)
        def _():
            m_ref[...] = jnp.full_like(m_ref, NEG)
            l_ref[...] = jnp.zeros_like(l_ref)
            acc_ref[...] = jnp.zeros_like(acc_ref)

        def step(on_diagonal):
            kb = k_ref[...].astype(BF16)
            vb = v_ref[...].astype(BF16)
            ck = ck_ref[0]
            scores = [_fox_sT(kb, q_ref[g * qg:(g + 1) * qg, :], ck, g, on_diagonal)[0] for g in range(ns)]
            for g in range(ns):
                cols = slice(g * qg, (g + 1) * qg)
                sT = scores[g]
                m_old = m_ref[:, cols]
                m_new = jnp.maximum(m_old, jnp.max(sT, axis=0, keepdims=True))
                a = jnp.exp(m_old - m_new)
                pT = jnp.exp(sT - m_new)
                l_ref[:, cols] = a * l_ref[:, cols] + jnp.sum(pT, axis=0, keepdims=True)
                acc_ref[:, cols] = a * acc_ref[:, cols] + _tn(vb, pT.astype(BF16))
                m_ref[:, cols] = m_new

        @pl.when(j < i)
        def _():
            step(False)

        @pl.when(j == i)
        def _():
            step(True)
            l = l_ref[...]
            o_ref[...] = acc_ref[...] / l
            lse_ref[0] = m_ref[...] + jnp.log(l)

        if fused is not None:
            pl.when((hd == H - 1) & (t == npairs - 1))(finish)

    outs = pl.pallas_call(
        body, name="fox_fwd",
        grid_spec=pltpu.PrefetchScalarGridSpec(
            num_scalar_prefetch=2, grid=(H, npairs),
            in_specs=[pl.BlockSpec((tq, HEAD), lambda h, t, qi, kj: (qi[t], 4 * H + h)),
                      pl.BlockSpec((tq, HEAD), lambda h, t, qi, kj: (kj[t], 5 * H + h)),
                      pl.BlockSpec((tq, HEAD), lambda h, t, qi, kj: (kj[t], 6 * H + h)),
                      pl.BlockSpec((1, tq, HEAD), lambda h, t, qi, kj: (h, kj[t], 0))] + [HBM] * n_in,
            out_specs=[pl.BlockSpec((HEAD, tq), lambda h, t, qi, kj: (h, qi[t])),
                       pl.BlockSpec((1, 1, tq), lambda h, t, qi, kj: (h, 0, qi[t]))] + [HBM] * n_out,
            scratch_shapes=[pltpu.VMEM((1, tq), F32), pltpu.VMEM((1, tq), F32), pltpu.VMEM((HEAD, tq), F32)]
            + list(x_sems)),
        out_shape=[jax.ShapeDtypeStruct((dm.AW, T), F32), jax.ShapeDtypeStruct((H, 1, T), F32)] + list(x_outs),
        compiler_params=_cparams(("arbitrary", "arbitrary")),
    )(qi_tab, kj_tab, main, main, main, ckb, *x_ins)
    return outs[0], outs[1], tuple(outs[2:])


def _foxt_delta(doT, oT, dm, tc=1024):
    T, H = dm.T, dm.H
    tc = min(tc, T)

    def body(do_ref, o_ref, d_ref):
        d_ref[0] = jnp.sum(do_ref[...] * o_ref[...], axis=0, keepdims=True)

    blk = pl.BlockSpec((HEAD, tc), lambda h, i: (h, i))
    return pl.pallas_call(
        body, name="fox_delta", grid=(H, T // tc),
        in_specs=[blk, blk], out_specs=pl.BlockSpec((1, 1, tc), lambda h, i: (h, 0, i)),
        out_shape=jax.ShapeDtypeStruct((H, 1, T), F32),
        compiler_params=_cparams(("parallel", "parallel")),
    )(doT, oT)


def _foxt_bwd(main, ckb, lse, delta, doT, dm, fused=None, tq=1024):
    T, H = dm.T, dm.H
    tq = min(tq, T)
    nq = T // tq
    qg = min(FOX_QG, tq)
    ns = tq // qg
    qi_tab, kj_tab = _fox_pairs(nq, kv_major=True)
    npairs = int(qi_tab.shape[0])
    x_ins, x_outs, x_sems, n_in, n_out = _fused_parts(fused)

    def body(qi_ref, kj_ref, q_ref, k_ref, v_ref, ck_ref, lse_ref, dl_ref, do_ref, *rest):
        x_in = rest[:n_in]
        dq_ref, dk_ref, dv_ref, dr_ref, dc_ref = rest[n_in:n_in + 5]
        x_out = rest[n_in + 5:n_in + 5 + n_out]
        dq_acc, dk_acc, dv_acc = rest[n_in + 5 + n_out:n_in + 8 + n_out]
        hd = pl.program_id(0)
        t = pl.program_id(1)
        i = qi_ref[t]
        kj = kj_ref[t]
        if fused is not None:
            start, middle, finish = fused.hooks(x_in, x_out, *rest[n_in + 8 + n_out:])
            pl.when((hd == 0) & (t == 0))(start)
            pl.when((hd == H // 2) & (t == 0))(middle)

        @pl.when(t == 0)
        def _():
            dq_acc[...] = jnp.zeros_like(dq_acc)
            dr_ref[...] = jnp.zeros_like(dr_ref)

        @pl.when(i == kj)
        def _():
            dk_acc[...] = jnp.zeros_like(dk_acc)
            dv_acc[...] = jnp.zeros_like(dv_acc)

        def step(on_diagonal):
            kb = k_ref[...].astype(BF16)
            vb = v_ref[...].astype(BF16)
            ck = ck_ref[0]
            ones = jnp.ones((qg, HEAD), BF16)
            dq_g, dr_g = [], []
            ahead = []
            for g in range(ns):
                cols = slice(g * qg, (g + 1) * qg)
                sT, qs = _fox_sT(kb, q_ref[cols, :], ck, g, on_diagonal)
                dob = do_ref[:, cols].astype(BF16)
                ahead.append((sT, qs, dob, _nn(vb, dob)))
            for g in range(ns):
                cols = slice(g * qg, (g + 1) * qg)
                sT, qs, dob, dpT = ahead[g]
                pT = jnp.exp(sT - lse_ref[0, :, cols])
                dsT = pT * (dpT - dl_ref[0, :, cols])
                dsb = dsT.astype(BF16)
                dv_acc[...] += _nt(pT.astype(BF16), dob)
                dk_acc[...] += _nn(dsb, jnp.concatenate([qs, ones], axis=1))
                dq_g.append(_tn(kb, dsb) * (HEAD ** -0.5))
                dr_g.append(_nn(jnp.ones((8, tq), BF16), dsb)[0:1])
            dq_acc[i] += jnp.concatenate(dq_g, axis=1)
            dr_ref[0, pl.ds(i, 1), :] += jnp.concatenate(dr_g, axis=1)

        @pl.when(i == kj)
        def _():
            step(True)

        @pl.when(i > kj)
        def _():
            step(False)

        @pl.when(i == nq - 1)
        def _():
            acc = dk_acc[...]
            dk_ref[...] = acc[:, :HEAD].astype(BF16)
            dc_ref[0] = acc[:, HEAD:HEAD + 1]
            dv_ref[...] = dv_acc[...].astype(BF16)

        @pl.when(t == npairs - 1)
        def _():
            for b in range(nq):
                dq_ref[b * tq:(b + 1) * tq, :] = dq_acc[b].T.astype(BF16)

        if fused is not None:
            pl.when((hd == H - 1) & (t == npairs - 1))(finish)

    qrow = pl.BlockSpec((1, 1, tq), lambda h, t, qi, kj: (h, 0, qi[t]))
    outs = pl.pallas_call(
        body, name="fox_bwd",
        grid_spec=pltpu.PrefetchScalarGridSpec(
            num_scalar_prefetch=2, grid=(H, npairs),
            in_specs=[pl.BlockSpec((tq, HEAD), lambda h, t, qi, kj: (qi[t], 4 * H + h)),
                      pl.BlockSpec((tq, HEAD), lambda h, t, qi, kj: (kj[t], 5 * H + h)),
                      pl.BlockSpec((tq, HEAD), lambda h, t, qi, kj: (kj[t], 6 * H + h)),
                      pl.BlockSpec((1, tq, HEAD), lambda h, t, qi, kj: (h, kj[t], 0)),
                      qrow, qrow,
                      pl.BlockSpec((HEAD, tq), lambda h, t, qi, kj: (h, qi[t]))] + [HBM] * n_in,
            out_specs=[pl.BlockSpec((T, HEAD), lambda h, t, qi, kj: (0, h)),
                       pl.BlockSpec((tq, HEAD), lambda h, t, qi, kj: (kj[t], h)),
                       pl.BlockSpec((tq, HEAD), lambda h, t, qi, kj: (kj[t], h)),
                       pl.BlockSpec((1, nq, tq), lambda h, t, qi, kj: (h, 0, 0)),
                       pl.BlockSpec((1, tq, 1), lambda h, t, qi, kj: (h, kj[t], 0))] + [HBM] * n_out,
            scratch_shapes=[pltpu.VMEM((nq, HEAD, tq), F32), pltpu.VMEM((tq, 2 * HEAD), F32),
                            pltpu.VMEM((tq, HEAD), F32)] + list(x_sems)),
        out_shape=[jax.ShapeDtypeStruct((T, dm.AW), BF16), jax.ShapeDtypeStruct((T, dm.AW), BF16),
                   jax.ShapeDtypeStruct((T, dm.AW), BF16), jax.ShapeDtypeStruct((H, nq, tq), F32),
                   jax.ShapeDtypeStruct((H, T, 1), F32)] + list(x_outs),
        compiler_params=_cparams(("arbitrary", "arbitrary")),
    )(qi_tab, kj_tab, main, main, main, ckb, lse, delta, doT, *x_ins)
    return outs[:5], tuple(outs[5:])


def _local_step(dm, x, tgt, w_main, w_rest, later_weights, bias_p, lb_logits, norm_w, g1, g2, g3, g4,
                reduce_hooks=None):
    T, D, H = dm.T, dm.D, dm.H
    h1, r1 = _rms_fwd(x, g1)
    main = _mm(h1, w_main, "nn", F32, "proj_main")
    rest = _mm(h1, w_rest, "nn", F32, "proj_rest", tn=1408)
    ya, o_a, states = _hgrn_fwd(main, lb_logits, norm_w, dm)
    cum = _cum_fwd(rest, bias_p, dm)
    ckb = jnp.broadcast_to(cum[:, :H].T[:, :, None], (H, T, HEAD))
    if isinstance(later_weights, Fused):
        oT_b, lse, later_weights = _foxt_fwd(main, ckb, dm, fused=later_weights)
    else:
        oT_b, lse, _ = _foxt_fwd(main, ckb, dm)
    w_up_a, w_up_b, w_o, w_ffn_in, w_ffn_down = later_weights
    y_a = _mm(ya, w_up_a, "nn", F32, "up_a")
    y_b = _mm(oT_b, w_up_b, "tn", F32, "up_b")
    merged = _merge_fwd(rest, y_a, y_b)
    u = _mm(merged, w_o, "nn", F32, "w_o")
    x1, r2, h3, r3 = _post_pre(x, u, g2, g3)
    gate, up, act = _ffn_in_swiglu(h3, w_ffn_in)
    w = _mm(act, w_ffn_down, "nn", F32, "ffn_down")
    loss, dy, dw, dg4 = _loss_bwd(x1, w, g4, tgt)
    dgate, dup = _d_act_swiglu(dw, w_ffn_down, gate, up)
    d_ffn_down = _mm(act, dw, "tn", F32, "dw_ffn_down", tm=1408)
    dh3 = _mm(dgate, w_ffn_in[:, :dm.FF], "nt", F32, "d_h3_gate")
    dh3 = _mm(dup, w_ffn_in[:, dm.FF:], "nt", F32, "d_h3_up", add=dh3)
    d_ffn_in = jnp.concatenate([_mm(h3, dgate, "tn", F32, "dw_ffn_gate", tn=1408),
                                _mm(h3, dup, "tn", F32, "dw_ffn_up", tn=1408)], axis=1)
    dx1, du, dg3, dg2 = _rms_bwd2(dy, dh3, x1, r3, g3, u, r2, g2)
    dy_a, dy_b, dg_a, dg_b = _d_merged_gates(du, w_o, rest, y_a, y_b)
    d_w_o = _mm(merged, du, "tn", F32, "dw_o")
    dya = _mm(dy_a, w_up_a, "nt", F32, "d_ya")
    d_up_a = _mm(ya, dy_a, "tn", F32, "dw_up_a")
    doT = _mm(w_up_b, dy_b, "nt", F32, "d_ob")
    d_up_b = _mm(oT_b, dy_b, "nn", F32, "dw_up_b")
    later_grads = [d_up_a, d_up_b, d_w_o, d_ffn_in, d_ffn_down]
    d_a, dlb, dnw_h, got = _hgrn_bwd(main, lb_logits, norm_w, o_a, states, dya, dm,
                                     fused=reduce_hooks.swap_later(later_grads) if reduce_hooks else None)
    (dq_b, dk_b, dv_b, d_over_keys, d_over_queries), later_from_chips = _foxt_bwd(
        main, ckb, lse, _foxt_delta(doT, oT_b, dm), doT, dm,
        fused=reduce_hooks.scatter_later(got) if reduce_hooks else None)
    dcum = jnp.pad((d_over_keys.reshape(H, T) - d_over_queries.reshape(H, T)).T, ((0, 0), (0, HEAD - H)))
    dbf, dbias = _cum_bwd(dcum, rest, bias_p, dm)
    dmain = jnp.concatenate([d_a, dq_b, dk_b, dv_b], axis=1)
    drest = jnp.concatenate([dg_a, dg_b, dbf], axis=1)
    d_main = _mm(h1, dmain, "tn", F32, "dw_main")
    d_rest = _mm(h1, drest, "tn", F32, "dw_rest", tn=1408)
    if reduce_hooks:
        dh1, w_in_from_chips = _mm(dmain, w_main, "nt", F32, "d_h1_main",
                                   fused=reduce_hooks.scatter_w_in(d_main, d_rest))
    else:
        dh1, w_in_from_chips = _mm(dmain, w_main, "nt", F32, "d_h1_main"), ()
    dh1 = _mm(drest, w_rest, "nt", F32, "d_h1_rest", add=dh1)
    dx, dg1 = _rms_bwd1(dx1, dh1, x, r1, g1)
    big = dict(main=d_main, rest=d_rest, up_a=d_up_a, up_b=d_up_b, w_o=d_w_o, ffn_in=d_ffn_in, ffn_down=d_ffn_down)
    small = dict(loss=loss, bias=dbias, norm_w=jnp.sum(dnw_h, axis=0), lb=dlb, g1=dg1, g2=dg2, g3=dg3, g4=dg4)
    return dx, big, small, tuple(w_in_from_chips) + tuple(later_from_chips)


HBM = pl.BlockSpec(memory_space=pltpu.HBM)


def _place():
    x, y, c = lax.axis_index("x"), lax.axis_index("y"), lax.axis_index("c")
    chips = [(1 - x, y), (x, 1 - y), (1 - x, 1 - y)]
    return x, y, c, chips


class Sharded(NamedTuple):
    kind: str
    r: int
    c: int

    @property
    def full(self):
        return (self.r, 4 * self.c) if self.kind == "col" else (4 * self.r, self.c)

    @property
    def half(self):
        return (self.r // 2, self.c) if self.kind == "col" else (self.r, self.c // 2)

    @property
    def half_of_full(self):
        return (self.r // 2, 4 * self.c) if self.kind == "col" else (4 * self.r, self.c // 2)

    def shard_window(self, ref, s):
        if self.kind == "col":
            return ref.at[:, pl.ds(pl.multiple_of(s * self.c, 128), self.c)]
        return ref.at[pl.ds(pl.multiple_of(s * self.r, 16), self.r), :]

    def half_window(self, ref, s, h):
        if self.kind == "col":
            return ref.at[pl.ds(pl.multiple_of(h * (self.r // 2), 16), self.r // 2),
                          pl.ds(pl.multiple_of(s * self.c, 128), self.c)]
        return ref.at[pl.ds(pl.multiple_of(s * self.r, 16), self.r),
                      pl.ds(pl.multiple_of(h * (self.c // 2), 128), self.c // 2)]

    def half_of(self, ref, h):
        if self.kind == "col":
            n = ref.shape[0] // 2
            return ref.at[pl.ds(pl.multiple_of(h * n, 16), n), :]
        n = ref.shape[1] // 2
        return ref.at[:, pl.ds(pl.multiple_of(h * n, 128), n)]

    def window_of_half(self, ref, s):
        if self.kind == "col":
            return ref.at[:, pl.ds(pl.multiple_of(s * self.c, 128), self.c)]
        return ref.at[pl.ds(pl.multiple_of(s * self.r, 16), self.r), :]


def _gather_hooks(specs):
    n = len(specs)

    def hooks(w_refs, f_refs, send_sems, recv_sems):
        x, y, c, chips = _place()
        q = 2 * x + y
        sibling = (x, y, 1 - c)

        def copy(t, k, src, dst, to):
            return pltpu.make_async_remote_copy(
                src_ref=src, dst_ref=dst, send_sem=send_sems.at[7 * t + k], recv_sem=recv_sems.at[7 * t + k],
                device_id=to, device_id_type=MESH)

        def over_ici(t, j, chip_from, to):
            src = specs[t].half_of(w_refs[t], c)
            return copy(t, j, src, specs[t].half_window(f_refs[t], 2 * chip_from[0] + chip_from[1], c), to)

        def passed_on(t, j, chip_from, half):
            win = specs[t].half_window(f_refs[t], 2 * chip_from[0] + chip_from[1], half)
            return copy(t, 3 + j, win, win, sibling)

        def own_shard(t):
            return copy(t, 6, w_refs[t], specs[t].shard_window(f_refs[t], q), sibling)

        def start():
            for t in range(n):
                own_shard(t).start()
                for j, chip in enumerate(chips):
                    over_ici(t, j, (x, y), (*chip, c)).start()

        def middle():
            for t in range(n):
                for j, chip in enumerate(chips):
                    over_ici(t, j, chip, sibling).wait_recv()
                    passed_on(t, j, chip, c).start()

        def finish():
            for t in range(n):
                own_shard(t).wait_recv()
                for j, chip in enumerate(chips):
                    passed_on(t, j, chip, 1 - c).wait_recv()
            for t in range(n):
                own_shard(t).wait_send()
                for j, chip in enumerate(chips):
                    over_ici(t, j, (x, y), (*chip, c)).wait_send()
                    passed_on(t, j, chip, c).wait_send()

        return start, middle, finish

    sems = (pltpu.SemaphoreType.DMA((7 * n,)), pltpu.SemaphoreType.DMA((7 * n,)))
    return hooks, sems


def _gather_fused(shards, specs):
    hooks, sems = _gather_hooks(specs)
    outs = tuple(jax.ShapeDtypeStruct(sp.full, w.dtype) for sp, w in zip(specs, shards))
    return Fused(ins=tuple(shards), outs=outs, sems=sems, hooks=hooks)


def _run_fused(fused, name):
    n_in, n_out = len(fused.ins), len(fused.outs)

    def body(*refs):
        start, middle, finish = fused.hooks(refs[:n_in], refs[n_in:n_in + n_out], *refs[n_in + n_out:])
        start()
        middle()
        finish()

    return pl.pallas_call(
        body, name=name, in_specs=[HBM] * n_in, out_specs=[HBM] * n_out,
        out_shape=list(fused.outs), scratch_shapes=list(fused.sems),
    )(*fused.ins)


def _swap_fused(fulls, specs):
    n = len(fulls)

    def hooks(g_refs, o_refs, send_sems, recv_sems):
        x, y, c, _ = _place()

        def copies():
            return [pltpu.make_async_remote_copy(
                src_ref=specs[t].half_of(g_refs[t], 1 - c), dst_ref=o_refs[t],
                send_sem=send_sems.at[t], recv_sem=recv_sems.at[t],
                device_id=(x, y, 1 - c), device_id_type=MESH) for t in range(n)]

        def start():
            for cp in copies():
                cp.start()

        def finish():
            for cp in copies():
                cp.wait_recv()
            for cp in copies():
                cp.wait_send()

        return start, lambda: None, finish

    outs = tuple(jax.ShapeDtypeStruct(sp.half_of_full, g.dtype) for sp, g in zip(specs, fulls))
    sems = (pltpu.SemaphoreType.DMA((n,)), pltpu.SemaphoreType.DMA((n,)))
    return Fused(ins=tuple(fulls), outs=outs, sems=sems, hooks=hooks)


def _rtile(n, target):
    t = min(n, (target // 16) * 16)
    while n % t:
        t -= 16
    return t


def _add_sibling(full, got, sp, cq_idx, name):
    hr, hc = sp.half
    tr = _rtile(hr, 256)
    nrt = hr // tr

    def body(cq_ref, g_ref, r_ref, ob_ref, of_ref):
        s = pl.program_id(1)
        v = g_ref[...] + r_ref[...]
        ob_ref[...] = v.astype(BF16)

        @pl.when(s == cq_ref[1])
        def _():
            of_ref[...] = v

    if sp.kind == "col":
        g_spec = pl.BlockSpec((tr, hc), lambda i, s, cq: (cq[0] * nrt + i, s))
        r_spec = pl.BlockSpec((tr, hc), lambda i, s, cq: (i, s))
    else:
        g_spec = pl.BlockSpec((tr, hc), lambda i, s, cq: (s * nrt + i, cq[0]))
        r_spec = pl.BlockSpec((tr, hc), lambda i, s, cq: (s * nrt + i, 0))
    return pl.pallas_call(
        body, name=name,
        grid_spec=pltpu.PrefetchScalarGridSpec(
            num_scalar_prefetch=1, grid=(nrt, 4),
            in_specs=[g_spec, r_spec],
            out_specs=[r_spec, pl.BlockSpec((tr, hc), lambda i, s, cq: (i, 0))]),
        out_shape=[jax.ShapeDtypeStruct(sp.half_of_full, BF16), jax.ShapeDtypeStruct(sp.half, F32)],
        compiler_params=_cparams(("parallel", "arbitrary")),
    )(cq_idx, full, got)


def _scatter_fused(sums, specs):
    n = len(sums)

    def hooks(a_refs, o_refs, send_sems, recv_sems):
        x, y, c, chips = _place()

        def copies():
            return [pltpu.make_async_remote_copy(
                src_ref=specs[t].window_of_half(a_refs[t], 2 * chip[0] + chip[1]), dst_ref=o_refs[t].at[j],
                send_sem=send_sems.at[3 * t + j], recv_sem=recv_sems.at[3 * t + j],
                device_id=(*chip, c), device_id_type=MESH) for t in range(n) for j, chip in enumerate(chips)]

        def start():
            for cp in copies():
                cp.start()

        def finish():
            for cp in copies():
                cp.wait_recv()
            for cp in copies():
                cp.wait_send()

        return start, lambda: None, finish

    outs = tuple(jax.ShapeDtypeStruct((3,) + sp.half, a.dtype) for sp, a in zip(specs, sums))
    sems = (pltpu.SemaphoreType.DMA((3 * n,)), pltpu.SemaphoreType.DMA((3 * n,)))
    return Fused(ins=tuple(sums), outs=outs, sems=sems, hooks=hooks)


def _add_chips(own, got, sp, cq_idx, name):
    hr, hc = sp.half
    tr = _rtile(hr, 256)
    nrt = hr // tr

    def body(cq_ref, a_ref, r_ref, o_ref):
        o_ref[...] = ((a_ref[...] + r_ref[0].astype(F32)) + r_ref[1].astype(F32)) + r_ref[2].astype(F32)

    if sp.kind == "col":
        o_spec = pl.BlockSpec((tr, hc), lambda i, cq: (cq[0] * nrt + i, 0))
    else:
        o_spec = pl.BlockSpec((tr, hc), lambda i, cq: (i, cq[0]))
    return pl.pallas_call(
        body, name=name,
        grid_spec=pltpu.PrefetchScalarGridSpec(
            num_scalar_prefetch=1, grid=(nrt,),
            in_specs=[pl.BlockSpec((tr, hc), lambda i, cq: (i, 0)), pl.BlockSpec((3, tr, hc), lambda i, cq: (0, i, 0))],
            out_specs=o_spec),
        out_shape=jax.ShapeDtypeStruct((sp.r, sp.c), F32),
        compiler_params=_cparams(("parallel",)),
    )(cq_idx, own, got)


def _share_halves(shards, specs):
    n = len(shards)

    def body(*refs):
        o_refs = refs[n:2 * n]
        send_sems, recv_sems = refs[2 * n:]
        x, y, c, _ = _place()

        def copy(t, half):
            win = specs[t].half_of(o_refs[t], half)
            return pltpu.make_async_remote_copy(
                src_ref=win, dst_ref=win, send_sem=send_sems.at[t], recv_sem=recv_sems.at[t],
                device_id=(x, y, 1 - c), device_id_type=MESH)

        for t in range(n):
            copy(t, c).start()
        for t in range(n):
            copy(t, 1 - c).wait_recv()
        for t in range(n):
            copy(t, c).wait_send()

    return pl.pallas_call(
        body, name="share_halves",
        in_specs=[HBM] * n, out_specs=[HBM] * n,
        out_shape=[jax.ShapeDtypeStruct((sp.r, sp.c), F32) for sp in specs],
        input_output_aliases={t: t for t in range(n)},
        scratch_shapes=[pltpu.SemaphoreType.DMA((n,)), pltpu.SemaphoreType.DMA((n,))],
    )(*shards)


def _sum_small(vec):
    rows, w = vec.shape

    def body(v_ref, o_ref, buf, send_sems, recv_sems):
        x, y, c, _ = _place()
        me = 4 * x + 2 * y + c
        buf[me] = v_ref[...]
        cps = []
        for k in range(1, 8):
            to = (x ^ (k >> 2), y ^ ((k >> 1) & 1), c ^ (k & 1))
            cps.append(pltpu.make_async_remote_copy(
                src_ref=v_ref, dst_ref=buf.at[me], send_sem=send_sems.at[k - 1], recv_sem=recv_sems.at[k - 1],
                device_id=to, device_id_type=MESH))
        for cp in cps:
            cp.start()
        for k in range(1, 8):
            pltpu.make_async_remote_copy(
                src_ref=v_ref, dst_ref=buf.at[me ^ k], send_sem=send_sems.at[k - 1], recv_sem=recv_sems.at[k - 1],
                device_id=(x, y, c), device_id_type=MESH).wait_recv()
        for cp in cps:
            cp.wait_send()
        total = buf[0]
        for d in range(1, 8):
            total = total + buf[d]
        o_ref[...] = total

    return pl.pallas_call(
        body, name="sum_small",
        in_specs=[pl.BlockSpec(memory_space=pltpu.VMEM)], out_specs=pl.BlockSpec(memory_space=pltpu.VMEM),
        out_shape=jax.ShapeDtypeStruct((rows, w), F32),
        scratch_shapes=[pltpu.VMEM((8, rows, w), F32), pltpu.SemaphoreType.DMA((7,)), pltpu.SemaphoreType.DMA((7,))],
    )(vec)


def _adam_math(w, g, m, v):
    m = ADAM_B1 * m + (1.0 - ADAM_B1) * g
    v = ADAM_B2 * v + (1.0 - ADAM_B2) * (g * g)
    m_hat = m / (1.0 - ADAM_B1 ** ADAM_STEP)
    v_hat = v / (1.0 - ADAM_B2 ** ADAM_STEP)
    delta = -ADAM_LR * (m_hat / (jnp.sqrt(v_hat) + ADAM_EPS) + ADAM_WD * w)
    return delta, m, v


def _adamw(w, g, m, v, name, tr=128):
    R, Cn = w.shape
    tr = min(tr, R)
    assert R % tr == 0

    def body(w_ref, g_ref, m_ref, v_ref, go_ref, d_ref, mo_ref, vo_ref):
        gv = g_ref[...]
        d, mn, vn = _adam_math(w_ref[...], gv, m_ref[...], v_ref[...])
        go_ref[...] = gv
        d_ref[...] = d
        mo_ref[...] = mn
        vo_ref[...] = vn

    blk = pl.BlockSpec((tr, Cn), lambda i: (i, 0))
    sds = jax.ShapeDtypeStruct((R, Cn), F32)
    return pl.pallas_call(
        body, name=name, grid=(R // tr,),
        in_specs=[blk] * 4, out_specs=[blk] * 4, out_shape=[sds] * 4,
        compiler_params=_cparams(("parallel",)),
    )(w, g, m, v)


ROW_LOSS, ROW_BIAS, ROW_NORM, ROW_LB0, ROW_G1, ROW_G2, ROW_G3, ROW_G4, ROW_LB1 = range(9)
SMALL_ROWS = 16


def _small_update(gsum, wp, mp, vp):
    _, w = gsum.shape

    def body(g_ref, w_ref, m_ref, v_ref, go_ref, d_ref, mo_ref, vo_ref):
        wv = w_ref[...]
        l0 = wv[ROW_LB0:ROW_LB0 + 1, :]
        l1 = wv[ROW_LB1:ROW_LB1 + 1, :]
        mx = jnp.maximum(l0, l1)
        e0 = jnp.exp(l0 - mx)
        e1 = jnp.exp(l1 - mx)
        p0 = e0 / (e0 + e1)
        gs = g_ref[...]
        dl0 = gs[ROW_LB0:ROW_LB0 + 1, :] * p0 * (1.0 - p0)
        row8 = lax.broadcasted_iota(jnp.int32, gs.shape, 0)
        top = jnp.where(row8 == ROW_LB0, dl0, jnp.where(row8 == ROW_LOSS, 0.0, gs))
        bot = jnp.where(row8 == ROW_LB1 - 8, -dl0, 0.0)
        g16 = jnp.concatenate([top, bot], axis=0)
        d, mn, vn = _adam_math(wv, g16, m_ref[...], v_ref[...])
        go_ref[...] = g16
        d_ref[...] = d
        mo_ref[...] = mn
        vo_ref[...] = vn

    sds = jax.ShapeDtypeStruct((SMALL_ROWS, w), F32)
    full = pl.BlockSpec(memory_space=pltpu.VMEM)
    return pl.pallas_call(
        body, name="small_update", in_specs=[full] * 4, out_specs=[full] * 4, out_shape=[sds] * 4,
    )(gsum, wp, mp, vp)


def _w_in_layout(dm):
    cs = dm.NIN // 4
    place = [((cs * q) // 128, (cs * q) % 128) for q in range(4)]
    cp = -(-(cs + max(sh for _, sh in place)) // 128) * 128
    return cs, cp, place


def _zeros_cols(rows, n, dtype):
    return jnp.zeros((rows, n), dtype)


def _a_part_tiles(v, H, to_head_major):
    tile = lambda k: v[:, k * HEAD:(k + 1) * HEAD]
    if to_head_major:
        tiles = [tile(s * H + h) for h in range(H) for s in range(4)]
    else:
        tiles = [tile(h * 4 + s) for s in range(4) for h in range(H)]
    return jnp.concatenate(tiles + [v[:, 4 * H * HEAD:]], axis=1)


def _unshuffle_w_in(wp, dm, tr=128):
    D, H = dm.D, dm.H
    cs, cp, place = _w_in_layout(dm)
    nm, nrest = dm.NMAIN, dm.NREST
    ng = nm + nrest
    tr = min(tr, D)

    def body(x_ref, main_ref, rest_ref):
        g = None
        for q, (t0, sh) in enumerate(place):
            xq = x_ref[:, q * cp:(q + 1) * cp]
            yq = pltpu.roll(xq, sh, axis=1) if sh else xq
            width = min(cp, ng - t0 * 128)
            parts = [_zeros_cols(tr, t0 * 128, wp.dtype)] if t0 else []
            parts.append(yq[:, :width])
            if ng - t0 * 128 - width:
                parts.append(_zeros_cols(tr, ng - t0 * 128 - width, wp.dtype))
            placed = jnp.concatenate(parts, axis=1)
            g = placed if g is None else g + placed
        main_ref[...] = _a_part_tiles(g[:, :nm], H, to_head_major=True)
        tail = g[:, nm:]
        gates = pltpu.roll(tail, nrest - H, axis=1)[:, :2 * D]
        lane = lax.broadcasted_iota(jnp.int32, (tr, HEAD), 1)
        rest_ref[...] = jnp.concatenate([gates, jnp.where(lane < H, tail[:, :HEAD], 0)], axis=1)

    return pl.pallas_call(
        body, name="unshuffle_w_in", grid=(D // tr,),
        in_specs=[pl.BlockSpec((tr, 4 * cp), lambda i: (i, 0))],
        out_specs=[pl.BlockSpec((tr, nm), lambda i: (i, 0)), pl.BlockSpec((tr, nrest), lambda i: (i, 0))],
        out_shape=[jax.ShapeDtypeStruct((D, nm), wp.dtype), jax.ShapeDtypeStruct((D, nrest), wp.dtype)],
        compiler_params=_cparams(("parallel",)),
    )(wp)


def _shuffle_w_in(d_main, d_rest, dm, tr=64):
    D, H = dm.D, dm.H
    cs, cp, place = _w_in_layout(dm)
    nm, nrest = dm.NMAIN, dm.NREST
    ng = nm + nrest
    tr = min(tr, D)

    def body(m_ref, r_ref, o_ref):
        rv = r_ref[...]
        lane = lax.broadcasted_iota(jnp.int32, (tr, HEAD), 1)
        tail = pltpu.roll(jnp.concatenate([rv[:, :2 * D], _zeros_cols(tr, HEAD, F32)], axis=1), H, axis=1)
        head = jnp.where(lane < H, rv[:, 2 * D:], 0.0) + tail[:, :HEAD]
        main_std = _a_part_tiles(m_ref[...], H, to_head_major=False)
        g = jnp.concatenate([main_std, head, tail[:, HEAD:]], axis=1)
        lanes = lax.broadcasted_iota(jnp.int32, (tr, cp), 1)
        outs = []
        for q, (t0, sh) in enumerate(place):
            width = min(cp, ng - t0 * 128)
            win = g[:, t0 * 128:t0 * 128 + width]
            if width < cp:
                win = jnp.concatenate([win, _zeros_cols(tr, cp - width, F32)], axis=1)
            xq = pltpu.roll(win, cp - sh, axis=1) if sh else win
            outs.append(jnp.where(lanes < cs, xq, 0.0))
        o_ref[...] = jnp.concatenate(outs, axis=1)

    return pl.pallas_call(
        body, name="shuffle_w_in", grid=(D // tr,),
        in_specs=[pl.BlockSpec((tr, nm), lambda i: (i, 0)), pl.BlockSpec((tr, nrest), lambda i: (i, 0))],
        out_specs=pl.BlockSpec((tr, 4 * cp), lambda i: (i, 0)),
        out_shape=jax.ShapeDtypeStruct((D, 4 * cp), F32),
        compiler_params=_cparams(("parallel",)),
    )(d_main, d_rest)


def _pack_small(dm, bias, norm_w, lb_logits, g1, g2, g3, g4):
    D = dm.D
    row = lambda v: jnp.pad(v.reshape(1, -1), ((0, 0), (0, D - v.size)))
    rows = [jnp.zeros((1, D), F32), row(bias), row(norm_w), row(lb_logits[0]), row(g1), row(g2), row(g3), row(g4),
            row(lb_logits[1]), jnp.zeros((SMALL_ROWS - 9, D), F32)]
    return jnp.concatenate(rows, axis=0)


def _unpack_small(p, dm):
    H, AW = dm.H, dm.AW
    return (p[ROW_BIAS:ROW_BIAS + 1, :H], jnp.concatenate([p[ROW_LB0:ROW_LB0 + 1, :AW], p[ROW_LB1:ROW_LB1 + 1, :AW]], axis=0),
            p[ROW_NORM:ROW_NORM + 1, :HEAD], p[ROW_G1:ROW_G1 + 1], p[ROW_G2:ROW_G2 + 1], p[ROW_G3:ROW_G3 + 1],
            p[ROW_G4:ROW_G4 + 1])


def _step(dm, x, w_in, b_fox_f, hgrn_lb_logits, hgrn_norm_w, w_up_a, w_up_b, w_o, norm_mix_pre, norm_mix_post,
          norm_ffn_pre, norm_ffn_post, w_ffn_in, w_ffn_down, loss_target, moments_m, moments_v):
    xi, yi, ci = lax.axis_index("x"), lax.axis_index("y"), lax.axis_index("c")
    cq_idx = jnp.stack([ci, 2 * xi + yi]).astype(jnp.int32)
    D, AW, FF = dm.D, dm.AW, dm.FF
    cs, cp, _ = _w_in_layout(dm)
    big_names = ["w_in", "w_up_a", "w_up_b", "w_o", "w_ffn_in", "w_ffn_down"]
    specs = [Sharded("col", D, cp), Sharded("col", AW, D // 4), Sharded("col", AW, D // 4),
             Sharded("row", D // 4, D), Sharded("col", D, 2 * FF // 4), Sharded("row", FF // 4, D)]
    shards = [w_in[0], w_up_a[0], w_up_b[0], w_o[0], w_ffn_in[0], w_ffn_down[0]]

    sent = [jnp.pad(shards[0].astype(BF16), ((0, 0), (0, cp - cs)))] + [w.astype(BF16) for w in shards[1:]]
    (f_in,) = _run_fused(_gather_fused(sent[:1], specs[:1]), "gather_w_in")
    w_main, w_rest = _unshuffle_w_in(f_in, dm)

    class ReduceHooks:
        pairs = [None] * 6
        fulls = None

        def pair_sums(self, fulls, from_sibling, first):
            for t, (g, r) in enumerate(zip(fulls, from_sibling), start=first):
                self.pairs[t] = _add_sibling(g, r, specs[t], cq_idx, "add_sibling_" + big_names[t])
            return _scatter_fused([self.pairs[t][0] for t in range(first, first + len(fulls))],
                                  specs[first:first + len(fulls)])

        def swap_later(self, fulls):
            self.fulls = fulls
            return _swap_fused(fulls, specs[1:])

        def scatter_later(self, from_sibling):
            return self.pair_sums(self.fulls, from_sibling, 1)

        def scatter_w_in(self, d_main, d_rest):
            full = [_shuffle_w_in(d_main, d_rest, dm)]
            return self.pair_sums(full, _run_fused(_swap_fused(full, specs[:1]), "swap_halves_w_in"), 0)

    hooks = ReduceHooks()

    bias_p = jnp.pad(b_fox_f, ((0, 0), (0, HEAD - dm.H)))
    dx, _, small, from_chips = _local_step(
        dm, x[0], loss_target[0], w_main, w_rest, _gather_fused(sent[1:], specs[1:]), bias_p,
        hgrn_lb_logits, hgrn_norm_w, norm_mix_pre, norm_mix_post, norm_ffn_pre, norm_ffn_post,
        reduce_hooks=hooks)

    halves = [_add_chips(p[1], r, sp, cq_idx, "add_chips_" + n)
              for p, r, sp, n in zip(hooks.pairs, from_chips, specs, big_names)]
    grads = list(_share_halves(halves, specs))
    grads[0] = grads[0][:, :cs]

    row = lambda v: jnp.pad(v.reshape(1, -1), ((0, 0), (0, D - v.size)))
    vec = jnp.concatenate([row(small["loss"][:, :1]), row(small["bias"][:, :dm.H]), row(small["norm_w"]),
                           row(small["lb"]), small["g1"], small["g2"], small["g3"], small["g4"]], axis=0)
    gsum = _sum_small(vec)
    loss = gsum[ROW_LOSS, 0]

    smalls = lambda t: (t["b_fox_f"], t["hgrn_norm_w"], t["hgrn_lb_logits"], t["norm_mix_pre"], t["norm_mix_post"],
                        t["norm_ffn_pre"], t["norm_ffn_post"])
    params = dict(b_fox_f=b_fox_f, hgrn_norm_w=hgrn_norm_w, hgrn_lb_logits=hgrn_lb_logits, norm_mix_pre=norm_mix_pre,
                  norm_mix_post=norm_mix_post, norm_ffn_pre=norm_ffn_pre, norm_ffn_post=norm_ffn_post)
    sg, sd, sm, sv = _small_update(gsum, _pack_small(dm, *smalls(params)), _pack_small(dm, *smalls(moments_m)),
                                   _pack_small(dm, *smalls(moments_v)))
    big_out = {}
    for name, wsh, g in zip(big_names, shards, grads):
        go, d, mn, vn = _adamw(wsh, g, moments_m[name][0], moments_v[name][0], "adamw_" + name)
        big_out[name] = (go[None], d[None], mn[None], vn[None])

    order = ["w_in", "b_fox_f", "hgrn_lb_logits", "hgrn_norm_w", "w_up_a", "w_up_b", "w_o", "norm_mix_pre",
             "norm_mix_post", "norm_ffn_pre", "norm_ffn_post", "w_ffn_in", "w_ffn_down"]
    outs = []
    for kind, packed in enumerate([sg, sd, sm, sv]):
        b, lbl, nw, p1, p2, p3, p4 = _unpack_small(packed, dm)
        sm_map = dict(b_fox_f=b, hgrn_lb_logits=lbl, hgrn_norm_w=nw, norm_mix_pre=p1, norm_mix_post=p2,
                      norm_ffn_pre=p3, norm_ffn_post=p4)
        outs.append([big_out[n][kind] if n in big_out else sm_map[n] for n in order])
    return (loss, dx[None], *outs[0], *outs[1], *outs[2], *outs[3])


def kernel(x, w_in, b_fox_f, hgrn_lb_logits, hgrn_norm_w, w_up_a, w_up_b, w_o, norm_mix_pre, norm_mix_post, norm_ffn_pre, norm_ffn_post, w_ffn_in, w_ffn_down, loss_target, m_w_in, m_b_fox_f, m_hgrn_lb_logits, m_hgrn_norm_w, m_w_up_a, m_w_up_b, m_w_o, m_norm_mix_pre, m_norm_mix_post, m_norm_ffn_pre, m_norm_ffn_post, m_w_ffn_in, m_w_ffn_down, v_w_in, v_b_fox_f, v_hgrn_lb_logits, v_hgrn_norm_w, v_w_up_a, v_w_up_b, v_w_o, v_norm_mix_pre, v_norm_mix_post, v_norm_ffn_pre, v_norm_ffn_post, v_w_ffn_in, v_w_ffn_down):
    dm = Dims(T=x.shape[1], D=x.shape[2], FF=w_ffn_down.shape[1] * 4)
    moments_m = dict(w_in=m_w_in, b_fox_f=m_b_fox_f, hgrn_lb_logits=m_hgrn_lb_logits, hgrn_norm_w=m_hgrn_norm_w,
                     w_up_a=m_w_up_a, w_up_b=m_w_up_b, w_o=m_w_o, norm_mix_pre=m_norm_mix_pre,
                     norm_mix_post=m_norm_mix_post, norm_ffn_pre=m_norm_ffn_pre, norm_ffn_post=m_norm_ffn_post,
                     w_ffn_in=m_w_ffn_in, w_ffn_down=m_w_ffn_down)
    moments_v = dict(w_in=v_w_in, b_fox_f=v_b_fox_f, hgrn_lb_logits=v_hgrn_lb_logits, hgrn_norm_w=v_hgrn_norm_w,
                     w_up_a=v_w_up_a, w_up_b=v_w_up_b, w_o=v_w_o, norm_mix_pre=v_norm_mix_pre,
                     norm_mix_post=v_norm_mix_post, norm_ffn_pre=v_norm_ffn_pre, norm_ffn_post=v_norm_ffn_post,
                     w_ffn_in=v_w_ffn_in, w_ffn_down=v_w_ffn_down)
    return _step(dm, x, w_in, b_fox_f, hgrn_lb_logits, hgrn_norm_w, w_up_a, w_up_b, w_o, norm_mix_pre, norm_mix_post,
                 norm_ffn_pre, norm_ffn_post, w_ffn_in, w_ffn_down, loss_target, moments_m, moments_v)
```
